```python
import jax, jax.numpy as jnp
from jax import lax
import numpy as np

D_MODEL = 1024
BATCH = 2
SEQ = 8192
DEPTH = 2
DEC_BATCH = 32
DEC_SEQ = 8
PAST_LEN = 16384
PAGE_SIZE = 128

D_CONV = D_MODEL // 4
CONV_W = 31
CONV_HIST = CONV_W - 1
HEAD_DIM = 64
D_ATTN = D_MODEL // 2
N_HEADS = D_ATTN // HEAD_DIM
N_KV_HEADS = 2
KV_DIM = N_KV_HEADS * HEAD_DIM
GROUP = N_HEADS // N_KV_HEADS
IDX_HEADS = 8
IDX_DIM = 64
TOPK_MAX = 256
Q_BLOCK = 128
D_POOL = D_MODEL // 4
POOL_WINDOWS = (2, 4, 8, 16)
N_POOL_GROUPS = len(POOL_WINDOWS)
POOL_GC = D_POOL // N_POOL_GROUPS
POOL_HIST = max(POOL_WINDOWS) - 1
N_BRANCH = 3
D_FF = 4 * D_MODEL
ROPE_THETA = 10000.0
RMS_EPS = 1e-6
LN_EPS = 1e-5
IN_SPLITS = (D_CONV, D_CONV, D_ATTN, KV_DIM, KV_DIM, IDX_HEADS * IDX_DIM, IDX_DIM, IDX_HEADS, D_POOL, N_BRANCH * D_MODEL)
D_IN = sum(IN_SPLITS)
SPLIT_AT = tuple(int(i) for i in np.cumsum(IN_SPLITS)[:-1])

kernel_name = "hybrid_gated_conv_dsa_pool_decoder_step"

F32 = jnp.float32


def rmsnorm(x, g):
    xf = x.astype(F32)
    y = xf * lax.rsqrt(jnp.mean(xf * xf, axis=-1, keepdims=True) + RMS_EPS)
    return (y * g.astype(F32)).astype(x.dtype)


def layernorm(x, g, b):
    xf = x.astype(F32)
    mu = jnp.mean(xf, axis=-1, keepdims=True)
    var = jnp.mean(jnp.square(xf - mu), axis=-1, keepdims=True)
    y = (xf - mu) * lax.rsqrt(var + LN_EPS)
    return (y * g.astype(F32) + b.astype(F32)).astype(x.dtype)


def rope(x, pos):
    half = x.shape[-1] // 2
    inv = ROPE_THETA ** (-jnp.arange(half, dtype=F32) / half)
    ang = pos.astype(F32)[:, None] * inv[None, :]
    cos = jnp.cos(ang)[None, :, None, :]
    sin = jnp.sin(ang)[None, :, None, :]
    xf = x.astype(F32)
    x1, x2 = xf[..., :half], xf[..., half:]
    return jnp.concatenate([x1 * cos - x2 * sin, x2 * cos + x1 * sin], axis=-1).astype(x.dtype)


def index_scores(qi, wi, ki, q_pos, k_pos):
    s = jnp.einsum('bthd,bsd->bths', qi, ki).astype(F32)
    s = jnp.einsum('bths,bth->bts', jax.nn.relu(s), wi.astype(F32))
    causal = k_pos[None, :] <= q_pos[:, None]
    return jnp.where(causal[None], s, -jnp.inf)


def sparse_attend(q, k_sel, v_sel, valid):
    B, T = q.shape[:2]
    qg = q.reshape(B, T, N_KV_HEADS, GROUP, HEAD_DIM)
    s = jnp.einsum('btngd,btknd->btngk', qg, k_sel).astype(F32) * (HEAD_DIM ** -0.5)
    s = jnp.where(valid[:, :, None, None, :], s, -jnp.inf)
    p = jax.nn.softmax(s, axis=-1).astype(v_sel.dtype)
    o = jnp.einsum('btngk,btknd->btngd', p, v_sel)
    return o.reshape(B, T, N_HEADS * HEAD_DIM)


def prompt_attend(q, k, v, qi, ki, wi, pos):
    B, T = q.shape[:2]
    nb = T // Q_BLOCK
    top = min(TOPK_MAX, T // 4)
    k_pos = jnp.arange(T)
    bi = jnp.arange(B)[:, None, None]

    def block(args):
        qb, qib, wib, pb = args
        s = index_scores(qib, wib, ki, pb, k_pos)
        _, idx = lax.top_k(s, top)
        valid = idx <= pb[None, :, None]
        return sparse_attend(qb, k[bi, idx], v[bi, idx], valid)

    def to_blocks(t):
        return jnp.moveaxis(t.reshape((B, nb, Q_BLOCK) + t.shape[2:]), 1, 0)

    out = lax.map(block, (to_blocks(q), to_blocks(qi), to_blocks(wi), pos.reshape(nb, Q_BLOCK)))
    return jnp.moveaxis(out, 0, 1).reshape(B, T, D_ATTN)


def make_sample_attend(ck, cv, cki, page_table):
    def attend(q, k, v, qi, ki, wi, pos):
        B, T = q.shape[:2]
        L = PAST_LEN + T
        top = min(TOPK_MAX, L // 4)
        ki_past = cki[page_table].reshape(B, PAST_LEN, IDX_DIM)
        ki_all = jnp.concatenate([ki_past, ki], axis=1)
        s = index_scores(qi, wi, ki_all, pos, jnp.arange(L))
        _, idx = lax.top_k(s, top)
        bi = jnp.arange(B)[:, None, None]
        in_past = (idx < PAST_LEN)[..., None, None]
        pidx = jnp.minimum(idx, PAST_LEN - 1)
        phys = page_table[bi, pidx // PAGE_SIZE]
        off = pidx % PAGE_SIZE
        nidx = jnp.clip(idx - PAST_LEN, 0, T - 1)
        k_sel = jnp.where(in_past, ck[phys, off], k[bi, nidx])
        v_sel = jnp.where(in_past, cv[phys, off], v[bi, nidx])
        valid = idx <= pos[None, :, None]
        return sparse_attend(q, k_sel, v_sel, valid)
    return attend


def depthwise_causal_conv(ext, w, b):
    y = lax.conv_general_dilated(ext, w[:, None, :].astype(ext.dtype), window_strides=(1,), padding='VALID',
                                 dimension_numbers=('NWC', 'WIO', 'NWC'), feature_group_count=ext.shape[-1])
    return y + b


def pool_mix(xc, hist, pos, pool_w, pool_scale):
    B, T, _ = xc.shape
    ext = jnp.concatenate([hist, xc], axis=1).astype(F32)
    cs = jnp.concatenate([jnp.zeros((B, 1, D_POOL), F32), jnp.cumsum(ext, axis=1)], axis=1)
    end = cs[:, POOL_HIST + 1:]
    means = []
    for g, w in enumerate(POOL_WINDOWS):
        sl = slice(g * POOL_GC, (g + 1) * POOL_GC)
        start = cs[:, POOL_HIST + 1 - w: POOL_HIST + 1 - w + T, sl]
        cnt = jnp.minimum(pos + 1, w).astype(F32)[None, :, None]
        means.append((end[..., sl] - start) / cnt)
    d = (jnp.concatenate(means, axis=-1) - xc.astype(F32)).astype(xc.dtype)
    d = d.reshape(B, T, N_POOL_GROUPS, POOL_GC)
    y = jnp.einsum('btgc,gcd->btgd', d, pool_w).reshape(B, T, D_POOL)
    return y * pool_scale


def mixer(h, pos, conv_hist, pool_hist, attend, w_in, conv_w, conv_b, conv_ln_g, conv_ln_b,
          w_conv_out, w_attn_out, pool_w, pool_scale, w_pool_out, w_out):
    B, T, _ = h.shape
    z = h @ w_in
    a_in, a_gate, q, k, v, qi, ki, wi, xc, gates = jnp.split(z, SPLIT_AT, axis=-1)
    u = a_in * jax.nn.sigmoid(a_gate)
    ext = jnp.concatenate([conv_hist, u], axis=1)
    c = layernorm(depthwise_causal_conv(ext, conv_w, conv_b), conv_ln_g, conv_ln_b)
    y_a = (c * jax.nn.sigmoid(c)) @ w_conv_out
    new_conv = ext[:, -CONV_HIST:]
    q = rope(q.reshape(B, T, N_HEADS, HEAD_DIM), pos)
    k = rope(k.reshape(B, T, N_KV_HEADS, HEAD_DIM), pos)
    v = v.reshape(B, T, N_KV_HEADS, HEAD_DIM)
    qi = rope(qi.reshape(B, T, IDX_HEADS, IDX_DIM), pos)
    ki = rope(ki[:, :, None, :], pos)[:, :, 0, :]
    wi = wi * (IDX_HEADS ** -0.5 * IDX_DIM ** -0.5)
    y_b = attend(q, k, v, qi, ki, wi, pos) @ w_attn_out
    y_c = pool_mix(xc, pool_hist, pos, pool_w, pool_scale) @ w_pool_out
    new_pool = jnp.concatenate([pool_hist, xc], axis=1)[:, -POOL_HIST:]
    g_a, g_b, g_c = jnp.split(jax.nn.sigmoid(gates), N_BRANCH, axis=-1)
    out = (g_a * y_a + g_b * y_b + g_c * y_c) @ w_out
    return out, (k, v, ki, new_conv, new_pool)


def ffn(h, w1, w2):
    r = jax.nn.relu(h @ w1)
    return (r * r) @ w2


def setup_inputs(seed: int = 0) -> dict:
    key = jax.random.key(seed)
    ks = jax.random.split(key, 24)
    n_pages = PAST_LEN // PAGE_SIZE
    used = DEC_BATCH * n_pages
    n_pool = used + max(1, used // 4)

    def nrm(k, shape, scale):
        return jax.random.normal(k, shape, F32) * scale

    page_table = jax.random.permutation(ks[7], n_pool)[:used].reshape(DEC_BATCH, n_pages).astype(jnp.int32)
    return {
        "x_prompt": nrm(ks[0], (BATCH, SEQ, D_MODEL), 1.0),
        "x_sample": nrm(ks[1], (DEC_BATCH, DEC_SEQ, D_MODEL), 1.0),
        "cache_k": nrm(ks[2], (DEPTH, n_pool, PAGE_SIZE, N_KV_HEADS, HEAD_DIM), 1.0),
        "cache_v": nrm(ks[3], (DEPTH, n_pool, PAGE_SIZE, N_KV_HEADS, HEAD_DIM), 1.0),
        "cache_kidx": nrm(ks[4], (DEPTH, n_pool, PAGE_SIZE, IDX_DIM), 1.0),
        "state_conv": nrm(ks[5], (DEPTH, DEC_BATCH, CONV_HIST, D_CONV), 0.5),
        "state_pool": nrm(ks[6], (DEPTH, DEC_BATCH, POOL_HIST, D_POOL), 1.0),
        "page_table": page_table,
        "g_mix": 1.0 + nrm(ks[8], (DEPTH, D_MODEL), 0.1),
        "w_in": nrm(ks[9], (DEPTH, D_MODEL, D_IN), D_MODEL ** -0.5),
        "conv_w": nrm(ks[10], (DEPTH, CONV_W, D_CONV), CONV_W ** -0.5),
        "conv_b": nrm(ks[11], (DEPTH, D_CONV), 0.02),
        "conv_ln_g": 1.0 + nrm(ks[12], (DEPTH, D_CONV), 0.1),
        "conv_ln_b": nrm(ks[13], (DEPTH, D_CONV), 0.02),
        "w_conv_out": nrm(ks[14], (DEPTH, D_CONV, D_MODEL), D_CONV ** -0.5),
        "w_attn_out": nrm(ks[15], (DEPTH, D_ATTN, D_MODEL), D_ATTN ** -0.5),
        "pool_w": nrm(ks[16], (DEPTH, N_POOL_GROUPS, POOL_GC, POOL_GC), POOL_GC ** -0.5),
        "pool_scale": 1.0 + nrm(ks[17], (DEPTH, D_POOL), 0.1),
        "w_pool_out": nrm(ks[18], (DEPTH, D_POOL, D_MODEL), D_POOL ** -0.5),
        "w_out": nrm(ks[19], (DEPTH, D_MODEL, D_MODEL), D_MODEL ** -0.5),
        "g_ffn": 1.0 + nrm(ks[20], (DEPTH, D_MODEL), 0.1),
        "w_ff1": nrm(ks[21], (DEPTH, D_MODEL, D_FF), D_MODEL ** -0.5),
        "w_ff2": nrm(ks[22], (DEPTH, D_FF, D_MODEL), D_FF ** -0.5),
        "g_final": 1.0 + nrm(ks[23], (D_MODEL,), 0.1),
    }


def reference(x_prompt, x_sample, cache_k, cache_v, cache_kidx, state_conv, state_pool, page_table,
              g_mix, w_in, conv_w, conv_b, conv_ln_g, conv_ln_b, w_conv_out, w_attn_out,
              pool_w, pool_scale, w_pool_out, w_out, g_ffn, w_ff1, w_ff2, g_final):
    xp, xs = x_prompt, x_sample
    B, T = xp.shape[:2]
    pos_p = jnp.arange(T)
    pos_s = PAST_LEN + jnp.arange(xs.shape[1])
    conv_hist_p = jnp.zeros((B, CONV_HIST, D_CONV), xp.dtype)
    pool_hist_p = jnp.zeros((B, POOL_HIST, D_POOL), xp.dtype)
    st_p = [[] for _ in range(5)]
    st_s = [[] for _ in range(5)]
    for l in range(DEPTH):
        lw = (w_in[l], conv_w[l], conv_b[l], conv_ln_g[l], conv_ln_b[l], w_conv_out[l], w_attn_out[l],
              pool_w[l], pool_scale[l], w_pool_out[l], w_out[l])
        m_p, new_p = mixer(rmsnorm(xp, g_mix[l]), pos_p, conv_hist_p, pool_hist_p, prompt_attend, *lw)
        xp = xp + m_p
        xp = xp + ffn(rmsnorm(xp, g_ffn[l]), w_ff1[l], w_ff2[l])
        attend_s = make_sample_attend(cache_k[l], cache_v[l], cache_kidx[l], page_table)
        m_s, new_s = mixer(rmsnorm(xs, g_mix[l]), pos_s, state_conv[l], state_pool[l], attend_s, *lw)
        xs = xs + m_s
        xs = xs + ffn(rmsnorm(xs, g_ffn[l]), w_ff1[l], w_ff2[l])
        for i in range(5):
            st_p[i].append(new_p[i])
            st_s[i].append(new_s[i])
    y_prompt = rmsnorm(xp, g_final)
    y_sample = rmsnorm(xs, g_final)
    sp = [jnp.stack(a, axis=0) for a in st_p]
    ss = [jnp.stack(a, axis=0) for a in st_s]
    return (y_prompt, y_sample, sp[0], sp[1], sp[2], sp[3], sp[4], ss[0], ss[1], ss[2], ss[3], ss[4])
```

```python
import functools

import jax
import jax.numpy as jnp
from jax import lax
from jax.experimental import pallas as pl
from jax.experimental.pallas import tpu as pltpu

F32 = jnp.float32
BF16 = jnp.bfloat16

D_MODEL = 1024
D_CONV = 256
CONV_W = 31
CONV_HIST = CONV_W - 1
HEAD_DIM = 64
D_ATTN = 512
N_HEADS = 8
N_KV_HEADS = 2
KV_DIM = N_KV_HEADS * HEAD_DIM
GROUP = N_HEADS // N_KV_HEADS
IDX_HEADS = 8
IDX_DIM = 64
TOPK_MAX = 256
D_POOL = 256
POOL_WINDOWS = (2, 4, 8, 16)
POOL_GC = D_POOL // len(POOL_WINDOWS)
POOL_HIST = max(POOL_WINDOWS) - 1
N_BRANCH = 3
D_FF = 4 * D_MODEL
ROPE_THETA = 10000.0
RMS_EPS = 1e-6
LN_EPS = 1e-5
PAGE_SIZE = 128

LANES = 128
CONV_PAD = 32
POOL_PAD = 16
BIG = 3.0e38
NEG_INF = float("-inf")

C_AIN, C_AGATE, C_Q, C_K, C_V, C_QI, C_XC, C_KI, C_WI, D_A = 0, 256, 512, 1024, 1152, 1280, 1792, 2048, 2176, 2304

VMEM_LIMIT = 56 * 1024 * 1024


def _cparams(sem):
    return pltpu.CompilerParams(dimension_semantics=sem, vmem_limit_bytes=VMEM_LIMIT)


def _const_spec(shape):
    n = len(shape)
    return pl.BlockSpec(shape, lambda *_: (0,) * n)


def _proj_kernel(x_ref, g_ref, w_ref, cos_ref, sin_ref, chist_ref, phist_ref, convw_ref, convb_ref,
                 lng_ref, lnb_ref, poolw_ref, pscale_ref,
                 k_ref, v_ref, ki_ref, u_ref, xc_ref, kbf_ref, vbf_ref, kibf_ref, q_ref, qi_ref, wi_ref,
                 ca_ref, py_ref,
                 cs_ref, ps_ref, p2_ref, p4_ref, p8_ref, p16_ref, *, G, TM, pos0, n_tiles):
    j = pl.program_id(1)
    R = G * TM
    x = x_ref[...].reshape(R, D_MODEL)
    ms = jnp.mean(x * x, axis=-1, keepdims=True)
    h = (x * lax.rsqrt(ms + RMS_EPS) * g_ref[...]).astype(BF16)
    z = jnp.dot(h, w_ref[...], preferred_element_type=F32)

    cos = cos_ref[...]
    sin = sin_ref[...]
    if G > 1:
        cos = jnp.concatenate([cos] * G, axis=0)
        sin = jnp.concatenate([sin] * G, axis=0)

    def rope(t):
        n = t.shape[1] // LANES
        c = jnp.concatenate([cos] * n, axis=1) if n > 1 else cos
        s = jnp.concatenate([sin] * n, axis=1) if n > 1 else sin
        lane = lax.broadcasted_iota(jnp.int32, t.shape, 1)
        first = (lane & (HEAD_DIM - 1)) < HEAD_DIM // 2
        sw = jnp.where(first, pltpu.roll(t, t.shape[1] - HEAD_DIM // 2, 1), pltpu.roll(t, HEAD_DIM // 2, 1))
        return t * c + sw * s

    q = rope(z[:, C_Q:C_Q + D_ATTN]) * (HEAD_DIM ** -0.5)
    k = rope(z[:, C_K:C_K + KV_DIM])
    v = z[:, C_V:C_V + KV_DIM]
    qi = rope(z[:, C_QI:C_QI + IDX_HEADS * IDX_DIM])
    ki = rope(z[:, C_KI:C_KI + LANES])[:, :IDX_DIM]
    wi = z[:, C_WI:C_WI + LANES] * (IDX_HEADS ** -0.5 * IDX_DIM ** -0.5)

    k_ref[...] = k.reshape(G, TM, KV_DIM)
    v_ref[...] = v.reshape(G, TM, KV_DIM)
    ki_ref[...] = ki.reshape(G, TM, IDX_DIM)
    kbf_ref[...] = k.reshape(G, TM, KV_DIM).astype(kbf_ref.dtype)
    vbf_ref[...] = v.reshape(G, TM, KV_DIM).astype(vbf_ref.dtype)
    kibf_ref[...] = ki.reshape(G, TM, IDX_DIM).astype(kibf_ref.dtype)
    q_ref[...] = q.reshape(G, TM, D_ATTN).astype(q_ref.dtype)
    qi_ref[...] = qi.reshape(G, TM, IDX_HEADS * IDX_DIM).astype(qi_ref.dtype)
    wi_ref[...] = wi.reshape(G, TM, LANES)

    a_in = z[:, C_AIN:C_AIN + D_CONV]
    a_gate = z[:, C_AGATE:C_AGATE + D_CONV]
    u = (a_in * jax.nn.sigmoid(a_gate)).reshape(G, TM, D_CONV)
    u_ref[...] = u

    @pl.when(j == 0)
    def _():
        cs_ref[:, 0:CONV_PAD, :] = chist_ref[...]
        ps_ref[:, 0:POOL_PAD, :] = phist_ref[...]

    cs_ref[:, CONV_PAD:CONV_PAD + TM, :] = u
    off = CONV_PAD - CONV_HIST
    rc = min(TM, 64)
    gc = min(G, 8)
    for g0 in range(0, G, gc):
        for r0 in range(0, TM, rc):
            acc = jnp.zeros((gc, rc, D_CONV), F32) + convb_ref[...][None]
            for t in range(CONV_W):
                acc = acc + convw_ref[t:t + 1, :][None] * cs_ref[g0:g0 + gc, off + t + r0:off + t + r0 + rc, :]
            mu = jnp.mean(acc, axis=-1, keepdims=True)
            var = jnp.mean(jnp.square(acc - mu), axis=-1, keepdims=True)
            y = (acc - mu) * lax.rsqrt(var + LN_EPS) * lng_ref[...][None] + lnb_ref[...][None]
            ca_ref[g0:g0 + gc, r0:r0 + rc, :] = (y * jax.nn.sigmoid(y)).astype(ca_ref.dtype)
    if n_tiles > 1:
        cs_ref[:, 0:CONV_PAD, :] = cs_ref[:, TM:TM + CONV_PAD, :]

    xc = z[:, C_XC:C_XC + D_POOL].reshape(G, TM, D_POOL)
    xc_ref[...] = xc
    ps_ref[:, POOL_PAD:POOL_PAD + TM, :] = xc
    RR = POOL_PAD + TM
    p2_ref[:, 1:RR, :] = ps_ref[:, 1:RR, :] + ps_ref[:, 0:RR - 1, :]
    p4_ref[:, 3:RR, :] = p2_ref[:, 3:RR, :] + p2_ref[:, 1:RR - 2, :]
    p8_ref[:, 7:RR, :] = p4_ref[:, 7:RR, :] + p4_ref[:, 3:RR - 4, :]
    p16_ref[:, 15:RR, :] = p8_ref[:, 15:RR, :] + p8_ref[:, 7:RR - 8, :]
    lane = lax.broadcasted_iota(jnp.int32, (G, TM, D_POOL), 2)
    row = lax.broadcasted_iota(jnp.int32, (G, TM, D_POOL), 1)
    g0m, g1m, g2m = lane < POOL_GC, lane < 2 * POOL_GC, lane < 3 * POOL_GC
    s_tok = slice(POOL_PAD, POOL_PAD + TM)
    wsum = jnp.where(g0m, p2_ref[:, s_tok, :],
                     jnp.where(g1m, p4_ref[:, s_tok, :], jnp.where(g2m, p8_ref[:, s_tok, :], p16_ref[:, s_tok, :])))
    win = jnp.where(g0m, POOL_WINDOWS[0], jnp.where(g1m, POOL_WINDOWS[1],
                                                     jnp.where(g2m, POOL_WINDOWS[2], POOL_WINDOWS[3])))
    cnt = jnp.minimum(pos0 + j * TM + row + 1, win).astype(F32)
    d = (wsum / cnt - xc).reshape(R, D_POOL).astype(BF16)
    py = jnp.dot(d, poolw_ref[...], preferred_element_type=F32) * pscale_ref[...]
    py_ref[...] = py.reshape(G, TM, D_POOL).astype(py_ref.dtype)
    if n_tiles > 1:
        ps_ref[:, 0:POOL_PAD, :] = ps_ref[:, TM:TM + POOL_PAD, :]


def _proj(x3, pos0, chist, phist, lw, *, G, TM, adt):
    NS, T, _ = x3.shape
    n_tiles = T // TM
    pos = (pos0 + jnp.arange(T)).astype(F32)
    half = HEAD_DIM // 2
    inv = ROPE_THETA ** (-jnp.arange(half, dtype=F32) / half)
    ang = pos[:, None] * inv[None, :]
    cos = jnp.tile(jnp.cos(ang), (1, 2 * LANES // HEAD_DIM))
    sin = jnp.tile(jnp.concatenate([-jnp.sin(ang), jnp.sin(ang)], axis=1), (1, LANES // HEAD_DIM))

    def tok(c, dt=F32):
        return jax.ShapeDtypeStruct((NS, T, c), dt)

    def tspec(c):
        return pl.BlockSpec((G, TM, c), lambda s, j: (s, j, 0))

    out_cols = [(KV_DIM, F32), (KV_DIM, F32), (IDX_DIM, F32), (D_CONV, F32), (D_POOL, F32),
                (KV_DIM, adt), (KV_DIM, adt), (IDX_DIM, adt), (D_ATTN, adt), (IDX_HEADS * IDX_DIM, adt),
                (LANES, F32), (D_CONV, adt), (D_POOL, adt)]
    kern = functools.partial(_proj_kernel, G=G, TM=TM, pos0=pos0, n_tiles=n_tiles)
    return pl.pallas_call(
        kern,
        grid=(NS // G, n_tiles),
        in_specs=[tspec(D_MODEL), _const_spec((1, D_MODEL)), _const_spec((D_MODEL, D_A)),
                  pl.BlockSpec((TM, LANES), lambda s, j: (j, 0)), pl.BlockSpec((TM, LANES), lambda s, j: (j, 0)),
                  pl.BlockSpec((G, CONV_PAD, D_CONV), lambda s, j: (s, 0, 0)),
                  pl.BlockSpec((G, POOL_PAD, D_POOL), lambda s, j: (s, 0, 0)),
                  _const_spec((CONV_PAD, D_CONV)), _const_spec((1, D_CONV)), _const_spec((1, D_CONV)),
                  _const_spec((1, D_CONV)), _const_spec((D_POOL, D_POOL)), _const_spec((1, D_POOL))],
        out_specs=[tspec(c) for c, _ in out_cols],
        out_shape=[tok(c, dt) for c, dt in out_cols],
        scratch_shapes=[pltpu.VMEM((G, CONV_PAD + TM, D_CONV), F32)] +
                       [pltpu.VMEM((G, POOL_PAD + TM, D_POOL), F32)] * 5,
        compiler_params=_cparams(("arbitrary", "arbitrary")),
        name="proj",
    )(x3, lw["g_mix"], lw["w_a"], cos, sin, chist, phist, lw["conv_w"], lw["conv_b"], lw["ln_g"], lw["ln_b"],
      lw["pool_w"], lw["pool_scale"])


def _kth_bisect(count_ge, smin, smax, nvalid, kk):
    done0 = (nvalid <= kk).astype(F32)
    init = (smin, jnp.full_like(smin, BIG), nvalid, jnp.zeros_like(smin), smax, done0)

    def cond(st):
        return jnp.min(st[5]) < 0.5

    def body(st):
        lo, hi, clo, chi, cand, done = st
        c = count_ge(cand)
        act = done < 0.5
        ge = c >= kk
        up = act & ge
        dn = act & jnp.logical_not(ge)
        lo = jnp.where(up, cand, lo)
        clo = jnp.where(up, c, clo)
        hi = jnp.where(dn, cand, hi)
        chi = jnp.where(dn, c, chi)
        mid = lo * 0.5 + hi * 0.5
        inside = (mid > lo) & (mid < hi)
        fin = (clo == kk) | jnp.logical_not(inside)
        done = jnp.where(fin, 1.0, done)
        return lo, hi, clo, chi, mid, done

    lo, _, clo, chi, _, _ = lax.while_loop(cond, body, init)
    return lo, clo, chi


def _attend_prompt_kernel(qit_ref, wq_ref, qtz_ref, ki_ref, k_ref, vt_ref, o_ref, sc_ref, acc_ref, *, CK, kk):
    i = pl.program_id(1)
    QB = LANES
    nch = (i * QB + QB + CK - 1) // CK
    qit = qit_ref[...]
    wq = wq_ref[...]
    qcol = i * QB + lax.broadcasted_iota(jnp.int32, (CK, QB), 1)
    krow0 = lax.broadcasted_iota(jnp.int32, (CK, QB), 0)

    def score_body(c, carry):
        smax, smin = carry
        s = jnp.dot(ki_ref[c], qit, preferred_element_type=F32)
        s = jnp.maximum(s, 0.0) * wq
        tot = s[:, 0:QB]
        for hh in range(1, IDX_HEADS):
            tot = tot + s[:, hh * QB:(hh + 1) * QB]
        valid = (krow0 + c * CK) <= qcol
        sc_ref[pl.ds(pl.multiple_of(c * CK, CK), CK), :] = jnp.where(valid, tot, NEG_INF)
        smax = jnp.maximum(smax, jnp.max(jnp.where(valid, tot, NEG_INF), axis=0, keepdims=True))
        smin = jnp.minimum(smin, jnp.min(jnp.where(valid, tot, BIG), axis=0, keepdims=True))
        return smax, smin

    smax, smin = lax.fori_loop(0, nch, score_body,
                               (jnp.full((1, QB), NEG_INF, F32), jnp.full((1, QB), BIG, F32)))

    def count_ge(cand):
        def body(c, acc):
            blk = sc_ref[pl.ds(pl.multiple_of(c * CK, CK), CK), :]
            return acc + jnp.sum(jnp.where(blk >= cand, 1.0, 0.0).reshape(CK // 8, 8, QB), axis=0)
        acc = lax.fori_loop(0, nch, body, jnp.zeros((8, QB), F32))
        return jnp.sum(acc, axis=0, keepdims=True)

    nvalid = (i * QB + lax.broadcasted_iota(jnp.int32, (1, QB), 1) + 1).astype(F32)
    lo, clo, chi = _kth_bisect(count_ge, smin, smax, nvalid, float(kk))

    fix = (clo > kk) & (nvalid > kk)

    @pl.when(jnp.max(fix.astype(F32)) > 0.5)
    def _():
        keep = kk - chi

        def body(r, run):
            row = sc_ref[pl.ds(r, 1), :]
            tie = fix & (row == lo)
            run = run + tie.astype(F32)
            sc_ref[pl.ds(r, 1), :] = jnp.where(tie & (run > keep), NEG_INF, row)
            return run
        lax.fori_loop(0, nch * CK, body, jnp.zeros((1, QB), F32))

    qtz = qtz_ref[...]
    acc_ref[...] = jnp.zeros_like(acc_ref)

    def att_body(c, carry):
        ms, ls = carry
        s_all = jnp.dot(k_ref[c], qtz, preferred_element_type=F32)
        sel = sc_ref[pl.ds(pl.multiple_of(c * CK, CK), CK), :] >= lo
        vt = vt_ref[c]
        new_ms, new_ls = [], []
        for hh in range(N_HEADS):
            s = jnp.where(sel, s_all[:, hh * QB:(hh + 1) * QB], NEG_INF)
            mnew = jnp.maximum(ms[hh], jnp.max(s, axis=0, keepdims=True))
            msafe = jnp.where(mnew == NEG_INF, 0.0, mnew)
            p = jnp.exp(s - msafe)
            alpha = jnp.exp(ms[hh] - msafe)
            new_ls.append(alpha * ls[hh] + jnp.sum(p, axis=0, keepdims=True))
            new_ms.append(mnew)
            pv = jnp.dot(vt, p.astype(BF16), preferred_element_type=F32)
            n = hh // GROUP
            rows = slice(hh * HEAD_DIM, (hh + 1) * HEAD_DIM)
            acc_ref[rows, :] = alpha * acc_ref[rows, :] + pv[n * HEAD_DIM:(n + 1) * HEAD_DIM, :]
        return tuple(new_ms), tuple(new_ls)

    m0 = tuple(jnp.full((1, QB), NEG_INF, F32) for _ in range(N_HEADS))
    l0 = tuple(jnp.zeros((1, QB), F32) for _ in range(N_HEADS))
    _, ls = lax.fori_loop(0, nch, att_body, (m0, l0))
    for hh in range(N_HEADS):
        rows = slice(hh * HEAD_DIM, (hh + 1) * HEAD_DIM)
        acc_ref[rows, :] = acc_ref[rows, :] / ls[hh]
    o_ref[...] = acc_ref[...].T.astype(BF16)


def _attend_prompt(q, qi, wi, kbf, vbf, kibf, *, CK):
    B, T, _ = q.shape
    QB = LANES
    nq = T // QB
    nc = T // CK
    kk = min(TOPK_MAX, T // 4)

    def heads_t(a):
        return a.reshape(B, nq, QB, 8, 64).transpose(0, 1, 4, 3, 2).reshape(B, nq, 64, 8 * QB)

    qit = heads_t(qi)
    qt = heads_t(q)
    col_head = jnp.arange(8 * QB) // QB
    qtz = jnp.concatenate([jnp.where((col_head // GROUP == n)[None, None, None, :], qt, jnp.zeros_like(qt))
                           for n in range(N_KV_HEADS)], axis=2)
    wq = wi[:, :, :IDX_HEADS].reshape(B, nq, QB, IDX_HEADS).transpose(0, 1, 3, 2).reshape(B, nq, 1, 8 * QB)
    kib = kibf.reshape(B, nc, CK, IDX_DIM)
    kb = kbf.reshape(B, nc, CK, KV_DIM)
    vtb = vbf.reshape(B, nc, CK, KV_DIM).transpose(0, 1, 3, 2)

    kern = functools.partial(_attend_prompt_kernel, CK=CK, kk=kk)
    return pl.pallas_call(
        kern,
        grid=(B, nq),
        in_specs=[pl.BlockSpec((None, None, IDX_DIM, 8 * QB), lambda b, i: (b, i, 0, 0)),
                  pl.BlockSpec((None, None, 1, 8 * QB), lambda b, i: (b, i, 0, 0)),
                  pl.BlockSpec((None, None, KV_DIM, 8 * QB), lambda b, i: (b, i, 0, 0)),
                  pl.BlockSpec((None, nc, CK, IDX_DIM), lambda b, i: (b, 0, 0, 0)),
                  pl.BlockSpec((None, nc, CK, KV_DIM), lambda b, i: (b, 0, 0, 0)),
                  pl.BlockSpec((None, nc, KV_DIM, CK), lambda b, i: (b, 0, 0, 0))],
        out_specs=pl.BlockSpec((None, QB, D_ATTN), lambda b, i: (b, i, 0)),
        out_shape=jax.ShapeDtypeStruct((B, T, D_ATTN), BF16),
        scratch_shapes=[pltpu.VMEM((T, QB), F32), pltpu.VMEM((D_ATTN, QB), F32)],
        compiler_params=_cparams(("arbitrary", "arbitrary")),
        name="attend_prompt",
    )(qit, wq, qtz, kib, kb, vtb)


def _sample_score_kernel(pt_ref, *refs, P, n_steps, past_len, kk):
    pages = refs[:P]
    qi_ref, wb_ref, kin_ref, sc_ref, thr_ref = refs[P:P + 5]
    j = pl.program_id(1)
    TQ = sc_ref.shape[1]
    NBLK = sc_ref.shape[0]
    qi = qi_ref[...]
    wb = wb_ref[...]

    def block_scores(kpage):
        s = lax.dot_general(qi, kpage.astype(BF16), (((1,), (1,)), ((), ())), preferred_element_type=F32)
        s = jnp.maximum(s, 0.0) * wb
        return jnp.sum(s.reshape(IDX_HEADS, TQ, LANES), axis=0)

    for r in range(P):
        sc_ref[j * P + r] = block_scores(pages[r][...])

    @pl.when(j == n_steps - 1)
    def _():
        tot = block_scores(kin_ref[...])
        qrow = lax.broadcasted_iota(jnp.int32, (TQ, LANES), 0)
        kcol = lax.broadcasted_iota(jnp.int32, (TQ, LANES), 1)
        sc_ref[NBLK - 1] = jnp.where(kcol <= qrow, tot, NEG_INF)

        def lane_bcast(a):
            return jnp.broadcast_to(a, (TQ, LANES))

        sc = sc_ref[...]
        smax = lane_bcast(jnp.max(jnp.max(sc, axis=0), axis=1, keepdims=True))
        smin = lane_bcast(jnp.min(jnp.min(jnp.where(sc > NEG_INF, sc, BIG), axis=0), axis=1, keepdims=True))
        nvalid = (past_len + qrow + 1).astype(F32)

        def count_ge(cand):
            c = jnp.sum(jnp.where(sc_ref[...] >= cand[None], 1.0, 0.0), axis=0)
            return lane_bcast(jnp.sum(c, axis=1, keepdims=True))

        lo, clo, chi = _kth_bisect(count_ge, smin, smax, nvalid, float(kk))
        thr_ref[...] = lo

        fix = (clo > kk) & (nvalid > kk)

        @pl.when(jnp.max(fix.astype(F32)) > 0.5)
        def _():
            keep = kk - chi
            tri = (lax.broadcasted_iota(jnp.int32, (LANES, LANES), 0)
                   <= lax.broadcasted_iota(jnp.int32, (LANES, LANES), 1)).astype(BF16)

            def body(c, run):
                blk = sc_ref[c]
                tie = fix & (blk == lo)
                rank = run + jnp.dot(tie.astype(BF16), tri, preferred_element_type=F32)
                sc_ref[c] = jnp.where(tie & (rank > keep), NEG_INF, blk)
                return run + lane_bcast(jnp.sum(tie.astype(F32), axis=1, keepdims=True))
            lax.fori_loop(0, NBLK, body, jnp.zeros((TQ, LANES), F32))


def _sample_scores(page_table, cache_kidx_l, qi_rows, w_rows, ki_new, *, P):
    NB, n_pages = page_table.shape
    TQ = qi_rows.shape[1] // IDX_HEADS
    n_steps = n_pages // P
    past_len = n_pages * PAGE_SIZE
    kk = min(TOPK_MAX, (past_len + TQ) // 4)
    NBLK = n_pages + 1

    def page_spec(r):
        return pl.BlockSpec((None, PAGE_SIZE, IDX_DIM), lambda b, j, pt: (pt[b, j * P + r], 0, 0))

    kern = functools.partial(_sample_score_kernel, P=P, n_steps=n_steps, past_len=past_len, kk=kk)
    grid_spec = pltpu.PrefetchScalarGridSpec(
        num_scalar_prefetch=1,
        grid=(NB, n_steps),
        in_specs=[page_spec(r) for r in range(P)] + [
            pl.BlockSpec((None, IDX_HEADS * TQ, IDX_DIM), lambda b, j, pt: (b, 0, 0)),
            pl.BlockSpec((None, IDX_HEADS * TQ, LANES), lambda b, j, pt: (b, 0, 0)),
            pl.BlockSpec((None, PAGE_SIZE, IDX_DIM), lambda b, j, pt: (b, 0, 0))],
        out_specs=[pl.BlockSpec((None, NBLK, TQ, LANES), lambda b, j, pt: (b, 0, 0, 0)),
                   pl.BlockSpec((None, TQ, LANES), lambda b, j, pt: (b, 0, 0))],
    )
    return pl.pallas_call(
        kern,
        grid_spec=grid_spec,
        out_shape=[jax.ShapeDtypeStruct((NB, NBLK, TQ, LANES), F32), jax.ShapeDtypeStruct((NB, TQ, LANES), F32)],
        compiler_params=_cparams(("arbitrary", "arbitrary")),
        name="sample_scores",
    )(page_table, *([cache_kidx_l] * P), qi_rows, w_rows, ki_new)


def _sample_attend_kernel(pt_ref, *refs, P, n_steps):
    kpages = refs[:P]
    vpages = refs[P:2 * P]
    sc_ref, scn_ref, thr_ref, q_ref, kn_ref, vn_ref, o_ref, m_ref, l_ref, acc_ref = refs[2 * P:2 * P + 10]
    j = pl.program_id(1)
    qz = q_ref[...]
    thr = thr_ref[...]

    @pl.when(j == 0)
    def _():
        m_ref[...] = jnp.full_like(m_ref, NEG_INF)
        l_ref[...] = jnp.zeros_like(l_ref)
        acc_ref[...] = jnp.zeros_like(acc_ref)

    def update(kblocks, vblocks, scblocks):
        s_parts = []
        for kb, scb in zip(kblocks, scblocks):
            s = lax.dot_general(qz, kb.astype(BF16), (((1,), (1,)), ((), ())), preferred_element_type=F32)
            bias = jnp.where(scb >= thr, 0.0, NEG_INF)
            s_parts.append(s + jnp.concatenate([bias] * N_HEADS, axis=0))
        s = jnp.concatenate(s_parts, axis=1) if len(s_parts) > 1 else s_parts[0]
        m_old = m_ref[...]
        mnew = jnp.maximum(m_old, jnp.broadcast_to(jnp.max(s, axis=1, keepdims=True), m_old.shape))
        msafe = jnp.where(mnew == NEG_INF, 0.0, mnew)
        alpha = jnp.exp(m_old - msafe)
        p = jnp.exp(s - jnp.concatenate([msafe] * len(s_parts), axis=1) if len(s_parts) > 1 else s - msafe)
        l_ref[...] = alpha * l_ref[...] + jnp.broadcast_to(jnp.sum(p, axis=1, keepdims=True), m_old.shape)
        m_ref[...] = mnew
        pv = jnp.zeros(acc_ref.shape, F32)
        for r, vb in enumerate(vblocks):
            pv = pv + jnp.dot(p[:, r * LANES:(r + 1) * LANES].astype(BF16), vb.astype(BF16),
                              preferred_element_type=F32)
        acc_ref[...] = alpha * acc_ref[...] + pv

    update([kp[...] for kp in kpages], [vp[...] for vp in vpages], [sc_ref[r] for r in range(P)])

    @pl.when(j == n_steps - 1)
    def _():
        update([kn_ref[...]], [vn_ref[...]], [scn_ref[0]])
        o_ref[...] = acc_ref[...] / l_ref[...]


def _sample_attend(page_table, cache_k_l, cache_v_l, sc, thr, q_rows, k_new, v_new, *, P):
    NB, n_pages = page_table.shape
    R = q_rows.shape[1]
    TQ = R // N_HEADS
    n_steps = n_pages // P

    def page_spec(r):
        return pl.BlockSpec((None, PAGE_SIZE, KV_DIM), lambda b, j, pt: (pt[b, j * P + r], 0, 0))

    kern = functools.partial(_sample_attend_kernel, P=P, n_steps=n_steps)
    grid_spec = pltpu.PrefetchScalarGridSpec(
        num_scalar_prefetch=1,
        grid=(NB, n_steps),
        in_specs=[page_spec(r) for r in range(P)] + [page_spec(r) for r in range(P)] + [
            pl.BlockSpec((None, P, TQ, LANES), lambda b, j, pt: (b, j, 0, 0)),
            pl.BlockSpec((None, 1, TQ, LANES), lambda b, j, pt: (b, n_pages, 0, 0)),
            pl.BlockSpec((None, TQ, LANES), lambda b, j, pt: (b, 0, 0)),
            pl.BlockSpec((None, R, KV_DIM), lambda b, j, pt: (b, 0, 0)),
            pl.BlockSpec((None, PAGE_SIZE, KV_DIM), lambda b, j, pt: (b, 0, 0)),
            pl.BlockSpec((None, PAGE_SIZE, KV_DIM), lambda b, j, pt: (b, 0, 0))],
        out_specs=pl.BlockSpec((None, R, KV_DIM), lambda b, j, pt: (b, 0, 0)),
        scratch_shapes=[pltpu.VMEM((R, LANES), F32), pltpu.VMEM((R, LANES), F32), pltpu.VMEM((R, KV_DIM), F32)],
    )
    return pl.pallas_call(
        kern,
        grid_spec=grid_spec,
        out_shape=jax.ShapeDtypeStruct((NB, R, KV_DIM), F32),
        compiler_params=_cparams(("arbitrary", "arbitrary")),
        name="sample_attend",
    )(page_table, *([cache_k_l] * P), *([cache_v_l] * P), sc, sc, thr, q_rows, k_new, v_new)


def _mix_out_kernel(x_ref, g_ref, wg_ref, ca_ref, ao_ref, py_ref, wco_ref, wao_ref, wpo_ref, wo_ref, o_ref):
    x = x_ref[...]
    ms = jnp.mean(x * x, axis=-1, keepdims=True)
    h = (x * lax.rsqrt(ms + RMS_EPS) * g_ref[...]).astype(BF16)
    gates = jax.nn.sigmoid(jnp.dot(h, wg_ref[...], preferred_element_type=F32))
    y_a = jnp.dot(ca_ref[...], wco_ref[...], preferred_element_type=F32)
    y_b = jnp.dot(ao_ref[...], wao_ref[...], preferred_element_type=F32)
    y_c = jnp.dot(py_ref[...], wpo_ref[...], preferred_element_type=F32)
    merged = (gates[:, 0:D_MODEL] * y_a + gates[:, D_MODEL:2 * D_MODEL] * y_b
              + gates[:, 2 * D_MODEL:3 * D_MODEL] * y_c)
    o_ref[...] = x + jnp.dot(merged.astype(BF16), wo_ref[...], preferred_element_type=F32)


def _mix_out(x2, ca, ao, py, lw, *, TM):
    N = x2.shape[0]

    def tspec(c):
        return pl.BlockSpec((TM, c), lambda i: (i, 0))

    return pl.pallas_call(
        _mix_out_kernel,
        grid=(N // TM,),
        in_specs=[tspec(D_MODEL), _const_spec((1, D_MODEL)), _const_spec((D_MODEL, N_BRANCH * D_MODEL)),
                  tspec(D_CONV), tspec(D_ATTN), tspec(D_POOL),
                  _const_spec((D_CONV, D_MODEL)), _const_spec((D_ATTN, D_MODEL)), _const_spec((D_POOL, D_MODEL)),
                  _const_spec((D_MODEL, D_MODEL))],
        out_specs=tspec(D_MODEL),
        out_shape=jax.ShapeDtypeStruct((N, D_MODEL), F32),
        compiler_params=_cparams(("arbitrary",)),
        name="mix_out",
    )(x2, lw["g_mix"], lw["w_g"], ca, ao, py, lw["w_conv_out"], lw["w_attn_out"], lw["w_pool_out"], lw["w_out"])


def _ffn_kernel(x_ref, g_ref, w1_ref, w2_ref, gf_ref, o_ref, *, final):
    x = x_ref[...]
    ms = jnp.mean(x * x, axis=-1, keepdims=True)
    h = (x * lax.rsqrt(ms + RMS_EPS) * g_ref[...]).astype(BF16)
    r = jnp.maximum(jnp.dot(h, w1_ref[...], preferred_element_type=F32), 0.0)
    y = x + jnp.dot((r * r).astype(BF16), w2_ref[...], preferred_element_type=F32)
    if final:
        ms2 = jnp.mean(y * y, axis=-1, keepdims=True)
        y = y * lax.rsqrt(ms2 + RMS_EPS) * gf_ref[...]
    o_ref[...] = y


def _ffn(x2, lw, g_final, *, TM, final):
    N = x2.shape[0]
    once = pl.Buffered(1)
    return pl.pallas_call(
        functools.partial(_ffn_kernel, final=final),
        grid=(N // TM,),
        in_specs=[pl.BlockSpec((TM, D_MODEL), lambda i: (i, 0)), _const_spec((1, D_MODEL)),
                  pl.BlockSpec((D_MODEL, D_FF), lambda i: (0, 0), pipeline_mode=once),
                  pl.BlockSpec((D_FF, D_MODEL), lambda i: (0, 0), pipeline_mode=once),
                  _const_spec((1, D_MODEL))],
        out_specs=pl.BlockSpec((TM, D_MODEL), lambda i: (i, 0)),
        out_shape=jax.ShapeDtypeStruct((N, D_MODEL), F32),
        compiler_params=_cparams(("arbitrary",)),
        name="ffn",
    )(x2, lw["g_ffn"], lw["w_ff1"], lw["w_ff2"], g_final)


def _layer_weights(l, g_mix, w_in, conv_w, conv_b, conv_ln_g, conv_ln_b, w_conv_out, w_attn_out,
                   pool_w, pool_scale, w_pool_out, w_out, g_ffn, w_ff1, w_ff2):
    W = w_in[l]
    o_ki = C_QI + IDX_HEADS * IDX_DIM
    o_wi = o_ki + IDX_DIM
    o_xc = o_wi + IDX_HEADS
    o_g = o_xc + D_POOL

    def pad_to_tile(a):
        return jnp.pad(a, ((0, 0), (0, LANES - a.shape[1])))

    w_a = jnp.concatenate([W[:, :o_ki], W[:, o_xc:o_g], pad_to_tile(W[:, o_ki:o_wi]),
                           pad_to_tile(W[:, o_wi:o_xc])], axis=1).astype(BF16)
    pw = jnp.zeros((D_POOL, D_POOL), F32)
    for g in range(len(POOL_WINDOWS)):
        pw = pw.at[g * POOL_GC:(g + 1) * POOL_GC, g * POOL_GC:(g + 1) * POOL_GC].set(pool_w[l, g])
    return dict(
        g_mix=g_mix[l][None], w_a=w_a, w_g=W[:, o_g:].astype(BF16),
        conv_w=jnp.pad(conv_w[l], ((0, CONV_PAD - CONV_W), (0, 0))), conv_b=conv_b[l][None],
        ln_g=conv_ln_g[l][None], ln_b=conv_ln_b[l][None],
        pool_w=pw.astype(BF16), pool_scale=pool_scale[l][None],
        w_conv_out=w_conv_out[l].astype(BF16), w_attn_out=w_attn_out[l].astype(BF16),
        w_pool_out=w_pool_out[l].astype(BF16), w_out=w_out[l].astype(BF16),
        g_ffn=g_ffn[l][None], w_ff1=w_ff1[l].astype(BF16), w_ff2=w_ff2[l].astype(BF16))


def _pad_hist(hist, rows):
    return jnp.pad(hist, ((0, 0), (rows - hist.shape[1], 0), (0, 0)))


def _tail_state(hist, new, keep):
    return jnp.concatenate([hist, new], axis=1)[:, -keep:]


def kernel(x_prompt, x_sample, cache_k, cache_v, cache_kidx, state_conv, state_pool, page_table, g_mix, w_in,
           conv_w, conv_b, conv_ln_g, conv_ln_b, w_conv_out, w_attn_out, pool_w, pool_scale, w_pool_out, w_out,
           g_ffn, w_ff1, w_ff2, g_final):
    B, T, _ = x_prompt.shape
    NB, TS, _ = x_sample.shape
    depth = w_in.shape[0]
    n_pool = cache_k.shape[1]
    past_len = page_table.shape[1] * PAGE_SIZE
    TM_P = 256
    gf = g_final[None]
    xp = x_prompt
    xs = x_sample
    zero_conv = jnp.zeros((B, CONV_PAD, D_CONV), F32)
    zero_pool = jnp.zeros((B, POOL_PAD, D_POOL), F32)
    st_p = [[] for _ in range(5)]
    st_s = [[] for _ in range(5)]
    for l in range(depth):
        lw = _layer_weights(l, g_mix, w_in, conv_w, conv_b, conv_ln_g, conv_ln_b, w_conv_out, w_attn_out,
                            pool_w, pool_scale, w_pool_out, w_out, g_ffn, w_ff1, w_ff2)
        final = l == depth - 1

        (k, v, ki, u, xc, kbf, vbf, kibf, q, qi, wi, ca, py) = _proj(xp, 0, zero_conv, zero_pool, lw, G=1, TM=TM_P,
                                                                      adt=BF16)
        ao = _attend_prompt(q, qi, wi, kbf, vbf, kibf, CK=256)
        x2 = _mix_out(xp.reshape(B * T, D_MODEL), ca.reshape(B * T, D_CONV), ao.reshape(B * T, D_ATTN),
                      py.reshape(B * T, D_POOL), lw, TM=TM_P)
        xp = _ffn(x2, lw, gf, TM=TM_P, final=final).reshape(B, T, D_MODEL)
        st_p[0].append(k.reshape(B, T, N_KV_HEADS, HEAD_DIM))
        st_p[1].append(v.reshape(B, T, N_KV_HEADS, HEAD_DIM))
        st_p[2].append(ki)
        st_p[3].append(u[:, T - CONV_HIST:])
        st_p[4].append(xc[:, T - POOL_HIST:])

        (k, v, ki, u, xc, _, _, _, q, qi, wi, ca, py) = _proj(
            xs, past_len, _pad_hist(state_conv[l], CONV_PAD), _pad_hist(state_pool[l], POOL_PAD), lw, G=NB, TM=TS,
            adt=F32)
        q, qi, ca, py = q.astype(BF16), qi.astype(BF16), ca.astype(BF16), py.astype(BF16)
        qi_rows = qi.reshape(NB, TS, IDX_HEADS, IDX_DIM).transpose(0, 2, 1, 3).reshape(NB, IDX_HEADS * TS, IDX_DIM)
        w_rows = jnp.broadcast_to(wi[:, :, :IDX_HEADS].transpose(0, 2, 1).reshape(NB, IDX_HEADS * TS, 1),
                                  (NB, IDX_HEADS * TS, LANES))
        qh = q.reshape(NB, TS, N_HEADS, HEAD_DIM).transpose(0, 2, 1, 3)
        head_kv = (jnp.arange(N_HEADS) // GROUP)[None, :, None, None]
        q_rows = jnp.concatenate([jnp.where(head_kv == n, qh, jnp.zeros_like(qh)) for n in range(N_KV_HEADS)],
                                 axis=-1).reshape(NB, N_HEADS * TS, KV_DIM)
        pad_rows = ((0, 0), (0, PAGE_SIZE - TS), (0, 0))
        sc, thr = _sample_scores(page_table, cache_kidx[l], qi_rows, w_rows, jnp.pad(ki, pad_rows), P=8)
        o_rows = _sample_attend(page_table, cache_k[l].reshape(n_pool, PAGE_SIZE, KV_DIM),
                                cache_v[l].reshape(n_pool, PAGE_SIZE, KV_DIM), sc, thr, q_rows,
                                jnp.pad(k, pad_rows), jnp.pad(v, pad_rows), P=8)
        o5 = o_rows.reshape(NB, N_KV_HEADS, GROUP, TS, N_KV_HEADS, HEAD_DIM)
        ao = jnp.stack([o5[:, n, :, :, n, :] for n in range(N_KV_HEADS)], axis=1)
        ao = ao.reshape(NB, N_HEADS, TS, HEAD_DIM).transpose(0, 2, 1, 3).reshape(NB * TS, D_ATTN).astype(BF16)
        x2 = _mix_out(xs.reshape(NB * TS, D_MODEL), ca.reshape(NB * TS, D_CONV), ao,
                      py.reshape(NB * TS, D_POOL), lw, TM=NB * TS)
        xs = _ffn(x2, lw, gf, TM=NB * TS, final=final).reshape(NB, TS, D_MODEL)
        st_s[0].append(k.reshape(NB, TS, N_KV_HEADS, HEAD_DIM))
        st_s[1].append(v.reshape(NB, TS, N_KV_HEADS, HEAD_DIM))
        st_s[2].append(ki)
        st_s[3].append(_tail_state(state_conv[l], u, CONV_HIST))
        st_s[4].append(_tail_state(state_pool[l], xc, POOL_HIST))

    sp = [jnp.stack(a, axis=0) for a in st_p]
    ss = [jnp.stack(a, axis=0) for a in st_s]
    return (xp, xs, sp[0], sp[1], sp[2], sp[3], sp[4], ss[0], ss[1], ss[2], ss[3], ss[4])
```

```python
import functools

import jax
import jax.numpy as jnp
from jax import lax
from jax.experimental import pallas as pl
from jax.experimental.pallas import tpu as pltpu

F32 = jnp.float32
BF16 = jnp.bfloat16

D_MODEL = 1024
D_CONV = 256
CONV_W = 31
CONV_HIST = CONV_W - 1
HEAD_DIM = 64
D_ATTN = 512
N_HEADS = 8
N_KV_HEADS = 2
KV_DIM = N_KV_HEADS * HEAD_DIM
GROUP = N_HEADS // N_KV_HEADS
IDX_HEADS = 8
IDX_DIM = 64
TOPK_MAX = 256
D_POOL = 256
POOL_WINDOWS = (2, 4, 8, 16)
POOL_GC = D_POOL // len(POOL_WINDOWS)
POOL_HIST = max(POOL_WINDOWS) - 1
N_BRANCH = 3
D_FF = 4 * D_MODEL
ROPE_THETA = 10000.0
RMS_EPS = 1e-6
LN_EPS = 1e-5
PAGE_SIZE = 128

LANES = 128
CONV_PAD = 32
POOL_PAD = 16
BIG = 3.0e38
NEG_INF = float("-inf")

C_AIN, C_AGATE, C_Q, C_K, C_V, C_QI, C_XC, C_KI, C_WI, D_A = 0, 256, 512, 1024, 1152, 1280, 1792, 2048, 2176, 2304

VMEM_LIMIT = 56 * 1024 * 1024


def _cparams(sem):
    return pltpu.CompilerParams(dimension_semantics=sem, vmem_limit_bytes=VMEM_LIMIT)


def _const_spec(shape):
    n = len(shape)
    return pl.BlockSpec(shape, lambda *_: (0,) * n)


def _proj_kernel(x_ref, g_ref, w_ref, cos_ref, sin_ref, chist_ref, phist_ref, convw_ref, convb_ref,
                 lng_ref, lnb_ref, poolw_ref, pscale_ref,
                 k_ref, v_ref, ki_ref, u_ref, xc_ref, kbf_ref, vbf_ref, kibf_ref, q_ref, qi_ref, wi_ref,
                 ca_ref, py_ref,
                 cs_ref, ps_ref, p2_ref, p4_ref, p8_ref, p16_ref, *, G, TM, pos0, n_tiles):
    j = pl.program_id(1)
    R = G * TM
    x = x_ref[...].reshape(R, D_MODEL)
    ms = jnp.mean(x * x, axis=-1, keepdims=True)
    h = (x * lax.rsqrt(ms + RMS_EPS) * g_ref[...]).astype(BF16)
    z = jnp.dot(h, w_ref[...], preferred_element_type=F32)

    cos = cos_ref[...]
    sin = sin_ref[...]
    if G > 1:
        cos = jnp.concatenate([cos] * G, axis=0)
        sin = jnp.concatenate([sin] * G, axis=0)

    def rope(t):
        n = t.shape[1] // LANES
        c = jnp.concatenate([cos] * n, axis=1) if n > 1 else cos
        s = jnp.concatenate([sin] * n, axis=1) if n > 1 else sin
        lane = lax.broadcasted_iota(jnp.int32, t.shape, 1)
        first = (lane & (HEAD_DIM - 1)) < HEAD_DIM // 2
        sw = jnp.where(first, pltpu.roll(t, t.shape[1] - HEAD_DIM // 2, 1), pltpu.roll(t, HEAD_DIM // 2, 1))
        return t * c + sw * s

    q = rope(z[:, C_Q:C_Q + D_ATTN]) * (HEAD_DIM ** -0.5)
    k = rope(z[:, C_K:C_K + KV_DIM])
    v = z[:, C_V:C_V + KV_DIM]
    qi = rope(z[:, C_QI:C_QI + IDX_HEADS * IDX_DIM])
    ki = rope(z[:, C_KI:C_KI + LANES])[:, :IDX_DIM]
    wi = z[:, C_WI:C_WI + LANES] * (IDX_HEADS ** -0.5 * IDX_DIM ** -0.5)

    k_ref[...] = k.reshape(G, TM, KV_DIM)
    v_ref[...] = v.reshape(G, TM, KV_DIM)
    ki_ref[...] = ki.reshape(G, TM, IDX_DIM)
    kbf_ref[...] = k.reshape(G, TM, KV_DIM).astype(kbf_ref.dtype)
    vbf_ref[...] = v.reshape(G, TM, KV_DIM).astype(vbf_ref.dtype)
    kibf_ref[...] = ki.reshape(G, TM, IDX_DIM).astype(kibf_ref.dtype)
    q_ref[...] = q.reshape(G, TM, D_ATTN).astype(q_ref.dtype)
    qi_ref[...] = qi.reshape(G, TM, IDX_HEADS * IDX_DIM).astype(qi_ref.dtype)
    wi_ref[...] = wi.reshape(G, TM, LANES)

    a_in = z[:, C_AIN:C_AIN + D_CONV]
    a_gate = z[:, C_AGATE:C_AGATE + D_CONV]
    u = (a_in * jax.nn.sigmoid(a_gate)).reshape(G, TM, D_CONV)
    u_ref[...] = u

    @pl.when(j == 0)
    def _():
        cs_ref[:, 0:CONV_PAD, :] = chist_ref[...]
        ps_ref[:, 0:POOL_PAD, :] = phist_ref[...]

    cs_ref[:, CONV_PAD:CONV_PAD + TM, :] = u
    off = CONV_PAD - CONV_HIST
    rc = min(TM, 64)
    gc = min(G, 8)
    for g0 in range(0, G, gc):
        for r0 in range(0, TM, rc):
            acc = jnp.zeros((gc, rc, D_CONV), F32) + convb_ref[...][None]
            for t in range(CONV_W):
                acc = acc + convw_ref[t:t + 1, :][None] * cs_ref[g0:g0 + gc, off + t + r0:off + t + r0 + rc, :]
            mu = jnp.mean(acc, axis=-1, keepdims=True)
            var = jnp.mean(jnp.square(acc - mu), axis=-1, keepdims=True)
            y = (acc - mu) * lax.rsqrt(var + LN_EPS) * lng_ref[...][None] + lnb_ref[...][None]
            ca_ref[g0:g0 + gc, r0:r0 + rc, :] = (y * jax.nn.sigmoid(y)).astype(ca_ref.dtype)
    if n_tiles > 1:
        cs_ref[:, 0:CONV_PAD, :] = cs_ref[:, TM:TM + CONV_PAD, :]

    xc = z[:, C_XC:C_XC + D_POOL].reshape(G, TM, D_POOL)
    xc_ref[...] = xc
    ps_ref[:, POOL_PAD:POOL_PAD + TM, :] = xc
    RR = POOL_PAD + TM
    p2_ref[:, 1:RR, :] = ps_ref[:, 1:RR, :] + ps_ref[:, 0:RR - 1, :]
    p4_ref[:, 3:RR, :] = p2_ref[:, 3:RR, :] + p2_ref[:, 1:RR - 2, :]
    p8_ref[:, 7:RR, :] = p4_ref[:, 7:RR, :] + p4_ref[:, 3:RR - 4, :]
    p16_ref[:, 15:RR, :] = p8_ref[:, 15:RR, :] + p8_ref[:, 7:RR - 8, :]
    lane = lax.broadcasted_iota(jnp.int32, (G, TM, D_POOL), 2)
    row = lax.broadcasted_iota(jnp.int32, (G, TM, D_POOL), 1)
    g0m, g1m, g2m = lane < POOL_GC, lane < 2 * POOL_GC, lane < 3 * POOL_GC
    s_tok = slice(POOL_PAD, POOL_PAD + TM)
    wsum = jnp.where(g0m, p2_ref[:, s_tok, :],
                     jnp.where(g1m, p4_ref[:, s_tok, :], jnp.where(g2m, p8_ref[:, s_tok, :], p16_ref[:, s_tok, :])))
    win = jnp.where(g0m, POOL_WINDOWS[0], jnp.where(g1m, POOL_WINDOWS[1],
                                                     jnp.where(g2m, POOL_WINDOWS[2], POOL_WINDOWS[3])))
    cnt = jnp.minimum(pos0 + j * TM + row + 1, win).astype(F32)
    d = (wsum / cnt - xc).reshape(R, D_POOL).astype(BF16)
    py = jnp.dot(d, poolw_ref[...], preferred_element_type=F32) * pscale_ref[...]
    py_ref[...] = py.reshape(G, TM, D_POOL).astype(py_ref.dtype)
    if n_tiles > 1:
        ps_ref[:, 0:POOL_PAD, :] = ps_ref[:, TM:TM + POOL_PAD, :]


def _proj(x3, pos0, chist, phist, lw, *, G, TM, adt):
    NS, T, _ = x3.shape
    n_tiles = T // TM
    pos = (pos0 + jnp.arange(T)).astype(F32)
    half = HEAD_DIM // 2
    inv = ROPE_THETA ** (-jnp.arange(half, dtype=F32) / half)
    ang = pos[:, None] * inv[None, :]
    cos = jnp.tile(jnp.cos(ang), (1, 2 * LANES // HEAD_DIM))
    sin = jnp.tile(jnp.concatenate([-jnp.sin(ang), jnp.sin(ang)], axis=1), (1, LANES // HEAD_DIM))

    def tok(c, dt=F32):
        return jax.ShapeDtypeStruct((NS, T, c), dt)

    def tspec(c):
        return pl.BlockSpec((G, TM, c), lambda s, j: (s, j, 0))

    out_cols = [(KV_DIM, F32), (KV_DIM, F32), (IDX_DIM, F32), (D_CONV, F32), (D_POOL, F32),
                (KV_DIM, adt), (KV_DIM, adt), (IDX_DIM, adt), (D_ATTN, adt), (IDX_HEADS * IDX_DIM, adt),
                (LANES, F32), (D_CONV, adt), (D_POOL, adt)]
    kern = functools.partial(_proj_kernel, G=G, TM=TM, pos0=pos0, n_tiles=n_tiles)
    return pl.pallas_call(
        kern,
        grid=(NS // G, n_tiles),
        in_specs=[tspec(D_MODEL), _const_spec((1, D_MODEL)), _const_spec((D_MODEL, D_A)),
                  pl.BlockSpec((TM, LANES), lambda s, j: (j, 0)), pl.BlockSpec((TM, LANES), lambda s, j: (j, 0)),
                  pl.BlockSpec((G, CONV_PAD, D_CONV), lambda s, j: (s, 0, 0)),
                  pl.BlockSpec((G, POOL_PAD, D_POOL), lambda s, j: (s, 0, 0)),
                  _const_spec((CONV_PAD, D_CONV)), _const_spec((1, D_CONV)), _const_spec((1, D_CONV)),
                  _const_spec((1, D_CONV)), _const_spec((D_POOL, D_POOL)), _const_spec((1, D_POOL))],
        out_specs=[tspec(c) for c, _ in out_cols],
        out_shape=[tok(c, dt) for c, dt in out_cols],
        scratch_shapes=[pltpu.VMEM((G, CONV_PAD + TM, D_CONV), F32)] +
                       [pltpu.VMEM((G, POOL_PAD + TM, D_POOL), F32)] * 5,
        compiler_params=_cparams(("arbitrary", "arbitrary")),
        name="proj",
    )(x3, lw["g_mix"], lw["w_a"], cos, sin, chist, phist, lw["conv_w"], lw["conv_b"], lw["ln_g"], lw["ln_b"],
      lw["pool_w"], lw["pool_scale"])


def _kth_bisect(count_ge, smin, smax, nvalid, kk, steps):
    done0 = (nvalid <= kk).astype(F32)
    init = (smin, jnp.full_like(smin, BIG), nvalid, jnp.zeros_like(smin), smax, done0)

    def cond(st):
        return jnp.min(st[5]) < 0.5

    def body(st):
        for _ in range(steps):
            st = step(st)
        return st

    def step(st):
        lo, hi, clo, chi, cand, done = st
        c = count_ge(cand)
        act = done < 0.5
        ge = c >= kk
        up = act & ge
        dn = act & jnp.logical_not(ge)
        lo = jnp.where(up, cand, lo)
        clo = jnp.where(up, c, clo)
        hi = jnp.where(dn, cand, hi)
        chi = jnp.where(dn, c, chi)
        mid = lo * 0.5 + hi * 0.5
        inside = (mid > lo) & (mid < hi)
        fin = (clo == kk) | jnp.logical_not(inside)
        done = jnp.where(fin, 1.0, done)
        return lo, hi, clo, chi, mid, done

    lo, _, clo, chi, _, _ = lax.while_loop(cond, body, init)
    return lo, clo, chi


def _attend_prompt_kernel(qit_ref, wq_ref, qtz_ref, ki_ref, k_ref, vt_ref, o_ref, sc_ref, acc_ref, *, CKS, CK, kk):
    i = pl.program_id(1)
    QB = LANES
    nchs = (i * QB + QB + CKS - 1) // CKS
    nch = nchs * (CKS // CK)
    qit = qit_ref[...]
    wq = wq_ref[...]
    qcol = i * QB + lax.broadcasted_iota(jnp.int32, (CKS, QB), 1)
    krow0 = lax.broadcasted_iota(jnp.int32, (CKS, QB), 0)

    def score_body(c, carry):
        smax, smin = carry
        s = jnp.dot(ki_ref[c], qit, preferred_element_type=F32)
        s = jnp.maximum(s, 0.0) * wq
        tot = s[:, 0:QB]
        for hh in range(1, IDX_HEADS):
            tot = tot + s[:, hh * QB:(hh + 1) * QB]
        valid = (krow0 + c * CKS) <= qcol
        masked = jnp.where(valid, tot, NEG_INF)
        sc_ref[pl.ds(pl.multiple_of(c * CKS, CKS), CKS), :] = masked
        smax = jnp.maximum(smax, jnp.max(masked.reshape(CKS // 8, 8, QB), axis=0))
        smin = jnp.minimum(smin, jnp.min(jnp.where(valid, tot, BIG).reshape(CKS // 8, 8, QB), axis=0))
        return smax, smin

    smax, smin = lax.fori_loop(0, nchs, score_body,
                               (jnp.full((8, QB), NEG_INF, F32), jnp.full((8, QB), BIG, F32)))
    smax = jnp.max(smax, axis=0, keepdims=True)
    smin = jnp.min(smin, axis=0, keepdims=True)

    n_acc = 4
    rows = CKS // n_acc

    def count_ge(cand):
        def body(c, accs):
            base = pl.multiple_of(c * CKS, CKS)
            out = []
            for a in range(n_acc):
                blk = sc_ref[pl.ds(base + a * rows, rows), :]
                ones = jnp.where(blk >= cand, 1.0, 0.0).reshape(rows // 8, 8, QB)
                out.append(accs[a] + jnp.sum(ones, axis=0))
            return tuple(out)
        accs = lax.fori_loop(0, nchs, body, tuple(jnp.zeros((8, QB), F32) for _ in range(n_acc)))
        return jnp.sum((accs[0] + accs[1]) + (accs[2] + accs[3]), axis=0, keepdims=True)

    nvalid = (i * QB + lax.broadcasted_iota(jnp.int32, (1, QB), 1) + 1).astype(F32)
    lo, clo, chi = _kth_bisect(count_ge, smin, smax, nvalid, float(kk), steps=4)

    fix = (clo > kk) & (nvalid > kk)

    @pl.when(jnp.max(fix.astype(F32)) > 0.5)
    def _():
        keep = kk - chi

        def body(r, run):
            row = sc_ref[pl.ds(r, 1), :]
            tie = fix & (row == lo)
            run = run + tie.astype(F32)
            sc_ref[pl.ds(r, 1), :] = jnp.where(tie & (run > keep), NEG_INF, row)
            return run
        lax.fori_loop(0, nch * CK, body, jnp.zeros((1, QB), F32))

    qtz = qtz_ref[...]
    acc_ref[...] = jnp.zeros_like(acc_ref)

    def att_body(c, carry):
        ms, ls = carry
        s_all = jnp.dot(k_ref[c], qtz, preferred_element_type=F32)
        sel = sc_ref[pl.ds(pl.multiple_of(c * CK, CK), CK), :] >= lo
        vt = vt_ref[c]
        new_ms, new_ls = [], []
        for hh in range(N_HEADS):
            s = jnp.where(sel, s_all[:, hh * QB:(hh + 1) * QB], NEG_INF)
            mnew = jnp.maximum(ms[hh], jnp.max(s, axis=0, keepdims=True))
            msafe = jnp.where(mnew == NEG_INF, 0.0, mnew)
            p = jnp.exp(s - msafe)
            alpha = jnp.exp(ms[hh] - msafe)
            new_ls.append(alpha * ls[hh] + jnp.sum(p, axis=0, keepdims=True))
            new_ms.append(mnew)
            pv = jnp.dot(vt, p.astype(BF16), preferred_element_type=F32)
            n = hh // GROUP
            rows = slice(hh * HEAD_DIM, (hh + 1) * HEAD_DIM)
            acc_ref[rows, :] = alpha * acc_ref[rows, :] + pv[n * HEAD_DIM:(n + 1) * HEAD_DIM, :]
        return tuple(new_ms), tuple(new_ls)

    m0 = tuple(jnp.full((1, QB), NEG_INF, F32) for _ in range(N_HEADS))
    l0 = tuple(jnp.zeros((1, QB), F32) for _ in range(N_HEADS))
    _, ls = lax.fori_loop(0, nch, att_body, (m0, l0))
    for hh in range(N_HEADS):
        rows = slice(hh * HEAD_DIM, (hh + 1) * HEAD_DIM)
        acc_ref[rows, :] = acc_ref[rows, :] / ls[hh]
    o_ref[...] = acc_ref[...].T.astype(BF16)


def _attend_prompt(q, qi, wi, kbf, vbf, kibf, *, CKS, CK):
    B, T, _ = q.shape
    QB = LANES
    nq = T // QB
    nc = T // CK
    ncs = T // CKS
    kk = min(TOPK_MAX, T // 4)

    def heads_t(a):
        return a.reshape(B, nq, QB, 8, 64).transpose(0, 1, 4, 3, 2).reshape(B, nq, 64, 8 * QB)

    qit = heads_t(qi)
    qt = heads_t(q)
    col_head = jnp.arange(8 * QB) // QB
    qtz = jnp.concatenate([jnp.where((col_head // GROUP == n)[None, None, None, :], qt, jnp.zeros_like(qt))
                           for n in range(N_KV_HEADS)], axis=2)
    wq = wi[:, :, :IDX_HEADS].reshape(B, nq, QB, IDX_HEADS).transpose(0, 1, 3, 2).reshape(B, nq, 1, 8 * QB)
    kib = kibf.reshape(B, ncs, CKS, IDX_DIM)
    kb = kbf.reshape(B, nc, CK, KV_DIM)
    vtb = vbf.reshape(B, nc, CK, KV_DIM).transpose(0, 1, 3, 2)

    kern = functools.partial(_attend_prompt_kernel, CKS=CKS, CK=CK, kk=kk)
    return pl.pallas_call(
        kern,
        grid=(B, nq),
        in_specs=[pl.BlockSpec((None, None, IDX_DIM, 8 * QB), lambda b, i: (b, i, 0, 0)),
                  pl.BlockSpec((None, None, 1, 8 * QB), lambda b, i: (b, i, 0, 0)),
                  pl.BlockSpec((None, None, KV_DIM, 8 * QB), lambda b, i: (b, i, 0, 0)),
                  pl.BlockSpec((None, ncs, CKS, IDX_DIM), lambda b, i: (b, 0, 0, 0)),
                  pl.BlockSpec((None, nc, CK, KV_DIM), lambda b, i: (b, 0, 0, 0)),
                  pl.BlockSpec((None, nc, KV_DIM, CK), lambda b, i: (b, 0, 0, 0))],
        out_specs=pl.BlockSpec((None, QB, D_ATTN), lambda b, i: (b, i, 0)),
        out_shape=jax.ShapeDtypeStruct((B, T, D_ATTN), BF16),
        scratch_shapes=[pltpu.VMEM((T, QB), F32), pltpu.VMEM((D_ATTN, QB), F32)],
        compiler_params=_cparams(("arbitrary", "arbitrary")),
        name="attend_prompt",
    )(qit, wq, qtz, kib, kb, vtb)


def _sample_score_kernel(pt_ref, *refs, P, n_steps, past_len, kk):
    pages = refs[:P]
    qi_ref, wb_ref, kin_ref, sc_ref, thr_ref = refs[P:P + 5]
    j = pl.program_id(1)
    TQ = sc_ref.shape[1]
    NBLK = sc_ref.shape[0]
    qi = qi_ref[...]
    wb = wb_ref[...]

    def block_scores(kpage):
        s = lax.dot_general(qi, kpage.astype(BF16), (((1,), (1,)), ((), ())), preferred_element_type=F32)
        s = jnp.maximum(s, 0.0) * wb
        return jnp.sum(s.reshape(IDX_HEADS, TQ, LANES), axis=0)

    for r in range(P):
        sc_ref[j * P + r] = block_scores(pages[r][...])

    @pl.when(j == n_steps - 1)
    def _():
        tot = block_scores(kin_ref[...])
        qrow = lax.broadcasted_iota(jnp.int32, (TQ, LANES), 0)
        kcol = lax.broadcasted_iota(jnp.int32, (TQ, LANES), 1)
        sc_ref[NBLK - 1] = jnp.where(kcol <= qrow, tot, NEG_INF)

        def lane_bcast(a):
            return jnp.broadcast_to(a, (TQ, LANES))

        sc = sc_ref[...]
        smax = lane_bcast(jnp.max(jnp.max(sc, axis=0), axis=1, keepdims=True))
        smin = lane_bcast(jnp.min(jnp.min(jnp.where(sc > NEG_INF, sc, BIG), axis=0), axis=1, keepdims=True))
        nvalid = (past_len + qrow + 1).astype(F32)

        def count_ge(cand):
            c = jnp.sum(jnp.where(sc_ref[...] >= cand[None], 1.0, 0.0), axis=0)
            return lane_bcast(jnp.sum(c, axis=1, keepdims=True))

        lo, clo, chi = _kth_bisect(count_ge, smin, smax, nvalid, float(kk), steps=4)
        thr_ref[...] = lo

        fix = (clo > kk) & (nvalid > kk)

        @pl.when(jnp.max(fix.astype(F32)) > 0.5)
        def _():
            keep = kk - chi
            tri = (lax.broadcasted_iota(jnp.int32, (LANES, LANES), 0)
                   <= lax.broadcasted_iota(jnp.int32, (LANES, LANES), 1)).astype(BF16)

            def body(c, run):
                blk = sc_ref[c]
                tie = fix & (blk == lo)
                rank = run + jnp.dot(tie.astype(BF16), tri, preferred_element_type=F32)
                sc_ref[c] = jnp.where(tie & (rank > keep), NEG_INF, blk)
                return run + lane_bcast(jnp.sum(tie.astype(F32), axis=1, keepdims=True))
            lax.fori_loop(0, NBLK, body, jnp.zeros((TQ, LANES), F32))


def _sample_scores(page_table, cache_kidx_l, qi_rows, w_rows, ki_new, *, P):
    NB, n_pages = page_table.shape
    TQ = qi_rows.shape[1] // IDX_HEADS
    n_steps = n_pages // P
    past_len = n_pages * PAGE_SIZE
    kk = min(TOPK_MAX, (past_len + TQ) // 4)
    NBLK = n_pages + 1

    def page_spec(r):
        return pl.BlockSpec((None, PAGE_SIZE, IDX_DIM), lambda b, j, pt: (pt[b, j * P + r], 0, 0))

    kern = functools.partial(_sample_score_kernel, P=P, n_steps=n_steps, past_len=past_len, kk=kk)
    grid_spec = pltpu.PrefetchScalarGridSpec(
        num_scalar_prefetch=1,
        grid=(NB, n_steps),
        in_specs=[page_spec(r) for r in range(P)] + [
            pl.BlockSpec((None, IDX_HEADS * TQ, IDX_DIM), lambda b, j, pt: (b, 0, 0)),
            pl.BlockSpec((None, IDX_HEADS * TQ, LANES), lambda b, j, pt: (b, 0, 0)),
            pl.BlockSpec((None, PAGE_SIZE, IDX_DIM), lambda b, j, pt: (b, 0, 0))],
        out_specs=[pl.BlockSpec((None, NBLK, TQ, LANES), lambda b, j, pt: (b, 0, 0, 0)),
                   pl.BlockSpec((None, TQ, LANES), lambda b, j, pt: (b, 0, 0))],
    )
    return pl.pallas_call(
        kern,
        grid_spec=grid_spec,
        out_shape=[jax.ShapeDtypeStruct((NB, NBLK, TQ, LANES), F32), jax.ShapeDtypeStruct((NB, TQ, LANES), F32)],
        compiler_params=_cparams(("arbitrary", "arbitrary")),
        name="sample_scores",
    )(page_table, *([cache_kidx_l] * P), qi_rows, w_rows, ki_new)


def _sample_attend_kernel(pt_ref, *refs, P, n_steps):
    kpages = refs[:P]
    vpages = refs[P:2 * P]
    sc_ref, scn_ref, thr_ref, q_ref, kn_ref, vn_ref, o_ref, m_ref, l_ref, acc_ref = refs[2 * P:2 * P + 10]
    j = pl.program_id(1)
    qz = q_ref[...]
    thr = thr_ref[...]

    @pl.when(j == 0)
    def _():
        m_ref[...] = jnp.full_like(m_ref, NEG_INF)
        l_ref[...] = jnp.zeros_like(l_ref)
        acc_ref[...] = jnp.zeros_like(acc_ref)

    def update(kblocks, vblocks, scblocks):
        s_parts = []
        for kb, scb in zip(kblocks, scblocks):
            s = lax.dot_general(qz, kb.astype(BF16), (((1,), (1,)), ((), ())), preferred_element_type=F32)
            bias = jnp.where(scb >= thr, 0.0, NEG_INF)
            s_parts.append(s + jnp.concatenate([bias] * N_HEADS, axis=0))
        s = jnp.concatenate(s_parts, axis=1) if len(s_parts) > 1 else s_parts[0]
        m_old = m_ref[...]
        mnew = jnp.maximum(m_old, jnp.broadcast_to(jnp.max(s, axis=1, keepdims=True), m_old.shape))
        msafe = jnp.where(mnew == NEG_INF, 0.0, mnew)
        alpha = jnp.exp(m_old - msafe)
        p = jnp.exp(s - jnp.concatenate([msafe] * len(s_parts), axis=1) if len(s_parts) > 1 else s - msafe)
        l_ref[...] = alpha * l_ref[...] + jnp.broadcast_to(jnp.sum(p, axis=1, keepdims=True), m_old.shape)
        m_ref[...] = mnew
        pv = jnp.zeros(acc_ref.shape, F32)
        for r, vb in enumerate(vblocks):
            pv = pv + jnp.dot(p[:, r * LANES:(r + 1) * LANES].astype(BF16), vb.astype(BF16),
                              preferred_element_type=F32)
        acc_ref[...] = alpha * acc_ref[...] + pv

    update([kp[...] for kp in kpages], [vp[...] for vp in vpages], [sc_ref[r] for r in range(P)])

    @pl.when(j == n_steps - 1)
    def _():
        update([kn_ref[...]], [vn_ref[...]], [scn_ref[0]])
        o_ref[...] = acc_ref[...] / l_ref[...]


def _sample_attend(page_table, cache_k_l, cache_v_l, sc, thr, q_rows, k_new, v_new, *, P):
    NB, n_pages = page_table.shape
    R = q_rows.shape[1]
    TQ = R // N_HEADS
    n_steps = n_pages // P

    def page_spec(r):
        return pl.BlockSpec((None, PAGE_SIZE, KV_DIM), lambda b, j, pt: (pt[b, j * P + r], 0, 0))

    kern = functools.partial(_sample_attend_kernel, P=P, n_steps=n_steps)
    grid_spec = pltpu.PrefetchScalarGridSpec(
        num_scalar_prefetch=1,
        grid=(NB, n_steps),
        in_specs=[page_spec(r) for r in range(P)] + [page_spec(r) for r in range(P)] + [
            pl.BlockSpec((None, P, TQ, LANES), lambda b, j, pt: (b, j, 0, 0)),
            pl.BlockSpec((None, 1, TQ, LANES), lambda b, j, pt: (b, n_pages, 0, 0)),
            pl.BlockSpec((None, TQ, LANES), lambda b, j, pt: (b, 0, 0)),
            pl.BlockSpec((None, R, KV_DIM), lambda b, j, pt: (b, 0, 0)),
            pl.BlockSpec((None, PAGE_SIZE, KV_DIM), lambda b, j, pt: (b, 0, 0)),
            pl.BlockSpec((None, PAGE_SIZE, KV_DIM), lambda b, j, pt: (b, 0, 0))],
        out_specs=pl.BlockSpec((None, R, KV_DIM), lambda b, j, pt: (b, 0, 0)),
        scratch_shapes=[pltpu.VMEM((R, LANES), F32), pltpu.VMEM((R, LANES), F32), pltpu.VMEM((R, KV_DIM), F32)],
    )
    return pl.pallas_call(
        kern,
        grid_spec=grid_spec,
        out_shape=jax.ShapeDtypeStruct((NB, R, KV_DIM), F32),
        compiler_params=_cparams(("arbitrary", "arbitrary")),
        name="sample_attend",
    )(page_table, *([cache_k_l] * P), *([cache_v_l] * P), sc, sc, thr, q_rows, k_new, v_new)


def _mix_out_kernel(x_ref, g_ref, wg_ref, ca_ref, ao_ref, py_ref, wco_ref, wao_ref, wpo_ref, wo_ref, o_ref):
    x = x_ref[...]
    ms = jnp.mean(x * x, axis=-1, keepdims=True)
    h = (x * lax.rsqrt(ms + RMS_EPS) * g_ref[...]).astype(BF16)
    gates = jax.nn.sigmoid(jnp.dot(h, wg_ref[...], preferred_element_type=F32))
    y_a = jnp.dot(ca_ref[...], wco_ref[...], preferred_element_type=F32)
    y_b = jnp.dot(ao_ref[...], wao_ref[...], preferred_element_type=F32)
    y_c = jnp.dot(py_ref[...], wpo_ref[...], preferred_element_type=F32)
    merged = (gates[:, 0:D_MODEL] * y_a + gates[:, D_MODEL:2 * D_MODEL] * y_b
              + gates[:, 2 * D_MODEL:3 * D_MODEL] * y_c)
    o_ref[...] = x + jnp.dot(merged.astype(BF16), wo_ref[...], preferred_element_type=F32)


def _mix_out(x2, ca, ao, py, lw, *, TM):
    N = x2.shape[0]

    def tspec(c):
        return pl.BlockSpec((TM, c), lambda i: (i, 0))

    return pl.pallas_call(
        _mix_out_kernel,
        grid=(N // TM,),
        in_specs=[tspec(D_MODEL), _const_spec((1, D_MODEL)), _const_spec((D_MODEL, N_BRANCH * D_MODEL)),
                  tspec(D_CONV), tspec(D_ATTN), tspec(D_POOL),
                  _const_spec((D_CONV, D_MODEL)), _const_spec((D_ATTN, D_MODEL)), _const_spec((D_POOL, D_MODEL)),
                  _const_spec((D_MODEL, D_MODEL))],
        out_specs=tspec(D_MODEL),
        out_shape=jax.ShapeDtypeStruct((N, D_MODEL), F32),
        compiler_params=_cparams(("arbitrary",)),
        name="mix_out",
    )(x2, lw["g_mix"], lw["w_g"], ca, ao, py, lw["w_conv_out"], lw["w_attn_out"], lw["w_pool_out"], lw["w_out"])


def _ffn_kernel(x_ref, g_ref, w1_ref, w2_ref, gf_ref, o_ref, *, final):
    x = x_ref[...]
    ms = jnp.mean(x * x, axis=-1, keepdims=True)
    h = (x * lax.rsqrt(ms + RMS_EPS) * g_ref[...]).astype(BF16)
    r = jnp.maximum(jnp.dot(h, w1_ref[...], preferred_element_type=F32), 0.0)
    y = x + jnp.dot((r * r).astype(BF16), w2_ref[...], preferred_element_type=F32)
    if final:
        ms2 = jnp.mean(y * y, axis=-1, keepdims=True)
        y = y * lax.rsqrt(ms2 + RMS_EPS) * gf_ref[...]
    o_ref[...] = y


def _ffn(x2, lw, g_final, *, TM, final):
    N = x2.shape[0]
    once = pl.Buffered(1)
    return pl.pallas_call(
        functools.partial(_ffn_kernel, final=final),
        grid=(N // TM,),
        in_specs=[pl.BlockSpec((TM, D_MODEL), lambda i: (i, 0)), _const_spec((1, D_MODEL)),
                  pl.BlockSpec((D_MODEL, D_FF), lambda i: (0, 0), pipeline_mode=once),
                  pl.BlockSpec((D_FF, D_MODEL), lambda i: (0, 0), pipeline_mode=once),
                  _const_spec((1, D_MODEL))],
        out_specs=pl.BlockSpec((TM, D_MODEL), lambda i: (i, 0)),
        out_shape=jax.ShapeDtypeStruct((N, D_MODEL), F32),
        compiler_params=_cparams(("arbitrary",)),
        name="ffn",
    )(x2, lw["g_ffn"], lw["w_ff1"], lw["w_ff2"], g_final)


def _layer_weights(l, g_mix, w_in, conv_w, conv_b, conv_ln_g, conv_ln_b, w_conv_out, w_attn_out,
                   pool_w, pool_scale, w_pool_out, w_out, g_ffn, w_ff1, w_ff2):
    W = w_in[l]
    o_ki = C_QI + IDX_HEADS * IDX_DIM
    o_wi = o_ki + IDX_DIM
    o_xc = o_wi + IDX_HEADS
    o_g = o_xc + D_POOL

    def pad_to_tile(a):
        return jnp.pad(a, ((0, 0), (0, LANES - a.shape[1])))

    w_a = jnp.concatenate([W[:, :o_ki], W[:, o_xc:o_g], pad_to_tile(W[:, o_ki:o_wi]),
                           pad_to_tile(W[:, o_wi:o_xc])], axis=1).astype(BF16)
    pw = jnp.zeros((D_POOL, D_POOL), F32)
    for g in range(len(POOL_WINDOWS)):
        pw = pw.at[g * POOL_GC:(g + 1) * POOL_GC, g * POOL_GC:(g + 1) * POOL_GC].set(pool_w[l, g])
    return dict(
        g_mix=g_mix[l][None], w_a=w_a, w_g=W[:, o_g:].astype(BF16),
        conv_w=jnp.pad(conv_w[l], ((0, CONV_PAD - CONV_W), (0, 0))), conv_b=conv_b[l][None],
        ln_g=conv_ln_g[l][None], ln_b=conv_ln_b[l][None],
        pool_w=pw.astype(BF16), pool_scale=pool_scale[l][None],
        w_conv_out=w_conv_out[l].astype(BF16), w_attn_out=w_attn_out[l].astype(BF16),
        w_pool_out=w_pool_out[l].astype(BF16), w_out=w_out[l].astype(BF16),
        g_ffn=g_ffn[l][None], w_ff1=w_ff1[l].astype(BF16), w_ff2=w_ff2[l].astype(BF16))


def _pad_hist(hist, rows):
    return jnp.pad(hist, ((0, 0), (rows - hist.shape[1], 0), (0, 0)))


def _tail_state(hist, new, keep):
    return jnp.concatenate([hist, new], axis=1)[:, -keep:]


def kernel(x_prompt, x_sample, cache_k, cache_v, cache_kidx, state_conv, state_pool, page_table, g_mix, w_in,
           conv_w, conv_b, conv_ln_g, conv_ln_b, w_conv_out, w_attn_out, pool_w, pool_scale, w_pool_out, w_out,
           g_ffn, w_ff1, w_ff2, g_final):
    B, T, _ = x_prompt.shape
    NB, TS, _ = x_sample.shape
    depth = w_in.shape[0]
    n_pool = cache_k.shape[1]
    past_len = page_table.shape[1] * PAGE_SIZE
    TM_P = 256
    gf = g_final[None]
    xp = x_prompt
    xs = x_sample
    zero_conv = jnp.zeros((B, CONV_PAD, D_CONV), F32)
    zero_pool = jnp.zeros((B, POOL_PAD, D_POOL), F32)
    st_p = [[] for _ in range(5)]
    st_s = [[] for _ in range(5)]
    for l in range(depth):
        lw = _layer_weights(l, g_mix, w_in, conv_w, conv_b, conv_ln_g, conv_ln_b, w_conv_out, w_attn_out,
                            pool_w, pool_scale, w_pool_out, w_out, g_ffn, w_ff1, w_ff2)
        final = l == depth - 1

        (k, v, ki, u, xc, kbf, vbf, kibf, q, qi, wi, ca, py) = _proj(xp, 0, zero_conv, zero_pool, lw, G=1, TM=TM_P,
                                                                      adt=BF16)
        ao = _attend_prompt(q, qi, wi, kbf, vbf, kibf, CKS=min(512, T), CK=256)
        x2 = _mix_out(xp.reshape(B * T, D_MODEL), ca.reshape(B * T, D_CONV), ao.reshape(B * T, D_ATTN),
                      py.reshape(B * T, D_POOL), lw, TM=TM_P)
        xp = _ffn(x2, lw, gf, TM=TM_P, final=final).reshape(B, T, D_MODEL)
        st_p[0].append(k.reshape(B, T, N_KV_HEADS, HEAD_DIM))
        st_p[1].append(v.reshape(B, T, N_KV_HEADS, HEAD_DIM))
        st_p[2].append(ki)
        st_p[3].append(u[:, T - CONV_HIST:])
        st_p[4].append(xc[:, T - POOL_HIST:])

        (k, v, ki, u, xc, _, _, _, q, qi, wi, ca, py) = _proj(
            xs, past_len, _pad_hist(state_conv[l], CONV_PAD), _pad_hist(state_pool[l], POOL_PAD), lw, G=NB, TM=TS,
            adt=F32)
        q, qi, ca, py = q.astype(BF16), qi.astype(BF16), ca.astype(BF16), py.astype(BF16)
        qi_rows = qi.reshape(NB, TS, IDX_HEADS, IDX_DIM).transpose(0, 2, 1, 3).reshape(NB, IDX_HEADS * TS, IDX_DIM)
        w_rows = jnp.broadcast_to(wi[:, :, :IDX_HEADS].transpose(0, 2, 1).reshape(NB, IDX_HEADS * TS, 1),
                                  (NB, IDX_HEADS * TS, LANES))
        qh = q.reshape(NB, TS, N_HEADS, HEAD_DIM).transpose(0, 2, 1, 3)
        head_kv = (jnp.arange(N_HEADS) // GROUP)[None, :, None, None]
        q_rows = jnp.concatenate([jnp.where(head_kv == n, qh, jnp.zeros_like(qh)) for n in range(N_KV_HEADS)],
                                 axis=-1).reshape(NB, N_HEADS * TS, KV_DIM)
        pad_rows = ((0, 0), (0, PAGE_SIZE - TS), (0, 0))
        pages_per_step = min(16, page_table.shape[1])
        sc, thr = _sample_scores(page_table, cache_kidx[l].astype(BF16), qi_rows, w_rows, jnp.pad(ki, pad_rows),
                                 P=pages_per_step)
        o_rows = _sample_attend(page_table, cache_k[l].reshape(n_pool, PAGE_SIZE, KV_DIM).astype(BF16),
                                cache_v[l].reshape(n_pool, PAGE_SIZE, KV_DIM).astype(BF16), sc, thr, q_rows,
                                jnp.pad(k, pad_rows), jnp.pad(v, pad_rows), P=pages_per_step)
        o5 = o_rows.reshape(NB, N_KV_HEADS, GROUP, TS, N_KV_HEADS, HEAD_DIM)
        ao = jnp.stack([o5[:, n, :, :, n, :] for n in range(N_KV_HEADS)], axis=1)
        ao = ao.reshape(NB, N_HEADS, TS, HEAD_DIM).transpose(0, 2, 1, 3).reshape(NB * TS, D_ATTN).astype(BF16)
        x2 = _mix_out(xs.reshape(NB * TS, D_MODEL), ca.reshape(NB * TS, D_CONV), ao,
                      py.reshape(NB * TS, D_POOL), lw, TM=NB * TS)
        xs = _ffn(x2, lw, gf, TM=NB * TS, final=final).reshape(NB, TS, D_MODEL)
        st_s[0].append(k.reshape(NB, TS, N_KV_HEADS, HEAD_DIM))
        st_s[1].append(v.reshape(NB, TS, N_KV_HEADS, HEAD_DIM))
        st_s[2].append(ki)
        st_s[3].append(_tail_state(state_conv[l], u, CONV_HIST))
        st_s[4].append(_tail_state(state_pool[l], xc, POOL_HIST))

    sp = [jnp.stack(a, axis=0) for a in st_p]
    ss = [jnp.stack(a, axis=0) for a in st_s]
    return (xp, xs, sp[0], sp[1], sp[2], sp[3], sp[4], ss[0], ss[1], ss[2], ss[3], ss[4])
```

```python
import functools

import jax
import jax.numpy as jnp
from jax import lax
from jax.experimental import pallas as pl
from jax.experimental.pallas import tpu as pltpu

F32 = jnp.float32
BF16 = jnp.bfloat16

D_MODEL = 1024
D_CONV = 256
CONV_W = 31
CONV_HIST = CONV_W - 1
HEAD_DIM = 64
D_ATTN = 512
N_HEADS = 8
N_KV_HEADS = 2
KV_DIM = N_KV_HEADS * HEAD_DIM
GROUP = N_HEADS // N_KV_HEADS
IDX_HEADS = 8
IDX_DIM = 64
TOPK_MAX = 256
D_POOL = 256
POOL_WINDOWS = (2, 4, 8, 16)
POOL_GC = D_POOL // len(POOL_WINDOWS)
POOL_HIST = max(POOL_WINDOWS) - 1
N_BRANCH = 3
D_FF = 4 * D_MODEL
ROPE_THETA = 10000.0
RMS_EPS = 1e-6
LN_EPS = 1e-5
PAGE_SIZE = 128

LANES = 128
CONV_PAD = 32
POOL_PAD = 16
BIG = 3.0e38
NEG_INF = float("-inf")

C_AIN, C_AGATE, C_Q, C_K, C_V, C_QI, C_XC, C_KI, C_WI, D_A = 0, 256, 512, 1024, 1152, 1280, 1792, 2048, 2176, 2304

VMEM_LIMIT = 56 * 1024 * 1024


def _cparams(sem):
    return pltpu.CompilerParams(dimension_semantics=sem, vmem_limit_bytes=VMEM_LIMIT)


def _const_spec(shape):
    n = len(shape)
    return pl.BlockSpec(shape, lambda *_: (0,) * n)


def _proj_kernel(x_ref, g_ref, w_ref, cos_ref, sin_ref, chist_ref, phist_ref, convw_ref, convb_ref,
                 lng_ref, lnb_ref, poolw_ref, pscale_ref,
                 k_ref, v_ref, ki_ref, u_ref, xc_ref, kbf_ref, vbf_ref, kibf_ref, q_ref, qi_ref, wi_ref,
                 ca_ref, py_ref,
                 cs_ref, ps_ref, p2_ref, p4_ref, p8_ref, p16_ref, *, G, TM, pos0, n_tiles):
    j = pl.program_id(1)
    R = G * TM
    x = x_ref[...].reshape(R, D_MODEL)
    ms = jnp.mean(x * x, axis=-1, keepdims=True)
    h = (x * lax.rsqrt(ms + RMS_EPS) * g_ref[...]).astype(BF16)
    z = jnp.dot(h, w_ref[...], preferred_element_type=F32)

    cos = cos_ref[...]
    sin = sin_ref[...]
    if G > 1:
        cos = jnp.concatenate([cos] * G, axis=0)
        sin = jnp.concatenate([sin] * G, axis=0)

    def rope(t):
        n = t.shape[1] // LANES
        c = jnp.concatenate([cos] * n, axis=1) if n > 1 else cos
        s = jnp.concatenate([sin] * n, axis=1) if n > 1 else sin
        lane = lax.broadcasted_iota(jnp.int32, t.shape, 1)
        first = (lane & (HEAD_DIM - 1)) < HEAD_DIM // 2
        sw = jnp.where(first, pltpu.roll(t, t.shape[1] - HEAD_DIM // 2, 1), pltpu.roll(t, HEAD_DIM // 2, 1))
        return t * c + sw * s

    q = rope(z[:, C_Q:C_Q + D_ATTN]) * (HEAD_DIM ** -0.5)
    k = rope(z[:, C_K:C_K + KV_DIM])
    v = z[:, C_V:C_V + KV_DIM]
    qi = rope(z[:, C_QI:C_QI + IDX_HEADS * IDX_DIM])
    ki = rope(z[:, C_KI:C_KI + LANES])[:, :IDX_DIM]
    wi = z[:, C_WI:C_WI + LANES] * (IDX_HEADS ** -0.5 * IDX_DIM ** -0.5)

    k_ref[...] = k.reshape(G, TM, KV_DIM)
    v_ref[...] = v.reshape(G, TM, KV_DIM)
    ki_ref[...] = ki.reshape(G, TM, IDX_DIM)
    kbf_ref[...] = k.reshape(G, TM, KV_DIM).astype(kbf_ref.dtype)
    vbf_ref[...] = v.reshape(G, TM, KV_DIM).astype(vbf_ref.dtype)
    kibf_ref[...] = ki.reshape(G, TM, IDX_DIM).astype(kibf_ref.dtype)
    q_ref[...] = q.reshape(G, TM, D_ATTN).astype(q_ref.dtype)
    qi_ref[...] = qi.reshape(G, TM, IDX_HEADS * IDX_DIM).astype(qi_ref.dtype)
    wi_ref[...] = wi.reshape(G, TM, LANES)

    a_in = z[:, C_AIN:C_AIN + D_CONV]
    a_gate = z[:, C_AGATE:C_AGATE + D_CONV]
    u = (a_in * jax.nn.sigmoid(a_gate)).reshape(G, TM, D_CONV)
    u_ref[...] = u

    @pl.when(j == 0)
    def _():
        cs_ref[:, 0:CONV_PAD, :] = chist_ref[...]
        ps_ref[:, 0:POOL_PAD, :] = phist_ref[...]

    cs_ref[:, CONV_PAD:CONV_PAD + TM, :] = u
    off = CONV_PAD - CONV_HIST
    rc = min(TM, 64)
    gc = min(G, 8)
    for g0 in range(0, G, gc):
        for r0 in range(0, TM, rc):
            acc = jnp.zeros((gc, rc, D_CONV), F32) + convb_ref[...][None]
            for t in range(CONV_W):
                acc = acc + convw_ref[t:t + 1, :][None] * cs_ref[g0:g0 + gc, off + t + r0:off + t + r0 + rc, :]
            mu = jnp.mean(acc, axis=-1, keepdims=True)
            var = jnp.mean(jnp.square(acc - mu), axis=-1, keepdims=True)
            y = (acc - mu) * lax.rsqrt(var + LN_EPS) * lng_ref[...][None] + lnb_ref[...][None]
            ca_ref[g0:g0 + gc, r0:r0 + rc, :] = (y * jax.nn.sigmoid(y)).astype(ca_ref.dtype)
    if n_tiles > 1:
        cs_ref[:, 0:CONV_PAD, :] = cs_ref[:, TM:TM + CONV_PAD, :]

    xc = z[:, C_XC:C_XC + D_POOL].reshape(G, TM, D_POOL)
    xc_ref[...] = xc
    ps_ref[:, POOL_PAD:POOL_PAD + TM, :] = xc
    RR = POOL_PAD + TM
    p2_ref[:, 1:RR, :] = ps_ref[:, 1:RR, :] + ps_ref[:, 0:RR - 1, :]
    p4_ref[:, 3:RR, :] = p2_ref[:, 3:RR, :] + p2_ref[:, 1:RR - 2, :]
    p8_ref[:, 7:RR, :] = p4_ref[:, 7:RR, :] + p4_ref[:, 3:RR - 4, :]
    p16_ref[:, 15:RR, :] = p8_ref[:, 15:RR, :] + p8_ref[:, 7:RR - 8, :]
    lane = lax.broadcasted_iota(jnp.int32, (G, TM, D_POOL), 2)
    row = lax.broadcasted_iota(jnp.int32, (G, TM, D_POOL), 1)
    g0m, g1m, g2m = lane < POOL_GC, lane < 2 * POOL_GC, lane < 3 * POOL_GC
    s_tok = slice(POOL_PAD, POOL_PAD + TM)
    wsum = jnp.where(g0m, p2_ref[:, s_tok, :],
                     jnp.where(g1m, p4_ref[:, s_tok, :], jnp.where(g2m, p8_ref[:, s_tok, :], p16_ref[:, s_tok, :])))
    win = jnp.where(g0m, POOL_WINDOWS[0], jnp.where(g1m, POOL_WINDOWS[1],
                                                     jnp.where(g2m, POOL_WINDOWS[2], POOL_WINDOWS[3])))
    cnt = jnp.minimum(pos0 + j * TM + row + 1, win).astype(F32)
    d = (wsum / cnt - xc).reshape(R, D_POOL).astype(BF16)
    py = jnp.dot(d, poolw_ref[...], preferred_element_type=F32) * pscale_ref[...]
    py_ref[...] = py.reshape(G, TM, D_POOL).astype(py_ref.dtype)
    if n_tiles > 1:
        ps_ref[:, 0:POOL_PAD, :] = ps_ref[:, TM:TM + POOL_PAD, :]


def _proj(x3, pos0, chist, phist, lw, *, G, TM, adt):
    NS, T, _ = x3.shape
    n_tiles = T // TM
    pos = (pos0 + jnp.arange(T)).astype(F32)
    half = HEAD_DIM // 2
    inv = ROPE_THETA ** (-jnp.arange(half, dtype=F32) / half)
    ang = pos[:, None] * inv[None, :]
    cos = jnp.tile(jnp.cos(ang), (1, 2 * LANES // HEAD_DIM))
    sin = jnp.tile(jnp.concatenate([-jnp.sin(ang), jnp.sin(ang)], axis=1), (1, LANES // HEAD_DIM))

    def tok(c, dt=F32):
        return jax.ShapeDtypeStruct((NS, T, c), dt)

    def tspec(c):
        return pl.BlockSpec((G, TM, c), lambda s, j: (s, j, 0))

    out_cols = [(KV_DIM, F32), (KV_DIM, F32), (IDX_DIM, F32), (D_CONV, F32), (D_POOL, F32),
                (KV_DIM, adt), (KV_DIM, adt), (IDX_DIM, adt), (D_ATTN, adt), (IDX_HEADS * IDX_DIM, adt),
                (LANES, F32), (D_CONV, adt), (D_POOL, adt)]
    kern = functools.partial(_proj_kernel, G=G, TM=TM, pos0=pos0, n_tiles=n_tiles)
    return pl.pallas_call(
        kern,
        grid=(NS // G, n_tiles),
        in_specs=[tspec(D_MODEL), _const_spec((1, D_MODEL)), _const_spec((D_MODEL, D_A)),
                  pl.BlockSpec((TM, LANES), lambda s, j: (j, 0)), pl.BlockSpec((TM, LANES), lambda s, j: (j, 0)),
                  pl.BlockSpec((G, CONV_PAD, D_CONV), lambda s, j: (s, 0, 0)),
                  pl.BlockSpec((G, POOL_PAD, D_POOL), lambda s, j: (s, 0, 0)),
                  _const_spec((CONV_PAD, D_CONV)), _const_spec((1, D_CONV)), _const_spec((1, D_CONV)),
                  _const_spec((1, D_CONV)), _const_spec((D_POOL, D_POOL)), _const_spec((1, D_POOL))],
        out_specs=[tspec(c) for c, _ in out_cols],
        out_shape=[tok(c, dt) for c, dt in out_cols],
        scratch_shapes=[pltpu.VMEM((G, CONV_PAD + TM, D_CONV), F32)] +
                       [pltpu.VMEM((G, POOL_PAD + TM, D_POOL), F32)] * 5,
        compiler_params=_cparams(("arbitrary", "arbitrary")),
        name="proj",
    )(x3, lw["g_mix"], lw["w_a"], cos, sin, chist, phist, lw["conv_w"], lw["conv_b"], lw["ln_g"], lw["ln_b"],
      lw["pool_w"], lw["pool_scale"])


def _kth_bisect(count, smin, smax, nvalid, kk, steps):
    zero = jnp.zeros_like(smin)
    c_ge0 = count(zero, False)
    c_gt0 = count(zero, True)
    pos = c_gt0 >= kk
    neg = c_ge0 < kk
    lo0 = jnp.where(neg, smin, zero)
    clo0 = jnp.where(neg, nvalid, c_ge0)
    hi0 = jnp.where(pos, BIG, zero)
    chi0 = jnp.where(pos, zero, jnp.where(neg, c_ge0, c_gt0))
    cand0 = jnp.where(pos, smax, smin * 0.5)
    small = nvalid <= kk
    done0 = (small | jnp.logical_not(pos | neg)).astype(F32)
    init = (jnp.where(small, smin, lo0), hi0, jnp.where(small, nvalid, clo0), chi0, cand0, done0)

    def count_ge(cand):
        return count(cand, False)

    def cond(st):
        return jnp.min(st[5]) < 0.5

    def body(st):
        for _ in range(steps):
            st = step(st)
        return st

    def step(st):
        lo, hi, clo, chi, cand, done = st
        c = count_ge(cand)
        act = done < 0.5
        ge = c >= kk
        up = act & ge
        dn = act & jnp.logical_not(ge)
        lo = jnp.where(up, cand, lo)
        clo = jnp.where(up, c, clo)
        hi = jnp.where(dn, cand, hi)
        chi = jnp.where(dn, c, chi)
        mid = lo * 0.5 + hi * 0.5
        inside = (mid > lo) & (mid < hi)
        fin = (clo == kk) | jnp.logical_not(inside)
        done = jnp.where(fin, 1.0, done)
        return lo, hi, clo, chi, mid, done

    lo, _, clo, chi, _, _ = lax.while_loop(cond, body, init)
    return lo, clo, chi


def _attend_prompt_kernel(qit_ref, wq_ref, qtz_ref, ki_ref, k_ref, vt_ref, o_ref, sc_ref, acc_ref, *, CKS, CK, kk):
    i = pl.program_id(1)
    QB = LANES
    nchs = (i * QB + QB + CKS - 1) // CKS
    nch = nchs * (CKS // CK)
    qit = qit_ref[...]
    wq = wq_ref[...]
    qcol = i * QB + lax.broadcasted_iota(jnp.int32, (CKS, QB), 1)
    krow0 = lax.broadcasted_iota(jnp.int32, (CKS, QB), 0)

    def score_body(c, carry):
        smax, smin = carry
        s = jnp.dot(ki_ref[c], qit, preferred_element_type=F32)
        s = jnp.maximum(s, 0.0) * wq
        tot = s[:, 0:QB]
        for hh in range(1, IDX_HEADS):
            tot = tot + s[:, hh * QB:(hh + 1) * QB]
        valid = (krow0 + c * CKS) <= qcol
        masked = jnp.where(valid, tot, NEG_INF)
        sc_ref[pl.ds(pl.multiple_of(c * CKS, CKS), CKS), :] = masked
        smax = jnp.maximum(smax, jnp.max(masked.reshape(CKS // 8, 8, QB), axis=0))
        smin = jnp.minimum(smin, jnp.min(jnp.where(valid, tot, BIG).reshape(CKS // 8, 8, QB), axis=0))
        return smax, smin

    smax, smin = lax.fori_loop(0, nchs, score_body,
                               (jnp.full((8, QB), NEG_INF, F32), jnp.full((8, QB), BIG, F32)))
    smax = jnp.max(smax, axis=0, keepdims=True)
    smin = jnp.min(smin, axis=0, keepdims=True)

    n_acc = 4
    rows = CKS // n_acc

    def count(cand, strict):
        def body(c, accs):
            base = pl.multiple_of(c * CKS, CKS)
            out = []
            for a in range(n_acc):
                blk = sc_ref[pl.ds(base + a * rows, rows), :]
                hit = (blk > cand) if strict else (blk >= cand)
                ones = jnp.where(hit, 1.0, 0.0).reshape(rows // 8, 8, QB)
                out.append(accs[a] + jnp.sum(ones, axis=0))
            return tuple(out)
        accs = lax.fori_loop(0, nchs, body, tuple(jnp.zeros((8, QB), F32) for _ in range(n_acc)))
        return jnp.sum((accs[0] + accs[1]) + (accs[2] + accs[3]), axis=0, keepdims=True)

    nvalid = (i * QB + lax.broadcasted_iota(jnp.int32, (1, QB), 1) + 1).astype(F32)
    lo, clo, chi = _kth_bisect(count, smin, smax, nvalid, float(kk), steps=4)

    fix = (clo > kk) & (nvalid > kk)

    @pl.when(jnp.max(fix.astype(F32)) > 0.5)
    def _():
        keep = kk - chi
        tri = (lax.broadcasted_iota(jnp.int32, (CK, CK), 1)
               <= lax.broadcasted_iota(jnp.int32, (CK, CK), 0)).astype(BF16)

        def body(c, run):
            rows_c = pl.ds(pl.multiple_of(c * CK, CK), CK)
            blk = sc_ref[rows_c, :]
            tie = jnp.where(fix & (blk == lo), 1.0, 0.0)
            rank = run + jnp.dot(tri, tie.astype(BF16), preferred_element_type=F32)
            sc_ref[rows_c, :] = jnp.where((tie > 0.5) & (rank > keep), NEG_INF, blk)
            return run + jnp.sum(tie, axis=0, keepdims=True)
        lax.fori_loop(0, nch, body, jnp.zeros((1, QB), F32))

    qtz = qtz_ref[...]
    acc_ref[...] = jnp.zeros_like(acc_ref)

    def att_body(c, carry):
        ms, ls = carry
        s_all = jnp.dot(k_ref[c], qtz, preferred_element_type=F32)
        sel = sc_ref[pl.ds(pl.multiple_of(c * CK, CK), CK), :] >= lo
        vt = vt_ref[c]
        new_ms, new_ls = [], []
        for hh in range(N_HEADS):
            s = jnp.where(sel, s_all[:, hh * QB:(hh + 1) * QB], NEG_INF)
            mnew = jnp.maximum(ms[hh], jnp.max(s, axis=0, keepdims=True))
            msafe = jnp.where(mnew == NEG_INF, 0.0, mnew)
            p = jnp.exp(s - msafe)
            alpha = jnp.exp(ms[hh] - msafe)
            new_ls.append(alpha * ls[hh] + jnp.sum(p, axis=0, keepdims=True))
            new_ms.append(mnew)
            pv = jnp.dot(vt, p.astype(BF16), preferred_element_type=F32)
            n = hh // GROUP
            rows = slice(hh * HEAD_DIM, (hh + 1) * HEAD_DIM)
            acc_ref[rows, :] = alpha * acc_ref[rows, :] + pv[n * HEAD_DIM:(n + 1) * HEAD_DIM, :]
        return tuple(new_ms), tuple(new_ls)

    m0 = tuple(jnp.full((1, QB), NEG_INF, F32) for _ in range(N_HEADS))
    l0 = tuple(jnp.zeros((1, QB), F32) for _ in range(N_HEADS))
    _, ls = lax.fori_loop(0, nch, att_body, (m0, l0))
    for hh in range(N_HEADS):
        rows = slice(hh * HEAD_DIM, (hh + 1) * HEAD_DIM)
        acc_ref[rows, :] = acc_ref[rows, :] / ls[hh]
    o_ref[...] = acc_ref[...].T.astype(BF16)


def _attend_prompt(q, qi, wi, kbf, vbf, kibf, *, CKS, CK):
    B, T, _ = q.shape
    QB = LANES
    nq = T // QB
    nc = T // CK
    ncs = T // CKS
    kk = min(TOPK_MAX, T // 4)

    def heads_t(a):
        return a.reshape(B, nq, QB, 8, 64).transpose(0, 1, 4, 3, 2).reshape(B, nq, 64, 8 * QB)

    qit = heads_t(qi)
    qt = heads_t(q)
    col_head = jnp.arange(8 * QB) // QB
    qtz = jnp.concatenate([jnp.where((col_head // GROUP == n)[None, None, None, :], qt, jnp.zeros_like(qt))
                           for n in range(N_KV_HEADS)], axis=2)
    wq = wi[:, :, :IDX_HEADS].reshape(B, nq, QB, IDX_HEADS).transpose(0, 1, 3, 2).reshape(B, nq, 1, 8 * QB)
    kib = kibf.reshape(B, ncs, CKS, IDX_DIM)
    kb = kbf.reshape(B, nc, CK, KV_DIM)
    vtb = vbf.reshape(B, nc, CK, KV_DIM).transpose(0, 1, 3, 2)

    kern = functools.partial(_attend_prompt_kernel, CKS=CKS, CK=CK, kk=kk)
    return pl.pallas_call(
        kern,
        grid=(B, nq),
        in_specs=[pl.BlockSpec((None, None, IDX_DIM, 8 * QB), lambda b, i: (b, i, 0, 0)),
                  pl.BlockSpec((None, None, 1, 8 * QB), lambda b, i: (b, i, 0, 0)),
                  pl.BlockSpec((None, None, KV_DIM, 8 * QB), lambda b, i: (b, i, 0, 0)),
                  pl.BlockSpec((None, ncs, CKS, IDX_DIM), lambda b, i: (b, 0, 0, 0)),
                  pl.BlockSpec((None, nc, CK, KV_DIM), lambda b, i: (b, 0, 0, 0)),
                  pl.BlockSpec((None, nc, KV_DIM, CK), lambda b, i: (b, 0, 0, 0))],
        out_specs=pl.BlockSpec((None, QB, D_ATTN), lambda b, i: (b, i, 0)),
        out_shape=jax.ShapeDtypeStruct((B, T, D_ATTN), BF16),
        scratch_shapes=[pltpu.VMEM((T, QB), F32), pltpu.VMEM((D_ATTN, QB), F32)],
        compiler_params=_cparams(("arbitrary", "arbitrary")),
        name="attend_prompt",
    )(qit, wq, qtz, kib, kb, vtb)


def _sample_score_kernel(pt_ref, *refs, P, n_steps, past_len, kk):
    pages = refs[:P]
    qi_ref, wb_ref, kin_ref, sc_ref, thr_ref = refs[P:P + 5]
    j = pl.program_id(1)
    TQ = sc_ref.shape[1]
    NBLK = sc_ref.shape[0]
    qi = qi_ref[...]
    wb = wb_ref[...]

    def block_scores(kpage):
        s = lax.dot_general(qi, kpage.astype(BF16), (((1,), (1,)), ((), ())), preferred_element_type=F32)
        s = jnp.maximum(s, 0.0) * wb
        return jnp.sum(s.reshape(IDX_HEADS, TQ, LANES), axis=0)

    for r in range(P):
        sc_ref[j * P + r] = block_scores(pages[r][...])

    @pl.when(j == n_steps - 1)
    def _():
        tot = block_scores(kin_ref[...])
        qrow = lax.broadcasted_iota(jnp.int32, (TQ, LANES), 0)
        kcol = lax.broadcasted_iota(jnp.int32, (TQ, LANES), 1)
        sc_ref[NBLK - 1] = jnp.where(kcol <= qrow, tot, NEG_INF)

        def lane_bcast(a):
            return jnp.broadcast_to(a, (TQ, LANES))

        sc = sc_ref[...]
        smax = lane_bcast(jnp.max(jnp.max(sc, axis=0), axis=1, keepdims=True))
        smin = lane_bcast(jnp.min(jnp.min(jnp.where(sc > NEG_INF, sc, BIG), axis=0), axis=1, keepdims=True))
        nvalid = (past_len + qrow + 1).astype(F32)

        def count(cand, strict):
            blk = sc_ref[...]
            hit = (blk > cand[None]) if strict else (blk >= cand[None])
            c = jnp.sum(jnp.where(hit, 1.0, 0.0), axis=0)
            return lane_bcast(jnp.sum(c, axis=1, keepdims=True))

        lo, clo, chi = _kth_bisect(count, smin, smax, nvalid, float(kk), steps=4)
        thr_ref[...] = lo

        fix = (clo > kk) & (nvalid > kk)

        @pl.when(jnp.max(fix.astype(F32)) > 0.5)
        def _():
            keep = kk - chi
            tri = (lax.broadcasted_iota(jnp.int32, (LANES, LANES), 0)
                   <= lax.broadcasted_iota(jnp.int32, (LANES, LANES), 1)).astype(BF16)

            def body(c, run):
                blk = sc_ref[c]
                tie = fix & (blk == lo)
                rank = run + jnp.dot(tie.astype(BF16), tri, preferred_element_type=F32)
                sc_ref[c] = jnp.where(tie & (rank > keep), NEG_INF, blk)
                return run + lane_bcast(jnp.sum(tie.astype(F32), axis=1, keepdims=True))
            lax.fori_loop(0, NBLK, body, jnp.zeros((TQ, LANES), F32))


def _sample_scores(page_table, cache_kidx_l, qi_rows, w_rows, ki_new, *, P):
    NB, n_pages = page_table.shape
    TQ = qi_rows.shape[1] // IDX_HEADS
    n_steps = n_pages // P
    past_len = n_pages * PAGE_SIZE
    kk = min(TOPK_MAX, (past_len + TQ) // 4)
    NBLK = n_pages + 1

    def page_spec(r):
        return pl.BlockSpec((None, PAGE_SIZE, IDX_DIM), lambda b, j, pt: (pt[b, j * P + r], 0, 0))

    kern = functools.partial(_sample_score_kernel, P=P, n_steps=n_steps, past_len=past_len, kk=kk)
    grid_spec = pltpu.PrefetchScalarGridSpec(
        num_scalar_prefetch=1,
        grid=(NB, n_steps),
        in_specs=[page_spec(r) for r in range(P)] + [
            pl.BlockSpec((None, IDX_HEADS * TQ, IDX_DIM), lambda b, j, pt: (b, 0, 0)),
            pl.BlockSpec((None, IDX_HEADS * TQ, LANES), lambda b, j, pt: (b, 0, 0)),
            pl.BlockSpec((None, PAGE_SIZE, IDX_DIM), lambda b, j, pt: (b, 0, 0))],
        out_specs=[pl.BlockSpec((None, NBLK, TQ, LANES), lambda b, j, pt: (b, 0, 0, 0)),
                   pl.BlockSpec((None, TQ, LANES), lambda b, j, pt: (b, 0, 0))],
    )
    return pl.pallas_call(
        kern,
        grid_spec=grid_spec,
        out_shape=[jax.ShapeDtypeStruct((NB, NBLK, TQ, LANES), F32), jax.ShapeDtypeStruct((NB, TQ, LANES), F32)],
        compiler_params=_cparams(("arbitrary", "arbitrary")),
        name="sample_scores",
    )(page_table, *([cache_kidx_l] * P), qi_rows, w_rows, ki_new)


def _sample_attend_kernel(pt_ref, *refs, P, n_steps):
    kpages = refs[:P]
    vpages = refs[P:2 * P]
    sc_ref, scn_ref, thr_ref, q_ref, kn_ref, vn_ref, o_ref, m_ref, l_ref, acc_ref = refs[2 * P:2 * P + 10]
    j = pl.program_id(1)
    qz = q_ref[...]
    thr = thr_ref[...]

    @pl.when(j == 0)
    def _():
        m_ref[...] = jnp.full_like(m_ref, NEG_INF)
        l_ref[...] = jnp.zeros_like(l_ref)
        acc_ref[...] = jnp.zeros_like(acc_ref)

    def update(kblocks, vblocks, scblocks):
        s_parts = []
        for kb, scb in zip(kblocks, scblocks):
            s = lax.dot_general(qz, kb.astype(BF16), (((1,), (1,)), ((), ())), preferred_element_type=F32)
            bias = jnp.where(scb >= thr, 0.0, NEG_INF)
            s_parts.append(s + jnp.concatenate([bias] * N_HEADS, axis=0))
        s = jnp.concatenate(s_parts, axis=1) if len(s_parts) > 1 else s_parts[0]
        m_old = m_ref[...]
        mnew = jnp.maximum(m_old, jnp.broadcast_to(jnp.max(s, axis=1, keepdims=True), m_old.shape))
        msafe = jnp.where(mnew == NEG_INF, 0.0, mnew)
        alpha = jnp.exp(m_old - msafe)
        p = jnp.exp(s - jnp.concatenate([msafe] * len(s_parts), axis=1) if len(s_parts) > 1 else s - msafe)
        l_ref[...] = alpha * l_ref[...] + jnp.broadcast_to(jnp.sum(p, axis=1, keepdims=True), m_old.shape)
        m_ref[...] = mnew
        pv = jnp.zeros(acc_ref.shape, F32)
        for r, vb in enumerate(vblocks):
            pv = pv + jnp.dot(p[:, r * LANES:(r + 1) * LANES].astype(BF16), vb.astype(BF16),
                              preferred_element_type=F32)
        acc_ref[...] = alpha * acc_ref[...] + pv

    update([kp[...] for kp in kpages], [vp[...] for vp in vpages], [sc_ref[r] for r in range(P)])

    @pl.when(j == n_steps - 1)
    def _():
        update([kn_ref[...]], [vn_ref[...]], [scn_ref[0]])
        o_ref[...] = acc_ref[...] / l_ref[...]


def _sample_attend(page_table, cache_k_l, cache_v_l, sc, thr, q_rows, k_new, v_new, *, P):
    NB, n_pages = page_table.shape
    R = q_rows.shape[1]
    TQ = R // N_HEADS
    n_steps = n_pages // P

    def page_spec(r):
        return pl.BlockSpec((None, PAGE_SIZE, KV_DIM), lambda b, j, pt: (pt[b, j * P + r], 0, 0))

    kern = functools.partial(_sample_attend_kernel, P=P, n_steps=n_steps)
    grid_spec = pltpu.PrefetchScalarGridSpec(
        num_scalar_prefetch=1,
        grid=(NB, n_steps),
        in_specs=[page_spec(r) for r in range(P)] + [page_spec(r) for r in range(P)] + [
            pl.BlockSpec((None, P, TQ, LANES), lambda b, j, pt: (b, j, 0, 0)),
            pl.BlockSpec((None, 1, TQ, LANES), lambda b, j, pt: (b, n_pages, 0, 0)),
            pl.BlockSpec((None, TQ, LANES), lambda b, j, pt: (b, 0, 0)),
            pl.BlockSpec((None, R, KV_DIM), lambda b, j, pt: (b, 0, 0)),
            pl.BlockSpec((None, PAGE_SIZE, KV_DIM), lambda b, j, pt: (b, 0, 0)),
            pl.BlockSpec((None, PAGE_SIZE, KV_DIM), lambda b, j, pt: (b, 0, 0))],
        out_specs=pl.BlockSpec((None, R, KV_DIM), lambda b, j, pt: (b, 0, 0)),
        scratch_shapes=[pltpu.VMEM((R, LANES), F32), pltpu.VMEM((R, LANES), F32), pltpu.VMEM((R, KV_DIM), F32)],
    )
    return pl.pallas_call(
        kern,
        grid_spec=grid_spec,
        out_shape=jax.ShapeDtypeStruct((NB, R, KV_DIM), F32),
        compiler_params=_cparams(("arbitrary", "arbitrary")),
        name="sample_attend",
    )(page_table, *([cache_k_l] * P), *([cache_v_l] * P), sc, sc, thr, q_rows, k_new, v_new)


def _mix_out_kernel(x_ref, g_ref, wg_ref, ca_ref, ao_ref, py_ref, wco_ref, wao_ref, wpo_ref, wo_ref, o_ref):
    x = x_ref[...]
    ms = jnp.mean(x * x, axis=-1, keepdims=True)
    h = (x * lax.rsqrt(ms + RMS_EPS) * g_ref[...]).astype(BF16)
    gates = jax.nn.sigmoid(jnp.dot(h, wg_ref[...], preferred_element_type=F32))
    y_a = jnp.dot(ca_ref[...], wco_ref[...], preferred_element_type=F32)
    y_b = jnp.dot(ao_ref[...], wao_ref[...], preferred_element_type=F32)
    y_c = jnp.dot(py_ref[...], wpo_ref[...], preferred_element_type=F32)
    merged = (gates[:, 0:D_MODEL] * y_a + gates[:, D_MODEL:2 * D_MODEL] * y_b
              + gates[:, 2 * D_MODEL:3 * D_MODEL] * y_c)
    o_ref[...] = x + jnp.dot(merged.astype(BF16), wo_ref[...], preferred_element_type=F32)


def _mix_out(x2, ca, ao, py, lw, *, TM):
    N = x2.shape[0]

    def tspec(c):
        return pl.BlockSpec((TM, c), lambda i: (i, 0))

    return pl.pallas_call(
        _mix_out_kernel,
        grid=(N // TM,),
        in_specs=[tspec(D_MODEL), _const_spec((1, D_MODEL)), _const_spec((D_MODEL, N_BRANCH * D_MODEL)),
                  tspec(D_CONV), tspec(D_ATTN), tspec(D_POOL),
                  _const_spec((D_CONV, D_MODEL)), _const_spec((D_ATTN, D_MODEL)), _const_spec((D_POOL, D_MODEL)),
                  _const_spec((D_MODEL, D_MODEL))],
        out_specs=tspec(D_MODEL),
        out_shape=jax.ShapeDtypeStruct((N, D_MODEL), F32),
        compiler_params=_cparams(("arbitrary",)),
        name="mix_out",
    )(x2, lw["g_mix"], lw["w_g"], ca, ao, py, lw["w_conv_out"], lw["w_attn_out"], lw["w_pool_out"], lw["w_out"])


def _ffn_kernel(x_ref, g_ref, w1_ref, w2_ref, gf_ref, o_ref, *, final):
    x = x_ref[...]
    ms = jnp.mean(x * x, axis=-1, keepdims=True)
    h = (x * lax.rsqrt(ms + RMS_EPS) * g_ref[...]).astype(BF16)
    r = jnp.maximum(jnp.dot(h, w1_ref[...], preferred_element_type=F32), 0.0)
    y = x + jnp.dot((r * r).astype(BF16), w2_ref[...], preferred_element_type=F32)
    if final:
        ms2 = jnp.mean(y * y, axis=-1, keepdims=True)
        y = y * lax.rsqrt(ms2 + RMS_EPS) * gf_ref[...]
    o_ref[...] = y


def _ffn(x2, lw, g_final, *, TM, final):
    N = x2.shape[0]
    once = pl.Buffered(1)
    return pl.pallas_call(
        functools.partial(_ffn_kernel, final=final),
        grid=(N // TM,),
        in_specs=[pl.BlockSpec((TM, D_MODEL), lambda i: (i, 0)), _const_spec((1, D_MODEL)),
                  pl.BlockSpec((D_MODEL, D_FF), lambda i: (0, 0), pipeline_mode=once),
                  pl.BlockSpec((D_FF, D_MODEL), lambda i: (0, 0), pipeline_mode=once),
                  _const_spec((1, D_MODEL))],
        out_specs=pl.BlockSpec((TM, D_MODEL), lambda i: (i, 0)),
        out_shape=jax.ShapeDtypeStruct((N, D_MODEL), F32),
        compiler_params=_cparams(("arbitrary",)),
        name="ffn",
    )(x2, lw["g_ffn"], lw["w_ff1"], lw["w_ff2"], g_final)


def _layer_weights(l, g_mix, w_in, conv_w, conv_b, conv_ln_g, conv_ln_b, w_conv_out, w_attn_out,
                   pool_w, pool_scale, w_pool_out, w_out, g_ffn, w_ff1, w_ff2):
    W = w_in[l]
    o_ki = C_QI + IDX_HEADS * IDX_DIM
    o_wi = o_ki + IDX_DIM
    o_xc = o_wi + IDX_HEADS
    o_g = o_xc + D_POOL

    def pad_to_tile(a):
        return jnp.pad(a, ((0, 0), (0, LANES - a.shape[1])))

    w_a = jnp.concatenate([W[:, :o_ki], W[:, o_xc:o_g], pad_to_tile(W[:, o_ki:o_wi]),
                           pad_to_tile(W[:, o_wi:o_xc])], axis=1).astype(BF16)
    pw = jnp.zeros((D_POOL, D_POOL), F32)
    for g in range(len(POOL_WINDOWS)):
        pw = pw.at[g * POOL_GC:(g + 1) * POOL_GC, g * POOL_GC:(g + 1) * POOL_GC].set(pool_w[l, g])
    return dict(
        g_mix=g_mix[l][None], w_a=w_a, w_g=W[:, o_g:].astype(BF16),
        conv_w=jnp.pad(conv_w[l], ((0, CONV_PAD - CONV_W), (0, 0))), conv_b=conv_b[l][None],
        ln_g=conv_ln_g[l][None], ln_b=conv_ln_b[l][None],
        pool_w=pw.astype(BF16), pool_scale=pool_scale[l][None],
        w_conv_out=w_conv_out[l].astype(BF16), w_attn_out=w_attn_out[l].astype(BF16),
        w_pool_out=w_pool_out[l].astype(BF16), w_out=w_out[l].astype(BF16),
        g_ffn=g_ffn[l][None], w_ff1=w_ff1[l].astype(BF16), w_ff2=w_ff2[l].astype(BF16))


def _pad_hist(hist, rows):
    return jnp.pad(hist, ((0, 0), (rows - hist.shape[1], 0), (0, 0)))


def _tail_state(hist, new, keep):
    return jnp.concatenate([hist, new], axis=1)[:, -keep:]


def kernel(x_prompt, x_sample, cache_k, cache_v, cache_kidx, state_conv, state_pool, page_table, g_mix, w_in,
           conv_w, conv_b, conv_ln_g, conv_ln_b, w_conv_out, w_attn_out, pool_w, pool_scale, w_pool_out, w_out,
           g_ffn, w_ff1, w_ff2, g_final):
    B, T, _ = x_prompt.shape
    NB, TS, _ = x_sample.shape
    depth = w_in.shape[0]
    n_pool = cache_k.shape[1]
    past_len = page_table.shape[1] * PAGE_SIZE
    TM_P = 256
    gf = g_final[None]
    xp = x_prompt
    xs = x_sample
    zero_conv = jnp.zeros((B, CONV_PAD, D_CONV), F32)
    zero_pool = jnp.zeros((B, POOL_PAD, D_POOL), F32)
    st_p = [[] for _ in range(5)]
    st_s = [[] for _ in range(5)]
    for l in range(depth):
        lw = _layer_weights(l, g_mix, w_in, conv_w, conv_b, conv_ln_g, conv_ln_b, w_conv_out, w_attn_out,
                            pool_w, pool_scale, w_pool_out, w_out, g_ffn, w_ff1, w_ff2)
        final = l == depth - 1

        (k, v, ki, u, xc, kbf, vbf, kibf, q, qi, wi, ca, py) = _proj(xp, 0, zero_conv, zero_pool, lw, G=1, TM=TM_P,
                                                                      adt=BF16)
        ao = _attend_prompt(q, qi, wi, kbf, vbf, kibf, CKS=min(512, T), CK=256)
        x2 = _mix_out(xp.reshape(B * T, D_MODEL), ca.reshape(B * T, D_CONV), ao.reshape(B * T, D_ATTN),
                      py.reshape(B * T, D_POOL), lw, TM=TM_P)
        xp = _ffn(x2, lw, gf, TM=TM_P, final=final).reshape(B, T, D_MODEL)
        st_p[0].append(k.reshape(B, T, N_KV_HEADS, HEAD_DIM))
        st_p[1].append(v.reshape(B, T, N_KV_HEADS, HEAD_DIM))
        st_p[2].append(ki)
        st_p[3].append(u[:, T - CONV_HIST:])
        st_p[4].append(xc[:, T - POOL_HIST:])

        (k, v, ki, u, xc, _, _, _, q, qi, wi, ca, py) = _proj(
            xs, past_len, _pad_hist(state_conv[l], CONV_PAD), _pad_hist(state_pool[l], POOL_PAD), lw, G=NB, TM=TS,
            adt=F32)
        q, qi, ca, py = q.astype(BF16), qi.astype(BF16), ca.astype(BF16), py.astype(BF16)
        qi_rows = qi.reshape(NB, TS, IDX_HEADS, IDX_DIM).transpose(0, 2, 1, 3).reshape(NB, IDX_HEADS * TS, IDX_DIM)
        w_rows = jnp.broadcast_to(wi[:, :, :IDX_HEADS].transpose(0, 2, 1).reshape(NB, IDX_HEADS * TS, 1),
                                  (NB, IDX_HEADS * TS, LANES))
        qh = q.reshape(NB, TS, N_HEADS, HEAD_DIM).transpose(0, 2, 1, 3)
        head_kv = (jnp.arange(N_HEADS) // GROUP)[None, :, None, None]
        q_rows = jnp.concatenate([jnp.where(head_kv == n, qh, jnp.zeros_like(qh)) for n in range(N_KV_HEADS)],
                                 axis=-1).reshape(NB, N_HEADS * TS, KV_DIM)
        pad_rows = ((0, 0), (0, PAGE_SIZE - TS), (0, 0))
        pages_per_step = min(16, page_table.shape[1])
        sc, thr = _sample_scores(page_table, cache_kidx[l].astype(BF16), qi_rows, w_rows, jnp.pad(ki, pad_rows),
                                 P=pages_per_step)
        o_rows = _sample_attend(page_table, cache_k[l].reshape(n_pool, PAGE_SIZE, KV_DIM).astype(BF16),
                                cache_v[l].reshape(n_pool, PAGE_SIZE, KV_DIM).astype(BF16), sc, thr, q_rows,
                                jnp.pad(k, pad_rows), jnp.pad(v, pad_rows), P=pages_per_step)
        o5 = o_rows.reshape(NB, N_KV_HEADS, GROUP, TS, N_KV_HEADS, HEAD_DIM)
        ao = jnp.stack([o5[:, n, :, :, n, :] for n in range(N_KV_HEADS)], axis=1)
        ao = ao.reshape(NB, N_HEADS, TS, HEAD_DIM).transpose(0, 2, 1, 3).reshape(NB * TS, D_ATTN).astype(BF16)
        x2 = _mix_out(xs.reshape(NB * TS, D_MODEL), ca.reshape(NB * TS, D_CONV), ao,
                      py.reshape(NB * TS, D_POOL), lw, TM=NB * TS)
        xs = _ffn(x2, lw, gf, TM=NB * TS, final=final).reshape(NB, TS, D_MODEL)
        st_s[0].append(k.reshape(NB, TS, N_KV_HEADS, HEAD_DIM))
        st_s[1].append(v.reshape(NB, TS, N_KV_HEADS, HEAD_DIM))
        st_s[2].append(ki)
        st_s[3].append(_tail_state(state_conv[l], u, CONV_HIST))
        st_s[4].append(_tail_state(state_pool[l], xc, POOL_HIST))

    sp = [jnp.stack(a, axis=0) for a in st_p]
    ss = [jnp.stack(a, axis=0) for a in st_s]
    return (xp, xs, sp[0], sp[1], sp[2], sp[3], sp[4], ss[0], ss[1], ss[2], ss[3], ss[4])
```

```python
import functools

import jax
import jax.numpy as jnp
from jax import lax
from jax.experimental import pallas as pl
from jax.experimental.pallas import tpu as pltpu

F32 = jnp.float32
BF16 = jnp.bfloat16

D_MODEL = 1024
D_CONV = 256
CONV_W = 31
CONV_HIST = CONV_W - 1
HEAD_DIM = 64
D_ATTN = 512
N_HEADS = 8
N_KV_HEADS = 2
KV_DIM = N_KV_HEADS * HEAD_DIM
GROUP = N_HEADS // N_KV_HEADS
IDX_HEADS = 8
IDX_DIM = 64
TOPK_MAX = 256
D_POOL = 256
POOL_WINDOWS = (2, 4, 8, 16)
POOL_GC = D_POOL // len(POOL_WINDOWS)
POOL_HIST = max(POOL_WINDOWS) - 1
N_BRANCH = 3
D_FF = 4 * D_MODEL
ROPE_THETA = 10000.0
RMS_EPS = 1e-6
LN_EPS = 1e-5
PAGE_SIZE = 128

LANES = 128
CONV_PAD = 32
POOL_PAD = 16
Q_SCALE = HEAD_DIM ** -0.5 * 1.4426950408889634
BIG = 3.0e38
NEG_INF = float("-inf")

C_AIN, C_AGATE, C_Q, C_K, C_V, C_QI, C_XC, C_KI, C_WI, D_A = 0, 256, 512, 1024, 1152, 1280, 1792, 2048, 2176, 2304

VMEM_LIMIT = 56 * 1024 * 1024


def _cparams(sem):
    return pltpu.CompilerParams(dimension_semantics=sem, vmem_limit_bytes=VMEM_LIMIT)


def _const_spec(shape):
    n = len(shape)
    return pl.BlockSpec(shape, lambda *_: (0,) * n)


def _proj_kernel(x_ref, g_ref, w_ref, cos_ref, sin_ref, chist_ref, phist_ref, convw_ref, convb_ref,
                 lng_ref, lnb_ref, poolw_ref, pscale_ref,
                 k_ref, v_ref, ki_ref, u_ref, xc_ref, kbf_ref, vbf_ref, kibf_ref, q_ref, qi_ref, wi_ref,
                 ca_ref, py_ref,
                 cs_ref, ps_ref, p2_ref, p4_ref, p8_ref, p16_ref, *, G, TM, pos0, n_tiles):
    j = pl.program_id(1)
    R = G * TM
    x = x_ref[...].reshape(R, D_MODEL)
    ms = jnp.mean(x * x, axis=-1, keepdims=True)
    h = (x * lax.rsqrt(ms + RMS_EPS) * g_ref[...]).astype(BF16)
    z = jnp.dot(h, w_ref[...], preferred_element_type=F32)

    cos = cos_ref[...]
    sin = sin_ref[...]
    if G > 1:
        cos = jnp.concatenate([cos] * G, axis=0)
        sin = jnp.concatenate([sin] * G, axis=0)

    def rope(t):
        n = t.shape[1] // LANES
        c = jnp.concatenate([cos] * n, axis=1) if n > 1 else cos
        s = jnp.concatenate([sin] * n, axis=1) if n > 1 else sin
        lane = lax.broadcasted_iota(jnp.int32, t.shape, 1)
        first = (lane & (HEAD_DIM - 1)) < HEAD_DIM // 2
        sw = jnp.where(first, pltpu.roll(t, t.shape[1] - HEAD_DIM // 2, 1), pltpu.roll(t, HEAD_DIM // 2, 1))
        return t * c + sw * s

    q = rope(z[:, C_Q:C_Q + D_ATTN]) * Q_SCALE
    k = rope(z[:, C_K:C_K + KV_DIM])
    v = z[:, C_V:C_V + KV_DIM]
    qi = rope(z[:, C_QI:C_QI + IDX_HEADS * IDX_DIM])
    ki = rope(z[:, C_KI:C_KI + LANES])[:, :IDX_DIM]
    wi = z[:, C_WI:C_WI + LANES] * (IDX_HEADS ** -0.5 * IDX_DIM ** -0.5)

    k_ref[...] = k.reshape(G, TM, KV_DIM)
    v_ref[...] = v.reshape(G, TM, KV_DIM)
    ki_ref[...] = ki.reshape(G, TM, IDX_DIM)
    kbf_ref[...] = k.reshape(G, TM, KV_DIM).astype(kbf_ref.dtype)
    vbf_ref[...] = v.reshape(G, TM, KV_DIM).astype(vbf_ref.dtype)
    kibf_ref[...] = ki.reshape(G, TM, IDX_DIM).astype(kibf_ref.dtype)
    q_ref[...] = q.reshape(G, TM, D_ATTN).astype(q_ref.dtype)
    qi_ref[...] = qi.reshape(G, TM, IDX_HEADS * IDX_DIM).astype(qi_ref.dtype)
    wi_ref[...] = wi.reshape(G, TM, LANES)

    a_in = z[:, C_AIN:C_AIN + D_CONV]
    a_gate = z[:, C_AGATE:C_AGATE + D_CONV]
    u = (a_in * jax.nn.sigmoid(a_gate)).reshape(G, TM, D_CONV)
    u_ref[...] = u

    @pl.when(j == 0)
    def _():
        cs_ref[:, 0:CONV_PAD, :] = chist_ref[...]
        ps_ref[:, 0:POOL_PAD, :] = phist_ref[...]

    cs_ref[:, CONV_PAD:CONV_PAD + TM, :] = u
    off = CONV_PAD - CONV_HIST
    rc = min(TM, 64)
    gc = min(G, 8)
    for g0 in range(0, G, gc):
        for r0 in range(0, TM, rc):
            acc = jnp.zeros((gc, rc, D_CONV), F32) + convb_ref[...][None]
            for t in range(CONV_W):
                acc = acc + convw_ref[t:t + 1, :][None] * cs_ref[g0:g0 + gc, off + t + r0:off + t + r0 + rc, :]
            mu = jnp.mean(acc, axis=-1, keepdims=True)
            var = jnp.mean(jnp.square(acc - mu), axis=-1, keepdims=True)
            y = (acc - mu) * lax.rsqrt(var + LN_EPS) * lng_ref[...][None] + lnb_ref[...][None]
            ca_ref[g0:g0 + gc, r0:r0 + rc, :] = (y * jax.nn.sigmoid(y)).astype(ca_ref.dtype)
    if n_tiles > 1:
        cs_ref[:, 0:CONV_PAD, :] = cs_ref[:, TM:TM + CONV_PAD, :]

    xc = z[:, C_XC:C_XC + D_POOL].reshape(G, TM, D_POOL)
    xc_ref[...] = xc
    ps_ref[:, POOL_PAD:POOL_PAD + TM, :] = xc
    RR = POOL_PAD + TM
    p2_ref[:, 1:RR, :] = ps_ref[:, 1:RR, :] + ps_ref[:, 0:RR - 1, :]
    p4_ref[:, 3:RR, :] = p2_ref[:, 3:RR, :] + p2_ref[:, 1:RR - 2, :]
    p8_ref[:, 7:RR, :] = p4_ref[:, 7:RR, :] + p4_ref[:, 3:RR - 4, :]
    p16_ref[:, 15:RR, :] = p8_ref[:, 15:RR, :] + p8_ref[:, 7:RR - 8, :]
    lane = lax.broadcasted_iota(jnp.int32, (G, TM, D_POOL), 2)
    row = lax.broadcasted_iota(jnp.int32, (G, TM, D_POOL), 1)
    g0m, g1m, g2m = lane < POOL_GC, lane < 2 * POOL_GC, lane < 3 * POOL_GC
    s_tok = slice(POOL_PAD, POOL_PAD + TM)
    wsum = jnp.where(g0m, p2_ref[:, s_tok, :],
                     jnp.where(g1m, p4_ref[:, s_tok, :], jnp.where(g2m, p8_ref[:, s_tok, :], p16_ref[:, s_tok, :])))
    win = jnp.where(g0m, POOL_WINDOWS[0], jnp.where(g1m, POOL_WINDOWS[1],
                                                     jnp.where(g2m, POOL_WINDOWS[2], POOL_WINDOWS[3])))
    cnt = jnp.minimum(pos0 + j * TM + row + 1, win).astype(F32)
    d = (wsum / cnt - xc).reshape(R, D_POOL).astype(BF16)
    py = jnp.dot(d, poolw_ref[...], preferred_element_type=F32) * pscale_ref[...]
    py_ref[...] = py.reshape(G, TM, D_POOL).astype(py_ref.dtype)
    if n_tiles > 1:
        ps_ref[:, 0:POOL_PAD, :] = ps_ref[:, TM:TM + POOL_PAD, :]


def _proj(x3, pos0, chist, phist, lw, *, G, TM, adt):
    NS, T, _ = x3.shape
    n_tiles = T // TM
    pos = (pos0 + jnp.arange(T)).astype(F32)
    half = HEAD_DIM // 2
    inv = ROPE_THETA ** (-jnp.arange(half, dtype=F32) / half)
    ang = pos[:, None] * inv[None, :]
    cos = jnp.tile(jnp.cos(ang), (1, 2 * LANES // HEAD_DIM))
    sin = jnp.tile(jnp.concatenate([-jnp.sin(ang), jnp.sin(ang)], axis=1), (1, LANES // HEAD_DIM))

    def tok(c, dt=F32):
        return jax.ShapeDtypeStruct((NS, T, c), dt)

    def tspec(c):
        return pl.BlockSpec((G, TM, c), lambda s, j: (s, j, 0))

    out_cols = [(KV_DIM, F32), (KV_DIM, F32), (IDX_DIM, F32), (D_CONV, F32), (D_POOL, F32),
                (KV_DIM, adt), (KV_DIM, adt), (IDX_DIM, adt), (D_ATTN, adt), (IDX_HEADS * IDX_DIM, adt),
                (LANES, F32), (D_CONV, adt), (D_POOL, adt)]
    kern = functools.partial(_proj_kernel, G=G, TM=TM, pos0=pos0, n_tiles=n_tiles)
    return pl.pallas_call(
        kern,
        grid=(NS // G, n_tiles),
        in_specs=[tspec(D_MODEL), _const_spec((1, D_MODEL)), _const_spec((D_MODEL, D_A)),
                  pl.BlockSpec((TM, LANES), lambda s, j: (j, 0)), pl.BlockSpec((TM, LANES), lambda s, j: (j, 0)),
                  pl.BlockSpec((G, CONV_PAD, D_CONV), lambda s, j: (s, 0, 0)),
                  pl.BlockSpec((G, POOL_PAD, D_POOL), lambda s, j: (s, 0, 0)),
                  _const_spec((CONV_PAD, D_CONV)), _const_spec((1, D_CONV)), _const_spec((1, D_CONV)),
                  _const_spec((1, D_CONV)), _const_spec((D_POOL, D_POOL)), _const_spec((1, D_POOL))],
        out_specs=[tspec(c) for c, _ in out_cols],
        out_shape=[tok(c, dt) for c, dt in out_cols],
        scratch_shapes=[pltpu.VMEM((G, CONV_PAD + TM, D_CONV), F32)] +
                       [pltpu.VMEM((G, POOL_PAD + TM, D_POOL), F32)] * 5,
        compiler_params=_cparams(("arbitrary", "arbitrary")),
        name="proj",
    )(x3, lw["g_mix"], lw["w_a"], cos, sin, chist, phist, lw["conv_w"], lw["conv_b"], lw["ln_g"], lw["ln_b"],
      lw["pool_w"], lw["pool_scale"])


def _kth_bisect(count, smin, smax, nvalid, kk, steps):
    zero = jnp.zeros_like(smin)
    c_ge0 = count(zero, False)
    c_gt0 = count(zero, True)
    pos = c_gt0 >= kk
    neg = c_ge0 < kk
    lo0 = jnp.where(neg, smin, zero)
    clo0 = jnp.where(neg, nvalid, c_ge0)
    hi0 = jnp.where(pos, BIG, zero)
    chi0 = jnp.where(pos, zero, jnp.where(neg, c_ge0, c_gt0))
    cand0 = jnp.where(pos, smax, smin * 0.5)
    small = nvalid <= kk
    done0 = (small | jnp.logical_not(pos | neg)).astype(F32)
    init = (jnp.where(small, smin, lo0), hi0, jnp.where(small, nvalid, clo0), chi0, cand0, done0)

    def count_ge(cand):
        return count(cand, False)

    def cond(st):
        return jnp.min(st[5]) < 0.5

    def body(st):
        for _ in range(steps):
            st = step(st)
        return st

    def step(st):
        lo, hi, clo, chi, cand, done = st
        c = count_ge(cand)
        act = done < 0.5
        ge = c >= kk
        up = act & ge
        dn = act & jnp.logical_not(ge)
        lo = jnp.where(up, cand, lo)
        clo = jnp.where(up, c, clo)
        hi = jnp.where(dn, cand, hi)
        chi = jnp.where(dn, c, chi)
        mid = lo * 0.5 + hi * 0.5
        inside = (mid > lo) & (mid < hi)
        fin = (clo == kk) | jnp.logical_not(inside)
        done = jnp.where(fin, 1.0, done)
        return lo, hi, clo, chi, mid, done

    lo, _, clo, chi, _, _ = lax.while_loop(cond, body, init)
    return lo, clo, chi


def _attend_prompt_kernel(qit_ref, wq_ref, q_ref, ki_ref, kt_ref, va_ref, o_ref, sc_ref, acc_ref, m_ref, *,
                          CKS, CK, kk):
    i = pl.program_id(1)
    QB = LANES
    nchs = (i * QB + QB + CKS - 1) // CKS
    nch = nchs * (CKS // CK)
    qit = qit_ref[...]
    wq = wq_ref[...]
    qcol = i * QB + lax.broadcasted_iota(jnp.int32, (CKS, QB), 1)
    krow0 = lax.broadcasted_iota(jnp.int32, (CKS, QB), 0)

    def score_body(c, carry):
        smax, smin = carry
        s = jnp.dot(ki_ref[c], qit, preferred_element_type=F32)
        s = jnp.maximum(s, 0.0) * wq
        tot = s[:, 0:QB]
        for hh in range(1, IDX_HEADS):
            tot = tot + s[:, hh * QB:(hh + 1) * QB]
        valid = (krow0 + c * CKS) <= qcol
        masked = jnp.where(valid, tot, NEG_INF)
        sc_ref[pl.ds(pl.multiple_of(c * CKS, CKS), CKS), :] = masked
        smax = jnp.maximum(smax, jnp.max(masked.reshape(CKS // 8, 8, QB), axis=0))
        smin = jnp.minimum(smin, jnp.min(jnp.where(valid, tot, BIG).reshape(CKS // 8, 8, QB), axis=0))
        return smax, smin

    smax, smin = lax.fori_loop(0, nchs, score_body,
                               (jnp.full((8, QB), NEG_INF, F32), jnp.full((8, QB), BIG, F32)))
    smax = jnp.max(smax, axis=0, keepdims=True)
    smin = jnp.min(smin, axis=0, keepdims=True)

    n_acc = 4
    rows = CKS // n_acc

    def count(cand, strict):
        def body(c, accs):
            base = pl.multiple_of(c * CKS, CKS)
            out = []
            for a in range(n_acc):
                blk = sc_ref[pl.ds(base + a * rows, rows), :]
                hit = (blk > cand) if strict else (blk >= cand)
                ones = jnp.where(hit, 1.0, 0.0).reshape(rows // 8, 8, QB)
                out.append(accs[a] + jnp.sum(ones, axis=0))
            return tuple(out)
        accs = lax.fori_loop(0, nchs, body, tuple(jnp.zeros((8, QB), F32) for _ in range(n_acc)))
        return jnp.sum((accs[0] + accs[1]) + (accs[2] + accs[3]), axis=0, keepdims=True)

    nvalid = (i * QB + lax.broadcasted_iota(jnp.int32, (1, QB), 1) + 1).astype(F32)
    lo, clo, chi = _kth_bisect(count, smin, smax, nvalid, float(kk), steps=4)

    fix = (clo > kk) & (nvalid > kk)

    @pl.when(jnp.max(fix.astype(F32)) > 0.5)
    def _():
        keep = kk - chi
        TF = min(256, CKS)
        tri = (lax.broadcasted_iota(jnp.int32, (TF, TF), 1)
               <= lax.broadcasted_iota(jnp.int32, (TF, TF), 0)).astype(BF16)

        def body(c, run):
            rows_c = pl.ds(pl.multiple_of(c * TF, TF), TF)
            blk = sc_ref[rows_c, :]
            tie = jnp.where(fix & (blk == lo), 1.0, 0.0)
            rank = run + jnp.dot(tri, tie.astype(BF16), preferred_element_type=F32)
            sc_ref[rows_c, :] = jnp.where((tie > 0.5) & (rank > keep), NEG_INF, blk)
            return run + jnp.sum(tie, axis=0, keepdims=True)
        lax.fori_loop(0, nchs * (CKS // TF), body, jnp.zeros((1, QB), F32))

    GQ = GROUP * QB
    acc_ref[...] = jnp.zeros_like(acc_ref)
    m_ref[...] = jnp.full_like(m_ref, NEG_INF)
    reps = CK // LANES

    def att_body(c, _):
        scb = sc_ref[pl.ds(pl.multiple_of(c * CK, CK), CK), :]
        bias_t = jnp.where(scb >= lo, 0.0, NEG_INF).T
        for n in range(N_KV_HEADS):
            qn = q_ref[n * GROUP:(n + 1) * GROUP].reshape(GQ, HEAD_DIM)
            s = jnp.dot(qn, kt_ref[c, n], preferred_element_type=F32)
            s = (s.reshape(GROUP, QB, CK) + bias_t[None]).reshape(GQ, CK)
            m_old = m_ref[n]
            mnew = jnp.maximum(m_old, jnp.max(s, axis=1, keepdims=True))
            msafe = jnp.where(mnew == NEG_INF, 0.0, mnew)
            p = jnp.exp2(s - jnp.concatenate([msafe] * reps, axis=1))
            alpha = jnp.exp2(m_old - msafe)
            pv = jnp.dot(p.astype(BF16), va_ref[c, n], preferred_element_type=F32)
            acc_ref[n] = alpha * acc_ref[n] + pv
            m_ref[n] = mnew
        return 0

    lax.fori_loop(0, nch, att_body, 0)
    for n in range(N_KV_HEADS):
        a = acc_ref[n]
        o = a[:, :HEAD_DIM] / a[:, HEAD_DIM:HEAD_DIM + 1]
        o_ref[n * GROUP:(n + 1) * GROUP] = o.reshape(GROUP, QB, HEAD_DIM).astype(BF16)


def _attend_prompt(q, qi, wi, kbf, vbf, kibf, *, CKS, CK):
    B, T, _ = q.shape
    QB = LANES
    nq = T // QB
    nc = T // CK
    ncs = T // CKS
    kk = min(TOPK_MAX, T // 4)

    def heads_t(a):
        return a.reshape(B, nq, QB, 8, 64).transpose(0, 1, 4, 3, 2).reshape(B, nq, 64, 8 * QB)

    qit = heads_t(qi)
    wq = wi[:, :, :IDX_HEADS].reshape(B, nq, QB, IDX_HEADS).transpose(0, 1, 3, 2).reshape(B, nq, 1, 8 * QB)
    kib = kibf.reshape(B, ncs, CKS, IDX_DIM)
    q_hm = q.reshape(B, T, N_HEADS, HEAD_DIM).transpose(0, 2, 1, 3)
    kt = kbf.reshape(B, nc, CK, N_KV_HEADS, HEAD_DIM).transpose(0, 1, 3, 4, 2)
    v_hm = vbf.reshape(B, nc, CK, N_KV_HEADS, HEAD_DIM).transpose(0, 1, 3, 2, 4)
    ones_col = (jnp.arange(LANES - HEAD_DIM) == 0).astype(BF16)
    va = jnp.concatenate([v_hm, jnp.broadcast_to(ones_col, v_hm.shape[:-1] + (LANES - HEAD_DIM,))], axis=-1)

    kern = functools.partial(_attend_prompt_kernel, CKS=CKS, CK=CK, kk=kk)
    o_hm = pl.pallas_call(
        kern,
        grid=(B, nq),
        in_specs=[pl.BlockSpec((None, None, IDX_DIM, 8 * QB), lambda b, i: (b, i, 0, 0)),
                  pl.BlockSpec((None, None, 1, 8 * QB), lambda b, i: (b, i, 0, 0)),
                  pl.BlockSpec((None, N_HEADS, QB, HEAD_DIM), lambda b, i: (b, 0, i, 0)),
                  pl.BlockSpec((None, ncs, CKS, IDX_DIM), lambda b, i: (b, 0, 0, 0)),
                  pl.BlockSpec((None, nc, N_KV_HEADS, HEAD_DIM, CK), lambda b, i: (b, 0, 0, 0, 0)),
                  pl.BlockSpec((None, nc, N_KV_HEADS, CK, LANES), lambda b, i: (b, 0, 0, 0, 0))],
        out_specs=pl.BlockSpec((None, N_HEADS, QB, HEAD_DIM), lambda b, i: (b, 0, i, 0)),
        out_shape=jax.ShapeDtypeStruct((B, N_HEADS, T, HEAD_DIM), BF16),
        scratch_shapes=[pltpu.VMEM((T, QB), F32), pltpu.VMEM((N_KV_HEADS, GROUP * QB, LANES), F32),
                        pltpu.VMEM((N_KV_HEADS, GROUP * QB, LANES), F32)],
        compiler_params=_cparams(("arbitrary", "arbitrary")),
        name="attend_prompt",
    )(qit, wq, q_hm, kib, kt, va)
    return o_hm.transpose(0, 2, 1, 3).reshape(B, T, D_ATTN)


def _sample_score_kernel(pt_ref, *refs, P, n_steps, past_len, kk):
    pages = refs[:P]
    qi_ref, wb_ref, kin_ref, sc_ref, thr_ref = refs[P:P + 5]
    j = pl.program_id(1)
    TQ = sc_ref.shape[1]
    NBLK = sc_ref.shape[0]
    qi = qi_ref[...]
    wb = wb_ref[...]

    def block_scores(kpage_t):
        s = jnp.dot(qi, kpage_t.astype(BF16), preferred_element_type=F32)
        s = jnp.maximum(s, 0.0) * wb
        return jnp.sum(s.reshape(IDX_HEADS, TQ, LANES), axis=0)

    for r in range(P):
        sc_ref[j * P + r] = block_scores(pages[r][...])

    @pl.when(j == n_steps - 1)
    def _():
        tot = block_scores(kin_ref[...])
        qrow = lax.broadcasted_iota(jnp.int32, (TQ, LANES), 0)
        kcol = lax.broadcasted_iota(jnp.int32, (TQ, LANES), 1)
        sc_ref[NBLK - 1] = jnp.where(kcol <= qrow, tot, NEG_INF)

        def lane_bcast(a):
            return jnp.broadcast_to(a, (TQ, LANES))

        sc = sc_ref[...]
        smax = lane_bcast(jnp.max(jnp.max(sc, axis=0), axis=1, keepdims=True))
        smin = lane_bcast(jnp.min(jnp.min(jnp.where(sc > NEG_INF, sc, BIG), axis=0), axis=1, keepdims=True))
        nvalid = (past_len + qrow + 1).astype(F32)

        def count(cand, strict):
            blk = sc_ref[...]
            hit = (blk > cand[None]) if strict else (blk >= cand[None])
            c = jnp.sum(jnp.where(hit, 1.0, 0.0), axis=0)
            return lane_bcast(jnp.sum(c, axis=1, keepdims=True))

        lo, clo, chi = _kth_bisect(count, smin, smax, nvalid, float(kk), steps=4)
        thr_ref[...] = lo

        fix = (clo > kk) & (nvalid > kk)

        @pl.when(jnp.max(fix.astype(F32)) > 0.5)
        def _():
            keep = kk - chi
            tri = (lax.broadcasted_iota(jnp.int32, (LANES, LANES), 0)
                   <= lax.broadcasted_iota(jnp.int32, (LANES, LANES), 1)).astype(BF16)

            def body(c, run):
                blk = sc_ref[c]
                tie = fix & (blk == lo)
                rank = run + jnp.dot(tie.astype(BF16), tri, preferred_element_type=F32)
                sc_ref[c] = jnp.where(tie & (rank > keep), NEG_INF, blk)
                return run + lane_bcast(jnp.sum(tie.astype(F32), axis=1, keepdims=True))
            lax.fori_loop(0, NBLK, body, jnp.zeros((TQ, LANES), F32))


def _sample_scores(page_table, cache_kidx_t, l, qi_rows, w_rows, ki_new_t, *, P):
    NB, n_pages = page_table.shape
    TQ = qi_rows.shape[1] // IDX_HEADS
    n_steps = n_pages // P
    past_len = n_pages * PAGE_SIZE
    kk = min(TOPK_MAX, (past_len + TQ) // 4)
    NBLK = n_pages + 1

    def page_spec(r):
        return pl.BlockSpec((None, None, IDX_DIM, PAGE_SIZE), lambda b, j, pt: (l, pt[b, j * P + r], 0, 0))

    kern = functools.partial(_sample_score_kernel, P=P, n_steps=n_steps, past_len=past_len, kk=kk)
    grid_spec = pltpu.PrefetchScalarGridSpec(
        num_scalar_prefetch=1,
        grid=(NB, n_steps),
        in_specs=[page_spec(r) for r in range(P)] + [
            pl.BlockSpec((None, IDX_HEADS * TQ, IDX_DIM), lambda b, j, pt: (b, 0, 0)),
            pl.BlockSpec((None, IDX_HEADS * TQ, LANES), lambda b, j, pt: (b, 0, 0)),
            pl.BlockSpec((None, IDX_DIM, PAGE_SIZE), lambda b, j, pt: (b, 0, 0))],
        out_specs=[pl.BlockSpec((None, NBLK, TQ, LANES), lambda b, j, pt: (b, 0, 0, 0)),
                   pl.BlockSpec((None, TQ, LANES), lambda b, j, pt: (b, 0, 0))],
    )
    return pl.pallas_call(
        kern,
        grid_spec=grid_spec,
        out_shape=[jax.ShapeDtypeStruct((NB, NBLK, TQ, LANES), F32), jax.ShapeDtypeStruct((NB, TQ, LANES), F32)],
        compiler_params=_cparams(("arbitrary", "arbitrary")),
        name="sample_scores",
    )(page_table, *([cache_kidx_t] * P), qi_rows, w_rows, ki_new_t)


def _sample_attend_kernel(pt_ref, *refs, P, n_steps):
    kpages = refs[:P]
    vpages = refs[P:2 * P]
    sc_ref, scn_ref, thr_ref, q_ref, kn_ref, vn_ref, o_ref, m_ref, l_ref, acc_ref = refs[2 * P:2 * P + 10]
    j = pl.program_id(1)
    qz = q_ref[...]
    thr = thr_ref[...]

    @pl.when(j == 0)
    def _():
        m_ref[...] = jnp.full_like(m_ref, NEG_INF)
        l_ref[...] = jnp.zeros_like(l_ref)
        acc_ref[...] = jnp.zeros_like(acc_ref)

    def update(kblocks, vblocks, scblocks):
        s_parts = []
        for kb, scb in zip(kblocks, scblocks):
            s = jnp.dot(qz, kb.astype(BF16), preferred_element_type=F32)
            bias = jnp.where(scb >= thr, 0.0, NEG_INF)
            s_parts.append(s + jnp.concatenate([bias] * N_HEADS, axis=0))
        s = jnp.concatenate(s_parts, axis=1) if len(s_parts) > 1 else s_parts[0]
        m_old = m_ref[...]
        mnew = jnp.maximum(m_old, jnp.broadcast_to(jnp.max(s, axis=1, keepdims=True), m_old.shape))
        msafe = jnp.where(mnew == NEG_INF, 0.0, mnew)
        alpha = jnp.exp2(m_old - msafe)
        p = jnp.exp2(s - jnp.concatenate([msafe] * len(s_parts), axis=1) if len(s_parts) > 1 else s - msafe)
        l_ref[...] = alpha * l_ref[...] + jnp.broadcast_to(jnp.sum(p, axis=1, keepdims=True), m_old.shape)
        m_ref[...] = mnew
        pv = jnp.zeros(acc_ref.shape, F32)
        for r, vb in enumerate(vblocks):
            pv = pv + lax.dot_general(p[:, r * LANES:(r + 1) * LANES].astype(BF16), vb.astype(BF16),
                                      (((1,), (1,)), ((), ())), preferred_element_type=F32)
        acc_ref[...] = alpha * acc_ref[...] + pv

    update([kp[...] for kp in kpages], [vp[...] for vp in vpages], [sc_ref[r] for r in range(P)])

    @pl.when(j == n_steps - 1)
    def _():
        update([kn_ref[...]], [vn_ref[...]], [scn_ref[0]])
        o_ref[...] = acc_ref[...] / l_ref[...]


def _sample_attend(page_table, cache_k_t, cache_v_t, l, sc, thr, q_rows, k_new_t, v_new_t, *, P):
    NB, n_pages = page_table.shape
    R = q_rows.shape[1]
    TQ = R // N_HEADS
    n_steps = n_pages // P

    def page_spec(r):
        return pl.BlockSpec((None, None, KV_DIM, PAGE_SIZE), lambda b, j, pt: (l, pt[b, j * P + r], 0, 0))

    kern = functools.partial(_sample_attend_kernel, P=P, n_steps=n_steps)
    grid_spec = pltpu.PrefetchScalarGridSpec(
        num_scalar_prefetch=1,
        grid=(NB, n_steps),
        in_specs=[page_spec(r) for r in range(P)] + [page_spec(r) for r in range(P)] + [
            pl.BlockSpec((None, P, TQ, LANES), lambda b, j, pt: (b, j, 0, 0)),
            pl.BlockSpec((None, 1, TQ, LANES), lambda b, j, pt: (b, n_pages, 0, 0)),
            pl.BlockSpec((None, TQ, LANES), lambda b, j, pt: (b, 0, 0)),
            pl.BlockSpec((None, R, KV_DIM), lambda b, j, pt: (b, 0, 0)),
            pl.BlockSpec((None, KV_DIM, PAGE_SIZE), lambda b, j, pt: (b, 0, 0)),
            pl.BlockSpec((None, KV_DIM, PAGE_SIZE), lambda b, j, pt: (b, 0, 0))],
        out_specs=pl.BlockSpec((None, R, KV_DIM), lambda b, j, pt: (b, 0, 0)),
        scratch_shapes=[pltpu.VMEM((R, LANES), F32), pltpu.VMEM((R, LANES), F32), pltpu.VMEM((R, KV_DIM), F32)],
    )
    return pl.pallas_call(
        kern,
        grid_spec=grid_spec,
        out_shape=jax.ShapeDtypeStruct((NB, R, KV_DIM), F32),
        compiler_params=_cparams(("arbitrary", "arbitrary")),
        name="sample_attend",
    )(page_table, *([cache_k_t] * P), *([cache_v_t] * P), sc, sc, thr, q_rows, k_new_t, v_new_t)


def _mix_out_kernel(x_ref, g_ref, wg_ref, ca_ref, ao_ref, py_ref, wco_ref, wao_ref, wpo_ref, wo_ref, o_ref):
    x = x_ref[...]
    ms = jnp.mean(x * x, axis=-1, keepdims=True)
    h = (x * lax.rsqrt(ms + RMS_EPS) * g_ref[...]).astype(BF16)
    gates = jax.nn.sigmoid(jnp.dot(h, wg_ref[...], preferred_element_type=F32))
    y_a = jnp.dot(ca_ref[...], wco_ref[...], preferred_element_type=F32)
    y_b = jnp.dot(ao_ref[...], wao_ref[...], preferred_element_type=F32)
    y_c = jnp.dot(py_ref[...], wpo_ref[...], preferred_element_type=F32)
    merged = (gates[:, 0:D_MODEL] * y_a + gates[:, D_MODEL:2 * D_MODEL] * y_b
              + gates[:, 2 * D_MODEL:3 * D_MODEL] * y_c)
    o_ref[...] = x + jnp.dot(merged.astype(BF16), wo_ref[...], preferred_element_type=F32)


def _mix_out(x2, ca, ao, py, lw, *, TM):
    N = x2.shape[0]

    def tspec(c):
        return pl.BlockSpec((TM, c), lambda i: (i, 0))

    return pl.pallas_call(
        _mix_out_kernel,
        grid=(N // TM,),
        in_specs=[tspec(D_MODEL), _const_spec((1, D_MODEL)), _const_spec((D_MODEL, N_BRANCH * D_MODEL)),
                  tspec(D_CONV), tspec(D_ATTN), tspec(D_POOL),
                  _const_spec((D_CONV, D_MODEL)), _const_spec((D_ATTN, D_MODEL)), _const_spec((D_POOL, D_MODEL)),
                  _const_spec((D_MODEL, D_MODEL))],
        out_specs=tspec(D_MODEL),
        out_shape=jax.ShapeDtypeStruct((N, D_MODEL), F32),
        compiler_params=_cparams(("arbitrary",)),
        name="mix_out",
    )(x2, lw["g_mix"], lw["w_g"], ca, ao, py, lw["w_conv_out"], lw["w_attn_out"], lw["w_pool_out"], lw["w_out"])


def _ffn_kernel(x_ref, g_ref, w1_ref, w2_ref, gf_ref, o_ref, *, final):
    x = x_ref[...]
    ms = jnp.mean(x * x, axis=-1, keepdims=True)
    h = (x * lax.rsqrt(ms + RMS_EPS) * g_ref[...]).astype(BF16)
    r = jnp.maximum(jnp.dot(h, w1_ref[...], preferred_element_type=F32), 0.0)
    y = x + jnp.dot((r * r).astype(BF16), w2_ref[...], preferred_element_type=F32)
    if final:
        ms2 = jnp.mean(y * y, axis=-1, keepdims=True)
        y = y * lax.rsqrt(ms2 + RMS_EPS) * gf_ref[...]
    o_ref[...] = y


def _ffn(x2, lw, g_final, *, TM, final):
    N = x2.shape[0]
    once = pl.Buffered(1)
    return pl.pallas_call(
        functools.partial(_ffn_kernel, final=final),
        grid=(N // TM,),
        in_specs=[pl.BlockSpec((TM, D_MODEL), lambda i: (i, 0)), _const_spec((1, D_MODEL)),
                  pl.BlockSpec((D_MODEL, D_FF), lambda i: (0, 0), pipeline_mode=once),
                  pl.BlockSpec((D_FF, D_MODEL), lambda i: (0, 0), pipeline_mode=once),
                  _const_spec((1, D_MODEL))],
        out_specs=pl.BlockSpec((TM, D_MODEL), lambda i: (i, 0)),
        out_shape=jax.ShapeDtypeStruct((N, D_MODEL), F32),
        compiler_params=_cparams(("arbitrary",)),
        name="ffn",
    )(x2, lw["g_ffn"], lw["w_ff1"], lw["w_ff2"], g_final)


def _layer_weights(l, g_mix, w_in, conv_w, conv_b, conv_ln_g, conv_ln_b, w_conv_out, w_attn_out,
                   pool_w, pool_scale, w_pool_out, w_out, g_ffn, w_ff1, w_ff2):
    W = w_in[l]
    o_ki = C_QI + IDX_HEADS * IDX_DIM
    o_wi = o_ki + IDX_DIM
    o_xc = o_wi + IDX_HEADS
    o_g = o_xc + D_POOL

    def pad_to_tile(a):
        return jnp.pad(a, ((0, 0), (0, LANES - a.shape[1])))

    w_a = jnp.concatenate([W[:, :o_ki], W[:, o_xc:o_g], pad_to_tile(W[:, o_ki:o_wi]),
                           pad_to_tile(W[:, o_wi:o_xc])], axis=1).astype(BF16)
    pw = jnp.zeros((D_POOL, D_POOL), F32)
    for g in range(len(POOL_WINDOWS)):
        pw = pw.at[g * POOL_GC:(g + 1) * POOL_GC, g * POOL_GC:(g + 1) * POOL_GC].set(pool_w[l, g])
    return dict(
        g_mix=g_mix[l][None], w_a=w_a, w_g=W[:, o_g:].astype(BF16),
        conv_w=jnp.pad(conv_w[l], ((0, CONV_PAD - CONV_W), (0, 0))), conv_b=conv_b[l][None],
        ln_g=conv_ln_g[l][None], ln_b=conv_ln_b[l][None],
        pool_w=pw.astype(BF16), pool_scale=pool_scale[l][None],
        w_conv_out=w_conv_out[l].astype(BF16), w_attn_out=w_attn_out[l].astype(BF16),
        w_pool_out=w_pool_out[l].astype(BF16), w_out=w_out[l].astype(BF16),
        g_ffn=g_ffn[l][None], w_ff1=w_ff1[l].astype(BF16), w_ff2=w_ff2[l].astype(BF16))


def _pad_hist(hist, rows):
    return jnp.pad(hist, ((0, 0), (rows - hist.shape[1], 0), (0, 0)))


def _tail_state(hist, new, keep):
    return jnp.concatenate([hist, new], axis=1)[:, -keep:]


def kernel(x_prompt, x_sample, cache_k, cache_v, cache_kidx, state_conv, state_pool, page_table, g_mix, w_in,
           conv_w, conv_b, conv_ln_g, conv_ln_b, w_conv_out, w_attn_out, pool_w, pool_scale, w_pool_out, w_out,
           g_ffn, w_ff1, w_ff2, g_final):
    B, T, _ = x_prompt.shape
    NB, TS, _ = x_sample.shape
    depth = w_in.shape[0]
    n_pool = cache_k.shape[1]
    past_len = page_table.shape[1] * PAGE_SIZE
    TM_P = 256
    gf = g_final[None]
    xp = x_prompt
    xs = x_sample
    zero_conv = jnp.zeros((B, CONV_PAD, D_CONV), F32)
    zero_pool = jnp.zeros((B, POOL_PAD, D_POOL), F32)
    cache_k_t = cache_k.transpose(0, 1, 3, 4, 2).reshape(depth, n_pool, KV_DIM, PAGE_SIZE)
    cache_v_t = cache_v.transpose(0, 1, 3, 4, 2).reshape(depth, n_pool, KV_DIM, PAGE_SIZE)
    cache_kidx_t = cache_kidx.transpose(0, 1, 3, 2)
    st_p = [[] for _ in range(5)]
    st_s = [[] for _ in range(5)]
    for l in range(depth):
        lw = _layer_weights(l, g_mix, w_in, conv_w, conv_b, conv_ln_g, conv_ln_b, w_conv_out, w_attn_out,
                            pool_w, pool_scale, w_pool_out, w_out, g_ffn, w_ff1, w_ff2)
        final = l == depth - 1

        (k, v, ki, u, xc, kbf, vbf, kibf, q, qi, wi, ca, py) = _proj(xp, 0, zero_conv, zero_pool, lw, G=1, TM=TM_P,
                                                                      adt=BF16)
        ao = _attend_prompt(q, qi, wi, kbf, vbf, kibf, CKS=min(512, T), CK=min(512, T))
        x2 = _mix_out(xp.reshape(B * T, D_MODEL), ca.reshape(B * T, D_CONV), ao.reshape(B * T, D_ATTN),
                      py.reshape(B * T, D_POOL), lw, TM=TM_P)
        xp = _ffn(x2, lw, gf, TM=TM_P, final=final).reshape(B, T, D_MODEL)
        st_p[0].append(k.reshape(B, T, N_KV_HEADS, HEAD_DIM))
        st_p[1].append(v.reshape(B, T, N_KV_HEADS, HEAD_DIM))
        st_p[2].append(ki)
        st_p[3].append(u[:, T - CONV_HIST:])
        st_p[4].append(xc[:, T - POOL_HIST:])

        (k, v, ki, u, xc, _, _, _, q, qi, wi, ca, py) = _proj(
            xs, past_len, _pad_hist(state_conv[l], CONV_PAD), _pad_hist(state_pool[l], POOL_PAD), lw, G=NB, TM=TS,
            adt=F32)
        q, qi, ca, py = q.astype(BF16), qi.astype(BF16), ca.astype(BF16), py.astype(BF16)
        qi_rows = qi.reshape(NB, TS, IDX_HEADS, IDX_DIM).transpose(0, 2, 1, 3).reshape(NB, IDX_HEADS * TS, IDX_DIM)
        w_rows = jnp.broadcast_to(wi[:, :, :IDX_HEADS].transpose(0, 2, 1).reshape(NB, IDX_HEADS * TS, 1),
                                  (NB, IDX_HEADS * TS, LANES))
        qh = q.reshape(NB, TS, N_HEADS, HEAD_DIM).transpose(0, 2, 1, 3)
        head_kv = (jnp.arange(N_HEADS) // GROUP)[None, :, None, None]
        q_rows = jnp.concatenate([jnp.where(head_kv == n, qh, jnp.zeros_like(qh)) for n in range(N_KV_HEADS)],
                                 axis=-1).reshape(NB, N_HEADS * TS, KV_DIM)
        pad_rows = ((0, 0), (0, PAGE_SIZE - TS), (0, 0))
        pages_per_step = min(16, page_table.shape[1])
        def new_page_t(a):
            return jnp.pad(a, pad_rows).transpose(0, 2, 1)

        sc, thr = _sample_scores(page_table, cache_kidx_t, l, qi_rows, w_rows, new_page_t(ki), P=pages_per_step)
        o_rows = _sample_attend(page_table, cache_k_t, cache_v_t, l, sc, thr, q_rows, new_page_t(k), new_page_t(v),
                                P=pages_per_step)
        o5 = o_rows.reshape(NB, N_KV_HEADS, GROUP, TS, N_KV_HEADS, HEAD_DIM)
        ao = jnp.stack([o5[:, n, :, :, n, :] for n in range(N_KV_HEADS)], axis=1)
        ao = ao.reshape(NB, N_HEADS, TS, HEAD_DIM).transpose(0, 2, 1, 3).reshape(NB * TS, D_ATTN).astype(BF16)
        x2 = _mix_out(xs.reshape(NB * TS, D_MODEL), ca.reshape(NB * TS, D_CONV), ao,
                      py.reshape(NB * TS, D_POOL), lw, TM=NB * TS)
        xs = _ffn(x2, lw, gf, TM=NB * TS, final=final).reshape(NB, TS, D_MODEL)
        st_s[0].append(k.reshape(NB, TS, N_KV_HEADS, HEAD_DIM))
        st_s[1].append(v.reshape(NB, TS, N_KV_HEADS, HEAD_DIM))
        st_s[2].append(ki)
        st_s[3].append(_tail_state(state_conv[l], u, CONV_HIST))
        st_s[4].append(_tail_state(state_pool[l], xc, POOL_HIST))

    sp = [jnp.stack(a, axis=0) for a in st_p]
    ss = [jnp.stack(a, axis=0) for a in st_s]
    return (xp, xs, sp[0], sp[1], sp[2], sp[3], sp[4], ss[0], ss[1], ss[2], ss[3], ss[4])
```

```python
import functools

import jax
import jax.numpy as jnp
from jax import lax
from jax.experimental import pallas as pl
from jax.experimental.pallas import tpu as pltpu

F32 = jnp.float32
BF16 = jnp.bfloat16

D_MODEL = 1024
D_CONV = 256
CONV_W = 31
CONV_HIST = CONV_W - 1
HEAD_DIM = 64
D_ATTN = 512
N_HEADS = 8
N_KV_HEADS = 2
KV_DIM = N_KV_HEADS * HEAD_DIM
GROUP = N_HEADS // N_KV_HEADS
IDX_HEADS = 8
IDX_DIM = 64
TOPK_MAX = 256
D_POOL = 256
POOL_WINDOWS = (2, 4, 8, 16)
POOL_GC = D_POOL // len(POOL_WINDOWS)
POOL_HIST = max(POOL_WINDOWS) - 1
N_BRANCH = 3
D_FF = 4 * D_MODEL
ROPE_THETA = 10000.0
RMS_EPS = 1e-6
LN_EPS = 1e-5
PAGE_SIZE = 128

LANES = 128
CONV_PAD = 32
POOL_PAD = 16
Q_SCALE = HEAD_DIM ** -0.5 * 1.4426950408889634
BIG = 3.0e38
NEG_INF = float("-inf")

C_AIN, C_AGATE, C_Q, C_K, C_V, C_QI, C_XC, C_KI, C_WI, D_A = 0, 256, 512, 1024, 1152, 1280, 1792, 2048, 2176, 2304

VMEM_LIMIT = 56 * 1024 * 1024


def _cparams(sem):
    return pltpu.CompilerParams(dimension_semantics=sem, vmem_limit_bytes=VMEM_LIMIT)


def _layer_spec(shape, l, **kw):
    n = len(shape)
    return pl.BlockSpec((None,) + tuple(shape), lambda *_: (l,) + (0,) * n, **kw)


def _proj_kernel(x_ref, g_ref, w_ref, cos_ref, sin_ref, chist_ref, phist_ref, convw_ref, convb_ref,
                 lng_ref, lnb_ref, poolw_ref, pscale_ref,
                 k_ref, v_ref, ki_ref, u_ref, xc_ref, kibf_ref, q_ref, qi_ref, wi_ref, ca_ref, py_ref,
                 *rest, G, TM, pos0, n_tiles, attn):
    if attn:
        qit_ref, wq_ref, kt_ref, va_ref = rest[:4]
        rest = rest[4:]
    cs_ref, ps_ref, p2_ref, p4_ref, p8_ref, p16_ref = rest
    j = pl.program_id(1)
    R = G * TM
    x = x_ref[...].reshape(R, D_MODEL)
    ms = jnp.mean(x * x, axis=-1, keepdims=True)
    h = (x * lax.rsqrt(ms + RMS_EPS) * g_ref[...]).astype(BF16)
    z = jnp.dot(h, w_ref[...], preferred_element_type=F32)

    cos = cos_ref[...]
    sin = sin_ref[...]
    if G > 1:
        cos = jnp.concatenate([cos] * G, axis=0)
        sin = jnp.concatenate([sin] * G, axis=0)

    def rope(t):
        n = t.shape[1] // LANES
        c = jnp.concatenate([cos] * n, axis=1) if n > 1 else cos
        s = jnp.concatenate([sin] * n, axis=1) if n > 1 else sin
        lane = lax.broadcasted_iota(jnp.int32, t.shape, 1)
        first = (lane & (HEAD_DIM - 1)) < HEAD_DIM // 2
        sw = jnp.where(first, pltpu.roll(t, t.shape[1] - HEAD_DIM // 2, 1), pltpu.roll(t, HEAD_DIM // 2, 1))
        return t * c + sw * s

    q = rope(z[:, C_Q:C_Q + D_ATTN]) * Q_SCALE
    k = rope(z[:, C_K:C_K + KV_DIM])
    v = z[:, C_V:C_V + KV_DIM]
    qi = rope(z[:, C_QI:C_QI + IDX_HEADS * IDX_DIM])
    ki = rope(z[:, C_KI:C_KI + LANES])[:, :IDX_DIM]
    wi = z[:, C_WI:C_WI + LANES] * (IDX_HEADS ** -0.5 * IDX_DIM ** -0.5)

    k_ref[...] = k.reshape(G, TM, KV_DIM)
    v_ref[...] = v.reshape(G, TM, KV_DIM)
    ki_ref[...] = ki.reshape(G, TM, IDX_DIM)
    kibf_ref[...] = ki.reshape(G, TM, IDX_DIM).astype(kibf_ref.dtype)
    q_ref[...] = q.reshape(G, TM, D_ATTN).astype(q_ref.dtype)
    qi_ref[...] = qi.reshape(G, TM, IDX_HEADS * IDX_DIM).astype(qi_ref.dtype)
    wi_ref[...] = wi.reshape(G, TM, LANES)
    if attn:
        for b in range(TM // LANES):
            blk = slice(b * LANES, (b + 1) * LANES)
            qi_t = qi[blk, :].T
            qit_ref[b] = jnp.concatenate([qi_t[hh * IDX_DIM:(hh + 1) * IDX_DIM, :] for hh in range(IDX_HEADS)],
                                         axis=1).astype(BF16)
            wi_t = wi[blk, :].T
            wq_ref[b] = jnp.concatenate([wi_t[hh:hh + 1, :] for hh in range(IDX_HEADS)], axis=1)
        k_t = k.T
        lane = lax.broadcasted_iota(jnp.int32, (TM, KV_DIM), 1)
        tail = jnp.where(lane == HEAD_DIM, 1.0, 0.0)
        for n in range(N_KV_HEADS):
            kt_ref[0, n] = k_t[n * HEAD_DIM:(n + 1) * HEAD_DIM, :].astype(BF16)
            vn = v if n == 0 else pltpu.roll(v, KV_DIM - n * HEAD_DIM, 1)
            va_ref[0, n] = jnp.where(lane < HEAD_DIM, vn, tail).astype(BF16)

    a_in = z[:, C_AIN:C_AIN + D_CONV]
    a_gate = z[:, C_AGATE:C_AGATE + D_CONV]
    u = (a_in * jax.nn.sigmoid(a_gate)).reshape(G, TM, D_CONV)
    u_ref[...] = u

    @pl.when(j == 0)
    def _():
        cs_ref[:, 0:CONV_PAD, :] = chist_ref[...]
        ps_ref[:, 0:POOL_PAD, :] = phist_ref[...]

    cs_ref[:, CONV_PAD:CONV_PAD + TM, :] = u
    off = CONV_PAD - CONV_HIST
    rc = min(TM, 64)
    gc = min(G, 8)
    for g0 in range(0, G, gc):
        for r0 in range(0, TM, rc):
            acc = jnp.zeros((gc, rc, D_CONV), F32) + convb_ref[...][None]
            for t in range(CONV_W):
                acc = acc + convw_ref[t:t + 1, :][None] * cs_ref[g0:g0 + gc, off + t + r0:off + t + r0 + rc, :]
            mu = jnp.mean(acc, axis=-1, keepdims=True)
            var = jnp.mean(jnp.square(acc - mu), axis=-1, keepdims=True)
            y = (acc - mu) * lax.rsqrt(var + LN_EPS) * lng_ref[...][None] + lnb_ref[...][None]
            ca_ref[g0:g0 + gc, r0:r0 + rc, :] = (y * jax.nn.sigmoid(y)).astype(ca_ref.dtype)
    if n_tiles > 1:
        cs_ref[:, 0:CONV_PAD, :] = cs_ref[:, TM:TM + CONV_PAD, :]

    xc = z[:, C_XC:C_XC + D_POOL].reshape(G, TM, D_POOL)
    xc_ref[...] = xc
    ps_ref[:, POOL_PAD:POOL_PAD + TM, :] = xc
    RR = POOL_PAD + TM
    p2_ref[:, 1:RR, :] = ps_ref[:, 1:RR, :] + ps_ref[:, 0:RR - 1, :]
    p4_ref[:, 3:RR, :] = p2_ref[:, 3:RR, :] + p2_ref[:, 1:RR - 2, :]
    p8_ref[:, 7:RR, :] = p4_ref[:, 7:RR, :] + p4_ref[:, 3:RR - 4, :]
    p16_ref[:, 15:RR, :] = p8_ref[:, 15:RR, :] + p8_ref[:, 7:RR - 8, :]
    lane = lax.broadcasted_iota(jnp.int32, (G, TM, D_POOL), 2)
    row = lax.broadcasted_iota(jnp.int32, (G, TM, D_POOL), 1)
    g0m, g1m, g2m = lane < POOL_GC, lane < 2 * POOL_GC, lane < 3 * POOL_GC
    s_tok = slice(POOL_PAD, POOL_PAD + TM)
    wsum = jnp.where(g0m, p2_ref[:, s_tok, :],
                     jnp.where(g1m, p4_ref[:, s_tok, :], jnp.where(g2m, p8_ref[:, s_tok, :], p16_ref[:, s_tok, :])))
    win = jnp.where(g0m, POOL_WINDOWS[0], jnp.where(g1m, POOL_WINDOWS[1],
                                                     jnp.where(g2m, POOL_WINDOWS[2], POOL_WINDOWS[3])))
    cnt = jnp.minimum(pos0 + j * TM + row + 1, win).astype(F32)
    d = (wsum / cnt - xc).reshape(R, D_POOL).astype(BF16)
    py = jnp.dot(d, poolw_ref[...], preferred_element_type=F32) * pscale_ref[...]
    py_ref[...] = py.reshape(G, TM, D_POOL).astype(py_ref.dtype)
    if n_tiles > 1:
        ps_ref[:, 0:POOL_PAD, :] = ps_ref[:, TM:TM + POOL_PAD, :]


def _rope_tables(pos0, T):
    half = HEAD_DIM // 2
    pos = (pos0 + jnp.arange(T)).astype(F32)
    inv = ROPE_THETA ** (-jnp.arange(half, dtype=F32) / half)
    ang = pos[:, None] * inv[None, :]
    cos = jnp.tile(jnp.cos(ang), (1, 2 * LANES // HEAD_DIM))
    sin = jnp.tile(jnp.concatenate([-jnp.sin(ang), jnp.sin(ang)], axis=1), (1, LANES // HEAD_DIM))
    return cos, sin


def _proj(x3, pos0, tables, chist, phist, sw, l, *, G, TM, adt, attn):
    NS, T, _ = x3.shape
    n_tiles = T // TM
    cos, sin = tables

    def tok(c, dt=F32):
        return jax.ShapeDtypeStruct((NS, T, c), dt)

    def tspec(c):
        return pl.BlockSpec((G, TM, c), lambda s, j: (s, j, 0))

    out_cols = [(KV_DIM, F32), (KV_DIM, F32), (IDX_DIM, F32), (D_CONV, F32), (D_POOL, F32),
                (IDX_DIM, adt), (D_ATTN, adt), (IDX_HEADS * IDX_DIM, adt), (LANES, F32), (D_CONV, adt), (D_POOL, adt)]
    out_specs = [tspec(c) for c, _ in out_cols]
    out_shape = [tok(c, dt) for c, dt in out_cols]
    if attn:
        assert G == 1 and TM % LANES == 0
        nb = TM // LANES
        out_specs += [pl.BlockSpec((None, nb, IDX_DIM, IDX_HEADS * LANES), lambda s, j: (s, j, 0, 0)),
                      pl.BlockSpec((None, nb, 1, IDX_HEADS * LANES), lambda s, j: (s, j, 0, 0)),
                      pl.BlockSpec((None, 1, N_KV_HEADS, HEAD_DIM, TM), lambda s, j: (s, j, 0, 0, 0)),
                      pl.BlockSpec((None, 1, N_KV_HEADS, TM, LANES), lambda s, j: (s, j, 0, 0, 0))]
        out_shape += [jax.ShapeDtypeStruct((NS, T // LANES, IDX_DIM, IDX_HEADS * LANES), BF16),
                      jax.ShapeDtypeStruct((NS, T // LANES, 1, IDX_HEADS * LANES), F32),
                      jax.ShapeDtypeStruct((NS, n_tiles, N_KV_HEADS, HEAD_DIM, TM), BF16),
                      jax.ShapeDtypeStruct((NS, n_tiles, N_KV_HEADS, TM, LANES), BF16)]
    kern = functools.partial(_proj_kernel, G=G, TM=TM, pos0=pos0, n_tiles=n_tiles, attn=attn)
    return pl.pallas_call(
        kern,
        grid=(NS // G, n_tiles),
        in_specs=[tspec(D_MODEL), _layer_spec((1, D_MODEL), l), _layer_spec((D_MODEL, D_A), l),
                  pl.BlockSpec((TM, LANES), lambda s, j: (j, 0)), pl.BlockSpec((TM, LANES), lambda s, j: (j, 0)),
                  pl.BlockSpec((G, CONV_PAD, D_CONV), lambda s, j: (s, 0, 0)),
                  pl.BlockSpec((G, POOL_PAD, D_POOL), lambda s, j: (s, 0, 0)),
                  _layer_spec((CONV_PAD, D_CONV), l), _layer_spec((1, D_CONV), l), _layer_spec((1, D_CONV), l),
                  _layer_spec((1, D_CONV), l), _layer_spec((D_POOL, D_POOL), l), _layer_spec((1, D_POOL), l)],
        out_specs=out_specs,
        out_shape=out_shape,
        scratch_shapes=[pltpu.VMEM((G, CONV_PAD + TM, D_CONV), F32)] +
                       [pltpu.VMEM((G, POOL_PAD + TM, D_POOL), F32)] * 5,
        compiler_params=_cparams(("arbitrary", "arbitrary")),
        name="proj",
    )(x3, sw["g_mix"], sw["w_a"], cos, sin, chist, phist, sw["conv_w"], sw["conv_b"], sw["ln_g"], sw["ln_b"],
      sw["pool_w"], sw["pool_scale"])


def _kth_bisect(count, smin, smax, nvalid, kk, steps, zero_counts=None):
    zero = jnp.zeros_like(smin)
    c_ge0, c_gt0 = zero_counts if zero_counts is not None else (count(zero, False), count(zero, True))
    pos = c_gt0 >= kk
    neg = c_ge0 < kk
    lo0 = jnp.where(neg, smin, zero)
    clo0 = jnp.where(neg, nvalid, c_ge0)
    hi0 = jnp.where(pos, BIG, zero)
    chi0 = jnp.where(pos, zero, jnp.where(neg, c_ge0, c_gt0))
    cand0 = jnp.where(pos, smax, smin * 0.5)
    small = nvalid <= kk
    done0 = (small | jnp.logical_not(pos | neg)).astype(F32)
    init = (jnp.where(small, smin, lo0), hi0, jnp.where(small, nvalid, clo0), chi0, cand0, done0)

    def count_ge(cand):
        return count(cand, False)

    def cond(st):
        return jnp.min(st[5]) < 0.5

    def body(st):
        for _ in range(steps):
            st = step(st)
        return st

    def step(st):
        lo, hi, clo, chi, cand, done = st
        c = count_ge(cand)
        act = done < 0.5
        ge = c >= kk
        up = act & ge
        dn = act & jnp.logical_not(ge)
        lo = jnp.where(up, cand, lo)
        clo = jnp.where(up, c, clo)
        hi = jnp.where(dn, cand, hi)
        chi = jnp.where(dn, c, chi)
        mid = lo * 0.5 + hi * 0.5
        inside = (mid > lo) & (mid < hi)
        fin = (clo == kk) | jnp.logical_not(inside)
        done = jnp.where(fin, 1.0, done)
        return lo, hi, clo, chi, mid, done

    lo, _, clo, chi, _, _ = lax.while_loop(cond, body, init)
    return lo, clo, chi


def _attend_prompt_kernel(qit_ref, wq_ref, q_ref, ki_ref, kt_ref, va_ref, o_ref, sc_ref, acc_ref, m_ref, *,
                          CKS, CK, kk):
    i = pl.program_id(1)
    QB = LANES
    nchs = (i * QB + QB + CKS - 1) // CKS
    nch = nchs * (CKS // CK)
    qit = qit_ref[...]
    wq = wq_ref[...]
    qcol = i * QB + lax.broadcasted_iota(jnp.int32, (CKS, QB), 1)
    krow0 = lax.broadcasted_iota(jnp.int32, (CKS, QB), 0)

    def score_body(c, carry):
        smax, smin, n_ge0, n_gt0 = carry
        s = jnp.dot(ki_ref[c], qit, preferred_element_type=F32)
        s = jnp.maximum(s, 0.0) * wq
        tot = s[:, 0:QB]
        for hh in range(1, IDX_HEADS):
            tot = tot + s[:, hh * QB:(hh + 1) * QB]
        valid = (krow0 + c * CKS) <= qcol
        masked = jnp.where(valid, tot, NEG_INF)
        sc_ref[pl.ds(pl.multiple_of(c * CKS, CKS), CKS), :] = masked
        smax = jnp.maximum(smax, jnp.max(masked.reshape(CKS // 8, 8, QB), axis=0))
        smin = jnp.minimum(smin, jnp.min(jnp.where(valid, tot, BIG).reshape(CKS // 8, 8, QB), axis=0))
        n_ge0 = n_ge0 + jnp.sum(jnp.where(masked >= 0.0, 1.0, 0.0).reshape(CKS // 8, 8, QB), axis=0)
        n_gt0 = n_gt0 + jnp.sum(jnp.where(masked > 0.0, 1.0, 0.0).reshape(CKS // 8, 8, QB), axis=0)
        return smax, smin, n_ge0, n_gt0

    smax, smin, n_ge0, n_gt0 = lax.fori_loop(
        0, nchs, score_body, (jnp.full((8, QB), NEG_INF, F32), jnp.full((8, QB), BIG, F32),
                              jnp.zeros((8, QB), F32), jnp.zeros((8, QB), F32)))
    smax = jnp.max(smax, axis=0, keepdims=True)
    smin = jnp.min(smin, axis=0, keepdims=True)
    zero_counts = (jnp.sum(n_ge0, axis=0, keepdims=True), jnp.sum(n_gt0, axis=0, keepdims=True))

    n_acc = 4
    rows = CKS // n_acc

    def count(cand, strict):
        def body(c, accs):
            base = pl.multiple_of(c * CKS, CKS)
            out = []
            for a in range(n_acc):
                blk = sc_ref[pl.ds(base + a * rows, rows), :]
                hit = (blk > cand) if strict else (blk >= cand)
                ones = jnp.where(hit, 1.0, 0.0).reshape(rows // 8, 8, QB)
                out.append(accs[a] + jnp.sum(ones, axis=0))
            return tuple(out)
        accs = lax.fori_loop(0, nchs, body, tuple(jnp.zeros((8, QB), F32) for _ in range(n_acc)))
        return jnp.sum((accs[0] + accs[1]) + (accs[2] + accs[3]), axis=0, keepdims=True)

    nvalid = (i * QB + lax.broadcasted_iota(jnp.int32, (1, QB), 1) + 1).astype(F32)
    lo, clo, chi = _kth_bisect(count, smin, smax, nvalid, float(kk), steps=4, zero_counts=zero_counts)

    fix = (clo > kk) & (nvalid > kk)

    @pl.when(jnp.max(fix.astype(F32)) > 0.5)
    def _():
        keep = kk - chi
        TF = min(256, CKS)
        tri = (lax.broadcasted_iota(jnp.int32, (TF, TF), 1)
               <= lax.broadcasted_iota(jnp.int32, (TF, TF), 0)).astype(BF16)

        def body(c, run):
            rows_c = pl.ds(pl.multiple_of(c * TF, TF), TF)
            blk = sc_ref[rows_c, :]
            tie = jnp.where(fix & (blk == lo), 1.0, 0.0)
            rank = run + jnp.dot(tri, tie.astype(BF16), preferred_element_type=F32)
            sc_ref[rows_c, :] = jnp.where((tie > 0.5) & (rank > keep), NEG_INF, blk)
            return run + jnp.sum(tie, axis=0, keepdims=True)
        lax.fori_loop(0, nchs * (CKS // TF), body, jnp.zeros((1, QB), F32))

    GQ = GROUP * QB
    q32 = q_ref[...].astype(F32)
    qn = [jnp.concatenate([q32[:, hh * HEAD_DIM:(hh + 1) * HEAD_DIM] for hh in range(n * GROUP, (n + 1) * GROUP)],
                          axis=0).astype(BF16) for n in range(N_KV_HEADS)]
    acc_ref[...] = jnp.zeros_like(acc_ref)
    m_ref[...] = jnp.full_like(m_ref, NEG_INF)
    reps = CK // LANES

    def att_body(c, _):
        scb = sc_ref[pl.ds(pl.multiple_of(c * CK, CK), CK), :]
        bias_t = jnp.where(scb >= lo, 0.0, NEG_INF).T
        for n in range(N_KV_HEADS):
            s = jnp.dot(qn[n], kt_ref[c, n], preferred_element_type=F32)
            s = (s.reshape(GROUP, QB, CK) + bias_t[None]).reshape(GQ, CK)
            m_old = m_ref[n]
            mnew = jnp.maximum(m_old, jnp.max(s, axis=1, keepdims=True))
            msafe = jnp.where(mnew == NEG_INF, 0.0, mnew)
            p = jnp.exp2(s - jnp.concatenate([msafe] * reps, axis=1))
            alpha = jnp.exp2(m_old - msafe)
            pv = jnp.dot(p.astype(BF16), va_ref[c, n], preferred_element_type=F32)
            acc_ref[n] = alpha * acc_ref[n] + pv
            m_ref[n] = mnew
        return 0

    lax.fori_loop(0, nch, att_body, 0)
    outs = []
    for n in range(N_KV_HEADS):
        a = acc_ref[n]
        o = a[:, :HEAD_DIM] / a[:, HEAD_DIM:HEAD_DIM + 1]
        outs += [o[g * QB:(g + 1) * QB, :] for g in range(GROUP)]
    o_ref[...] = jnp.concatenate(outs, axis=1).astype(BF16)


def _attend_prompt(q, qit, wq, kibf, kt, va, *, CKS):
    B, T, _ = q.shape
    QB = LANES
    nq = T // QB
    nc, CK = kt.shape[1], kt.shape[4]
    ncs = T // CKS
    kk = min(TOPK_MAX, T // 4)
    kib = kibf.reshape(B, ncs, CKS, IDX_DIM)

    kern = functools.partial(_attend_prompt_kernel, CKS=CKS, CK=CK, kk=kk)
    return pl.pallas_call(
        kern,
        grid=(B, nq),
        in_specs=[pl.BlockSpec((None, None, IDX_DIM, 8 * QB), lambda b, i: (b, i, 0, 0)),
                  pl.BlockSpec((None, None, 1, 8 * QB), lambda b, i: (b, i, 0, 0)),
                  pl.BlockSpec((None, QB, D_ATTN), lambda b, i: (b, i, 0)),
                  pl.BlockSpec((None, ncs, CKS, IDX_DIM), lambda b, i: (b, 0, 0, 0)),
                  pl.BlockSpec((None, nc, N_KV_HEADS, HEAD_DIM, CK), lambda b, i: (b, 0, 0, 0, 0)),
                  pl.BlockSpec((None, nc, N_KV_HEADS, CK, LANES), lambda b, i: (b, 0, 0, 0, 0))],
        out_specs=pl.BlockSpec((None, QB, D_ATTN), lambda b, i: (b, i, 0)),
        out_shape=jax.ShapeDtypeStruct((B, T, D_ATTN), BF16),
        scratch_shapes=[pltpu.VMEM((T, QB), F32), pltpu.VMEM((N_KV_HEADS, GROUP * QB, LANES), F32),
                        pltpu.VMEM((N_KV_HEADS, GROUP * QB, LANES), F32)],
        compiler_params=_cparams(("arbitrary", "arbitrary")),
        name="attend_prompt",
    )(qit, wq, q, kib, kt, va)


def _sample_score_kernel(pt_ref, *refs, P, n_steps, past_len, kk):
    pages = refs[:P]
    qi_ref, wb_ref, kin_ref, sc_ref, thr_ref = refs[P:P + 5]
    j = pl.program_id(1)
    TQ = sc_ref.shape[1]
    NBLK = sc_ref.shape[0]
    qi = qi_ref[...]
    wb = wb_ref[...]

    def block_scores(kpage_t):
        s = jnp.dot(qi, kpage_t.astype(BF16), preferred_element_type=F32)
        s = jnp.maximum(s, 0.0) * wb
        return jnp.sum(s.reshape(IDX_HEADS, TQ, LANES), axis=0)

    for r in range(P):
        sc_ref[j * P + r] = block_scores(pages[r][...])

    @pl.when(j == n_steps - 1)
    def _():
        tot = block_scores(kin_ref[...])
        qrow = lax.broadcasted_iota(jnp.int32, (TQ, LANES), 0)
        kcol = lax.broadcasted_iota(jnp.int32, (TQ, LANES), 1)
        sc_ref[NBLK - 1] = jnp.where(kcol <= qrow, tot, NEG_INF)

        def lane_bcast(a):
            return jnp.broadcast_to(a, (TQ, LANES))

        sc = sc_ref[...]
        smax = lane_bcast(jnp.max(jnp.max(sc, axis=0), axis=1, keepdims=True))
        smin = lane_bcast(jnp.min(jnp.min(jnp.where(sc > NEG_INF, sc, BIG), axis=0), axis=1, keepdims=True))
        nvalid = (past_len + qrow + 1).astype(F32)

        def count(cand, strict):
            blk = sc_ref[...]
            hit = (blk > cand[None]) if strict else (blk >= cand[None])
            c = jnp.sum(jnp.where(hit, 1.0, 0.0), axis=0)
            return lane_bcast(jnp.sum(c, axis=1, keepdims=True))

        lo, clo, chi = _kth_bisect(count, smin, smax, nvalid, float(kk), steps=4)
        thr_ref[...] = lo

        fix = (clo > kk) & (nvalid > kk)

        @pl.when(jnp.max(fix.astype(F32)) > 0.5)
        def _():
            keep = kk - chi
            tri = (lax.broadcasted_iota(jnp.int32, (LANES, LANES), 0)
                   <= lax.broadcasted_iota(jnp.int32, (LANES, LANES), 1)).astype(BF16)

            def body(c, run):
                blk = sc_ref[c]
                tie = fix & (blk == lo)
                rank = run + jnp.dot(tie.astype(BF16), tri, preferred_element_type=F32)
                sc_ref[c] = jnp.where(tie & (rank > keep), NEG_INF, blk)
                return run + lane_bcast(jnp.sum(tie.astype(F32), axis=1, keepdims=True))
            lax.fori_loop(0, NBLK, body, jnp.zeros((TQ, LANES), F32))


def _sample_scores(page_table, cache_kidx_t, l, qi_rows, w_rows, ki_new_t, *, P):
    NB, n_pages = page_table.shape
    TQ = qi_rows.shape[1] // IDX_HEADS
    n_steps = n_pages // P
    past_len = n_pages * PAGE_SIZE
    kk = min(TOPK_MAX, (past_len + TQ) // 4)
    NBLK = n_pages + 1

    def page_spec(r):
        return pl.BlockSpec((None, None, IDX_DIM, PAGE_SIZE), lambda b, j, pt: (l, pt[b, j * P + r], 0, 0))

    kern = functools.partial(_sample_score_kernel, P=P, n_steps=n_steps, past_len=past_len, kk=kk)
    grid_spec = pltpu.PrefetchScalarGridSpec(
        num_scalar_prefetch=1,
        grid=(NB, n_steps),
        in_specs=[page_spec(r) for r in range(P)] + [
            pl.BlockSpec((None, IDX_HEADS * TQ, IDX_DIM), lambda b, j, pt: (b, 0, 0)),
            pl.BlockSpec((None, IDX_HEADS * TQ, LANES), lambda b, j, pt: (b, 0, 0)),
            pl.BlockSpec((None, IDX_DIM, PAGE_SIZE), lambda b, j, pt: (b, 0, 0))],
        out_specs=[pl.BlockSpec((None, NBLK, TQ, LANES), lambda b, j, pt: (b, 0, 0, 0)),
                   pl.BlockSpec((None, TQ, LANES), lambda b, j, pt: (b, 0, 0))],
    )
    return pl.pallas_call(
        kern,
        grid_spec=grid_spec,
        out_shape=[jax.ShapeDtypeStruct((NB, NBLK, TQ, LANES), F32), jax.ShapeDtypeStruct((NB, TQ, LANES), F32)],
        compiler_params=_cparams(("arbitrary", "arbitrary")),
        name="sample_scores",
    )(page_table, *([cache_kidx_t] * P), qi_rows, w_rows, ki_new_t)


def _sample_attend_kernel(pt_ref, *refs, P, n_steps):
    kpages = refs[:P]
    vpages = refs[P:2 * P]
    sc_ref, scn_ref, thr_ref, q_ref, kn_ref, vn_ref, o_ref, m_ref, l_ref, acc_ref = refs[2 * P:2 * P + 10]
    j = pl.program_id(1)
    qz = q_ref[...]
    thr = thr_ref[...]

    @pl.when(j == 0)
    def _():
        m_ref[...] = jnp.full_like(m_ref, NEG_INF)
        l_ref[...] = jnp.zeros_like(l_ref)
        acc_ref[...] = jnp.zeros_like(acc_ref)

    def update(kblocks, vblocks, scblocks):
        s_parts = []
        for kb, scb in zip(kblocks, scblocks):
            s = jnp.dot(qz, kb.astype(BF16), preferred_element_type=F32)
            bias = jnp.where(scb >= thr, 0.0, NEG_INF)
            s_parts.append(s + jnp.concatenate([bias] * N_HEADS, axis=0))
        s = jnp.concatenate(s_parts, axis=1) if len(s_parts) > 1 else s_parts[0]
        m_old = m_ref[...]
        mnew = jnp.maximum(m_old, jnp.broadcast_to(jnp.max(s, axis=1, keepdims=True), m_old.shape))
        msafe = jnp.where(mnew == NEG_INF, 0.0, mnew)
        alpha = jnp.exp2(m_old - msafe)
        p = jnp.exp2(s - jnp.concatenate([msafe] * len(s_parts), axis=1) if len(s_parts) > 1 else s - msafe)
        l_ref[...] = alpha * l_ref[...] + jnp.broadcast_to(jnp.sum(p, axis=1, keepdims=True), m_old.shape)
        m_ref[...] = mnew
        pv = jnp.zeros(acc_ref.shape, F32)
        for r, vb in enumerate(vblocks):
            pv = pv + lax.dot_general(p[:, r * LANES:(r + 1) * LANES].astype(BF16), vb.astype(BF16),
                                      (((1,), (1,)), ((), ())), preferred_element_type=F32)
        acc_ref[...] = alpha * acc_ref[...] + pv

    update([kp[...] for kp in kpages], [vp[...] for vp in vpages], [sc_ref[r] for r in range(P)])

    @pl.when(j == n_steps - 1)
    def _():
        update([kn_ref[...]], [vn_ref[...]], [scn_ref[0]])
        o_ref[...] = acc_ref[...] / l_ref[...]


def _sample_attend(page_table, cache_k_t, cache_v_t, l, sc, thr, q_rows, k_new_t, v_new_t, *, P):
    NB, n_pages = page_table.shape
    R = q_rows.shape[1]
    TQ = R // N_HEADS
    n_steps = n_pages // P

    def page_spec(r):
        return pl.BlockSpec((None, None, KV_DIM, PAGE_SIZE), lambda b, j, pt: (l, pt[b, j * P + r], 0, 0))

    kern = functools.partial(_sample_attend_kernel, P=P, n_steps=n_steps)
    grid_spec = pltpu.PrefetchScalarGridSpec(
        num_scalar_prefetch=1,
        grid=(NB, n_steps),
        in_specs=[page_spec(r) for r in range(P)] + [page_spec(r) for r in range(P)] + [
            pl.BlockSpec((None, P, TQ, LANES), lambda b, j, pt: (b, j, 0, 0)),
            pl.BlockSpec((None, 1, TQ, LANES), lambda b, j, pt: (b, n_pages, 0, 0)),
            pl.BlockSpec((None, TQ, LANES), lambda b, j, pt: (b, 0, 0)),
            pl.BlockSpec((None, R, KV_DIM), lambda b, j, pt: (b, 0, 0)),
            pl.BlockSpec((None, KV_DIM, PAGE_SIZE), lambda b, j, pt: (b, 0, 0)),
            pl.BlockSpec((None, KV_DIM, PAGE_SIZE), lambda b, j, pt: (b, 0, 0))],
        out_specs=pl.BlockSpec((None, R, KV_DIM), lambda b, j, pt: (b, 0, 0)),
        scratch_shapes=[pltpu.VMEM((R, LANES), F32), pltpu.VMEM((R, LANES), F32), pltpu.VMEM((R, KV_DIM), F32)],
    )
    return pl.pallas_call(
        kern,
        grid_spec=grid_spec,
        out_shape=jax.ShapeDtypeStruct((NB, R, KV_DIM), F32),
        compiler_params=_cparams(("arbitrary", "arbitrary")),
        name="sample_attend",
    )(page_table, *([cache_k_t] * P), *([cache_v_t] * P), sc, sc, thr, q_rows, k_new_t, v_new_t)


def _mix_out_kernel(x_ref, g_ref, wg_ref, ca_ref, ao_ref, py_ref, wco_ref, wao_ref, wpo_ref, wo_ref, o_ref):
    x = x_ref[...]
    ms = jnp.mean(x * x, axis=-1, keepdims=True)
    h = (x * lax.rsqrt(ms + RMS_EPS) * g_ref[...]).astype(BF16)
    gates = jax.nn.sigmoid(jnp.dot(h, wg_ref[...], preferred_element_type=F32))
    y_a = jnp.dot(ca_ref[...], wco_ref[...], preferred_element_type=F32)
    y_b = jnp.dot(ao_ref[...], wao_ref[...], preferred_element_type=F32)
    y_c = jnp.dot(py_ref[...], wpo_ref[...], preferred_element_type=F32)
    merged = (gates[:, 0:D_MODEL] * y_a + gates[:, D_MODEL:2 * D_MODEL] * y_b
              + gates[:, 2 * D_MODEL:3 * D_MODEL] * y_c)
    o_ref[...] = x + jnp.dot(merged.astype(BF16), wo_ref[...], preferred_element_type=F32)


def _mix_out(x2, ca, ao, py, sw, l, *, TM):
    N = x2.shape[0]

    def tspec(c):
        return pl.BlockSpec((TM, c), lambda i: (i, 0))

    return pl.pallas_call(
        _mix_out_kernel,
        grid=(N // TM,),
        in_specs=[tspec(D_MODEL), _layer_spec((1, D_MODEL), l), _layer_spec((D_MODEL, N_BRANCH * D_MODEL), l),
                  tspec(D_CONV), tspec(D_ATTN), tspec(D_POOL),
                  _layer_spec((D_CONV, D_MODEL), l), _layer_spec((D_ATTN, D_MODEL), l),
                  _layer_spec((D_POOL, D_MODEL), l), _layer_spec((D_MODEL, D_MODEL), l)],
        out_specs=tspec(D_MODEL),
        out_shape=jax.ShapeDtypeStruct((N, D_MODEL), F32),
        compiler_params=_cparams(("arbitrary",)),
        name="mix_out",
    )(x2, sw["g_mix"], sw["w_g"], ca, ao, py, sw["w_conv_out"], sw["w_attn_out"], sw["w_pool_out"], sw["w_out"])


def _ffn_kernel(x_ref, g_ref, w1_ref, w2_ref, gf_ref, o_ref, *, final):
    x = x_ref[...]
    ms = jnp.mean(x * x, axis=-1, keepdims=True)
    h = (x * lax.rsqrt(ms + RMS_EPS) * g_ref[...]).astype(BF16)
    r = jnp.maximum(jnp.dot(h, w1_ref[...], preferred_element_type=F32), 0.0)
    y = x + jnp.dot((r * r).astype(BF16), w2_ref[...], preferred_element_type=F32)
    if final:
        ms2 = jnp.mean(y * y, axis=-1, keepdims=True)
        y = y * lax.rsqrt(ms2 + RMS_EPS) * gf_ref[...]
    o_ref[...] = y


def _ffn(x2, sw, l, g_final, *, TM, final):
    N = x2.shape[0]
    once = pl.Buffered(1)
    return pl.pallas_call(
        functools.partial(_ffn_kernel, final=final),
        grid=(N // TM,),
        in_specs=[pl.BlockSpec((TM, D_MODEL), lambda i: (i, 0)), _layer_spec((1, D_MODEL), l),
                  _layer_spec((D_MODEL, D_FF), l, pipeline_mode=once),
                  _layer_spec((D_FF, D_MODEL), l, pipeline_mode=once),
                  pl.BlockSpec((1, D_MODEL), lambda i: (0, 0))],
        out_specs=pl.BlockSpec((TM, D_MODEL), lambda i: (i, 0)),
        out_shape=jax.ShapeDtypeStruct((N, D_MODEL), F32),
        compiler_params=_cparams(("arbitrary",)),
        name="ffn",
    )(x2, sw["g_ffn"], sw["w_ff1"], sw["w_ff2"], g_final)


def _stacked_weights(g_mix, w_in, conv_w, conv_b, conv_ln_g, conv_ln_b, w_conv_out, w_attn_out,
                     pool_w, pool_scale, w_pool_out, w_out, g_ffn, w_ff1, w_ff2):
    depth = w_in.shape[0]
    o_ki = C_QI + IDX_HEADS * IDX_DIM
    o_wi = o_ki + IDX_DIM
    o_xc = o_wi + IDX_HEADS
    o_g = o_xc + D_POOL

    def pad_to_tile(a):
        return jnp.pad(a, ((0, 0), (0, 0), (0, LANES - a.shape[2])))

    w_a = jnp.concatenate([w_in[:, :, :o_ki], w_in[:, :, o_xc:o_g], pad_to_tile(w_in[:, :, o_ki:o_wi]),
                           pad_to_tile(w_in[:, :, o_wi:o_xc])], axis=2).astype(BF16)
    pw = jnp.zeros((depth, D_POOL, D_POOL), F32)
    for g in range(len(POOL_WINDOWS)):
        pw = pw.at[:, g * POOL_GC:(g + 1) * POOL_GC, g * POOL_GC:(g + 1) * POOL_GC].set(pool_w[:, g])

    def row(a):
        return a[:, None, :]

    return dict(
        g_mix=row(g_mix), w_a=w_a, w_g=w_in[:, :, o_g:].astype(BF16),
        conv_w=jnp.pad(conv_w, ((0, 0), (0, CONV_PAD - CONV_W), (0, 0))), conv_b=row(conv_b),
        ln_g=row(conv_ln_g), ln_b=row(conv_ln_b),
        pool_w=pw.astype(BF16), pool_scale=row(pool_scale),
        w_conv_out=w_conv_out.astype(BF16), w_attn_out=w_attn_out.astype(BF16),
        w_pool_out=w_pool_out.astype(BF16), w_out=w_out.astype(BF16),
        g_ffn=row(g_ffn), w_ff1=w_ff1.astype(BF16), w_ff2=w_ff2.astype(BF16))


def _tail_state(hist, new, keep):
    return jnp.concatenate([hist, new], axis=1)[:, -keep:]


def kernel(x_prompt, x_sample, cache_k, cache_v, cache_kidx, state_conv, state_pool, page_table, g_mix, w_in,
           conv_w, conv_b, conv_ln_g, conv_ln_b, w_conv_out, w_attn_out, pool_w, pool_scale, w_pool_out, w_out,
           g_ffn, w_ff1, w_ff2, g_final):
    B, T, _ = x_prompt.shape
    NB, TS, _ = x_sample.shape
    depth = w_in.shape[0]
    n_pool = cache_k.shape[1]
    past_len = page_table.shape[1] * PAGE_SIZE
    TM_P = min(512, T)
    gf = g_final[None]
    xp = x_prompt
    xs = x_sample
    zero_conv = jnp.zeros((B, CONV_PAD, D_CONV), F32)
    zero_pool = jnp.zeros((B, POOL_PAD, D_POOL), F32)
    cache_k_t = cache_k.transpose(0, 1, 3, 4, 2).reshape(depth, n_pool, KV_DIM, PAGE_SIZE)
    cache_v_t = cache_v.transpose(0, 1, 3, 4, 2).reshape(depth, n_pool, KV_DIM, PAGE_SIZE)
    cache_kidx_t = cache_kidx.transpose(0, 1, 3, 2)
    sw = _stacked_weights(g_mix, w_in, conv_w, conv_b, conv_ln_g, conv_ln_b, w_conv_out, w_attn_out,
                          pool_w, pool_scale, w_pool_out, w_out, g_ffn, w_ff1, w_ff2)
    tables_p = _rope_tables(0, T)
    tables_s = _rope_tables(past_len, TS)
    hist_conv_s = jnp.pad(state_conv, ((0, 0), (0, 0), (CONV_PAD - CONV_HIST, 0), (0, 0)))
    hist_pool_s = jnp.pad(state_pool, ((0, 0), (0, 0), (POOL_PAD - POOL_HIST, 0), (0, 0)))
    st_p = [[] for _ in range(5)]
    st_s = [[] for _ in range(5)]
    for l in range(depth):
        final = l == depth - 1

        (k, v, ki, u, xc, kibf, q, _, _, ca, py, qit, wq, kt, va) = _proj(
            xp, 0, tables_p, zero_conv, zero_pool, sw, l, G=1, TM=TM_P, adt=BF16, attn=True)
        ao = _attend_prompt(q, qit, wq, kibf, kt, va, CKS=TM_P)
        x2 = _mix_out(xp.reshape(B * T, D_MODEL), ca.reshape(B * T, D_CONV), ao.reshape(B * T, D_ATTN),
                      py.reshape(B * T, D_POOL), sw, l, TM=TM_P)
        xp = _ffn(x2, sw, l, gf, TM=TM_P, final=final).reshape(B, T, D_MODEL)
        st_p[0].append(k.reshape(B, T, N_KV_HEADS, HEAD_DIM))
        st_p[1].append(v.reshape(B, T, N_KV_HEADS, HEAD_DIM))
        st_p[2].append(ki)
        st_p[3].append(u[:, T - CONV_HIST:])
        st_p[4].append(xc[:, T - POOL_HIST:])

        (k, v, ki, u, xc, _, q, qi, wi, ca, py) = _proj(
            xs, past_len, tables_s, hist_conv_s[l], hist_pool_s[l], sw, l, G=NB, TM=TS, adt=F32, attn=False)
        q, qi, ca, py = q.astype(BF16), qi.astype(BF16), ca.astype(BF16), py.astype(BF16)
        qi_rows = qi.reshape(NB, TS, IDX_HEADS, IDX_DIM).transpose(0, 2, 1, 3).reshape(NB, IDX_HEADS * TS, IDX_DIM)
        w_rows = jnp.broadcast_to(wi[:, :, :IDX_HEADS].transpose(0, 2, 1).reshape(NB, IDX_HEADS * TS, 1),
                                  (NB, IDX_HEADS * TS, LANES))
        qh = q.reshape(NB, TS, N_HEADS, HEAD_DIM).transpose(0, 2, 1, 3)
        head_kv = (jnp.arange(N_HEADS) // GROUP)[None, :, None, None]
        q_rows = jnp.concatenate([jnp.where(head_kv == n, qh, jnp.zeros_like(qh)) for n in range(N_KV_HEADS)],
                                 axis=-1).reshape(NB, N_HEADS * TS, KV_DIM)
        pad_rows = ((0, 0), (0, PAGE_SIZE - TS), (0, 0))
        pages_per_step = min(32, page_table.shape[1])

        def new_page_t(a):
            return jnp.pad(a, pad_rows).transpose(0, 2, 1)

        sc, thr = _sample_scores(page_table, cache_kidx_t, l, qi_rows, w_rows, new_page_t(ki), P=pages_per_step)
        o_rows = _sample_attend(page_table, cache_k_t, cache_v_t, l, sc, thr, q_rows, new_page_t(k), new_page_t(v),
                                P=pages_per_step)
        o5 = o_rows.reshape(NB, N_KV_HEADS, GROUP, TS, N_KV_HEADS, HEAD_DIM)
        ao = jnp.stack([o5[:, n, :, :, n, :] for n in range(N_KV_HEADS)], axis=1)
        ao = ao.reshape(NB, N_HEADS, TS, HEAD_DIM).transpose(0, 2, 1, 3).reshape(NB * TS, D_ATTN).astype(BF16)
        x2 = _mix_out(xs.reshape(NB * TS, D_MODEL), ca.reshape(NB * TS, D_CONV), ao,
                      py.reshape(NB * TS, D_POOL), sw, l, TM=NB * TS)
        xs = _ffn(x2, sw, l, gf, TM=NB * TS, final=final).reshape(NB, TS, D_MODEL)
        st_s[0].append(k.reshape(NB, TS, N_KV_HEADS, HEAD_DIM))
        st_s[1].append(v.reshape(NB, TS, N_KV_HEADS, HEAD_DIM))
        st_s[2].append(ki)
        st_s[3].append(_tail_state(state_conv[l], u, CONV_HIST))
        st_s[4].append(_tail_state(state_pool[l], xc, POOL_HIST))

    sp = [jnp.stack(a, axis=0) for a in st_p]
    ss = [jnp.stack(a, axis=0) for a in st_s]
    return (xp, xs, sp[0], sp[1], sp[2], sp[3], sp[4], ss[0], ss[1], ss[2], ss[3], ss[4])
```

```python
import functools

import jax
import jax.numpy as jnp
from jax import lax
from jax.experimental import pallas as pl
from jax.experimental.pallas import tpu as pltpu

F32 = jnp.float32
BF16 = jnp.bfloat16

D_MODEL = 1024
D_CONV = 256
CONV_W = 31
CONV_HIST = CONV_W - 1
HEAD_DIM = 64
D_ATTN = 512
N_HEADS = 8
N_KV_HEADS = 2
KV_DIM = N_KV_HEADS * HEAD_DIM
GROUP = N_HEADS // N_KV_HEADS
IDX_HEADS = 8
IDX_DIM = 64
TOPK_MAX = 256
D_POOL = 256
POOL_WINDOWS = (2, 4, 8, 16)
POOL_GC = D_POOL // len(POOL_WINDOWS)
POOL_HIST = max(POOL_WINDOWS) - 1
N_BRANCH = 3
D_FF = 4 * D_MODEL
ROPE_THETA = 10000.0
RMS_EPS = 1e-6
LN_EPS = 1e-5
PAGE_SIZE = 128

LANES = 128
CONV_PAD = 32
POOL_PAD = 16
Q_SCALE = HEAD_DIM ** -0.5 * 1.4426950408889634
BIG = 3.0e38
NEG_INF = float("-inf")

C_AIN, C_AGATE, C_Q, C_K, C_V, C_QI, C_XC, C_KI, C_WI, D_A = 0, 256, 512, 1024, 1152, 1280, 1792, 2048, 2176, 2304

VMEM_LIMIT = 56 * 1024 * 1024


def _cparams(sem):
    return pltpu.CompilerParams(dimension_semantics=sem, vmem_limit_bytes=VMEM_LIMIT)


def _layer_spec(shape, l, **kw):
    n = len(shape)
    return pl.BlockSpec((None,) + tuple(shape), lambda *_: (l,) + (0,) * n, **kw)


def _proj_kernel(x_ref, g_ref, w_ref, cos_ref, sin_ref, chist_ref, phist_ref, convw_ref, convb_ref,
                 lng_ref, lnb_ref, poolw_ref, pscale_ref,
                 k_ref, v_ref, ki_ref, u_ref, xc_ref, kibf_ref, q_ref, qi_ref, wi_ref, ca_ref, py_ref,
                 *rest, G, TM, pos0, n_tiles, attn):
    if attn:
        qit_ref, wq_ref, kt_ref, va_ref = rest[:4]
        rest = rest[4:]
    cs_ref, ps_ref, p2_ref, p4_ref, p8_ref, p16_ref = rest
    j = pl.program_id(1)
    R = G * TM
    x = x_ref[...].reshape(R, D_MODEL)
    ms = jnp.mean(x * x, axis=-1, keepdims=True)
    h = (x * lax.rsqrt(ms + RMS_EPS) * g_ref[...]).astype(BF16)
    z = jnp.dot(h, w_ref[...], preferred_element_type=F32)

    cos = cos_ref[...]
    sin = sin_ref[...]
    if G > 1:
        cos = jnp.concatenate([cos] * G, axis=0)
        sin = jnp.concatenate([sin] * G, axis=0)

    def rope(t):
        n = t.shape[1] // LANES
        c = jnp.concatenate([cos] * n, axis=1) if n > 1 else cos
        s = jnp.concatenate([sin] * n, axis=1) if n > 1 else sin
        lane = lax.broadcasted_iota(jnp.int32, t.shape, 1)
        first = (lane & (HEAD_DIM - 1)) < HEAD_DIM // 2
        sw = jnp.where(first, pltpu.roll(t, t.shape[1] - HEAD_DIM // 2, 1), pltpu.roll(t, HEAD_DIM // 2, 1))
        return t * c + sw * s

    q = rope(z[:, C_Q:C_Q + D_ATTN]) * Q_SCALE
    k = rope(z[:, C_K:C_K + KV_DIM])
    v = z[:, C_V:C_V + KV_DIM]
    qi = rope(z[:, C_QI:C_QI + IDX_HEADS * IDX_DIM])
    ki = rope(z[:, C_KI:C_KI + LANES])[:, :IDX_DIM]
    wi = z[:, C_WI:C_WI + LANES] * (IDX_HEADS ** -0.5 * IDX_DIM ** -0.5)

    k_ref[...] = k.reshape(G, TM, KV_DIM)
    v_ref[...] = v.reshape(G, TM, KV_DIM)
    ki_ref[...] = ki.reshape(G, TM, IDX_DIM)
    kibf_ref[...] = ki.reshape(G, TM, IDX_DIM).astype(kibf_ref.dtype)
    q_ref[...] = q.reshape(G, TM, D_ATTN).astype(q_ref.dtype)
    qi_ref[...] = qi.reshape(G, TM, IDX_HEADS * IDX_DIM).astype(qi_ref.dtype)
    wi_ref[...] = wi.reshape(G, TM, LANES)
    if attn:
        for b in range(TM // LANES):
            blk = slice(b * LANES, (b + 1) * LANES)
            qi_t = qi[blk, :].T
            qit_ref[b] = jnp.concatenate([qi_t[hh * IDX_DIM:(hh + 1) * IDX_DIM, :] for hh in range(IDX_HEADS)],
                                         axis=1).astype(BF16)
            wi_t = wi[blk, :].T
            wq_ref[b] = jnp.concatenate([wi_t[hh:hh + 1, :] for hh in range(IDX_HEADS)], axis=1)
        k_t = k.T
        lane = lax.broadcasted_iota(jnp.int32, (TM, KV_DIM), 1)
        tail = jnp.where(lane == HEAD_DIM, 1.0, 0.0)
        for n in range(N_KV_HEADS):
            kt_ref[0, n] = k_t[n * HEAD_DIM:(n + 1) * HEAD_DIM, :].astype(BF16)
            vn = v if n == 0 else pltpu.roll(v, KV_DIM - n * HEAD_DIM, 1)
            va_ref[0, n] = jnp.where(lane < HEAD_DIM, vn, tail).astype(BF16)

    a_in = z[:, C_AIN:C_AIN + D_CONV]
    a_gate = z[:, C_AGATE:C_AGATE + D_CONV]
    u = (a_in * jax.nn.sigmoid(a_gate)).reshape(G, TM, D_CONV)
    u_ref[...] = u

    @pl.when(j == 0)
    def _():
        cs_ref[:, 0:CONV_PAD, :] = chist_ref[...]
        ps_ref[:, 0:POOL_PAD, :] = phist_ref[...]

    cs_ref[:, CONV_PAD:CONV_PAD + TM, :] = u
    off = CONV_PAD - CONV_HIST
    rc = min(TM, 64)
    gc = min(G, 8)
    for g0 in range(0, G, gc):
        for r0 in range(0, TM, rc):
            acc = jnp.zeros((gc, rc, D_CONV), F32) + convb_ref[...][None]
            for t in range(CONV_W):
                acc = acc + convw_ref[t:t + 1, :][None] * cs_ref[g0:g0 + gc, off + t + r0:off + t + r0 + rc, :]
            mu = jnp.mean(acc, axis=-1, keepdims=True)
            var = jnp.mean(jnp.square(acc - mu), axis=-1, keepdims=True)
            y = (acc - mu) * lax.rsqrt(var + LN_EPS) * lng_ref[...][None] + lnb_ref[...][None]
            ca_ref[g0:g0 + gc, r0:r0 + rc, :] = (y * jax.nn.sigmoid(y)).astype(ca_ref.dtype)
    if n_tiles > 1:
        cs_ref[:, 0:CONV_PAD, :] = cs_ref[:, TM:TM + CONV_PAD, :]

    xc = z[:, C_XC:C_XC + D_POOL].reshape(G, TM, D_POOL)
    xc_ref[...] = xc
    ps_ref[:, POOL_PAD:POOL_PAD + TM, :] = xc
    RR = POOL_PAD + TM
    p2_ref[:, 1:RR, :] = ps_ref[:, 1:RR, :] + ps_ref[:, 0:RR - 1, :]
    p4_ref[:, 3:RR, :] = p2_ref[:, 3:RR, :] + p2_ref[:, 1:RR - 2, :]
    p8_ref[:, 7:RR, :] = p4_ref[:, 7:RR, :] + p4_ref[:, 3:RR - 4, :]
    p16_ref[:, 15:RR, :] = p8_ref[:, 15:RR, :] + p8_ref[:, 7:RR - 8, :]
    lane = lax.broadcasted_iota(jnp.int32, (G, TM, D_POOL), 2)
    row = lax.broadcasted_iota(jnp.int32, (G, TM, D_POOL), 1)
    g0m, g1m, g2m = lane < POOL_GC, lane < 2 * POOL_GC, lane < 3 * POOL_GC
    s_tok = slice(POOL_PAD, POOL_PAD + TM)
    wsum = jnp.where(g0m, p2_ref[:, s_tok, :],
                     jnp.where(g1m, p4_ref[:, s_tok, :], jnp.where(g2m, p8_ref[:, s_tok, :], p16_ref[:, s_tok, :])))
    win = jnp.where(g0m, POOL_WINDOWS[0], jnp.where(g1m, POOL_WINDOWS[1],
                                                     jnp.where(g2m, POOL_WINDOWS[2], POOL_WINDOWS[3])))
    cnt = jnp.minimum(pos0 + j * TM + row + 1, win).astype(F32)
    d = (wsum / cnt - xc).reshape(R, D_POOL).astype(BF16)
    py = jnp.dot(d, poolw_ref[...], preferred_element_type=F32) * pscale_ref[...]
    py_ref[...] = py.reshape(G, TM, D_POOL).astype(py_ref.dtype)
    if n_tiles > 1:
        ps_ref[:, 0:POOL_PAD, :] = ps_ref[:, TM:TM + POOL_PAD, :]


def _rope_tables(pos0, T):
    half = HEAD_DIM // 2
    pos = (pos0 + jnp.arange(T)).astype(F32)
    inv = ROPE_THETA ** (-jnp.arange(half, dtype=F32) / half)
    ang = pos[:, None] * inv[None, :]
    cos = jnp.tile(jnp.cos(ang), (1, 2 * LANES // HEAD_DIM))
    sin = jnp.tile(jnp.concatenate([-jnp.sin(ang), jnp.sin(ang)], axis=1), (1, LANES // HEAD_DIM))
    return cos, sin


def _proj(x3, pos0, tables, chist, phist, sw, l, *, G, TM, adt, attn):
    NS, T, _ = x3.shape
    n_tiles = T // TM
    cos, sin = tables

    def tok(c, dt=F32):
        return jax.ShapeDtypeStruct((NS, T, c), dt)

    def tspec(c):
        return pl.BlockSpec((G, TM, c), lambda s, j: (s, j, 0))

    out_cols = [(KV_DIM, F32), (KV_DIM, F32), (IDX_DIM, F32), (D_CONV, F32), (D_POOL, F32),
                (IDX_DIM, adt), (D_ATTN, adt), (IDX_HEADS * IDX_DIM, adt), (LANES, F32), (D_CONV, adt), (D_POOL, adt)]
    out_specs = [tspec(c) for c, _ in out_cols]
    out_shape = [tok(c, dt) for c, dt in out_cols]
    if attn:
        assert G == 1 and TM % LANES == 0
        nb = TM // LANES
        out_specs += [pl.BlockSpec((None, nb, IDX_DIM, IDX_HEADS * LANES), lambda s, j: (s, j, 0, 0)),
                      pl.BlockSpec((None, nb, 1, IDX_HEADS * LANES), lambda s, j: (s, j, 0, 0)),
                      pl.BlockSpec((None, 1, N_KV_HEADS, HEAD_DIM, TM), lambda s, j: (s, j, 0, 0, 0)),
                      pl.BlockSpec((None, 1, N_KV_HEADS, TM, LANES), lambda s, j: (s, j, 0, 0, 0))]
        out_shape += [jax.ShapeDtypeStruct((NS, T // LANES, IDX_DIM, IDX_HEADS * LANES), BF16),
                      jax.ShapeDtypeStruct((NS, T // LANES, 1, IDX_HEADS * LANES), F32),
                      jax.ShapeDtypeStruct((NS, n_tiles, N_KV_HEADS, HEAD_DIM, TM), BF16),
                      jax.ShapeDtypeStruct((NS, n_tiles, N_KV_HEADS, TM, LANES), BF16)]
    kern = functools.partial(_proj_kernel, G=G, TM=TM, pos0=pos0, n_tiles=n_tiles, attn=attn)
    return pl.pallas_call(
        kern,
        grid=(NS // G, n_tiles),
        in_specs=[tspec(D_MODEL), _layer_spec((1, D_MODEL), l), _layer_spec((D_MODEL, D_A), l),
                  pl.BlockSpec((TM, LANES), lambda s, j: (j, 0)), pl.BlockSpec((TM, LANES), lambda s, j: (j, 0)),
                  pl.BlockSpec((G, CONV_PAD, D_CONV), lambda s, j: (s, 0, 0)),
                  pl.BlockSpec((G, POOL_PAD, D_POOL), lambda s, j: (s, 0, 0)),
                  _layer_spec((CONV_PAD, D_CONV), l), _layer_spec((1, D_CONV), l), _layer_spec((1, D_CONV), l),
                  _layer_spec((1, D_CONV), l), _layer_spec((D_POOL, D_POOL), l), _layer_spec((1, D_POOL), l)],
        out_specs=out_specs,
        out_shape=out_shape,
        scratch_shapes=[pltpu.VMEM((G, CONV_PAD + TM, D_CONV), F32)] +
                       [pltpu.VMEM((G, POOL_PAD + TM, D_POOL), F32)] * 5,
        compiler_params=_cparams(("arbitrary", "arbitrary")),
        name="proj",
    )(x3, sw["g_mix"], sw["w_a"], cos, sin, chist, phist, sw["conv_w"], sw["conv_b"], sw["ln_g"], sw["ln_b"],
      sw["pool_w"], sw["pool_scale"])


def _kth_bisect(count, smin, smax, nvalid, kk, steps, warm=0, zero_counts=None):
    zero = jnp.zeros_like(smin)
    c_ge0, c_gt0 = zero_counts if zero_counts is not None else (count(zero, False), count(zero, True))
    pos = c_gt0 >= kk
    neg = c_ge0 < kk
    lo0 = jnp.where(neg, smin, zero)
    clo0 = jnp.where(neg, nvalid, c_ge0)
    hi0 = jnp.where(pos, BIG, zero)
    chi0 = jnp.where(pos, zero, jnp.where(neg, c_ge0, c_gt0))
    cand0 = jnp.where(pos, smax, smin * 0.5)
    small = nvalid <= kk
    done0 = (small | jnp.logical_not(pos | neg)).astype(F32)
    init = (jnp.where(small, smin, lo0), hi0, jnp.where(small, nvalid, clo0), chi0, cand0, done0)

    def count_ge(cand):
        return count(cand, False)

    def cond(st):
        return jnp.min(st[5]) < 0.5

    def body(st):
        for _ in range(steps):
            st = step(st)
        return st

    def step(st):
        lo, hi, clo, chi, cand, done = st
        c = count_ge(cand)
        act = done < 0.5
        ge = c >= kk
        up = act & ge
        dn = act & jnp.logical_not(ge)
        lo = jnp.where(up, cand, lo)
        clo = jnp.where(up, c, clo)
        hi = jnp.where(dn, cand, hi)
        chi = jnp.where(dn, c, chi)
        mid = lo * 0.5 + hi * 0.5
        inside = (mid > lo) & (mid < hi)
        fin = (clo == kk) | jnp.logical_not(inside)
        done = jnp.where(fin, 1.0, done)
        return lo, hi, clo, chi, mid, done

    if warm:
        init = lax.fori_loop(0, warm, lambda _, st: step(st), init)
    lo, _, clo, chi, _, _ = lax.while_loop(cond, body, init)
    return lo, clo, chi


def _attend_prompt_kernel(qit_ref, wq_ref, q_ref, ki_ref, kt_ref, va_ref, o_ref, sc_ref, acc_ref, m_ref, *,
                          CKS, CK, kk):
    i = pl.program_id(1)
    QB = LANES
    nchs = (i * QB + QB + CKS - 1) // CKS
    nch = nchs * (CKS // CK)
    qit = qit_ref[...]
    wq = wq_ref[...]
    qcol = i * QB + lax.broadcasted_iota(jnp.int32, (CKS, QB), 1)
    krow0 = lax.broadcasted_iota(jnp.int32, (CKS, QB), 0)

    def score_chunk(c, carry, causal):
        smax, smin, n_ge0, n_gt0 = carry
        s = jnp.dot(ki_ref[c], qit, preferred_element_type=F32)
        s = jnp.maximum(s, 0.0) * wq
        tot = s[:, 0:QB]
        for hh in range(1, IDX_HEADS):
            tot = tot + s[:, hh * QB:(hh + 1) * QB]
        if causal:
            valid = (krow0 + c * CKS) <= qcol
            masked = jnp.where(valid, tot, NEG_INF)
            lowest = jnp.where(valid, tot, BIG)
        else:
            masked = lowest = tot
        sc_ref[pl.ds(pl.multiple_of(c * CKS, CKS), CKS), :] = masked
        smax = jnp.maximum(smax, jnp.max(masked.reshape(CKS // 8, 8, QB), axis=0))
        smin = jnp.minimum(smin, jnp.min(lowest.reshape(CKS // 8, 8, QB), axis=0))
        n_ge0 = n_ge0 + jnp.sum(jnp.where(masked >= 0.0, 1.0, 0.0).reshape(CKS // 8, 8, QB), axis=0)
        n_gt0 = n_gt0 + jnp.sum(jnp.where(masked > 0.0, 1.0, 0.0).reshape(CKS // 8, 8, QB), axis=0)
        return smax, smin, n_ge0, n_gt0

    carry = lax.fori_loop(0, nchs - 1, functools.partial(score_chunk, causal=False),
                          (jnp.full((8, QB), NEG_INF, F32), jnp.full((8, QB), BIG, F32),
                           jnp.zeros((8, QB), F32), jnp.zeros((8, QB), F32)))
    smax, smin, n_ge0, n_gt0 = score_chunk(nchs - 1, carry, causal=True)
    smax = jnp.max(smax, axis=0, keepdims=True)
    smin = jnp.min(smin, axis=0, keepdims=True)
    zero_counts = (jnp.sum(n_ge0, axis=0, keepdims=True), jnp.sum(n_gt0, axis=0, keepdims=True))

    n_acc = 4
    rows = CKS // n_acc

    def count(cand, strict):
        def body(c, accs):
            base = pl.multiple_of(c * CKS, CKS)
            out = []
            for a in range(n_acc):
                blk = sc_ref[pl.ds(base + a * rows, rows), :]
                hit = (blk > cand) if strict else (blk >= cand)
                ones = jnp.where(hit, 1.0, 0.0).reshape(rows // 8, 8, QB)
                out.append(accs[a] + jnp.sum(ones, axis=0))
            return tuple(out)
        accs = lax.fori_loop(0, nchs, body, tuple(jnp.zeros((8, QB), F32) for _ in range(n_acc)))
        return jnp.sum((accs[0] + accs[1]) + (accs[2] + accs[3]), axis=0, keepdims=True)

    nvalid = (i * QB + lax.broadcasted_iota(jnp.int32, (1, QB), 1) + 1).astype(F32)
    lo, clo, chi = _kth_bisect(count, smin, smax, nvalid, float(kk), steps=4, warm=12, zero_counts=zero_counts)

    fix = (clo > kk) & (nvalid > kk)

    @pl.when(jnp.max(fix.astype(F32)) > 0.5)
    def _():
        keep = kk - chi
        TF = min(256, CKS)
        tri = (lax.broadcasted_iota(jnp.int32, (TF, TF), 1)
               <= lax.broadcasted_iota(jnp.int32, (TF, TF), 0)).astype(BF16)

        def body(c, run):
            rows_c = pl.ds(pl.multiple_of(c * TF, TF), TF)
            blk = sc_ref[rows_c, :]
            tie = jnp.where(fix & (blk == lo), 1.0, 0.0)
            rank = run + jnp.dot(tri, tie.astype(BF16), preferred_element_type=F32)
            sc_ref[rows_c, :] = jnp.where((tie > 0.5) & (rank > keep), NEG_INF, blk)
            return run + jnp.sum(tie, axis=0, keepdims=True)
        lax.fori_loop(0, nchs * (CKS // TF), body, jnp.zeros((1, QB), F32))

    GQ = GROUP * QB
    q32 = q_ref[...].astype(F32)
    qn = [jnp.concatenate([q32[:, hh * HEAD_DIM:(hh + 1) * HEAD_DIM] for hh in range(n * GROUP, (n + 1) * GROUP)],
                          axis=0).astype(BF16) for n in range(N_KV_HEADS)]
    acc_ref[...] = jnp.zeros_like(acc_ref)
    m_ref[...] = jnp.full_like(m_ref, -BIG)
    reps = CK // LANES

    def att_body(c, _):
        scb = sc_ref[pl.ds(pl.multiple_of(c * CK, CK), CK), :]
        bias_t = jnp.where(scb >= lo, 0.0, NEG_INF).T
        for n in range(N_KV_HEADS):
            s = jnp.dot(qn[n], kt_ref[c, n], preferred_element_type=F32)
            s = (s.reshape(GROUP, QB, CK) + bias_t[None]).reshape(GQ, CK)
            m_old = m_ref[n]
            mnew = jnp.maximum(m_old, jnp.max(s, axis=1, keepdims=True))
            p = jnp.exp2(s - jnp.concatenate([mnew] * reps, axis=1))
            alpha = jnp.exp2(m_old - mnew)
            pv = jnp.dot(p.astype(BF16), va_ref[c, n], preferred_element_type=F32)
            acc_ref[n] = alpha * acc_ref[n] + pv
            m_ref[n] = mnew
        return 0

    lax.fori_loop(0, nch, att_body, 0)
    outs = []
    for n in range(N_KV_HEADS):
        a = acc_ref[n]
        o = a[:, :HEAD_DIM] / a[:, HEAD_DIM:HEAD_DIM + 1]
        outs += [o[g * QB:(g + 1) * QB, :] for g in range(GROUP)]
    o_ref[...] = jnp.concatenate(outs, axis=1).astype(BF16)


def _attend_prompt(q, qit, wq, kibf, kt, va, *, CKS):
    B, T, _ = q.shape
    QB = LANES
    nq = T // QB
    nc, CK = kt.shape[1], kt.shape[4]
    ncs = T // CKS
    kk = min(TOPK_MAX, T // 4)
    kib = kibf.reshape(B, ncs, CKS, IDX_DIM)

    kern = functools.partial(_attend_prompt_kernel, CKS=CKS, CK=CK, kk=kk)
    return pl.pallas_call(
        kern,
        grid=(B, nq),
        in_specs=[pl.BlockSpec((None, None, IDX_DIM, 8 * QB), lambda b, i: (b, i, 0, 0)),
                  pl.BlockSpec((None, None, 1, 8 * QB), lambda b, i: (b, i, 0, 0)),
                  pl.BlockSpec((None, QB, D_ATTN), lambda b, i: (b, i, 0)),
                  pl.BlockSpec((None, ncs, CKS, IDX_DIM), lambda b, i: (b, 0, 0, 0)),
                  pl.BlockSpec((None, nc, N_KV_HEADS, HEAD_DIM, CK), lambda b, i: (b, 0, 0, 0, 0)),
                  pl.BlockSpec((None, nc, N_KV_HEADS, CK, LANES), lambda b, i: (b, 0, 0, 0, 0))],
        out_specs=pl.BlockSpec((None, QB, D_ATTN), lambda b, i: (b, i, 0)),
        out_shape=jax.ShapeDtypeStruct((B, T, D_ATTN), BF16),
        scratch_shapes=[pltpu.VMEM((T, QB), F32), pltpu.VMEM((N_KV_HEADS, GROUP * QB, LANES), F32),
                        pltpu.VMEM((N_KV_HEADS, GROUP * QB, LANES), F32)],
        compiler_params=_cparams(("arbitrary", "arbitrary")),
        name="attend_prompt",
    )(qit, wq, q, kib, kt, va)


def _sample_score_kernel(pt_ref, *refs, P, n_steps, past_len, kk):
    pages = refs[:P]
    qi_ref, wb_ref, kin_ref, sc_ref, thr_ref = refs[P:P + 5]
    j = pl.program_id(1)
    TQ = sc_ref.shape[1]
    NBLK = sc_ref.shape[0]
    qi = qi_ref[...]
    wb = wb_ref[...]

    def block_scores(kpage_t):
        s = jnp.dot(qi, kpage_t.astype(BF16), preferred_element_type=F32)
        s = jnp.maximum(s, 0.0) * wb
        return jnp.sum(s.reshape(IDX_HEADS, TQ, LANES), axis=0)

    for r in range(P):
        sc_ref[j * P + r] = block_scores(pages[r][...])

    @pl.when(j == n_steps - 1)
    def _():
        tot = block_scores(kin_ref[...])
        qrow = lax.broadcasted_iota(jnp.int32, (TQ, LANES), 0)
        kcol = lax.broadcasted_iota(jnp.int32, (TQ, LANES), 1)
        sc_ref[NBLK - 1] = jnp.where(kcol <= qrow, tot, NEG_INF)

        def lane_bcast(a):
            return jnp.broadcast_to(a, (TQ, LANES))

        sc = sc_ref[...]
        smax = lane_bcast(jnp.max(jnp.max(sc, axis=0), axis=1, keepdims=True))
        smin = lane_bcast(jnp.min(jnp.min(jnp.where(sc > NEG_INF, sc, BIG), axis=0), axis=1, keepdims=True))
        nvalid = (past_len + qrow + 1).astype(F32)

        def count(cand, strict):
            blk = sc_ref[...]
            hit = (blk > cand[None]) if strict else (blk >= cand[None])
            c = jnp.sum(jnp.where(hit, 1.0, 0.0), axis=0)
            return lane_bcast(jnp.sum(c, axis=1, keepdims=True))

        lo, clo, chi = _kth_bisect(count, smin, smax, nvalid, float(kk), steps=4)
        thr_ref[...] = lo

        fix = (clo > kk) & (nvalid > kk)

        @pl.when(jnp.max(fix.astype(F32)) > 0.5)
        def _():
            keep = kk - chi
            tri = (lax.broadcasted_iota(jnp.int32, (LANES, LANES), 0)
                   <= lax.broadcasted_iota(jnp.int32, (LANES, LANES), 1)).astype(BF16)

            def body(c, run):
                blk = sc_ref[c]
                tie = fix & (blk == lo)
                rank = run + jnp.dot(tie.astype(BF16), tri, preferred_element_type=F32)
                sc_ref[c] = jnp.where(tie & (rank > keep), NEG_INF, blk)
                return run + lane_bcast(jnp.sum(tie.astype(F32), axis=1, keepdims=True))
            lax.fori_loop(0, NBLK, body, jnp.zeros((TQ, LANES), F32))


def _sample_scores(page_table, cache_kidx_t, l, qi_rows, w_rows, ki_new_t, *, P):
    NB, n_pages = page_table.shape
    TQ = qi_rows.shape[1] // IDX_HEADS
    n_steps = n_pages // P
    past_len = n_pages * PAGE_SIZE
    kk = min(TOPK_MAX, (past_len + TQ) // 4)
    NBLK = n_pages + 1

    def page_spec(r):
        return pl.BlockSpec((None, None, IDX_DIM, PAGE_SIZE), lambda b, j, pt: (l, pt[b, j * P + r], 0, 0))

    kern = functools.partial(_sample_score_kernel, P=P, n_steps=n_steps, past_len=past_len, kk=kk)
    grid_spec = pltpu.PrefetchScalarGridSpec(
        num_scalar_prefetch=1,
        grid=(NB, n_steps),
        in_specs=[page_spec(r) for r in range(P)] + [
            pl.BlockSpec((None, IDX_HEADS * TQ, IDX_DIM), lambda b, j, pt: (b, 0, 0)),
            pl.BlockSpec((None, IDX_HEADS * TQ, LANES), lambda b, j, pt: (b, 0, 0)),
            pl.BlockSpec((None, IDX_DIM, PAGE_SIZE), lambda b, j, pt: (b, 0, 0))],
        out_specs=[pl.BlockSpec((None, NBLK, TQ, LANES), lambda b, j, pt: (b, 0, 0, 0)),
                   pl.BlockSpec((None, TQ, LANES), lambda b, j, pt: (b, 0, 0))],
    )
    return pl.pallas_call(
        kern,
        grid_spec=grid_spec,
        out_shape=[jax.ShapeDtypeStruct((NB, NBLK, TQ, LANES), F32), jax.ShapeDtypeStruct((NB, TQ, LANES), F32)],
        compiler_params=_cparams(("arbitrary", "arbitrary")),
        name="sample_scores",
    )(page_table, *([cache_kidx_t] * P), qi_rows, w_rows, ki_new_t)


def _sample_attend_kernel(pt_ref, *refs, P, n_steps):
    kpages = refs[:P]
    vpages = refs[P:2 * P]
    sc_ref, scn_ref, thr_ref, q_ref, kn_ref, vn_ref, o_ref, m_ref, l_ref, acc_ref = refs[2 * P:2 * P + 10]
    j = pl.program_id(1)
    qz = q_ref[...]
    thr = thr_ref[...]

    @pl.when(j == 0)
    def _():
        m_ref[...] = jnp.full_like(m_ref, -BIG)
        l_ref[...] = jnp.zeros_like(l_ref)
        acc_ref[...] = jnp.zeros_like(acc_ref)

    def update(kblocks, vblocks, scblocks):
        s_parts = []
        for kb, scb in zip(kblocks, scblocks):
            s = jnp.dot(qz, kb.astype(BF16), preferred_element_type=F32)
            bias = jnp.where(scb >= thr, 0.0, NEG_INF)
            s_parts.append(s + jnp.concatenate([bias] * N_HEADS, axis=0))
        s = jnp.concatenate(s_parts, axis=1) if len(s_parts) > 1 else s_parts[0]
        m_old = m_ref[...]
        mnew = jnp.maximum(m_old, jnp.broadcast_to(jnp.max(s, axis=1, keepdims=True), m_old.shape))
        alpha = jnp.exp2(m_old - mnew)
        p = jnp.exp2(s - jnp.concatenate([mnew] * len(s_parts), axis=1) if len(s_parts) > 1 else s - mnew)
        l_ref[...] = alpha * l_ref[...] + jnp.broadcast_to(jnp.sum(p, axis=1, keepdims=True), m_old.shape)
        m_ref[...] = mnew
        pv = jnp.zeros(acc_ref.shape, F32)
        for r, vb in enumerate(vblocks):
            pv = pv + lax.dot_general(p[:, r * LANES:(r + 1) * LANES].astype(BF16), vb.astype(BF16),
                                      (((1,), (1,)), ((), ())), preferred_element_type=F32)
        acc_ref[...] = alpha * acc_ref[...] + pv

    update([kp[...] for kp in kpages], [vp[...] for vp in vpages], [sc_ref[r] for r in range(P)])

    @pl.when(j == n_steps - 1)
    def _():
        update([kn_ref[...]], [vn_ref[...]], [scn_ref[0]])
        o_ref[...] = acc_ref[...] / l_ref[...]


def _sample_attend(page_table, cache_k_t, cache_v_t, l, sc, thr, q_rows, k_new_t, v_new_t, *, P):
    NB, n_pages = page_table.shape
    R = q_rows.shape[1]
    TQ = R // N_HEADS
    n_steps = n_pages // P

    def page_spec(r):
        return pl.BlockSpec((None, None, KV_DIM, PAGE_SIZE), lambda b, j, pt: (l, pt[b, j * P + r], 0, 0))

    kern = functools.partial(_sample_attend_kernel, P=P, n_steps=n_steps)
    grid_spec = pltpu.PrefetchScalarGridSpec(
        num_scalar_prefetch=1,
        grid=(NB, n_steps),
        in_specs=[page_spec(r) for r in range(P)] + [page_spec(r) for r in range(P)] + [
            pl.BlockSpec((None, P, TQ, LANES), lambda b, j, pt: (b, j, 0, 0)),
            pl.BlockSpec((None, 1, TQ, LANES), lambda b, j, pt: (b, n_pages, 0, 0)),
            pl.BlockSpec((None, TQ, LANES), lambda b, j, pt: (b, 0, 0)),
            pl.BlockSpec((None, R, KV_DIM), lambda b, j, pt: (b, 0, 0)),
            pl.BlockSpec((None, KV_DIM, PAGE_SIZE), lambda b, j, pt: (b, 0, 0)),
            pl.BlockSpec((None, KV_DIM, PAGE_SIZE), lambda b, j, pt: (b, 0, 0))],
        out_specs=pl.BlockSpec((None, R, KV_DIM), lambda b, j, pt: (b, 0, 0)),
        scratch_shapes=[pltpu.VMEM((R, LANES), F32), pltpu.VMEM((R, LANES), F32), pltpu.VMEM((R, KV_DIM), F32)],
    )
    return pl.pallas_call(
        kern,
        grid_spec=grid_spec,
        out_shape=jax.ShapeDtypeStruct((NB, R, KV_DIM), F32),
        compiler_params=_cparams(("arbitrary", "arbitrary")),
        name="sample_attend",
    )(page_table, *([cache_k_t] * P), *([cache_v_t] * P), sc, sc, thr, q_rows, k_new_t, v_new_t)


def _mix_out_kernel(x_ref, g_ref, wg_ref, ca_ref, ao_ref, py_ref, wco_ref, wao_ref, wpo_ref, wo_ref, o_ref):
    x = x_ref[...]
    ms = jnp.mean(x * x, axis=-1, keepdims=True)
    h = (x * lax.rsqrt(ms + RMS_EPS) * g_ref[...]).astype(BF16)
    gates = jax.nn.sigmoid(jnp.dot(h, wg_ref[...], preferred_element_type=F32))
    y_a = jnp.dot(ca_ref[...], wco_ref[...], preferred_element_type=F32)
    y_b = jnp.dot(ao_ref[...], wao_ref[...], preferred_element_type=F32)
    y_c = jnp.dot(py_ref[...], wpo_ref[...], preferred_element_type=F32)
    merged = (gates[:, 0:D_MODEL] * y_a + gates[:, D_MODEL:2 * D_MODEL] * y_b
              + gates[:, 2 * D_MODEL:3 * D_MODEL] * y_c)
    o_ref[...] = x + jnp.dot(merged.astype(BF16), wo_ref[...], preferred_element_type=F32)


def _mix_out(x2, ca, ao, py, sw, l, *, TM):
    N = x2.shape[0]

    def tspec(c):
        return pl.BlockSpec((TM, c), lambda i: (i, 0))

    return pl.pallas_call(
        _mix_out_kernel,
        grid=(N // TM,),
        in_specs=[tspec(D_MODEL), _layer_spec((1, D_MODEL), l), _layer_spec((D_MODEL, N_BRANCH * D_MODEL), l),
                  tspec(D_CONV), tspec(D_ATTN), tspec(D_POOL),
                  _layer_spec((D_CONV, D_MODEL), l), _layer_spec((D_ATTN, D_MODEL), l),
                  _layer_spec((D_POOL, D_MODEL), l), _layer_spec((D_MODEL, D_MODEL), l)],
        out_specs=tspec(D_MODEL),
        out_shape=jax.ShapeDtypeStruct((N, D_MODEL), F32),
        compiler_params=_cparams(("arbitrary",)),
        name="mix_out",
    )(x2, sw["g_mix"], sw["w_g"], ca, ao, py, sw["w_conv_out"], sw["w_attn_out"], sw["w_pool_out"], sw["w_out"])


def _ffn_kernel(x_ref, g_ref, w1_ref, w2_ref, gf_ref, o_ref, *, final):
    x = x_ref[...]
    ms = jnp.mean(x * x, axis=-1, keepdims=True)
    h = (x * lax.rsqrt(ms + RMS_EPS) * g_ref[...]).astype(BF16)
    r = jnp.maximum(jnp.dot(h, w1_ref[...], preferred_element_type=F32), 0.0)
    y = x + jnp.dot((r * r).astype(BF16), w2_ref[...], preferred_element_type=F32)
    if final:
        ms2 = jnp.mean(y * y, axis=-1, keepdims=True)
        y = y * lax.rsqrt(ms2 + RMS_EPS) * gf_ref[...]
    o_ref[...] = y


def _ffn(x2, sw, l, g_final, *, TM, final):
    N = x2.shape[0]
    once = pl.Buffered(1)
    return pl.pallas_call(
        functools.partial(_ffn_kernel, final=final),
        grid=(N // TM,),
        in_specs=[pl.BlockSpec((TM, D_MODEL), lambda i: (i, 0)), _layer_spec((1, D_MODEL), l),
                  _layer_spec((D_MODEL, D_FF), l, pipeline_mode=once),
                  _layer_spec((D_FF, D_MODEL), l, pipeline_mode=once),
                  pl.BlockSpec((1, D_MODEL), lambda i: (0, 0))],
        out_specs=pl.BlockSpec((TM, D_MODEL), lambda i: (i, 0)),
        out_shape=jax.ShapeDtypeStruct((N, D_MODEL), F32),
        compiler_params=_cparams(("arbitrary",)),
        name="ffn",
    )(x2, sw["g_ffn"], sw["w_ff1"], sw["w_ff2"], g_final)


def _stacked_weights(g_mix, w_in, conv_w, conv_b, conv_ln_g, conv_ln_b, w_conv_out, w_attn_out,
                     pool_w, pool_scale, w_pool_out, w_out, g_ffn, w_ff1, w_ff2):
    depth = w_in.shape[0]
    o_ki = C_QI + IDX_HEADS * IDX_DIM
    o_wi = o_ki + IDX_DIM
    o_xc = o_wi + IDX_HEADS
    o_g = o_xc + D_POOL

    def pad_to_tile(a):
        return jnp.pad(a, ((0, 0), (0, 0), (0, LANES - a.shape[2])))

    w_a = jnp.concatenate([w_in[:, :, :o_ki], w_in[:, :, o_xc:o_g], pad_to_tile(w_in[:, :, o_ki:o_wi]),
                           pad_to_tile(w_in[:, :, o_wi:o_xc])], axis=2).astype(BF16)
    pw = jnp.zeros((depth, D_POOL, D_POOL), F32)
    for g in range(len(POOL_WINDOWS)):
        pw = pw.at[:, g * POOL_GC:(g + 1) * POOL_GC, g * POOL_GC:(g + 1) * POOL_GC].set(pool_w[:, g])

    def row(a):
        return a[:, None, :]

    return dict(
        g_mix=row(g_mix), w_a=w_a, w_g=w_in[:, :, o_g:].astype(BF16),
        conv_w=jnp.pad(conv_w, ((0, 0), (0, CONV_PAD - CONV_W), (0, 0))), conv_b=row(conv_b),
        ln_g=row(conv_ln_g), ln_b=row(conv_ln_b),
        pool_w=pw.astype(BF16), pool_scale=row(pool_scale),
        w_conv_out=w_conv_out.astype(BF16), w_attn_out=w_attn_out.astype(BF16),
        w_pool_out=w_pool_out.astype(BF16), w_out=w_out.astype(BF16),
        g_ffn=row(g_ffn), w_ff1=w_ff1.astype(BF16), w_ff2=w_ff2.astype(BF16))


def _tail_state(hist, new, keep):
    return jnp.concatenate([hist, new], axis=1)[:, -keep:]


def kernel(x_prompt, x_sample, cache_k, cache_v, cache_kidx, state_conv, state_pool, page_table, g_mix, w_in,
           conv_w, conv_b, conv_ln_g, conv_ln_b, w_conv_out, w_attn_out, pool_w, pool_scale, w_pool_out, w_out,
           g_ffn, w_ff1, w_ff2, g_final):
    B, T, _ = x_prompt.shape
    NB, TS, _ = x_sample.shape
    depth = w_in.shape[0]
    n_pool = cache_k.shape[1]
    past_len = page_table.shape[1] * PAGE_SIZE
    TM_P = min(512, T)
    gf = g_final[None]
    xp = x_prompt
    xs = x_sample
    zero_conv = jnp.zeros((B, CONV_PAD, D_CONV), F32)
    zero_pool = jnp.zeros((B, POOL_PAD, D_POOL), F32)
    cache_k_t = cache_k.transpose(0, 1, 3, 4, 2).reshape(depth, n_pool, KV_DIM, PAGE_SIZE)
    cache_v_t = cache_v.transpose(0, 1, 3, 4, 2).reshape(depth, n_pool, KV_DIM, PAGE_SIZE)
    cache_kidx_t = cache_kidx.transpose(0, 1, 3, 2)
    sw = _stacked_weights(g_mix, w_in, conv_w, conv_b, conv_ln_g, conv_ln_b, w_conv_out, w_attn_out,
                          pool_w, pool_scale, w_pool_out, w_out, g_ffn, w_ff1, w_ff2)
    tables_p = _rope_tables(0, T)
    tables_s = _rope_tables(past_len, TS)
    hist_conv_s = jnp.pad(state_conv, ((0, 0), (0, 0), (CONV_PAD - CONV_HIST, 0), (0, 0)))
    hist_pool_s = jnp.pad(state_pool, ((0, 0), (0, 0), (POOL_PAD - POOL_HIST, 0), (0, 0)))
    st_p = [[] for _ in range(5)]
    st_s = [[] for _ in range(5)]
    for l in range(depth):
        final = l == depth - 1

        (k, v, ki, u, xc, kibf, q, _, _, ca, py, qit, wq, kt, va) = _proj(
            xp, 0, tables_p, zero_conv, zero_pool, sw, l, G=1, TM=TM_P, adt=BF16, attn=True)
        ao = _attend_prompt(q, qit, wq, kibf, kt, va, CKS=TM_P)
        x2 = _mix_out(xp.reshape(B * T, D_MODEL), ca.reshape(B * T, D_CONV), ao.reshape(B * T, D_ATTN),
                      py.reshape(B * T, D_POOL), sw, l, TM=TM_P)
        xp = _ffn(x2, sw, l, gf, TM=TM_P, final=final).reshape(B, T, D_MODEL)
        st_p[0].append(k.reshape(B, T, N_KV_HEADS, HEAD_DIM))
        st_p[1].append(v.reshape(B, T, N_KV_HEADS, HEAD_DIM))
        st_p[2].append(ki)
        st_p[3].append(u[:, T - CONV_HIST:])
        st_p[4].append(xc[:, T - POOL_HIST:])

        (k, v, ki, u, xc, _, q, qi, wi, ca, py) = _proj(
            xs, past_len, tables_s, hist_conv_s[l], hist_pool_s[l], sw, l, G=NB, TM=TS, adt=F32, attn=False)
        q, qi, ca, py = q.astype(BF16), qi.astype(BF16), ca.astype(BF16), py.astype(BF16)
        qi_rows = qi.reshape(NB, TS, IDX_HEADS, IDX_DIM).transpose(0, 2, 1, 3).reshape(NB, IDX_HEADS * TS, IDX_DIM)
        w_rows = jnp.broadcast_to(wi[:, :, :IDX_HEADS].transpose(0, 2, 1).reshape(NB, IDX_HEADS * TS, 1),
                                  (NB, IDX_HEADS * TS, LANES))
        qh = q.reshape(NB, TS, N_HEADS, HEAD_DIM).transpose(0, 2, 1, 3)
        head_kv = (jnp.arange(N_HEADS) // GROUP)[None, :, None, None]
        q_rows = jnp.concatenate([jnp.where(head_kv == n, qh, jnp.zeros_like(qh)) for n in range(N_KV_HEADS)],
                                 axis=-1).reshape(NB, N_HEADS * TS, KV_DIM)
        pad_rows = ((0, 0), (0, PAGE_SIZE - TS), (0, 0))
        pages_per_step = min(32, page_table.shape[1])

        def new_page_t(a):
            return jnp.pad(a, pad_rows).transpose(0, 2, 1)

        sc, thr = _sample_scores(page_table, cache_kidx_t, l, qi_rows, w_rows, new_page_t(ki), P=pages_per_step)
        o_rows = _sample_attend(page_table, cache_k_t, cache_v_t, l, sc, thr, q_rows, new_page_t(k), new_page_t(v),
                                P=pages_per_step)
        o5 = o_rows.reshape(NB, N_KV_HEADS, GROUP, TS, N_KV_HEADS, HEAD_DIM)
        ao = jnp.stack([o5[:, n, :, :, n, :] for n in range(N_KV_HEADS)], axis=1)
        ao = ao.reshape(NB, N_HEADS, TS, HEAD_DIM).transpose(0, 2, 1, 3).reshape(NB * TS, D_ATTN).astype(BF16)
        x2 = _mix_out(xs.reshape(NB * TS, D_MODEL), ca.reshape(NB * TS, D_CONV), ao,
                      py.reshape(NB * TS, D_POOL), sw, l, TM=NB * TS)
        xs = _ffn(x2, sw, l, gf, TM=NB * TS, final=final).reshape(NB, TS, D_MODEL)
        st_s[0].append(k.reshape(NB, TS, N_KV_HEADS, HEAD_DIM))
        st_s[1].append(v.reshape(NB, TS, N_KV_HEADS, HEAD_DIM))
        st_s[2].append(ki)
        st_s[3].append(_tail_state(state_conv[l], u, CONV_HIST))
        st_s[4].append(_tail_state(state_pool[l], xc, POOL_HIST))

    sp = [jnp.stack(a, axis=0) for a in st_p]
    ss = [jnp.stack(a, axis=0) for a in st_s]
    return (xp, xs, sp[0], sp[1], sp[2], sp[3], sp[4], ss[0], ss[1], ss[2], ss[3], ss[4])
```

```python
import functools

import jax
import jax.numpy as jnp
from jax import lax
from jax.experimental import pallas as pl
from jax.experimental.pallas import tpu as pltpu

F32 = jnp.float32
BF16 = jnp.bfloat16

D_MODEL = 1024
D_CONV = 256
CONV_W = 31
CONV_HIST = CONV_W - 1
HEAD_DIM = 64
D_ATTN = 512
N_HEADS = 8
N_KV_HEADS = 2
KV_DIM = N_KV_HEADS * HEAD_DIM
GROUP = N_HEADS // N_KV_HEADS
IDX_HEADS = 8
IDX_DIM = 64
TOPK_MAX = 256
D_POOL = 256
POOL_WINDOWS = (2, 4, 8, 16)
POOL_GC = D_POOL // len(POOL_WINDOWS)
POOL_HIST = max(POOL_WINDOWS) - 1
N_BRANCH = 3
D_FF = 4 * D_MODEL
ROPE_THETA = 10000.0
RMS_EPS = 1e-6
LN_EPS = 1e-5
PAGE_SIZE = 128

LANES = 128
CONV_PAD = 32
POOL_PAD = 16
Q_SCALE = HEAD_DIM ** -0.5 * 1.4426950408889634
BIG = 3.0e38
NEG_INF = float("-inf")

C_AIN, C_AGATE, C_Q, C_K, C_V, C_QI, C_XC, C_KI, C_WI, D_A = 0, 256, 512, 1024, 1152, 1280, 1792, 2048, 2176, 2304

VMEM_LIMIT = 56 * 1024 * 1024


def _cparams(sem):
    return pltpu.CompilerParams(dimension_semantics=sem, vmem_limit_bytes=VMEM_LIMIT)


def _layer_spec(shape, l, **kw):
    n = len(shape)
    return pl.BlockSpec((None,) + tuple(shape), lambda *_: (l,) + (0,) * n, **kw)


def _proj_kernel(x_ref, g_ref, w_ref, cos_ref, sin_ref, chist_ref, phist_ref, convw_ref, convb_ref,
                 lng_ref, lnb_ref, poolw_ref, pscale_ref,
                 k_ref, v_ref, ki_ref, u_ref, xc_ref, kibf_ref, q_ref, qi_ref, wi_ref, ca_ref, py_ref,
                 *rest, G, TM, pos0, n_tiles, attn):
    if attn:
        qit_ref, wq_ref, kt_ref, va_ref = rest[:4]
        rest = rest[4:]
    cs_ref, ps_ref, p2_ref, p4_ref, p8_ref, p16_ref = rest
    j = pl.program_id(1)
    R = G * TM
    x = x_ref[...].reshape(R, D_MODEL)
    ms = jnp.mean(x * x, axis=-1, keepdims=True)
    h = (x * lax.rsqrt(ms + RMS_EPS) * g_ref[...]).astype(BF16)
    z = jnp.dot(h, w_ref[...], preferred_element_type=F32)

    cos = cos_ref[...]
    sin = sin_ref[...]
    if G > 1:
        cos = jnp.concatenate([cos] * G, axis=0)
        sin = jnp.concatenate([sin] * G, axis=0)

    def rope(t):
        n = t.shape[1] // LANES
        c = jnp.concatenate([cos] * n, axis=1) if n > 1 else cos
        s = jnp.concatenate([sin] * n, axis=1) if n > 1 else sin
        lane = lax.broadcasted_iota(jnp.int32, t.shape, 1)
        first = (lane & (HEAD_DIM - 1)) < HEAD_DIM // 2
        sw = jnp.where(first, pltpu.roll(t, t.shape[1] - HEAD_DIM // 2, 1), pltpu.roll(t, HEAD_DIM // 2, 1))
        return t * c + sw * s

    q = rope(z[:, C_Q:C_Q + D_ATTN]) * Q_SCALE
    k = rope(z[:, C_K:C_K + KV_DIM])
    v = z[:, C_V:C_V + KV_DIM]
    qi = rope(z[:, C_QI:C_QI + IDX_HEADS * IDX_DIM])
    ki = rope(z[:, C_KI:C_KI + LANES])[:, :IDX_DIM]
    wi = z[:, C_WI:C_WI + LANES] * (IDX_HEADS ** -0.5 * IDX_DIM ** -0.5)

    k_ref[...] = k.reshape(G, TM, KV_DIM)
    v_ref[...] = v.reshape(G, TM, KV_DIM)
    ki_ref[...] = ki.reshape(G, TM, IDX_DIM)
    kibf_ref[...] = ki.reshape(G, TM, IDX_DIM).astype(kibf_ref.dtype)
    q_ref[...] = q.reshape(G, TM, D_ATTN).astype(q_ref.dtype)
    qi_ref[...] = qi.reshape(G, TM, IDX_HEADS * IDX_DIM).astype(qi_ref.dtype)
    wi_ref[...] = wi.reshape(G, TM, LANES)
    if attn:
        for b in range(TM // LANES):
            blk = slice(b * LANES, (b + 1) * LANES)
            qi_t = qi[blk, :].T
            qit_ref[b] = jnp.concatenate([qi_t[hh * IDX_DIM:(hh + 1) * IDX_DIM, :] for hh in range(IDX_HEADS)],
                                         axis=1).astype(BF16)
            wi_t = wi[blk, :].T
            wq_ref[b] = jnp.concatenate([wi_t[hh:hh + 1, :] for hh in range(IDX_HEADS)], axis=1)
        k_t = k.T
        lane = lax.broadcasted_iota(jnp.int32, (TM, KV_DIM), 1)
        tail = jnp.where(lane == HEAD_DIM, 1.0, 0.0)
        for n in range(N_KV_HEADS):
            kt_ref[0, n] = k_t[n * HEAD_DIM:(n + 1) * HEAD_DIM, :].astype(BF16)
            vn = v if n == 0 else pltpu.roll(v, KV_DIM - n * HEAD_DIM, 1)
            va_ref[0, n] = jnp.where(lane < HEAD_DIM, vn, tail).astype(BF16)

    a_in = z[:, C_AIN:C_AIN + D_CONV]
    a_gate = z[:, C_AGATE:C_AGATE + D_CONV]
    u = (a_in * jax.nn.sigmoid(a_gate)).reshape(G, TM, D_CONV)
    u_ref[...] = u

    @pl.when(j == 0)
    def _():
        cs_ref[:, 0:CONV_PAD, :] = chist_ref[...]
        ps_ref[:, 0:POOL_PAD, :] = phist_ref[...]

    cs_ref[:, CONV_PAD:CONV_PAD + TM, :] = u
    off = CONV_PAD - CONV_HIST
    rc = min(TM, 64)
    gc = min(G, 8)
    for g0 in range(0, G, gc):
        for r0 in range(0, TM, rc):
            acc = jnp.zeros((gc, rc, D_CONV), F32) + convb_ref[...][None]
            for t in range(CONV_W):
                acc = acc + convw_ref[t:t + 1, :][None] * cs_ref[g0:g0 + gc, off + t + r0:off + t + r0 + rc, :]
            mu = jnp.mean(acc, axis=-1, keepdims=True)
            var = jnp.mean(jnp.square(acc - mu), axis=-1, keepdims=True)
            y = (acc - mu) * lax.rsqrt(var + LN_EPS) * lng_ref[...][None] + lnb_ref[...][None]
            ca_ref[g0:g0 + gc, r0:r0 + rc, :] = (y * jax.nn.sigmoid(y)).astype(ca_ref.dtype)
    if n_tiles > 1:
        cs_ref[:, 0:CONV_PAD, :] = cs_ref[:, TM:TM + CONV_PAD, :]

    xc = z[:, C_XC:C_XC + D_POOL].reshape(G, TM, D_POOL)
    xc_ref[...] = xc
    ps_ref[:, POOL_PAD:POOL_PAD + TM, :] = xc
    RR = POOL_PAD + TM
    p2_ref[:, 1:RR, :] = ps_ref[:, 1:RR, :] + ps_ref[:, 0:RR - 1, :]
    p4_ref[:, 3:RR, :] = p2_ref[:, 3:RR, :] + p2_ref[:, 1:RR - 2, :]
    p8_ref[:, 7:RR, :] = p4_ref[:, 7:RR, :] + p4_ref[:, 3:RR - 4, :]
    p16_ref[:, 15:RR, :] = p8_ref[:, 15:RR, :] + p8_ref[:, 7:RR - 8, :]
    lane = lax.broadcasted_iota(jnp.int32, (G, TM, D_POOL), 2)
    row = lax.broadcasted_iota(jnp.int32, (G, TM, D_POOL), 1)
    g0m, g1m, g2m = lane < POOL_GC, lane < 2 * POOL_GC, lane < 3 * POOL_GC
    s_tok = slice(POOL_PAD, POOL_PAD + TM)
    wsum = jnp.where(g0m, p2_ref[:, s_tok, :],
                     jnp.where(g1m, p4_ref[:, s_tok, :], jnp.where(g2m, p8_ref[:, s_tok, :], p16_ref[:, s_tok, :])))
    win = jnp.where(g0m, POOL_WINDOWS[0], jnp.where(g1m, POOL_WINDOWS[1],
                                                     jnp.where(g2m, POOL_WINDOWS[2], POOL_WINDOWS[3])))
    cnt = jnp.minimum(pos0 + j * TM + row + 1, win).astype(F32)
    d = (wsum / cnt - xc).reshape(R, D_POOL).astype(BF16)
    py = jnp.dot(d, poolw_ref[...], preferred_element_type=F32) * pscale_ref[...]
    py_ref[...] = py.reshape(G, TM, D_POOL).astype(py_ref.dtype)
    if n_tiles > 1:
        ps_ref[:, 0:POOL_PAD, :] = ps_ref[:, TM:TM + POOL_PAD, :]


def _rope_tables(pos0, T):
    half = HEAD_DIM // 2
    pos = (pos0 + jnp.arange(T)).astype(F32)
    inv = ROPE_THETA ** (-jnp.arange(half, dtype=F32) / half)
    ang = pos[:, None] * inv[None, :]
    cos = jnp.tile(jnp.cos(ang), (1, 2 * LANES // HEAD_DIM))
    sin = jnp.tile(jnp.concatenate([-jnp.sin(ang), jnp.sin(ang)], axis=1), (1, LANES // HEAD_DIM))
    return cos, sin


def _proj(x3, pos0, tables, chist, phist, sw, l, *, G, TM, adt, attn):
    NS, T, _ = x3.shape
    n_tiles = T // TM
    cos, sin = tables

    def tok(c, dt=F32):
        return jax.ShapeDtypeStruct((NS, T, c), dt)

    def tspec(c):
        return pl.BlockSpec((G, TM, c), lambda s, j: (s, j, 0))

    out_cols = [(KV_DIM, F32), (KV_DIM, F32), (IDX_DIM, F32), (D_CONV, F32), (D_POOL, F32),
                (IDX_DIM, adt), (D_ATTN, adt), (IDX_HEADS * IDX_DIM, adt), (LANES, F32), (D_CONV, adt), (D_POOL, adt)]
    out_specs = [tspec(c) for c, _ in out_cols]
    out_shape = [tok(c, dt) for c, dt in out_cols]
    if attn:
        assert G == 1 and TM % LANES == 0
        nb = TM // LANES
        out_specs += [pl.BlockSpec((None, nb, IDX_DIM, IDX_HEADS * LANES), lambda s, j: (s, j, 0, 0)),
                      pl.BlockSpec((None, nb, 1, IDX_HEADS * LANES), lambda s, j: (s, j, 0, 0)),
                      pl.BlockSpec((None, 1, N_KV_HEADS, HEAD_DIM, TM), lambda s, j: (s, j, 0, 0, 0)),
                      pl.BlockSpec((None, 1, N_KV_HEADS, TM, LANES), lambda s, j: (s, j, 0, 0, 0))]
        out_shape += [jax.ShapeDtypeStruct((NS, T // LANES, IDX_DIM, IDX_HEADS * LANES), BF16),
                      jax.ShapeDtypeStruct((NS, T // LANES, 1, IDX_HEADS * LANES), F32),
                      jax.ShapeDtypeStruct((NS, n_tiles, N_KV_HEADS, HEAD_DIM, TM), BF16),
                      jax.ShapeDtypeStruct((NS, n_tiles, N_KV_HEADS, TM, LANES), BF16)]
    kern = functools.partial(_proj_kernel, G=G, TM=TM, pos0=pos0, n_tiles=n_tiles, attn=attn)
    return pl.pallas_call(
        kern,
        grid=(NS // G, n_tiles),
        in_specs=[tspec(D_MODEL), _layer_spec((1, D_MODEL), l), _layer_spec((D_MODEL, D_A), l),
                  pl.BlockSpec((TM, LANES), lambda s, j: (j, 0)), pl.BlockSpec((TM, LANES), lambda s, j: (j, 0)),
                  pl.BlockSpec((G, CONV_PAD, D_CONV), lambda s, j: (s, 0, 0)),
                  pl.BlockSpec((G, POOL_PAD, D_POOL), lambda s, j: (s, 0, 0)),
                  _layer_spec((CONV_PAD, D_CONV), l), _layer_spec((1, D_CONV), l), _layer_spec((1, D_CONV), l),
                  _layer_spec((1, D_CONV), l), _layer_spec((D_POOL, D_POOL), l), _layer_spec((1, D_POOL), l)],
        out_specs=out_specs,
        out_shape=out_shape,
        scratch_shapes=[pltpu.VMEM((G, CONV_PAD + TM, D_CONV), F32)] +
                       [pltpu.VMEM((G, POOL_PAD + TM, D_POOL), F32)] * 5,
        compiler_params=_cparams(("arbitrary", "arbitrary")),
        name="proj",
    )(x3, sw["g_mix"], sw["w_a"], cos, sin, chist, phist, sw["conv_w"], sw["conv_b"], sw["ln_g"], sw["ln_b"],
      sw["pool_w"], sw["pool_scale"])


def _kth_bisect(count, smin, smax, nvalid, kk, steps, warm=0, zero_counts=None):
    zero = jnp.zeros_like(smin)
    c_ge0, c_gt0 = zero_counts if zero_counts is not None else (count(zero, False), count(zero, True))
    pos = c_gt0 >= kk
    neg = c_ge0 < kk
    lo0 = jnp.where(neg, smin, zero)
    clo0 = jnp.where(neg, nvalid, c_ge0)
    hi0 = jnp.where(pos, BIG, zero)
    chi0 = jnp.where(pos, zero, jnp.where(neg, c_ge0, c_gt0))
    cand0 = jnp.where(pos, smax, smin * 0.5)
    small = nvalid <= kk
    done0 = (small | jnp.logical_not(pos | neg)).astype(F32)
    init = (jnp.where(small, smin, lo0), hi0, jnp.where(small, nvalid, clo0), chi0, cand0, done0)

    def count_ge(cand):
        return count(cand, False)

    def cond(st):
        return jnp.min(st[5]) < 0.5

    def body(st):
        for _ in range(steps):
            st = step(st)
        return st

    def step(st):
        lo, hi, clo, chi, cand, done = st
        c = count_ge(cand)
        act = done < 0.5
        ge = c >= kk
        up = act & ge
        dn = act & jnp.logical_not(ge)
        lo = jnp.where(up, cand, lo)
        clo = jnp.where(up, c, clo)
        hi = jnp.where(dn, cand, hi)
        chi = jnp.where(dn, c, chi)
        mid = lo * 0.5 + hi * 0.5
        inside = (mid > lo) & (mid < hi)
        fin = (clo == kk) | jnp.logical_not(inside)
        done = jnp.where(fin, 1.0, done)
        return lo, hi, clo, chi, mid, done

    if warm:
        init = lax.fori_loop(0, warm, lambda _, st: step(st), init)
    lo, _, clo, chi, _, _ = lax.while_loop(cond, body, init)
    return lo, clo, chi


def _attend_prompt_kernel(qit_ref, wq_ref, q_ref, ki_ref, kt_ref, va_ref, o_ref, sc_ref, acc_ref, m_ref, *,
                          CKS, CK, kk):
    i = pl.program_id(1)
    QB = LANES
    nchs = (i * QB + QB + CKS - 1) // CKS
    nch = nchs * (CKS // CK)
    qit = qit_ref[...]
    wq = wq_ref[...]
    qcol = i * QB + lax.broadcasted_iota(jnp.int32, (CKS, QB), 1)
    krow0 = lax.broadcasted_iota(jnp.int32, (CKS, QB), 0)

    def score_chunk(c, carry, causal):
        smax, smin, n_ge0, n_gt0 = carry
        s = jnp.dot(ki_ref[c], qit, preferred_element_type=F32)
        s = jnp.maximum(s, 0.0) * wq
        tot = s[:, 0:QB]
        for hh in range(1, IDX_HEADS):
            tot = tot + s[:, hh * QB:(hh + 1) * QB]
        if causal:
            valid = (krow0 + c * CKS) <= qcol
            masked = jnp.where(valid, tot, NEG_INF)
            lowest = jnp.where(valid, tot, BIG)
        else:
            masked = lowest = tot
        sc_ref[pl.ds(pl.multiple_of(c * CKS, CKS), CKS), :] = masked
        smax = jnp.maximum(smax, jnp.max(masked.reshape(CKS // 8, 8, QB), axis=0))
        smin = jnp.minimum(smin, jnp.min(lowest.reshape(CKS // 8, 8, QB), axis=0))
        n_ge0 = n_ge0 + jnp.sum(jnp.where(masked >= 0.0, 1.0, 0.0).reshape(CKS // 8, 8, QB), axis=0)
        n_gt0 = n_gt0 + jnp.sum(jnp.where(masked > 0.0, 1.0, 0.0).reshape(CKS // 8, 8, QB), axis=0)
        return smax, smin, n_ge0, n_gt0

    carry = lax.fori_loop(0, nchs - 1, functools.partial(score_chunk, causal=False),
                          (jnp.full((8, QB), NEG_INF, F32), jnp.full((8, QB), BIG, F32),
                           jnp.zeros((8, QB), F32), jnp.zeros((8, QB), F32)))
    smax, smin, n_ge0, n_gt0 = score_chunk(nchs - 1, carry, causal=True)
    smax = jnp.max(smax, axis=0, keepdims=True)
    smin = jnp.min(smin, axis=0, keepdims=True)
    zero_counts = (jnp.sum(n_ge0, axis=0, keepdims=True), jnp.sum(n_gt0, axis=0, keepdims=True))

    n_acc = 4
    rows = CKS // n_acc

    def count(cand, strict):
        def body(c, accs):
            base = pl.multiple_of(c * CKS, CKS)
            out = []
            for a in range(n_acc):
                blk = sc_ref[pl.ds(base + a * rows, rows), :]
                hit = (blk > cand) if strict else (blk >= cand)
                ones = jnp.where(hit, 1.0, 0.0).reshape(rows // 8, 8, QB)
                out.append(accs[a] + jnp.sum(ones, axis=0))
            return tuple(out)
        def pair(j, accs):
            return body(2 * j + 1, body(2 * j, accs))
        accs = lax.fori_loop(0, nchs // 2, pair, tuple(jnp.zeros((8, QB), F32) for _ in range(n_acc)))
        accs = lax.cond(nchs % 2 == 1, lambda t: body(nchs - 1, t), lambda t: t, accs)
        return jnp.sum((accs[0] + accs[1]) + (accs[2] + accs[3]), axis=0, keepdims=True)

    nvalid = (i * QB + lax.broadcasted_iota(jnp.int32, (1, QB), 1) + 1).astype(F32)
    lo, clo, chi = _kth_bisect(count, smin, smax, nvalid, float(kk), steps=4, warm=12, zero_counts=zero_counts)

    fix = (clo > kk) & (nvalid > kk)

    @pl.when(jnp.max(fix.astype(F32)) > 0.5)
    def _():
        keep = kk - chi
        TF = min(256, CKS)
        tri = (lax.broadcasted_iota(jnp.int32, (TF, TF), 1)
               <= lax.broadcasted_iota(jnp.int32, (TF, TF), 0)).astype(BF16)

        def body(c, run):
            rows_c = pl.ds(pl.multiple_of(c * TF, TF), TF)
            blk = sc_ref[rows_c, :]
            tie = jnp.where(fix & (blk == lo), 1.0, 0.0)
            rank = run + jnp.dot(tri, tie.astype(BF16), preferred_element_type=F32)
            sc_ref[rows_c, :] = jnp.where((tie > 0.5) & (rank > keep), NEG_INF, blk)
            return run + jnp.sum(tie, axis=0, keepdims=True)
        lax.fori_loop(0, nchs * (CKS // TF), body, jnp.zeros((1, QB), F32))

    GQ = GROUP * QB
    q32 = q_ref[...].astype(F32)
    qn = [jnp.concatenate([q32[:, hh * HEAD_DIM:(hh + 1) * HEAD_DIM] for hh in range(n * GROUP, (n + 1) * GROUP)],
                          axis=0).astype(BF16) for n in range(N_KV_HEADS)]
    acc_ref[...] = jnp.zeros_like(acc_ref)
    m_ref[...] = jnp.full_like(m_ref, -BIG)
    reps = CK // LANES

    def att_chunk(c):
        scb = sc_ref[pl.ds(pl.multiple_of(c * CK, CK), CK), :]
        bias_t = jnp.where(scb >= lo, 0.0, NEG_INF).T
        for n in range(N_KV_HEADS):
            s = jnp.dot(qn[n], kt_ref[c, n], preferred_element_type=F32)
            s = (s.reshape(GROUP, QB, CK) + bias_t[None]).reshape(GQ, CK)
            m_old = m_ref[n]
            mnew = jnp.maximum(m_old, jnp.max(s, axis=1, keepdims=True))
            p = jnp.exp2(s - jnp.concatenate([mnew] * reps, axis=1))
            alpha = jnp.exp2(m_old - mnew)
            pv = jnp.dot(p.astype(BF16), va_ref[c, n], preferred_element_type=F32)
            acc_ref[n] = alpha * acc_ref[n] + pv
            m_ref[n] = mnew

    def att_pair(j, _):
        att_chunk(2 * j)
        att_chunk(2 * j + 1)
        return 0

    lax.fori_loop(0, nch // 2, att_pair, 0)

    @pl.when(nch % 2 == 1)
    def _():
        att_chunk(nch - 1)

    outs = []
    for n in range(N_KV_HEADS):
        a = acc_ref[n]
        o = a[:, :HEAD_DIM] / a[:, HEAD_DIM:HEAD_DIM + 1]
        outs += [o[g * QB:(g + 1) * QB, :] for g in range(GROUP)]
    o_ref[...] = jnp.concatenate(outs, axis=1).astype(BF16)


def _attend_prompt(q, qit, wq, kibf, kt, va, *, CKS):
    B, T, _ = q.shape
    QB = LANES
    nq = T // QB
    nc, CK = kt.shape[1], kt.shape[4]
    ncs = T // CKS
    kk = min(TOPK_MAX, T // 4)
    kib = kibf.reshape(B, ncs, CKS, IDX_DIM)

    kern = functools.partial(_attend_prompt_kernel, CKS=CKS, CK=CK, kk=kk)
    return pl.pallas_call(
        kern,
        grid=(B, nq),
        in_specs=[pl.BlockSpec((None, None, IDX_DIM, 8 * QB), lambda b, i: (b, i, 0, 0)),
                  pl.BlockSpec((None, None, 1, 8 * QB), lambda b, i: (b, i, 0, 0)),
                  pl.BlockSpec((None, QB, D_ATTN), lambda b, i: (b, i, 0)),
                  pl.BlockSpec((None, ncs, CKS, IDX_DIM), lambda b, i: (b, 0, 0, 0)),
                  pl.BlockSpec((None, nc, N_KV_HEADS, HEAD_DIM, CK), lambda b, i: (b, 0, 0, 0, 0)),
                  pl.BlockSpec((None, nc, N_KV_HEADS, CK, LANES), lambda b, i: (b, 0, 0, 0, 0))],
        out_specs=pl.BlockSpec((None, QB, D_ATTN), lambda b, i: (b, i, 0)),
        out_shape=jax.ShapeDtypeStruct((B, T, D_ATTN), BF16),
        scratch_shapes=[pltpu.VMEM((T, QB), F32), pltpu.VMEM((N_KV_HEADS, GROUP * QB, LANES), F32),
                        pltpu.VMEM((N_KV_HEADS, GROUP * QB, LANES), F32)],
        compiler_params=_cparams(("arbitrary", "arbitrary")),
        name="attend_prompt",
    )(qit, wq, q, kib, kt, va)


def _sample_score_kernel(pt_ref, *refs, P, n_steps, past_len, kk):
    pages = refs[:P]
    qi_ref, wb_ref, kin_ref, sc_ref, thr_ref = refs[P:P + 5]
    j = pl.program_id(1)
    TQ = sc_ref.shape[1]
    NBLK = sc_ref.shape[0]
    qi = qi_ref[...]
    wb = wb_ref[...]

    def block_scores(kpage_t):
        s = jnp.dot(qi, kpage_t.astype(BF16), preferred_element_type=F32)
        s = jnp.maximum(s, 0.0) * wb
        return jnp.sum(s.reshape(IDX_HEADS, TQ, LANES), axis=0)

    for r in range(P):
        sc_ref[j * P + r] = block_scores(pages[r][...])

    @pl.when(j == n_steps - 1)
    def _():
        tot = block_scores(kin_ref[...])
        qrow = lax.broadcasted_iota(jnp.int32, (TQ, LANES), 0)
        kcol = lax.broadcasted_iota(jnp.int32, (TQ, LANES), 1)
        sc_ref[NBLK - 1] = jnp.where(kcol <= qrow, tot, NEG_INF)

        def lane_bcast(a):
            return jnp.broadcast_to(a, (TQ, LANES))

        sc = sc_ref[...]
        smax = lane_bcast(jnp.max(jnp.max(sc, axis=0), axis=1, keepdims=True))
        smin = lane_bcast(jnp.min(jnp.min(jnp.where(sc > NEG_INF, sc, BIG), axis=0), axis=1, keepdims=True))
        nvalid = (past_len + qrow + 1).astype(F32)

        def count(cand, strict):
            blk = sc_ref[...]
            hit = (blk > cand[None]) if strict else (blk >= cand[None])
            c = jnp.sum(jnp.where(hit, 1.0, 0.0), axis=0)
            return lane_bcast(jnp.sum(c, axis=1, keepdims=True))

        lo, clo, chi = _kth_bisect(count, smin, smax, nvalid, float(kk), steps=4)
        thr_ref[...] = lo

        fix = (clo > kk) & (nvalid > kk)

        @pl.when(jnp.max(fix.astype(F32)) > 0.5)
        def _():
            keep = kk - chi
            tri = (lax.broadcasted_iota(jnp.int32, (LANES, LANES), 0)
                   <= lax.broadcasted_iota(jnp.int32, (LANES, LANES), 1)).astype(BF16)

            def body(c, run):
                blk = sc_ref[c]
                tie = fix & (blk == lo)
                rank = run + jnp.dot(tie.astype(BF16), tri, preferred_element_type=F32)
                sc_ref[c] = jnp.where(tie & (rank > keep), NEG_INF, blk)
                return run + lane_bcast(jnp.sum(tie.astype(F32), axis=1, keepdims=True))
            lax.fori_loop(0, NBLK, body, jnp.zeros((TQ, LANES), F32))


def _sample_scores(page_table, cache_kidx_t, l, qi_rows, w_rows, ki_new_t, *, P):
    NB, n_pages = page_table.shape
    TQ = qi_rows.shape[1] // IDX_HEADS
    n_steps = n_pages // P
    past_len = n_pages * PAGE_SIZE
    kk = min(TOPK_MAX, (past_len + TQ) // 4)
    NBLK = n_pages + 1

    def page_spec(r):
        return pl.BlockSpec((None, None, IDX_DIM, PAGE_SIZE), lambda b, j, pt: (l, pt[b, j * P + r], 0, 0))

    kern = functools.partial(_sample_score_kernel, P=P, n_steps=n_steps, past_len=past_len, kk=kk)
    grid_spec = pltpu.PrefetchScalarGridSpec(
        num_scalar_prefetch=1,
        grid=(NB, n_steps),
        in_specs=[page_spec(r) for r in range(P)] + [
            pl.BlockSpec((None, IDX_HEADS * TQ, IDX_DIM), lambda b, j, pt: (b, 0, 0)),
            pl.BlockSpec((None, IDX_HEADS * TQ, LANES), lambda b, j, pt: (b, 0, 0)),
            pl.BlockSpec((None, IDX_DIM, PAGE_SIZE), lambda b, j, pt: (b, 0, 0))],
        out_specs=[pl.BlockSpec((None, NBLK, TQ, LANES), lambda b, j, pt: (b, 0, 0, 0)),
                   pl.BlockSpec((None, TQ, LANES), lambda b, j, pt: (b, 0, 0))],
    )
    return pl.pallas_call(
        kern,
        grid_spec=grid_spec,
        out_shape=[jax.ShapeDtypeStruct((NB, NBLK, TQ, LANES), F32), jax.ShapeDtypeStruct((NB, TQ, LANES), F32)],
        compiler_params=_cparams(("arbitrary", "arbitrary")),
        name="sample_scores",
    )(page_table, *([cache_kidx_t] * P), qi_rows, w_rows, ki_new_t)


def _sample_attend_kernel(pt_ref, *refs, P, n_steps):
    kpages = refs[:P]
    vpages = refs[P:2 * P]
    sc_ref, scn_ref, thr_ref, q_ref, kn_ref, vn_ref, o_ref, m_ref, l_ref, acc_ref = refs[2 * P:2 * P + 10]
    j = pl.program_id(1)
    qz = q_ref[...]
    thr = thr_ref[...]

    @pl.when(j == 0)
    def _():
        m_ref[...] = jnp.full_like(m_ref, -BIG)
        l_ref[...] = jnp.zeros_like(l_ref)
        acc_ref[...] = jnp.zeros_like(acc_ref)

    def update(kblocks, vblocks, scblocks):
        s_parts = []
        for kb, scb in zip(kblocks, scblocks):
            s = jnp.dot(qz, kb.astype(BF16), preferred_element_type=F32)
            bias = jnp.where(scb >= thr, 0.0, NEG_INF)
            s_parts.append(s + jnp.concatenate([bias] * N_HEADS, axis=0))
        s = jnp.concatenate(s_parts, axis=1) if len(s_parts) > 1 else s_parts[0]
        m_old = m_ref[...]
        mnew = jnp.maximum(m_old, jnp.broadcast_to(jnp.max(s, axis=1, keepdims=True), m_old.shape))
        alpha = jnp.exp2(m_old - mnew)
        p = jnp.exp2(s - jnp.concatenate([mnew] * len(s_parts), axis=1) if len(s_parts) > 1 else s - mnew)
        l_ref[...] = alpha * l_ref[...] + jnp.broadcast_to(jnp.sum(p, axis=1, keepdims=True), m_old.shape)
        m_ref[...] = mnew
        pv = jnp.zeros(acc_ref.shape, F32)
        for r, vb in enumerate(vblocks):
            pv = pv + lax.dot_general(p[:, r * LANES:(r + 1) * LANES].astype(BF16), vb.astype(BF16),
                                      (((1,), (1,)), ((), ())), preferred_element_type=F32)
        acc_ref[...] = alpha * acc_ref[...] + pv

    update([kp[...] for kp in kpages], [vp[...] for vp in vpages], [sc_ref[r] for r in range(P)])

    @pl.when(j == n_steps - 1)
    def _():
        update([kn_ref[...]], [vn_ref[...]], [scn_ref[0]])
        o_ref[...] = acc_ref[...] / l_ref[...]


def _sample_attend(page_table, cache_k_t, cache_v_t, l, sc, thr, q_rows, k_new_t, v_new_t, *, P):
    NB, n_pages = page_table.shape
    R = q_rows.shape[1]
    TQ = R // N_HEADS
    n_steps = n_pages // P

    def page_spec(r):
        return pl.BlockSpec((None, None, KV_DIM, PAGE_SIZE), lambda b, j, pt: (l, pt[b, j * P + r], 0, 0))

    kern = functools.partial(_sample_attend_kernel, P=P, n_steps=n_steps)
    grid_spec = pltpu.PrefetchScalarGridSpec(
        num_scalar_prefetch=1,
        grid=(NB, n_steps),
        in_specs=[page_spec(r) for r in range(P)] + [page_spec(r) for r in range(P)] + [
            pl.BlockSpec((None, P, TQ, LANES), lambda b, j, pt: (b, j, 0, 0)),
            pl.BlockSpec((None, 1, TQ, LANES), lambda b, j, pt: (b, n_pages, 0, 0)),
            pl.BlockSpec((None, TQ, LANES), lambda b, j, pt: (b, 0, 0)),
            pl.BlockSpec((None, R, KV_DIM), lambda b, j, pt: (b, 0, 0)),
            pl.BlockSpec((None, KV_DIM, PAGE_SIZE), lambda b, j, pt: (b, 0, 0)),
            pl.BlockSpec((None, KV_DIM, PAGE_SIZE), lambda b, j, pt: (b, 0, 0))],
        out_specs=pl.BlockSpec((None, R, KV_DIM), lambda b, j, pt: (b, 0, 0)),
        scratch_shapes=[pltpu.VMEM((R, LANES), F32), pltpu.VMEM((R, LANES), F32), pltpu.VMEM((R, KV_DIM), F32)],
    )
    return pl.pallas_call(
        kern,
        grid_spec=grid_spec,
        out_shape=jax.ShapeDtypeStruct((NB, R, KV_DIM), F32),
        compiler_params=_cparams(("arbitrary", "arbitrary")),
        name="sample_attend",
    )(page_table, *([cache_k_t] * P), *([cache_v_t] * P), sc, sc, thr, q_rows, k_new_t, v_new_t)


def _mix_out_kernel(x_ref, g_ref, wg_ref, ca_ref, ao_ref, py_ref, wco_ref, wao_ref, wpo_ref, wo_ref, o_ref):
    x = x_ref[...]
    ms = jnp.mean(x * x, axis=-1, keepdims=True)
    h = (x * lax.rsqrt(ms + RMS_EPS) * g_ref[...]).astype(BF16)
    gates = jax.nn.sigmoid(jnp.dot(h, wg_ref[...], preferred_element_type=F32))
    y_a = jnp.dot(ca_ref[...], wco_ref[...], preferred_element_type=F32)
    y_b = jnp.dot(ao_ref[...], wao_ref[...], preferred_element_type=F32)
    y_c = jnp.dot(py_ref[...], wpo_ref[...], preferred_element_type=F32)
    merged = (gates[:, 0:D_MODEL] * y_a + gates[:, D_MODEL:2 * D_MODEL] * y_b
              + gates[:, 2 * D_MODEL:3 * D_MODEL] * y_c)
    o_ref[...] = x + jnp.dot(merged.astype(BF16), wo_ref[...], preferred_element_type=F32)


def _mix_out(x2, ca, ao, py, sw, l, *, TM):
    N = x2.shape[0]

    def tspec(c):
        return pl.BlockSpec((TM, c), lambda i: (i, 0))

    return pl.pallas_call(
        _mix_out_kernel,
        grid=(N // TM,),
        in_specs=[tspec(D_MODEL), _layer_spec((1, D_MODEL), l), _layer_spec((D_MODEL, N_BRANCH * D_MODEL), l),
                  tspec(D_CONV), tspec(D_ATTN), tspec(D_POOL),
                  _layer_spec((D_CONV, D_MODEL), l), _layer_spec((D_ATTN, D_MODEL), l),
                  _layer_spec((D_POOL, D_MODEL), l), _layer_spec((D_MODEL, D_MODEL), l)],
        out_specs=tspec(D_MODEL),
        out_shape=jax.ShapeDtypeStruct((N, D_MODEL), F32),
        compiler_params=_cparams(("arbitrary",)),
        name="mix_out",
    )(x2, sw["g_mix"], sw["w_g"], ca, ao, py, sw["w_conv_out"], sw["w_attn_out"], sw["w_pool_out"], sw["w_out"])


def _ffn_kernel(x_ref, g_ref, w1_ref, w2_ref, gf_ref, o_ref, *, final):
    x = x_ref[...]
    ms = jnp.mean(x * x, axis=-1, keepdims=True)
    h = (x * lax.rsqrt(ms + RMS_EPS) * g_ref[...]).astype(BF16)
    r = jnp.maximum(jnp.dot(h, w1_ref[...], preferred_element_type=F32), 0.0)
    y = x + jnp.dot((r * r).astype(BF16), w2_ref[...], preferred_element_type=F32)
    if final:
        ms2 = jnp.mean(y * y, axis=-1, keepdims=True)
        y = y * lax.rsqrt(ms2 + RMS_EPS) * gf_ref[...]
    o_ref[...] = y


def _ffn(x2, sw, l, g_final, *, TM, final):
    N = x2.shape[0]
    once = pl.Buffered(1)
    return pl.pallas_call(
        functools.partial(_ffn_kernel, final=final),
        grid=(N // TM,),
        in_specs=[pl.BlockSpec((TM, D_MODEL), lambda i: (i, 0)), _layer_spec((1, D_MODEL), l),
                  _layer_spec((D_MODEL, D_FF), l, pipeline_mode=once),
                  _layer_spec((D_FF, D_MODEL), l, pipeline_mode=once),
                  pl.BlockSpec((1, D_MODEL), lambda i: (0, 0))],
        out_specs=pl.BlockSpec((TM, D_MODEL), lambda i: (i, 0)),
        out_shape=jax.ShapeDtypeStruct((N, D_MODEL), F32),
        compiler_params=_cparams(("arbitrary",)),
        name="ffn",
    )(x2, sw["g_ffn"], sw["w_ff1"], sw["w_ff2"], g_final)


def _stacked_weights(g_mix, w_in, conv_w, conv_b, conv_ln_g, conv_ln_b, w_conv_out, w_attn_out,
                     pool_w, pool_scale, w_pool_out, w_out, g_ffn, w_ff1, w_ff2):
    depth = w_in.shape[0]
    o_ki = C_QI + IDX_HEADS * IDX_DIM
    o_wi = o_ki + IDX_DIM
    o_xc = o_wi + IDX_HEADS
    o_g = o_xc + D_POOL

    def pad_to_tile(a):
        return jnp.pad(a, ((0, 0), (0, 0), (0, LANES - a.shape[2])))

    w_a = jnp.concatenate([w_in[:, :, :o_ki], w_in[:, :, o_xc:o_g], pad_to_tile(w_in[:, :, o_ki:o_wi]),
                           pad_to_tile(w_in[:, :, o_wi:o_xc])], axis=2).astype(BF16)
    pw = jnp.zeros((depth, D_POOL, D_POOL), F32)
    for g in range(len(POOL_WINDOWS)):
        pw = pw.at[:, g * POOL_GC:(g + 1) * POOL_GC, g * POOL_GC:(g + 1) * POOL_GC].set(pool_w[:, g])

    def row(a):
        return a[:, None, :]

    return dict(
        g_mix=row(g_mix), w_a=w_a, w_g=w_in[:, :, o_g:].astype(BF16),
        conv_w=jnp.pad(conv_w, ((0, 0), (0, CONV_PAD - CONV_W), (0, 0))), conv_b=row(conv_b),
        ln_g=row(conv_ln_g), ln_b=row(conv_ln_b),
        pool_w=pw.astype(BF16), pool_scale=row(pool_scale),
        w_conv_out=w_conv_out.astype(BF16), w_attn_out=w_attn_out.astype(BF16),
        w_pool_out=w_pool_out.astype(BF16), w_out=w_out.astype(BF16),
        g_ffn=row(g_ffn), w_ff1=w_ff1.astype(BF16), w_ff2=w_ff2.astype(BF16))


def _tail_state(hist, new, keep):
    return jnp.concatenate([hist, new], axis=1)[:, -keep:]


def kernel(x_prompt, x_sample, cache_k, cache_v, cache_kidx, state_conv, state_pool, page_table, g_mix, w_in,
           conv_w, conv_b, conv_ln_g, conv_ln_b, w_conv_out, w_attn_out, pool_w, pool_scale, w_pool_out, w_out,
           g_ffn, w_ff1, w_ff2, g_final):
    B, T, _ = x_prompt.shape
    NB, TS, _ = x_sample.shape
    depth = w_in.shape[0]
    n_pool = cache_k.shape[1]
    past_len = page_table.shape[1] * PAGE_SIZE
    TM_P = min(512, T)
    gf = g_final[None]
    xp = x_prompt
    xs = x_sample
    zero_conv = jnp.zeros((B, CONV_PAD, D_CONV), F32)
    zero_pool = jnp.zeros((B, POOL_PAD, D_POOL), F32)
    cache_k_t = cache_k.transpose(0, 1, 3, 4, 2).reshape(depth, n_pool, KV_DIM, PAGE_SIZE)
    cache_v_t = cache_v.transpose(0, 1, 3, 4, 2).reshape(depth, n_pool, KV_DIM, PAGE_SIZE)
    cache_kidx_t = cache_kidx.transpose(0, 1, 3, 2)
    sw = _stacked_weights(g_mix, w_in, conv_w, conv_b, conv_ln_g, conv_ln_b, w_conv_out, w_attn_out,
                          pool_w, pool_scale, w_pool_out, w_out, g_ffn, w_ff1, w_ff2)
    tables_p = _rope_tables(0, T)
    tables_s = _rope_tables(past_len, TS)
    hist_conv_s = jnp.pad(state_conv, ((0, 0), (0, 0), (CONV_PAD - CONV_HIST, 0), (0, 0)))
    hist_pool_s = jnp.pad(state_pool, ((0, 0), (0, 0), (POOL_PAD - POOL_HIST, 0), (0, 0)))
    st_p = [[] for _ in range(5)]
    st_s = [[] for _ in range(5)]
    for l in range(depth):
        final = l == depth - 1

        (k, v, ki, u, xc, kibf, q, _, _, ca, py, qit, wq, kt, va) = _proj(
            xp, 0, tables_p, zero_conv, zero_pool, sw, l, G=1, TM=TM_P, adt=BF16, attn=True)
        ao = _attend_prompt(q, qit, wq, kibf, kt, va, CKS=TM_P)
        x2 = _mix_out(xp.reshape(B * T, D_MODEL), ca.reshape(B * T, D_CONV), ao.reshape(B * T, D_ATTN),
                      py.reshape(B * T, D_POOL), sw, l, TM=TM_P)
        xp = _ffn(x2, sw, l, gf, TM=TM_P, final=final).reshape(B, T, D_MODEL)
        st_p[0].append(k.reshape(B, T, N_KV_HEADS, HEAD_DIM))
        st_p[1].append(v.reshape(B, T, N_KV_HEADS, HEAD_DIM))
        st_p[2].append(ki)
        st_p[3].append(u[:, T - CONV_HIST:])
        st_p[4].append(xc[:, T - POOL_HIST:])

        (k, v, ki, u, xc, _, q, qi, wi, ca, py) = _proj(
            xs, past_len, tables_s, hist_conv_s[l], hist_pool_s[l], sw, l, G=NB, TM=TS, adt=F32, attn=False)
        q, qi, ca, py = q.astype(BF16), qi.astype(BF16), ca.astype(BF16), py.astype(BF16)
        qi_rows = qi.reshape(NB, TS, IDX_HEADS, IDX_DIM).transpose(0, 2, 1, 3).reshape(NB, IDX_HEADS * TS, IDX_DIM)
        w_rows = jnp.broadcast_to(wi[:, :, :IDX_HEADS].transpose(0, 2, 1).reshape(NB, IDX_HEADS * TS, 1),
                                  (NB, IDX_HEADS * TS, LANES))
        qh = q.reshape(NB, TS, N_HEADS, HEAD_DIM).transpose(0, 2, 1, 3)
        head_kv = (jnp.arange(N_HEADS) // GROUP)[None, :, None, None]
        q_rows = jnp.concatenate([jnp.where(head_kv == n, qh, jnp.zeros_like(qh)) for n in range(N_KV_HEADS)],
                                 axis=-1).reshape(NB, N_HEADS * TS, KV_DIM)
        pad_rows = ((0, 0), (0, PAGE_SIZE - TS), (0, 0))
        pages_per_step = min(32, page_table.shape[1])

        def new_page_t(a):
            return jnp.pad(a, pad_rows).transpose(0, 2, 1)

        sc, thr = _sample_scores(page_table, cache_kidx_t, l, qi_rows, w_rows, new_page_t(ki), P=pages_per_step)
        o_rows = _sample_attend(page_table, cache_k_t, cache_v_t, l, sc, thr, q_rows, new_page_t(k), new_page_t(v),
                                P=pages_per_step)
        o5 = o_rows.reshape(NB, N_KV_HEADS, GROUP, TS, N_KV_HEADS, HEAD_DIM)
        ao = jnp.stack([o5[:, n, :, :, n, :] for n in range(N_KV_HEADS)], axis=1)
        ao = ao.reshape(NB, N_HEADS, TS, HEAD_DIM).transpose(0, 2, 1, 3).reshape(NB * TS, D_ATTN).astype(BF16)
        x2 = _mix_out(xs.reshape(NB * TS, D_MODEL), ca.reshape(NB * TS, D_CONV), ao,
                      py.reshape(NB * TS, D_POOL), sw, l, TM=NB * TS)
        xs = _ffn(x2, sw, l, gf, TM=NB * TS, final=final).reshape(NB, TS, D_MODEL)
        st_s[0].append(k.reshape(NB, TS, N_KV_HEADS, HEAD_DIM))
        st_s[1].append(v.reshape(NB, TS, N_KV_HEADS, HEAD_DIM))
        st_s[2].append(ki)
        st_s[3].append(_tail_state(state_conv[l], u, CONV_HIST))
        st_s[4].append(_tail_state(state_pool[l], xc, POOL_HIST))

    sp = [jnp.stack(a, axis=0) for a in st_p]
    ss = [jnp.stack(a, axis=0) for a in st_s]
    return (xp, xs, sp[0], sp[1], sp[2], sp[3], sp[4], ss[0], ss[1], ss[2], ss[3], ss[4])
```

```python
import functools

import jax
import jax.numpy as jnp
from jax import lax
from jax.experimental import pallas as pl
from jax.experimental.pallas import tpu as pltpu

F32 = jnp.float32
BF16 = jnp.bfloat16

D_MODEL = 1024
D_CONV = 256
CONV_W = 31
CONV_HIST = CONV_W - 1
HEAD_DIM = 64
D_ATTN = 512
N_HEADS = 8
N_KV_HEADS = 2
KV_DIM = N_KV_HEADS * HEAD_DIM
GROUP = N_HEADS // N_KV_HEADS
IDX_HEADS = 8
IDX_DIM = 64
TOPK_MAX = 256
D_POOL = 256
POOL_WINDOWS = (2, 4, 8, 16)
POOL_GC = D_POOL // len(POOL_WINDOWS)
POOL_HIST = max(POOL_WINDOWS) - 1
N_BRANCH = 3
D_FF = 4 * D_MODEL
ROPE_THETA = 10000.0
RMS_EPS = 1e-6
LN_EPS = 1e-5
PAGE_SIZE = 128

LANES = 128
CONV_PAD = 32
POOL_PAD = 16
Q_SCALE = HEAD_DIM ** -0.5 * 1.4426950408889634
BIG = 3.0e38
NEG_INF = float("-inf")

C_AIN, C_AGATE, C_Q, C_K, C_V, C_QI, C_XC, C_KI, C_WI, D_A = 0, 256, 512, 1024, 1152, 1280, 1792, 2048, 2176, 2304

VMEM_LIMIT = 56 * 1024 * 1024


def _cparams(sem):
    return pltpu.CompilerParams(dimension_semantics=sem, vmem_limit_bytes=VMEM_LIMIT)


def _layer_spec(shape, l, **kw):
    n = len(shape)
    return pl.BlockSpec((None,) + tuple(shape), lambda *_: (l,) + (0,) * n, **kw)


def _proj_kernel(x_ref, g_ref, w_ref, cos_ref, sin_ref, chist_ref, phist_ref, convw_ref, convb_ref,
                 lng_ref, lnb_ref, poolw_ref, pscale_ref,
                 k_ref, v_ref, ki_ref, u_ref, xc_ref, kibf_ref, q_ref, qi_ref, wi_ref, ca_ref, py_ref,
                 *rest, G, TM, pos0, n_tiles, attn):
    if attn:
        qit_ref, wq_ref, kt_ref, va_ref = rest[:4]
        rest = rest[4:]
    cs_ref, ps_ref, p2_ref, p4_ref, p8_ref, p16_ref = rest
    j = pl.program_id(1)
    R = G * TM
    x = x_ref[...].reshape(R, D_MODEL)
    ms = jnp.mean(x * x, axis=-1, keepdims=True)
    h = (x * lax.rsqrt(ms + RMS_EPS) * g_ref[...]).astype(BF16)
    z = jnp.dot(h, w_ref[...], preferred_element_type=F32)

    cos = cos_ref[...]
    sin = sin_ref[...]
    if G > 1:
        cos = jnp.concatenate([cos] * G, axis=0)
        sin = jnp.concatenate([sin] * G, axis=0)

    def rope(t):
        n = t.shape[1] // LANES
        c = jnp.concatenate([cos] * n, axis=1) if n > 1 else cos
        s = jnp.concatenate([sin] * n, axis=1) if n > 1 else sin
        lane = lax.broadcasted_iota(jnp.int32, t.shape, 1)
        first = (lane & (HEAD_DIM - 1)) < HEAD_DIM // 2
        sw = jnp.where(first, pltpu.roll(t, t.shape[1] - HEAD_DIM // 2, 1), pltpu.roll(t, HEAD_DIM // 2, 1))
        return t * c + sw * s

    q = rope(z[:, C_Q:C_Q + D_ATTN]) * Q_SCALE
    k = rope(z[:, C_K:C_K + KV_DIM])
    v = z[:, C_V:C_V + KV_DIM]
    qi = rope(z[:, C_QI:C_QI + IDX_HEADS * IDX_DIM])
    ki = rope(z[:, C_KI:C_KI + LANES])[:, :IDX_DIM]
    wi = z[:, C_WI:C_WI + LANES] * (IDX_HEADS ** -0.5 * IDX_DIM ** -0.5)

    k_ref[...] = k.reshape(G, TM, KV_DIM)
    v_ref[...] = v.reshape(G, TM, KV_DIM)
    ki_ref[...] = ki.reshape(G, TM, IDX_DIM)
    kibf_ref[...] = ki.reshape(G, TM, IDX_DIM).astype(kibf_ref.dtype)
    q_ref[...] = q.reshape(G, TM, D_ATTN).astype(q_ref.dtype)
    qi_ref[...] = qi.reshape(G, TM, IDX_HEADS * IDX_DIM).astype(qi_ref.dtype)
    wi_ref[...] = wi.reshape(G, TM, LANES)
    if attn:
        for b in range(TM // LANES):
            blk = slice(b * LANES, (b + 1) * LANES)
            qi_t = qi[blk, :].T
            qit_ref[b] = jnp.concatenate([qi_t[hh * IDX_DIM:(hh + 1) * IDX_DIM, :] for hh in range(IDX_HEADS)],
                                         axis=1).astype(BF16)
            wi_t = wi[blk, :].T
            wq_ref[b] = jnp.concatenate([wi_t[hh:hh + 1, :] for hh in range(IDX_HEADS)], axis=1)
        k_t = k.T
        lane = lax.broadcasted_iota(jnp.int32, (TM, KV_DIM), 1)
        tail = jnp.where(lane == HEAD_DIM, 1.0, 0.0)
        for n in range(N_KV_HEADS):
            kt_ref[0, n] = k_t[n * HEAD_DIM:(n + 1) * HEAD_DIM, :].astype(BF16)
            vn = v if n == 0 else pltpu.roll(v, KV_DIM - n * HEAD_DIM, 1)
            va_ref[0, n] = jnp.where(lane < HEAD_DIM, vn, tail).astype(BF16)

    a_in = z[:, C_AIN:C_AIN + D_CONV]
    a_gate = z[:, C_AGATE:C_AGATE + D_CONV]
    u = (a_in * jax.nn.sigmoid(a_gate)).reshape(G, TM, D_CONV)
    u_ref[...] = u

    @pl.when(j == 0)
    def _():
        cs_ref[:, 0:CONV_PAD, :] = chist_ref[...]
        ps_ref[:, 0:POOL_PAD, :] = phist_ref[...]

    cs_ref[:, CONV_PAD:CONV_PAD + TM, :] = u
    off = CONV_PAD - CONV_HIST
    rc = min(TM, 64)
    gc = min(G, 8)
    for g0 in range(0, G, gc):
        for r0 in range(0, TM, rc):
            acc = jnp.zeros((gc, rc, D_CONV), F32) + convb_ref[...][None]
            for t in range(CONV_W):
                acc = acc + convw_ref[t:t + 1, :][None] * cs_ref[g0:g0 + gc, off + t + r0:off + t + r0 + rc, :]
            mu = jnp.mean(acc, axis=-1, keepdims=True)
            var = jnp.mean(jnp.square(acc - mu), axis=-1, keepdims=True)
            y = (acc - mu) * lax.rsqrt(var + LN_EPS) * lng_ref[...][None] + lnb_ref[...][None]
            ca_ref[g0:g0 + gc, r0:r0 + rc, :] = (y * jax.nn.sigmoid(y)).astype(ca_ref.dtype)
    if n_tiles > 1:
        cs_ref[:, 0:CONV_PAD, :] = cs_ref[:, TM:TM + CONV_PAD, :]

    xc = z[:, C_XC:C_XC + D_POOL].reshape(G, TM, D_POOL)
    xc_ref[...] = xc
    ps_ref[:, POOL_PAD:POOL_PAD + TM, :] = xc
    RR = POOL_PAD + TM
    p2_ref[:, 1:RR, :] = ps_ref[:, 1:RR, :] + ps_ref[:, 0:RR - 1, :]
    p4_ref[:, 3:RR, :] = p2_ref[:, 3:RR, :] + p2_ref[:, 1:RR - 2, :]
    p8_ref[:, 7:RR, :] = p4_ref[:, 7:RR, :] + p4_ref[:, 3:RR - 4, :]
    p16_ref[:, 15:RR, :] = p8_ref[:, 15:RR, :] + p8_ref[:, 7:RR - 8, :]
    lane = lax.broadcasted_iota(jnp.int32, (G, TM, D_POOL), 2)
    row = lax.broadcasted_iota(jnp.int32, (G, TM, D_POOL), 1)
    g0m, g1m, g2m = lane < POOL_GC, lane < 2 * POOL_GC, lane < 3 * POOL_GC
    s_tok = slice(POOL_PAD, POOL_PAD + TM)
    wsum = jnp.where(g0m, p2_ref[:, s_tok, :],
                     jnp.where(g1m, p4_ref[:, s_tok, :], jnp.where(g2m, p8_ref[:, s_tok, :], p16_ref[:, s_tok, :])))
    win = jnp.where(g0m, POOL_WINDOWS[0], jnp.where(g1m, POOL_WINDOWS[1],
                                                     jnp.where(g2m, POOL_WINDOWS[2], POOL_WINDOWS[3])))
    cnt = jnp.minimum(pos0 + j * TM + row + 1, win).astype(F32)
    d = (wsum / cnt - xc).reshape(R, D_POOL).astype(BF16)
    py = jnp.dot(d, poolw_ref[...], preferred_element_type=F32) * pscale_ref[...]
    py_ref[...] = py.reshape(G, TM, D_POOL).astype(py_ref.dtype)
    if n_tiles > 1:
        ps_ref[:, 0:POOL_PAD, :] = ps_ref[:, TM:TM + POOL_PAD, :]


def _rope_tables(pos0, T):
    half = HEAD_DIM // 2
    pos = (pos0 + jnp.arange(T)).astype(F32)
    inv = ROPE_THETA ** (-jnp.arange(half, dtype=F32) / half)
    ang = pos[:, None] * inv[None, :]
    cos = jnp.tile(jnp.cos(ang), (1, 2 * LANES // HEAD_DIM))
    sin = jnp.tile(jnp.concatenate([-jnp.sin(ang), jnp.sin(ang)], axis=1), (1, LANES // HEAD_DIM))
    return cos, sin


def _proj(x3, pos0, tables, chist, phist, sw, l, *, G, TM, adt, attn):
    NS, T, _ = x3.shape
    n_tiles = T // TM
    cos, sin = tables

    def tok(c, dt=F32):
        return jax.ShapeDtypeStruct((NS, T, c), dt)

    def tspec(c):
        return pl.BlockSpec((G, TM, c), lambda s, j: (s, j, 0))

    out_cols = [(KV_DIM, F32), (KV_DIM, F32), (IDX_DIM, F32), (D_CONV, F32), (D_POOL, F32),
                (IDX_DIM, adt), (D_ATTN, adt), (IDX_HEADS * IDX_DIM, adt), (LANES, F32), (D_CONV, adt), (D_POOL, adt)]
    out_specs = [tspec(c) for c, _ in out_cols]
    out_shape = [tok(c, dt) for c, dt in out_cols]
    if attn:
        assert G == 1 and TM % LANES == 0
        nb = TM // LANES
        out_specs += [pl.BlockSpec((None, nb, IDX_DIM, IDX_HEADS * LANES), lambda s, j: (s, j, 0, 0)),
                      pl.BlockSpec((None, nb, 1, IDX_HEADS * LANES), lambda s, j: (s, j, 0, 0)),
                      pl.BlockSpec((None, 1, N_KV_HEADS, HEAD_DIM, TM), lambda s, j: (s, j, 0, 0, 0)),
                      pl.BlockSpec((None, 1, N_KV_HEADS, TM, LANES), lambda s, j: (s, j, 0, 0, 0))]
        out_shape += [jax.ShapeDtypeStruct((NS, T // LANES, IDX_DIM, IDX_HEADS * LANES), BF16),
                      jax.ShapeDtypeStruct((NS, T // LANES, 1, IDX_HEADS * LANES), F32),
                      jax.ShapeDtypeStruct((NS, n_tiles, N_KV_HEADS, HEAD_DIM, TM), BF16),
                      jax.ShapeDtypeStruct((NS, n_tiles, N_KV_HEADS, TM, LANES), BF16)]
    kern = functools.partial(_proj_kernel, G=G, TM=TM, pos0=pos0, n_tiles=n_tiles, attn=attn)
    return pl.pallas_call(
        kern,
        grid=(NS // G, n_tiles),
        in_specs=[tspec(D_MODEL), _layer_spec((1, D_MODEL), l), _layer_spec((D_MODEL, D_A), l),
                  pl.BlockSpec((TM, LANES), lambda s, j: (j, 0)), pl.BlockSpec((TM, LANES), lambda s, j: (j, 0)),
                  pl.BlockSpec((G, CONV_PAD, D_CONV), lambda s, j: (s, 0, 0)),
                  pl.BlockSpec((G, POOL_PAD, D_POOL), lambda s, j: (s, 0, 0)),
                  _layer_spec((CONV_PAD, D_CONV), l), _layer_spec((1, D_CONV), l), _layer_spec((1, D_CONV), l),
                  _layer_spec((1, D_CONV), l), _layer_spec((D_POOL, D_POOL), l), _layer_spec((1, D_POOL), l)],
        out_specs=out_specs,
        out_shape=out_shape,
        scratch_shapes=[pltpu.VMEM((G, CONV_PAD + TM, D_CONV), F32)] +
                       [pltpu.VMEM((G, POOL_PAD + TM, D_POOL), F32)] * 5,
        compiler_params=_cparams(("arbitrary", "arbitrary")),
        name="proj",
    )(x3, sw["g_mix"], sw["w_a"], cos, sin, chist, phist, sw["conv_w"], sw["conv_b"], sw["ln_g"], sw["ln_b"],
      sw["pool_w"], sw["pool_scale"])


def _kth_bisect(count, smin, smax, nvalid, kk, steps, warm=0, zero_counts=None):
    zero = jnp.zeros_like(smin)
    c_ge0, c_gt0 = zero_counts if zero_counts is not None else (count(zero, False), count(zero, True))
    pos = c_gt0 >= kk
    neg = c_ge0 < kk
    lo0 = jnp.where(neg, smin, zero)
    clo0 = jnp.where(neg, nvalid, c_ge0)
    hi0 = jnp.where(pos, BIG, zero)
    chi0 = jnp.where(pos, zero, jnp.where(neg, c_ge0, c_gt0))
    cand0 = jnp.where(pos, smax, smin * 0.5)
    small = nvalid <= kk
    done0 = (small | jnp.logical_not(pos | neg)).astype(F32)
    init = (jnp.where(small, smin, lo0), hi0, jnp.where(small, nvalid, clo0), chi0, cand0, done0)

    def count_ge(cand):
        return count(cand, False)

    def cond(st):
        return jnp.min(st[5]) < 0.5

    def body(st):
        for _ in range(steps):
            st = step(st)
        return st

    def step(st):
        lo, hi, clo, chi, cand, done = st
        c = count_ge(cand)
        act = done < 0.5
        ge = c >= kk
        up = act & ge
        dn = act & jnp.logical_not(ge)
        lo = jnp.where(up, cand, lo)
        clo = jnp.where(up, c, clo)
        hi = jnp.where(dn, cand, hi)
        chi = jnp.where(dn, c, chi)
        mid = lo * 0.5 + hi * 0.5
        inside = (mid > lo) & (mid < hi)
        fin = (clo == kk) | jnp.logical_not(inside)
        done = jnp.where(fin, 1.0, done)
        return lo, hi, clo, chi, mid, done

    if warm:
        init = lax.fori_loop(0, warm, lambda _, st: step(st), init)
    lo, _, clo, chi, _, _ = lax.while_loop(cond, body, init)
    return lo, clo, chi


def _attend_prompt_kernel(qit_ref, wq_ref, q_ref, ki_ref, kt_ref, va_ref, o_ref, sc_ref, acc_ref, m_ref, *,
                          CKS, CK, kk):
    i = pl.program_id(1)
    QB = LANES
    nchs = (i * QB + QB + CKS - 1) // CKS
    nch = nchs * (CKS // CK)
    qit = qit_ref[...]
    wq = wq_ref[...]
    qcol = i * QB + lax.broadcasted_iota(jnp.int32, (CKS, QB), 1)
    krow0 = lax.broadcasted_iota(jnp.int32, (CKS, QB), 0)

    def score_chunk(c, carry, causal):
        smax, smin, n_ge0, n_gt0 = carry
        s = jnp.dot(ki_ref[c], qit, preferred_element_type=F32)
        s = jnp.maximum(s, 0.0) * wq
        tot = s[:, 0:QB]
        for hh in range(1, IDX_HEADS):
            tot = tot + s[:, hh * QB:(hh + 1) * QB]
        if causal:
            valid = (krow0 + c * CKS) <= qcol
            masked = jnp.where(valid, tot, NEG_INF)
            lowest = jnp.where(valid, tot, BIG)
        else:
            masked = lowest = tot
        sc_ref[pl.ds(pl.multiple_of(c * CKS, CKS), CKS), :] = masked
        smax = jnp.maximum(smax, jnp.max(masked.reshape(CKS // 8, 8, QB), axis=0))
        smin = jnp.minimum(smin, jnp.min(lowest.reshape(CKS // 8, 8, QB), axis=0))
        n_ge0 = n_ge0 + jnp.sum(jnp.where(masked >= 0.0, 1.0, 0.0).reshape(CKS // 8, 8, QB), axis=0)
        n_gt0 = n_gt0 + jnp.sum(jnp.where(masked > 0.0, 1.0, 0.0).reshape(CKS // 8, 8, QB), axis=0)
        return smax, smin, n_ge0, n_gt0

    carry = lax.fori_loop(0, nchs - 1, functools.partial(score_chunk, causal=False),
                          (jnp.full((8, QB), NEG_INF, F32), jnp.full((8, QB), BIG, F32),
                           jnp.zeros((8, QB), F32), jnp.zeros((8, QB), F32)))
    smax, smin, n_ge0, n_gt0 = score_chunk(nchs - 1, carry, causal=True)
    smax = jnp.max(smax, axis=0, keepdims=True)
    smin = jnp.min(smin, axis=0, keepdims=True)
    zero_counts = (jnp.sum(n_ge0, axis=0, keepdims=True), jnp.sum(n_gt0, axis=0, keepdims=True))

    n_acc = 4
    rows = CKS // n_acc

    def count(cand, strict):
        def body(c, accs):
            base = pl.multiple_of(c * CKS, CKS)
            out = []
            for a in range(n_acc):
                blk = sc_ref[pl.ds(base + a * rows, rows), :]
                hit = (blk > cand) if strict else (blk >= cand)
                ones = jnp.where(hit, 1.0, 0.0).reshape(rows // 8, 8, QB)
                out.append(accs[a] + jnp.sum(ones, axis=0))
            return tuple(out)
        def quad(j, accs):
            for u in range(4):
                accs = body(4 * j + u, accs)
            return accs
        accs = lax.fori_loop(0, nchs // 4, quad, tuple(jnp.zeros((8, QB), F32) for _ in range(n_acc)))
        done4 = (nchs // 4) * 4
        accs = lax.cond(nchs % 4 >= 2, lambda t: body(done4 + 1, body(done4, t)), lambda t: t, accs)
        accs = lax.cond(nchs % 2 == 1, lambda t: body(nchs - 1, t), lambda t: t, accs)
        return jnp.sum((accs[0] + accs[1]) + (accs[2] + accs[3]), axis=0, keepdims=True)

    nvalid = (i * QB + lax.broadcasted_iota(jnp.int32, (1, QB), 1) + 1).astype(F32)
    lo, clo, chi = _kth_bisect(count, smin, smax, nvalid, float(kk), steps=4, warm=12, zero_counts=zero_counts)

    fix = (clo > kk) & (nvalid > kk)

    @pl.when(jnp.max(fix.astype(F32)) > 0.5)
    def _():
        keep = kk - chi
        TF = min(256, CKS)
        tri = (lax.broadcasted_iota(jnp.int32, (TF, TF), 1)
               <= lax.broadcasted_iota(jnp.int32, (TF, TF), 0)).astype(BF16)

        def body(c, run):
            rows_c = pl.ds(pl.multiple_of(c * TF, TF), TF)
            blk = sc_ref[rows_c, :]
            tie = jnp.where(fix & (blk == lo), 1.0, 0.0)
            rank = run + jnp.dot(tri, tie.astype(BF16), preferred_element_type=F32)
            sc_ref[rows_c, :] = jnp.where((tie > 0.5) & (rank > keep), NEG_INF, blk)
            return run + jnp.sum(tie, axis=0, keepdims=True)
        lax.fori_loop(0, nchs * (CKS // TF), body, jnp.zeros((1, QB), F32))

    GQ = GROUP * QB
    q32 = q_ref[...].astype(F32)
    qn = [jnp.concatenate([q32[:, hh * HEAD_DIM:(hh + 1) * HEAD_DIM] for hh in range(n * GROUP, (n + 1) * GROUP)],
                          axis=0).astype(BF16) for n in range(N_KV_HEADS)]
    acc_ref[...] = jnp.zeros_like(acc_ref)
    m_ref[...] = jnp.full_like(m_ref, -BIG)
    reps = CK // LANES

    def att_chunk(c):
        scb = sc_ref[pl.ds(pl.multiple_of(c * CK, CK), CK), :]
        bias_t = jnp.where(scb >= lo, 0.0, NEG_INF).T
        for n in range(N_KV_HEADS):
            s = jnp.dot(qn[n], kt_ref[c, n], preferred_element_type=F32)
            s = (s.reshape(GROUP, QB, CK) + bias_t[None]).reshape(GQ, CK)
            m_old = m_ref[n]
            mnew = jnp.maximum(m_old, jnp.max(s, axis=1, keepdims=True))
            p = jnp.exp2(s - jnp.concatenate([mnew] * reps, axis=1))
            alpha = jnp.exp2(m_old - mnew)
            pv = jnp.dot(p.astype(BF16), va_ref[c, n], preferred_element_type=F32)
            acc_ref[n] = alpha * acc_ref[n] + pv
            m_ref[n] = mnew

    def att_quad(j, _):
        for u in range(4):
            att_chunk(4 * j + u)
        return 0

    lax.fori_loop(0, nch // 4, att_quad, 0)
    done4 = (nch // 4) * 4

    @pl.when(nch % 4 >= 2)
    def _():
        att_chunk(done4)
        att_chunk(done4 + 1)

    @pl.when(nch % 2 == 1)
    def _():
        att_chunk(nch - 1)

    outs = []
    for n in range(N_KV_HEADS):
        a = acc_ref[n]
        o = a[:, :HEAD_DIM] / a[:, HEAD_DIM:HEAD_DIM + 1]
        outs += [o[g * QB:(g + 1) * QB, :] for g in range(GROUP)]
    o_ref[...] = jnp.concatenate(outs, axis=1).astype(BF16)


def _attend_prompt(q, qit, wq, kibf, kt, va, *, CKS):
    B, T, _ = q.shape
    QB = LANES
    nq = T // QB
    nc, CK = kt.shape[1], kt.shape[4]
    ncs = T // CKS
    kk = min(TOPK_MAX, T // 4)
    kib = kibf.reshape(B, ncs, CKS, IDX_DIM)

    kern = functools.partial(_attend_prompt_kernel, CKS=CKS, CK=CK, kk=kk)
    return pl.pallas_call(
        kern,
        grid=(B, nq),
        in_specs=[pl.BlockSpec((None, None, IDX_DIM, 8 * QB), lambda b, i: (b, i, 0, 0)),
                  pl.BlockSpec((None, None, 1, 8 * QB), lambda b, i: (b, i, 0, 0)),
                  pl.BlockSpec((None, QB, D_ATTN), lambda b, i: (b, i, 0)),
                  pl.BlockSpec((None, ncs, CKS, IDX_DIM), lambda b, i: (b, 0, 0, 0)),
                  pl.BlockSpec((None, nc, N_KV_HEADS, HEAD_DIM, CK), lambda b, i: (b, 0, 0, 0, 0)),
                  pl.BlockSpec((None, nc, N_KV_HEADS, CK, LANES), lambda b, i: (b, 0, 0, 0, 0))],
        out_specs=pl.BlockSpec((None, QB, D_ATTN), lambda b, i: (b, i, 0)),
        out_shape=jax.ShapeDtypeStruct((B, T, D_ATTN), BF16),
        scratch_shapes=[pltpu.VMEM((T, QB), F32), pltpu.VMEM((N_KV_HEADS, GROUP * QB, LANES), F32),
                        pltpu.VMEM((N_KV_HEADS, GROUP * QB, LANES), F32)],
        compiler_params=_cparams(("arbitrary", "arbitrary")),
        name="attend_prompt",
    )(qit, wq, q, kib, kt, va)


def _sample_score_kernel(pt_ref, *refs, P, n_steps, past_len, kk):
    pages = refs[:P]
    qi_ref, wb_ref, kin_ref, sc_ref, thr_ref = refs[P:P + 5]
    j = pl.program_id(1)
    TQ = sc_ref.shape[1]
    NBLK = sc_ref.shape[0]
    qi = qi_ref[...]
    wb = wb_ref[...]

    def block_scores(kpage_t):
        s = jnp.dot(qi, kpage_t.astype(BF16), preferred_element_type=F32)
        s = jnp.maximum(s, 0.0) * wb
        return jnp.sum(s.reshape(IDX_HEADS, TQ, LANES), axis=0)

    for r in range(P):
        sc_ref[j * P + r] = block_scores(pages[r][...])

    @pl.when(j == n_steps - 1)
    def _():
        tot = block_scores(kin_ref[...])
        qrow = lax.broadcasted_iota(jnp.int32, (TQ, LANES), 0)
        kcol = lax.broadcasted_iota(jnp.int32, (TQ, LANES), 1)
        sc_ref[NBLK - 1] = jnp.where(kcol <= qrow, tot, NEG_INF)

        def lane_bcast(a):
            return jnp.broadcast_to(a, (TQ, LANES))

        sc = sc_ref[...]
        smax = lane_bcast(jnp.max(jnp.max(sc, axis=0), axis=1, keepdims=True))
        smin = lane_bcast(jnp.min(jnp.min(jnp.where(sc > NEG_INF, sc, BIG), axis=0), axis=1, keepdims=True))
        nvalid = (past_len + qrow + 1).astype(F32)

        def count(cand, strict):
            blk = sc_ref[...]
            hit = (blk > cand[None]) if strict else (blk >= cand[None])
            c = jnp.sum(jnp.where(hit, 1.0, 0.0), axis=0)
            return lane_bcast(jnp.sum(c, axis=1, keepdims=True))

        lo, clo, chi = _kth_bisect(count, smin, smax, nvalid, float(kk), steps=4)
        thr_ref[...] = lo

        fix = (clo > kk) & (nvalid > kk)

        @pl.when(jnp.max(fix.astype(F32)) > 0.5)
        def _():
            keep = kk - chi
            tri = (lax.broadcasted_iota(jnp.int32, (LANES, LANES), 0)
                   <= lax.broadcasted_iota(jnp.int32, (LANES, LANES), 1)).astype(BF16)

            def body(c, run):
                blk = sc_ref[c]
                tie = fix & (blk == lo)
                rank = run + jnp.dot(tie.astype(BF16), tri, preferred_element_type=F32)
                sc_ref[c] = jnp.where(tie & (rank > keep), NEG_INF, blk)
                return run + lane_bcast(jnp.sum(tie.astype(F32), axis=1, keepdims=True))
            lax.fori_loop(0, NBLK, body, jnp.zeros((TQ, LANES), F32))


def _sample_scores(page_table, cache_kidx_t, l, qi_rows, w_rows, ki_new_t, *, P):
    NB, n_pages = page_table.shape
    TQ = qi_rows.shape[1] // IDX_HEADS
    n_steps = n_pages // P
    past_len = n_pages * PAGE_SIZE
    kk = min(TOPK_MAX, (past_len + TQ) // 4)
    NBLK = n_pages + 1

    def page_spec(r):
        return pl.BlockSpec((None, None, IDX_DIM, PAGE_SIZE), lambda b, j, pt: (l, pt[b, j * P + r], 0, 0))

    kern = functools.partial(_sample_score_kernel, P=P, n_steps=n_steps, past_len=past_len, kk=kk)
    grid_spec = pltpu.PrefetchScalarGridSpec(
        num_scalar_prefetch=1,
        grid=(NB, n_steps),
        in_specs=[page_spec(r) for r in range(P)] + [
            pl.BlockSpec((None, IDX_HEADS * TQ, IDX_DIM), lambda b, j, pt: (b, 0, 0)),
            pl.BlockSpec((None, IDX_HEADS * TQ, LANES), lambda b, j, pt: (b, 0, 0)),
            pl.BlockSpec((None, IDX_DIM, PAGE_SIZE), lambda b, j, pt: (b, 0, 0))],
        out_specs=[pl.BlockSpec((None, NBLK, TQ, LANES), lambda b, j, pt: (b, 0, 0, 0)),
                   pl.BlockSpec((None, TQ, LANES), lambda b, j, pt: (b, 0, 0))],
    )
    return pl.pallas_call(
        kern,
        grid_spec=grid_spec,
        out_shape=[jax.ShapeDtypeStruct((NB, NBLK, TQ, LANES), F32), jax.ShapeDtypeStruct((NB, TQ, LANES), F32)],
        compiler_params=_cparams(("arbitrary", "arbitrary")),
        name="sample_scores",
    )(page_table, *([cache_kidx_t] * P), qi_rows, w_rows, ki_new_t)


def _sample_attend_kernel(pt_ref, *refs, P, n_steps):
    kpages = refs[:P]
    vpages = refs[P:2 * P]
    sc_ref, scn_ref, thr_ref, q_ref, kn_ref, vn_ref, o_ref, m_ref, l_ref, acc_ref = refs[2 * P:2 * P + 10]
    j = pl.program_id(1)
    qz = q_ref[...]
    thr = thr_ref[...]

    @pl.when(j == 0)
    def _():
        m_ref[...] = jnp.full_like(m_ref, -BIG)
        l_ref[...] = jnp.zeros_like(l_ref)
        acc_ref[...] = jnp.zeros_like(acc_ref)

    def update(kblocks, vblocks, scblocks):
        s_parts = []
        for kb, scb in zip(kblocks, scblocks):
            s = jnp.dot(qz, kb.astype(BF16), preferred_element_type=F32)
            bias = jnp.where(scb >= thr, 0.0, NEG_INF)
            s_parts.append(s + jnp.concatenate([bias] * N_HEADS, axis=0))
        s = jnp.concatenate(s_parts, axis=1) if len(s_parts) > 1 else s_parts[0]
        m_old = m_ref[...]
        mnew = jnp.maximum(m_old, jnp.broadcast_to(jnp.max(s, axis=1, keepdims=True), m_old.shape))
        alpha = jnp.exp2(m_old - mnew)
        p = jnp.exp2(s - jnp.concatenate([mnew] * len(s_parts), axis=1) if len(s_parts) > 1 else s - mnew)
        l_ref[...] = alpha * l_ref[...] + jnp.broadcast_to(jnp.sum(p, axis=1, keepdims=True), m_old.shape)
        m_ref[...] = mnew
        pv = jnp.zeros(acc_ref.shape, F32)
        for r, vb in enumerate(vblocks):
            pv = pv + lax.dot_general(p[:, r * LANES:(r + 1) * LANES].astype(BF16), vb.astype(BF16),
                                      (((1,), (1,)), ((), ())), preferred_element_type=F32)
        acc_ref[...] = alpha * acc_ref[...] + pv

    update([kp[...] for kp in kpages], [vp[...] for vp in vpages], [sc_ref[r] for r in range(P)])

    @pl.when(j == n_steps - 1)
    def _():
        update([kn_ref[...]], [vn_ref[...]], [scn_ref[0]])
        o_ref[...] = acc_ref[...] / l_ref[...]


def _sample_attend(page_table, cache_k_t, cache_v_t, l, sc, thr, q_rows, k_new_t, v_new_t, *, P):
    NB, n_pages = page_table.shape
    R = q_rows.shape[1]
    TQ = R // N_HEADS
    n_steps = n_pages // P

    def page_spec(r):
        return pl.BlockSpec((None, None, KV_DIM, PAGE_SIZE), lambda b, j, pt: (l, pt[b, j * P + r], 0, 0))

    kern = functools.partial(_sample_attend_kernel, P=P, n_steps=n_steps)
    grid_spec = pltpu.PrefetchScalarGridSpec(
        num_scalar_prefetch=1,
        grid=(NB, n_steps),
        in_specs=[page_spec(r) for r in range(P)] + [page_spec(r) for r in range(P)] + [
            pl.BlockSpec((None, P, TQ, LANES), lambda b, j, pt: (b, j, 0, 0)),
            pl.BlockSpec((None, 1, TQ, LANES), lambda b, j, pt: (b, n_pages, 0, 0)),
            pl.BlockSpec((None, TQ, LANES), lambda b, j, pt: (b, 0, 0)),
            pl.BlockSpec((None, R, KV_DIM), lambda b, j, pt: (b, 0, 0)),
            pl.BlockSpec((None, KV_DIM, PAGE_SIZE), lambda b, j, pt: (b, 0, 0)),
            pl.BlockSpec((None, KV_DIM, PAGE_SIZE), lambda b, j, pt: (b, 0, 0))],
        out_specs=pl.BlockSpec((None, R, KV_DIM), lambda b, j, pt: (b, 0, 0)),
        scratch_shapes=[pltpu.VMEM((R, LANES), F32), pltpu.VMEM((R, LANES), F32), pltpu.VMEM((R, KV_DIM), F32)],
    )
    return pl.pallas_call(
        kern,
        grid_spec=grid_spec,
        out_shape=jax.ShapeDtypeStruct((NB, R, KV_DIM), F32),
        compiler_params=_cparams(("arbitrary", "arbitrary")),
        name="sample_attend",
    )(page_table, *([cache_k_t] * P), *([cache_v_t] * P), sc, sc, thr, q_rows, k_new_t, v_new_t)


def _mix_out_kernel(x_ref, g_ref, wg_ref, ca_ref, ao_ref, py_ref, wco_ref, wao_ref, wpo_ref, wo_ref, o_ref):
    x = x_ref[...]
    ms = jnp.mean(x * x, axis=-1, keepdims=True)
    h = (x * lax.rsqrt(ms + RMS_EPS) * g_ref[...]).astype(BF16)
    gates = jax.nn.sigmoid(jnp.dot(h, wg_ref[...], preferred_element_type=F32))
    y_a = jnp.dot(ca_ref[...], wco_ref[...], preferred_element_type=F32)
    y_b = jnp.dot(ao_ref[...], wao_ref[...], preferred_element_type=F32)
    y_c = jnp.dot(py_ref[...], wpo_ref[...], preferred_element_type=F32)
    merged = (gates[:, 0:D_MODEL] * y_a + gates[:, D_MODEL:2 * D_MODEL] * y_b
              + gates[:, 2 * D_MODEL:3 * D_MODEL] * y_c)
    o_ref[...] = x + jnp.dot(merged.astype(BF16), wo_ref[...], preferred_element_type=F32)


def _mix_out(x2, ca, ao, py, sw, l, *, TM):
    N = x2.shape[0]

    def tspec(c):
        return pl.BlockSpec((TM, c), lambda i: (i, 0))

    return pl.pallas_call(
        _mix_out_kernel,
        grid=(N // TM,),
        in_specs=[tspec(D_MODEL), _layer_spec((1, D_MODEL), l), _layer_spec((D_MODEL, N_BRANCH * D_MODEL), l),
                  tspec(D_CONV), tspec(D_ATTN), tspec(D_POOL),
                  _layer_spec((D_CONV, D_MODEL), l), _layer_spec((D_ATTN, D_MODEL), l),
                  _layer_spec((D_POOL, D_MODEL), l), _layer_spec((D_MODEL, D_MODEL), l)],
        out_specs=tspec(D_MODEL),
        out_shape=jax.ShapeDtypeStruct((N, D_MODEL), F32),
        compiler_params=_cparams(("arbitrary",)),
        name="mix_out",
    )(x2, sw["g_mix"], sw["w_g"], ca, ao, py, sw["w_conv_out"], sw["w_attn_out"], sw["w_pool_out"], sw["w_out"])


def _ffn_kernel(x_ref, g_ref, w1_ref, w2_ref, gf_ref, o_ref, *, final):
    x = x_ref[...]
    ms = jnp.mean(x * x, axis=-1, keepdims=True)
    h = (x * lax.rsqrt(ms + RMS_EPS) * g_ref[...]).astype(BF16)
    r = jnp.maximum(jnp.dot(h, w1_ref[...], preferred_element_type=F32), 0.0)
    y = x + jnp.dot((r * r).astype(BF16), w2_ref[...], preferred_element_type=F32)
    if final:
        ms2 = jnp.mean(y * y, axis=-1, keepdims=True)
        y = y * lax.rsqrt(ms2 + RMS_EPS) * gf_ref[...]
    o_ref[...] = y


def _ffn(x2, sw, l, g_final, *, TM, final):
    N = x2.shape[0]
    once = pl.Buffered(1)
    return pl.pallas_call(
        functools.partial(_ffn_kernel, final=final),
        grid=(N // TM,),
        in_specs=[pl.BlockSpec((TM, D_MODEL), lambda i: (i, 0)), _layer_spec((1, D_MODEL), l),
                  _layer_spec((D_MODEL, D_FF), l, pipeline_mode=once),
                  _layer_spec((D_FF, D_MODEL), l, pipeline_mode=once),
                  pl.BlockSpec((1, D_MODEL), lambda i: (0, 0))],
        out_specs=pl.BlockSpec((TM, D_MODEL), lambda i: (i, 0)),
        out_shape=jax.ShapeDtypeStruct((N, D_MODEL), F32),
        compiler_params=_cparams(("arbitrary",)),
        name="ffn",
    )(x2, sw["g_ffn"], sw["w_ff1"], sw["w_ff2"], g_final)


def _stacked_weights(g_mix, w_in, conv_w, conv_b, conv_ln_g, conv_ln_b, w_conv_out, w_attn_out,
                     pool_w, pool_scale, w_pool_out, w_out, g_ffn, w_ff1, w_ff2):
    depth = w_in.shape[0]
    o_ki = C_QI + IDX_HEADS * IDX_DIM
    o_wi = o_ki + IDX_DIM
    o_xc = o_wi + IDX_HEADS
    o_g = o_xc + D_POOL

    def pad_to_tile(a):
        return jnp.pad(a, ((0, 0), (0, 0), (0, LANES - a.shape[2])))

    w_a = jnp.concatenate([w_in[:, :, :o_ki], w_in[:, :, o_xc:o_g], pad_to_tile(w_in[:, :, o_ki:o_wi]),
                           pad_to_tile(w_in[:, :, o_wi:o_xc])], axis=2).astype(BF16)
    pw = jnp.zeros((depth, D_POOL, D_POOL), F32)
    for g in range(len(POOL_WINDOWS)):
        pw = pw.at[:, g * POOL_GC:(g + 1) * POOL_GC, g * POOL_GC:(g + 1) * POOL_GC].set(pool_w[:, g])

    def row(a):
        return a[:, None, :]

    return dict(
        g_mix=row(g_mix), w_a=w_a, w_g=w_in[:, :, o_g:].astype(BF16),
        conv_w=jnp.pad(conv_w, ((0, 0), (0, CONV_PAD - CONV_W), (0, 0))), conv_b=row(conv_b),
        ln_g=row(conv_ln_g), ln_b=row(conv_ln_b),
        pool_w=pw.astype(BF16), pool_scale=row(pool_scale),
        w_conv_out=w_conv_out.astype(BF16), w_attn_out=w_attn_out.astype(BF16),
        w_pool_out=w_pool_out.astype(BF16), w_out=w_out.astype(BF16),
        g_ffn=row(g_ffn), w_ff1=w_ff1.astype(BF16), w_ff2=w_ff2.astype(BF16))


def _tail_state(hist, new, keep):
    return jnp.concatenate([hist, new], axis=1)[:, -keep:]


def kernel(x_prompt, x_sample, cache_k, cache_v, cache_kidx, state_conv, state_pool, page_table, g_mix, w_in,
           conv_w, conv_b, conv_ln_g, conv_ln_b, w_conv_out, w_attn_out, pool_w, pool_scale, w_pool_out, w_out,
           g_ffn, w_ff1, w_ff2, g_final):
    B, T, _ = x_prompt.shape
    NB, TS, _ = x_sample.shape
    depth = w_in.shape[0]
    n_pool = cache_k.shape[1]
    past_len = page_table.shape[1] * PAGE_SIZE
    TM_P = min(512, T)
    gf = g_final[None]
    xp = x_prompt
    xs = x_sample
    zero_conv = jnp.zeros((B, CONV_PAD, D_CONV), F32)
    zero_pool = jnp.zeros((B, POOL_PAD, D_POOL), F32)
    cache_k_t = cache_k.transpose(0, 1, 3, 4, 2).reshape(depth, n_pool, KV_DIM, PAGE_SIZE)
    cache_v_t = cache_v.transpose(0, 1, 3, 4, 2).reshape(depth, n_pool, KV_DIM, PAGE_SIZE)
    cache_kidx_t = cache_kidx.transpose(0, 1, 3, 2)
    sw = _stacked_weights(g_mix, w_in, conv_w, conv_b, conv_ln_g, conv_ln_b, w_conv_out, w_attn_out,
                          pool_w, pool_scale, w_pool_out, w_out, g_ffn, w_ff1, w_ff2)
    tables_p = _rope_tables(0, T)
    tables_s = _rope_tables(past_len, TS)
    hist_conv_s = jnp.pad(state_conv, ((0, 0), (0, 0), (CONV_PAD - CONV_HIST, 0), (0, 0)))
    hist_pool_s = jnp.pad(state_pool, ((0, 0), (0, 0), (POOL_PAD - POOL_HIST, 0), (0, 0)))
    st_p = [[] for _ in range(5)]
    st_s = [[] for _ in range(5)]
    for l in range(depth):
        final = l == depth - 1

        (k, v, ki, u, xc, kibf, q, _, _, ca, py, qit, wq, kt, va) = _proj(
            xp, 0, tables_p, zero_conv, zero_pool, sw, l, G=1, TM=TM_P, adt=BF16, attn=True)
        ao = _attend_prompt(q, qit, wq, kibf, kt, va, CKS=TM_P)
        x2 = _mix_out(xp.reshape(B * T, D_MODEL), ca.reshape(B * T, D_CONV), ao.reshape(B * T, D_ATTN),
                      py.reshape(B * T, D_POOL), sw, l, TM=TM_P)
        xp = _ffn(x2, sw, l, gf, TM=TM_P, final=final).reshape(B, T, D_MODEL)
        st_p[0].append(k.reshape(B, T, N_KV_HEADS, HEAD_DIM))
        st_p[1].append(v.reshape(B, T, N_KV_HEADS, HEAD_DIM))
        st_p[2].append(ki)
        st_p[3].append(u[:, T - CONV_HIST:])
        st_p[4].append(xc[:, T - POOL_HIST:])

        (k, v, ki, u, xc, _, q, qi, wi, ca, py) = _proj(
            xs, past_len, tables_s, hist_conv_s[l], hist_pool_s[l], sw, l, G=NB, TM=TS, adt=F32, attn=False)
        q, qi, ca, py = q.astype(BF16), qi.astype(BF16), ca.astype(BF16), py.astype(BF16)
        qi_rows = qi.reshape(NB, TS, IDX_HEADS, IDX_DIM).transpose(0, 2, 1, 3).reshape(NB, IDX_HEADS * TS, IDX_DIM)
        w_rows = jnp.broadcast_to(wi[:, :, :IDX_HEADS].transpose(0, 2, 1).reshape(NB, IDX_HEADS * TS, 1),
                                  (NB, IDX_HEADS * TS, LANES))
        qh = q.reshape(NB, TS, N_HEADS, HEAD_DIM).transpose(0, 2, 1, 3)
        head_kv = (jnp.arange(N_HEADS) // GROUP)[None, :, None, None]
        q_rows = jnp.concatenate([jnp.where(head_kv == n, qh, jnp.zeros_like(qh)) for n in range(N_KV_HEADS)],
                                 axis=-1).reshape(NB, N_HEADS * TS, KV_DIM)
        pad_rows = ((0, 0), (0, PAGE_SIZE - TS), (0, 0))
        pages_per_step = min(32, page_table.shape[1])

        def new_page_t(a):
            return jnp.pad(a, pad_rows).transpose(0, 2, 1)

        sc, thr = _sample_scores(page_table, cache_kidx_t, l, qi_rows, w_rows, new_page_t(ki), P=pages_per_step)
        o_rows = _sample_attend(page_table, cache_k_t, cache_v_t, l, sc, thr, q_rows, new_page_t(k), new_page_t(v),
                                P=pages_per_step)
        o5 = o_rows.reshape(NB, N_KV_HEADS, GROUP, TS, N_KV_HEADS, HEAD_DIM)
        ao = jnp.stack([o5[:, n, :, :, n, :] for n in range(N_KV_HEADS)], axis=1)
        ao = ao.reshape(NB, N_HEADS, TS, HEAD_DIM).transpose(0, 2, 1, 3).reshape(NB * TS, D_ATTN).astype(BF16)
        x2 = _mix_out(xs.reshape(NB * TS, D_MODEL), ca.reshape(NB * TS, D_CONV), ao,
                      py.reshape(NB * TS, D_POOL), sw, l, TM=NB * TS)
        xs = _ffn(x2, sw, l, gf, TM=NB * TS, final=final).reshape(NB, TS, D_MODEL)
        st_s[0].append(k.reshape(NB, TS, N_KV_HEADS, HEAD_DIM))
        st_s[1].append(v.reshape(NB, TS, N_KV_HEADS, HEAD_DIM))
        st_s[2].append(ki)
        st_s[3].append(_tail_state(state_conv[l], u, CONV_HIST))
        st_s[4].append(_tail_state(state_pool[l], xc, POOL_HIST))

    sp = [jnp.stack(a, axis=0) for a in st_p]
    ss = [jnp.stack(a, axis=0) for a in st_s]
    return (xp, xs, sp[0], sp[1], sp[2], sp[3], sp[4], ss[0], ss[1], ss[2], ss[3], ss[4])
```

```python
import functools

import jax
import jax.numpy as jnp
from jax import lax
from jax.experimental import pallas as pl
from jax.experimental.pallas import tpu as pltpu

F32 = jnp.float32
BF16 = jnp.bfloat16

D_MODEL = 1024
D_CONV = 256
CONV_W = 31
CONV_HIST = CONV_W - 1
HEAD_DIM = 64
D_ATTN = 512
N_HEADS = 8
N_KV_HEADS = 2
KV_DIM = N_KV_HEADS * HEAD_DIM
GROUP = N_HEADS // N_KV_HEADS
IDX_HEADS = 8
IDX_DIM = 64
TOPK_MAX = 256
D_POOL = 256
POOL_WINDOWS = (2, 4, 8, 16)
POOL_GC = D_POOL // len(POOL_WINDOWS)
POOL_HIST = max(POOL_WINDOWS) - 1
N_BRANCH = 3
D_FF = 4 * D_MODEL
ROPE_THETA = 10000.0
RMS_EPS = 1e-6
LN_EPS = 1e-5
PAGE_SIZE = 128

LANES = 128
CONV_PAD = 32
POOL_PAD = 16
Q_SCALE = HEAD_DIM ** -0.5 * 1.4426950408889634
BIG = 3.0e38
NEG_INF = float("-inf")

C_AIN, C_AGATE, C_Q, C_K, C_V, C_QI, C_XC, C_KI, C_WI, D_A = 0, 256, 512, 1024, 1152, 1280, 1792, 2048, 2176, 2304

VMEM_LIMIT = 56 * 1024 * 1024


def _cparams(sem):
    return pltpu.CompilerParams(dimension_semantics=sem, vmem_limit_bytes=VMEM_LIMIT)


def _layer_spec(shape, l, **kw):
    n = len(shape)
    return pl.BlockSpec((None,) + tuple(shape), lambda *_: (l,) + (0,) * n, **kw)


def _proj_kernel(x_ref, g_ref, w_ref, cos_ref, sin_ref, chist_ref, phist_ref, convw_ref, convb_ref,
                 lng_ref, lnb_ref, poolw_ref, pscale_ref,
                 k_ref, v_ref, ki_ref, u_ref, xc_ref, kibf_ref, q_ref, qi_ref, wi_ref, ca_ref, py_ref,
                 *rest, G, TM, pos0, n_tiles, attn):
    if attn:
        qit_ref, wq_ref, kt_ref, va_ref = rest[:4]
        rest = rest[4:]
    cs_ref, ps_ref, p2_ref, p4_ref, p8_ref, p16_ref = rest
    j = pl.program_id(1)
    R = G * TM
    x = x_ref[...].reshape(R, D_MODEL)
    ms = jnp.mean(x * x, axis=-1, keepdims=True)
    h = (x * lax.rsqrt(ms + RMS_EPS) * g_ref[...]).astype(BF16)
    z = jnp.dot(h, w_ref[...], preferred_element_type=F32)

    cos = cos_ref[...]
    sin = sin_ref[...]
    if G > 1:
        cos = jnp.concatenate([cos] * G, axis=0)
        sin = jnp.concatenate([sin] * G, axis=0)

    def rope(t):
        n = t.shape[1] // LANES
        c = jnp.concatenate([cos] * n, axis=1) if n > 1 else cos
        s = jnp.concatenate([sin] * n, axis=1) if n > 1 else sin
        lane = lax.broadcasted_iota(jnp.int32, t.shape, 1)
        first = (lane & (HEAD_DIM - 1)) < HEAD_DIM // 2
        sw = jnp.where(first, pltpu.roll(t, t.shape[1] - HEAD_DIM // 2, 1), pltpu.roll(t, HEAD_DIM // 2, 1))
        return t * c + sw * s

    q = rope(z[:, C_Q:C_Q + D_ATTN]) * Q_SCALE
    k = rope(z[:, C_K:C_K + KV_DIM])
    v = z[:, C_V:C_V + KV_DIM]
    qi = rope(z[:, C_QI:C_QI + IDX_HEADS * IDX_DIM])
    ki = rope(z[:, C_KI:C_KI + LANES])[:, :IDX_DIM]
    wi = z[:, C_WI:C_WI + LANES] * (IDX_HEADS ** -0.5 * IDX_DIM ** -0.5)

    k_ref[...] = k.reshape(G, TM, KV_DIM)
    v_ref[...] = v.reshape(G, TM, KV_DIM)
    ki_ref[...] = ki.reshape(G, TM, IDX_DIM)
    kibf_ref[...] = ki.reshape(G, TM, IDX_DIM).astype(kibf_ref.dtype)
    q_ref[...] = q.reshape(G, TM, D_ATTN).astype(q_ref.dtype)
    qi_ref[...] = qi.reshape(G, TM, IDX_HEADS * IDX_DIM).astype(qi_ref.dtype)
    wi_ref[...] = wi.reshape(G, TM, LANES)
    if attn:
        for b in range(TM // LANES):
            blk = slice(b * LANES, (b + 1) * LANES)
            qi_t = qi[blk, :].T
            qit_ref[b] = jnp.concatenate([qi_t[hh * IDX_DIM:(hh + 1) * IDX_DIM, :] for hh in range(IDX_HEADS)],
                                         axis=1).astype(BF16)
            wi_t = wi[blk, :].T
            wq_ref[b] = jnp.concatenate([wi_t[hh:hh + 1, :] for hh in range(IDX_HEADS)], axis=1)
        k_t = k.T
        lane = lax.broadcasted_iota(jnp.int32, (TM, KV_DIM), 1)
        tail = jnp.where(lane == HEAD_DIM, 1.0, 0.0)
        for n in range(N_KV_HEADS):
            kt_ref[0, n] = k_t[n * HEAD_DIM:(n + 1) * HEAD_DIM, :].astype(BF16)
            vn = v if n == 0 else pltpu.roll(v, KV_DIM - n * HEAD_DIM, 1)
            va_ref[0, n] = jnp.where(lane < HEAD_DIM, vn, tail).astype(BF16)

    a_in = z[:, C_AIN:C_AIN + D_CONV]
    a_gate = z[:, C_AGATE:C_AGATE + D_CONV]
    u = (a_in * jax.nn.sigmoid(a_gate)).reshape(G, TM, D_CONV)
    u_ref[...] = u

    @pl.when(j == 0)
    def _():
        cs_ref[:, 0:CONV_PAD, :] = chist_ref[...]
        ps_ref[:, 0:POOL_PAD, :] = phist_ref[...]

    cs_ref[:, CONV_PAD:CONV_PAD + TM, :] = u
    off = CONV_PAD - CONV_HIST
    rc = min(TM, 64)
    gc = min(G, 8)
    for g0 in range(0, G, gc):
        for r0 in range(0, TM, rc):
            acc = jnp.zeros((gc, rc, D_CONV), F32) + convb_ref[...][None]
            for t in range(CONV_W):
                acc = acc + convw_ref[t:t + 1, :][None] * cs_ref[g0:g0 + gc, off + t + r0:off + t + r0 + rc, :]
            mu = jnp.mean(acc, axis=-1, keepdims=True)
            var = jnp.mean(jnp.square(acc - mu), axis=-1, keepdims=True)
            y = (acc - mu) * lax.rsqrt(var + LN_EPS) * lng_ref[...][None] + lnb_ref[...][None]
            ca_ref[g0:g0 + gc, r0:r0 + rc, :] = (y * jax.nn.sigmoid(y)).astype(ca_ref.dtype)
    if n_tiles > 1:
        cs_ref[:, 0:CONV_PAD, :] = cs_ref[:, TM:TM + CONV_PAD, :]

    xc = z[:, C_XC:C_XC + D_POOL].reshape(G, TM, D_POOL)
    xc_ref[...] = xc
    ps_ref[:, POOL_PAD:POOL_PAD + TM, :] = xc
    RR = POOL_PAD + TM
    p2_ref[:, 1:RR, :] = ps_ref[:, 1:RR, :] + ps_ref[:, 0:RR - 1, :]
    p4_ref[:, 3:RR, :] = p2_ref[:, 3:RR, :] + p2_ref[:, 1:RR - 2, :]
    p8_ref[:, 7:RR, :] = p4_ref[:, 7:RR, :] + p4_ref[:, 3:RR - 4, :]
    p16_ref[:, 15:RR, :] = p8_ref[:, 15:RR, :] + p8_ref[:, 7:RR - 8, :]
    lane = lax.broadcasted_iota(jnp.int32, (G, TM, D_POOL), 2)
    row = lax.broadcasted_iota(jnp.int32, (G, TM, D_POOL), 1)
    g0m, g1m, g2m = lane < POOL_GC, lane < 2 * POOL_GC, lane < 3 * POOL_GC
    s_tok = slice(POOL_PAD, POOL_PAD + TM)
    wsum = jnp.where(g0m, p2_ref[:, s_tok, :],
                     jnp.where(g1m, p4_ref[:, s_tok, :], jnp.where(g2m, p8_ref[:, s_tok, :], p16_ref[:, s_tok, :])))
    win = jnp.where(g0m, POOL_WINDOWS[0], jnp.where(g1m, POOL_WINDOWS[1],
                                                     jnp.where(g2m, POOL_WINDOWS[2], POOL_WINDOWS[3])))
    cnt = jnp.minimum(pos0 + j * TM + row + 1, win).astype(F32)
    d = (wsum / cnt - xc).reshape(R, D_POOL).astype(BF16)
    py = jnp.dot(d, poolw_ref[...], preferred_element_type=F32) * pscale_ref[...]
    py_ref[...] = py.reshape(G, TM, D_POOL).astype(py_ref.dtype)
    if n_tiles > 1:
        ps_ref[:, 0:POOL_PAD, :] = ps_ref[:, TM:TM + POOL_PAD, :]


def _rope_tables(pos0, T):
    half = HEAD_DIM // 2
    pos = (pos0 + jnp.arange(T)).astype(F32)
    inv = ROPE_THETA ** (-jnp.arange(half, dtype=F32) / half)
    ang = pos[:, None] * inv[None, :]
    cos = jnp.tile(jnp.cos(ang), (1, 2 * LANES // HEAD_DIM))
    sin = jnp.tile(jnp.concatenate([-jnp.sin(ang), jnp.sin(ang)], axis=1), (1, LANES // HEAD_DIM))
    return cos, sin


def _proj(x3, pos0, tables, chist, phist, sw, l, *, G, TM, adt, attn):
    NS, T, _ = x3.shape
    n_tiles = T // TM
    cos, sin = tables

    def tok(c, dt=F32):
        return jax.ShapeDtypeStruct((NS, T, c), dt)

    def tspec(c):
        return pl.BlockSpec((G, TM, c), lambda s, j: (s, j, 0))

    out_cols = [(KV_DIM, F32), (KV_DIM, F32), (IDX_DIM, F32), (D_CONV, F32), (D_POOL, F32),
                (IDX_DIM, adt), (D_ATTN, adt), (IDX_HEADS * IDX_DIM, adt), (LANES, F32), (D_CONV, adt), (D_POOL, adt)]
    out_specs = [tspec(c) for c, _ in out_cols]
    out_shape = [tok(c, dt) for c, dt in out_cols]
    if attn:
        assert G == 1 and TM % LANES == 0
        nb = TM // LANES
        out_specs += [pl.BlockSpec((None, nb, IDX_DIM, IDX_HEADS * LANES), lambda s, j: (s, j, 0, 0)),
                      pl.BlockSpec((None, nb, 1, IDX_HEADS * LANES), lambda s, j: (s, j, 0, 0)),
                      pl.BlockSpec((None, 1, N_KV_HEADS, HEAD_DIM, TM), lambda s, j: (s, j, 0, 0, 0)),
                      pl.BlockSpec((None, 1, N_KV_HEADS, TM, LANES), lambda s, j: (s, j, 0, 0, 0))]
        out_shape += [jax.ShapeDtypeStruct((NS, T // LANES, IDX_DIM, IDX_HEADS * LANES), BF16),
                      jax.ShapeDtypeStruct((NS, T // LANES, 1, IDX_HEADS * LANES), F32),
                      jax.ShapeDtypeStruct((NS, n_tiles, N_KV_HEADS, HEAD_DIM, TM), BF16),
                      jax.ShapeDtypeStruct((NS, n_tiles, N_KV_HEADS, TM, LANES), BF16)]
    kern = functools.partial(_proj_kernel, G=G, TM=TM, pos0=pos0, n_tiles=n_tiles, attn=attn)
    return pl.pallas_call(
        kern,
        grid=(NS // G, n_tiles),
        in_specs=[tspec(D_MODEL), _layer_spec((1, D_MODEL), l), _layer_spec((D_MODEL, D_A), l),
                  pl.BlockSpec((TM, LANES), lambda s, j: (j, 0)), pl.BlockSpec((TM, LANES), lambda s, j: (j, 0)),
                  pl.BlockSpec((G, CONV_PAD, D_CONV), lambda s, j: (s, 0, 0)),
                  pl.BlockSpec((G, POOL_PAD, D_POOL), lambda s, j: (s, 0, 0)),
                  _layer_spec((CONV_PAD, D_CONV), l), _layer_spec((1, D_CONV), l), _layer_spec((1, D_CONV), l),
                  _layer_spec((1, D_CONV), l), _layer_spec((D_POOL, D_POOL), l), _layer_spec((1, D_POOL), l)],
        out_specs=out_specs,
        out_shape=out_shape,
        scratch_shapes=[pltpu.VMEM((G, CONV_PAD + TM, D_CONV), F32)] +
                       [pltpu.VMEM((G, POOL_PAD + TM, D_POOL), F32)] * 5,
        compiler_params=_cparams(("arbitrary", "arbitrary")),
        name="proj",
    )(x3, sw["g_mix"], sw["w_a"], cos, sin, chist, phist, sw["conv_w"], sw["conv_b"], sw["ln_g"], sw["ln_b"],
      sw["pool_w"], sw["pool_scale"])


def _kth_bisect(count, smin, smax, nvalid, kk, steps, warm=0, zero_counts=None):
    zero = jnp.zeros_like(smin)
    c_ge0, c_gt0 = zero_counts if zero_counts is not None else (count(zero, False), count(zero, True))
    pos = c_gt0 >= kk
    neg = c_ge0 < kk
    lo0 = jnp.where(neg, smin, zero)
    clo0 = jnp.where(neg, nvalid, c_ge0)
    hi0 = jnp.where(pos, BIG, zero)
    chi0 = jnp.where(pos, zero, jnp.where(neg, c_ge0, c_gt0))
    cand0 = jnp.where(pos, smax, smin * 0.5)
    small = nvalid <= kk
    done0 = (small | jnp.logical_not(pos | neg)).astype(F32)
    init = (jnp.where(small, smin, lo0), hi0, jnp.where(small, nvalid, clo0), chi0, cand0, done0)

    def count_ge(cand):
        return count(cand, False)

    def cond(st):
        return jnp.min(st[5]) < 0.5

    def body(st):
        for _ in range(steps):
            st = step(st)
        return st

    def step(st):
        lo, hi, clo, chi, cand, done = st
        c = count_ge(cand)
        act = done < 0.5
        ge = c >= kk
        up = act & ge
        dn = act & jnp.logical_not(ge)
        lo = jnp.where(up, cand, lo)
        clo = jnp.where(up, c, clo)
        hi = jnp.where(dn, cand, hi)
        chi = jnp.where(dn, c, chi)
        mid = lo * 0.5 + hi * 0.5
        inside = (mid > lo) & (mid < hi)
        fin = (clo == kk) | jnp.logical_not(inside)
        done = jnp.where(fin, 1.0, done)
        return lo, hi, clo, chi, mid, done

    if warm:
        init = lax.fori_loop(0, warm, lambda _, st: step(st), init)
    lo, _, clo, chi, _, _ = lax.while_loop(cond, body, init)
    return lo, clo, chi


def _attend_prompt_kernel(qit_ref, wq_ref, q_ref, ki_ref, kt_ref, va_ref, o_ref, sc_ref, acc_ref, m_ref, *,
                          CKS, CK, kk):
    i = pl.program_id(1)
    QB = LANES
    nchs = (i * QB + QB + CKS - 1) // CKS
    nch = nchs * (CKS // CK)
    qit = qit_ref[...]
    wq = wq_ref[...]
    qcol = i * QB + lax.broadcasted_iota(jnp.int32, (CKS, QB), 1)
    krow0 = lax.broadcasted_iota(jnp.int32, (CKS, QB), 0)

    def score_chunk(c, carry, causal):
        smax, smin, n_ge0, n_gt0 = carry
        s = jnp.dot(ki_ref[c], qit, preferred_element_type=F32)
        s = jnp.maximum(s, 0.0) * wq
        tot = s[:, 0:QB]
        for hh in range(1, IDX_HEADS):
            tot = tot + s[:, hh * QB:(hh + 1) * QB]
        if causal:
            valid = (krow0 + c * CKS) <= qcol
            masked = jnp.where(valid, tot, NEG_INF)
            lowest = jnp.where(valid, tot, BIG)
        else:
            masked = lowest = tot
        sc_ref[pl.ds(pl.multiple_of(c * CKS, CKS), CKS), :] = masked
        smax = jnp.maximum(smax, jnp.max(masked.reshape(CKS // 8, 8, QB), axis=0))
        smin = jnp.minimum(smin, jnp.min(lowest.reshape(CKS // 8, 8, QB), axis=0))
        n_ge0 = n_ge0 + jnp.sum(jnp.where(masked >= 0.0, 1.0, 0.0).reshape(CKS // 8, 8, QB), axis=0)
        n_gt0 = n_gt0 + jnp.sum(jnp.where(masked > 0.0, 1.0, 0.0).reshape(CKS // 8, 8, QB), axis=0)
        return smax, smin, n_ge0, n_gt0

    def score_run(first, count, carry):
        for u in range(count):
            carry = score_chunk(first + u, carry, causal=False)
        return carry

    n_full = nchs - 1
    full4 = (n_full // 4) * 4
    carry = lax.fori_loop(0, n_full // 4, lambda j, t: score_run(4 * j, 4, t),
                          (jnp.full((8, QB), NEG_INF, F32), jnp.full((8, QB), BIG, F32),
                           jnp.zeros((8, QB), F32), jnp.zeros((8, QB), F32)))
    carry = lax.cond(n_full % 4 >= 2, lambda t: score_run(full4, 2, t), lambda t: t, carry)
    carry = lax.cond(n_full % 2 == 1, lambda t: score_run(n_full - 1, 1, t), lambda t: t, carry)
    smax, smin, n_ge0, n_gt0 = score_chunk(nchs - 1, carry, causal=True)
    smax = jnp.max(smax, axis=0, keepdims=True)
    smin = jnp.min(smin, axis=0, keepdims=True)
    zero_counts = (jnp.sum(n_ge0, axis=0, keepdims=True), jnp.sum(n_gt0, axis=0, keepdims=True))

    n_acc = 4
    rows = CKS // n_acc

    def count(cand, strict):
        def body(c, accs):
            base = pl.multiple_of(c * CKS, CKS)
            out = []
            for a in range(n_acc):
                blk = sc_ref[pl.ds(base + a * rows, rows), :]
                hit = (blk > cand) if strict else (blk >= cand)
                ones = jnp.where(hit, 1.0, 0.0).reshape(rows // 8, 8, QB)
                out.append(accs[a] + jnp.sum(ones, axis=0))
            return tuple(out)
        def quad(j, accs):
            for u in range(4):
                accs = body(4 * j + u, accs)
            return accs
        accs = lax.fori_loop(0, nchs // 4, quad, tuple(jnp.zeros((8, QB), F32) for _ in range(n_acc)))
        done4 = (nchs // 4) * 4
        accs = lax.cond(nchs % 4 >= 2, lambda t: body(done4 + 1, body(done4, t)), lambda t: t, accs)
        accs = lax.cond(nchs % 2 == 1, lambda t: body(nchs - 1, t), lambda t: t, accs)
        return jnp.sum((accs[0] + accs[1]) + (accs[2] + accs[3]), axis=0, keepdims=True)

    nvalid = (i * QB + lax.broadcasted_iota(jnp.int32, (1, QB), 1) + 1).astype(F32)
    lo, clo, chi = _kth_bisect(count, smin, smax, nvalid, float(kk), steps=4, warm=12, zero_counts=zero_counts)

    fix = (clo > kk) & (nvalid > kk)

    @pl.when(jnp.max(fix.astype(F32)) > 0.5)
    def _():
        keep = kk - chi
        TF = min(256, CKS)
        tri = (lax.broadcasted_iota(jnp.int32, (TF, TF), 1)
               <= lax.broadcasted_iota(jnp.int32, (TF, TF), 0)).astype(BF16)

        def body(c, run):
            rows_c = pl.ds(pl.multiple_of(c * TF, TF), TF)
            blk = sc_ref[rows_c, :]
            tie = jnp.where(fix & (blk == lo), 1.0, 0.0)
            rank = run + jnp.dot(tri, tie.astype(BF16), preferred_element_type=F32)
            sc_ref[rows_c, :] = jnp.where((tie > 0.5) & (rank > keep), NEG_INF, blk)
            return run + jnp.sum(tie, axis=0, keepdims=True)
        lax.fori_loop(0, nchs * (CKS // TF), body, jnp.zeros((1, QB), F32))

    GQ = GROUP * QB
    q32 = q_ref[...].astype(F32)
    qn = [jnp.concatenate([q32[:, hh * HEAD_DIM:(hh + 1) * HEAD_DIM] for hh in range(n * GROUP, (n + 1) * GROUP)],
                          axis=0).astype(BF16) for n in range(N_KV_HEADS)]
    acc_ref[...] = jnp.zeros_like(acc_ref)
    m_ref[...] = jnp.full_like(m_ref, -BIG)
    reps = CK // LANES

    def att_chunk(c):
        scb = sc_ref[pl.ds(pl.multiple_of(c * CK, CK), CK), :]
        bias_t = jnp.where(scb >= lo, 0.0, NEG_INF).T
        for n in range(N_KV_HEADS):
            s = jnp.dot(qn[n], kt_ref[c, n], preferred_element_type=F32)
            s = (s.reshape(GROUP, QB, CK) + bias_t[None]).reshape(GQ, CK)
            m_old = m_ref[n]
            mnew = jnp.maximum(m_old, jnp.max(s, axis=1, keepdims=True))
            p = jnp.exp2(s - jnp.concatenate([mnew] * reps, axis=1))
            alpha = jnp.exp2(m_old - mnew)
            pv = jnp.dot(p.astype(BF16), va_ref[c, n], preferred_element_type=F32)
            acc_ref[n] = alpha * acc_ref[n] + pv
            m_ref[n] = mnew

    def att_quad(j, _):
        for u in range(4):
            att_chunk(4 * j + u)
        return 0

    lax.fori_loop(0, nch // 4, att_quad, 0)
    done4 = (nch // 4) * 4

    @pl.when(nch % 4 >= 2)
    def _():
        att_chunk(done4)
        att_chunk(done4 + 1)

    @pl.when(nch % 2 == 1)
    def _():
        att_chunk(nch - 1)

    outs = []
    for n in range(N_KV_HEADS):
        a = acc_ref[n]
        o = a[:, :HEAD_DIM] / a[:, HEAD_DIM:HEAD_DIM + 1]
        outs += [o[g * QB:(g + 1) * QB, :] for g in range(GROUP)]
    o_ref[...] = jnp.concatenate(outs, axis=1).astype(BF16)


def _attend_prompt(q, qit, wq, kibf, kt, va, *, CKS):
    B, T, _ = q.shape
    QB = LANES
    nq = T // QB
    nc, CK = kt.shape[1], kt.shape[4]
    ncs = T // CKS
    kk = min(TOPK_MAX, T // 4)
    kib = kibf.reshape(B, ncs, CKS, IDX_DIM)

    kern = functools.partial(_attend_prompt_kernel, CKS=CKS, CK=CK, kk=kk)
    return pl.pallas_call(
        kern,
        grid=(B, nq),
        in_specs=[pl.BlockSpec((None, None, IDX_DIM, 8 * QB), lambda b, i: (b, i, 0, 0)),
                  pl.BlockSpec((None, None, 1, 8 * QB), lambda b, i: (b, i, 0, 0)),
                  pl.BlockSpec((None, QB, D_ATTN), lambda b, i: (b, i, 0)),
                  pl.BlockSpec((None, ncs, CKS, IDX_DIM), lambda b, i: (b, 0, 0, 0)),
                  pl.BlockSpec((None, nc, N_KV_HEADS, HEAD_DIM, CK), lambda b, i: (b, 0, 0, 0, 0)),
                  pl.BlockSpec((None, nc, N_KV_HEADS, CK, LANES), lambda b, i: (b, 0, 0, 0, 0))],
        out_specs=pl.BlockSpec((None, QB, D_ATTN), lambda b, i: (b, i, 0)),
        out_shape=jax.ShapeDtypeStruct((B, T, D_ATTN), BF16),
        scratch_shapes=[pltpu.VMEM((T, QB), F32), pltpu.VMEM((N_KV_HEADS, GROUP * QB, LANES), F32),
                        pltpu.VMEM((N_KV_HEADS, GROUP * QB, LANES), F32)],
        compiler_params=_cparams(("arbitrary", "arbitrary")),
        name="attend_prompt",
    )(qit, wq, q, kib, kt, va)


def _sample_score_kernel(pt_ref, *refs, P, n_steps, past_len, kk):
    pages = refs[:P]
    qi_ref, wb_ref, kin_ref, sc_ref, thr_ref = refs[P:P + 5]
    j = pl.program_id(1)
    TQ = sc_ref.shape[1]
    NBLK = sc_ref.shape[0]
    qi = qi_ref[...]
    wb = wb_ref[...]

    def block_scores(kpage_t):
        s = jnp.dot(qi, kpage_t.astype(BF16), preferred_element_type=F32)
        s = jnp.maximum(s, 0.0) * wb
        return jnp.sum(s.reshape(IDX_HEADS, TQ, LANES), axis=0)

    for r in range(P):
        sc_ref[j * P + r] = block_scores(pages[r][...])

    @pl.when(j == n_steps - 1)
    def _():
        tot = block_scores(kin_ref[...])
        qrow = lax.broadcasted_iota(jnp.int32, (TQ, LANES), 0)
        kcol = lax.broadcasted_iota(jnp.int32, (TQ, LANES), 1)
        sc_ref[NBLK - 1] = jnp.where(kcol <= qrow, tot, NEG_INF)

        def lane_bcast(a):
            return jnp.broadcast_to(a, (TQ, LANES))

        sc = sc_ref[...]
        smax = lane_bcast(jnp.max(jnp.max(sc, axis=0), axis=1, keepdims=True))
        smin = lane_bcast(jnp.min(jnp.min(jnp.where(sc > NEG_INF, sc, BIG), axis=0), axis=1, keepdims=True))
        nvalid = (past_len + qrow + 1).astype(F32)

        def count(cand, strict):
            blk = sc_ref[...]
            hit = (blk > cand[None]) if strict else (blk >= cand[None])
            c = jnp.sum(jnp.where(hit, 1.0, 0.0), axis=0)
            return lane_bcast(jnp.sum(c, axis=1, keepdims=True))

        lo, clo, chi = _kth_bisect(count, smin, smax, nvalid, float(kk), steps=4)
        thr_ref[...] = lo

        fix = (clo > kk) & (nvalid > kk)

        @pl.when(jnp.max(fix.astype(F32)) > 0.5)
        def _():
            keep = kk - chi
            tri = (lax.broadcasted_iota(jnp.int32, (LANES, LANES), 0)
                   <= lax.broadcasted_iota(jnp.int32, (LANES, LANES), 1)).astype(BF16)

            def body(c, run):
                blk = sc_ref[c]
                tie = fix & (blk == lo)
                rank = run + jnp.dot(tie.astype(BF16), tri, preferred_element_type=F32)
                sc_ref[c] = jnp.where(tie & (rank > keep), NEG_INF, blk)
                return run + lane_bcast(jnp.sum(tie.astype(F32), axis=1, keepdims=True))
            lax.fori_loop(0, NBLK, body, jnp.zeros((TQ, LANES), F32))


def _sample_scores(page_table, cache_kidx_t, l, qi_rows, w_rows, ki_new_t, *, P):
    NB, n_pages = page_table.shape
    TQ = qi_rows.shape[1] // IDX_HEADS
    n_steps = n_pages // P
    past_len = n_pages * PAGE_SIZE
    kk = min(TOPK_MAX, (past_len + TQ) // 4)
    NBLK = n_pages + 1

    def page_spec(r):
        return pl.BlockSpec((None, None, IDX_DIM, PAGE_SIZE), lambda b, j, pt: (l, pt[b, j * P + r], 0, 0))

    kern = functools.partial(_sample_score_kernel, P=P, n_steps=n_steps, past_len=past_len, kk=kk)
    grid_spec = pltpu.PrefetchScalarGridSpec(
        num_scalar_prefetch=1,
        grid=(NB, n_steps),
        in_specs=[page_spec(r) for r in range(P)] + [
            pl.BlockSpec((None, IDX_HEADS * TQ, IDX_DIM), lambda b, j, pt: (b, 0, 0)),
            pl.BlockSpec((None, IDX_HEADS * TQ, LANES), lambda b, j, pt: (b, 0, 0)),
            pl.BlockSpec((None, IDX_DIM, PAGE_SIZE), lambda b, j, pt: (b, 0, 0))],
        out_specs=[pl.BlockSpec((None, NBLK, TQ, LANES), lambda b, j, pt: (b, 0, 0, 0)),
                   pl.BlockSpec((None, TQ, LANES), lambda b, j, pt: (b, 0, 0))],
    )
    return pl.pallas_call(
        kern,
        grid_spec=grid_spec,
        out_shape=[jax.ShapeDtypeStruct((NB, NBLK, TQ, LANES), F32), jax.ShapeDtypeStruct((NB, TQ, LANES), F32)],
        compiler_params=_cparams(("arbitrary", "arbitrary")),
        name="sample_scores",
    )(page_table, *([cache_kidx_t] * P), qi_rows, w_rows, ki_new_t)


def _sample_attend_kernel(pt_ref, *refs, P, n_steps):
    kpages = refs[:P]
    vpages = refs[P:2 * P]
    sc_ref, scn_ref, thr_ref, q_ref, kn_ref, vn_ref, o_ref, m_ref, l_ref, acc_ref = refs[2 * P:2 * P + 10]
    j = pl.program_id(1)
    qz = q_ref[...]
    thr = thr_ref[...]

    @pl.when(j == 0)
    def _():
        m_ref[...] = jnp.full_like(m_ref, -BIG)
        l_ref[...] = jnp.zeros_like(l_ref)
        acc_ref[...] = jnp.zeros_like(acc_ref)

    def update(kblocks, vblocks, scblocks):
        s_parts = []
        for kb, scb in zip(kblocks, scblocks):
            s = jnp.dot(qz, kb.astype(BF16), preferred_element_type=F32)
            bias = jnp.where(scb >= thr, 0.0, NEG_INF)
            s_parts.append(s + jnp.concatenate([bias] * N_HEADS, axis=0))
        s = jnp.concatenate(s_parts, axis=1) if len(s_parts) > 1 else s_parts[0]
        m_old = m_ref[...]
        mnew = jnp.maximum(m_old, jnp.broadcast_to(jnp.max(s, axis=1, keepdims=True), m_old.shape))
        alpha = jnp.exp2(m_old - mnew)
        p = jnp.exp2(s - jnp.concatenate([mnew] * len(s_parts), axis=1) if len(s_parts) > 1 else s - mnew)
        l_ref[...] = alpha * l_ref[...] + jnp.broadcast_to(jnp.sum(p, axis=1, keepdims=True), m_old.shape)
        m_ref[...] = mnew
        pv = jnp.zeros(acc_ref.shape, F32)
        for r, vb in enumerate(vblocks):
            pv = pv + lax.dot_general(p[:, r * LANES:(r + 1) * LANES].astype(BF16), vb.astype(BF16),
                                      (((1,), (1,)), ((), ())), preferred_element_type=F32)
        acc_ref[...] = alpha * acc_ref[...] + pv

    update([kp[...] for kp in kpages], [vp[...] for vp in vpages], [sc_ref[r] for r in range(P)])

    @pl.when(j == n_steps - 1)
    def _():
        update([kn_ref[...]], [vn_ref[...]], [scn_ref[0]])
        o_ref[...] = acc_ref[...] / l_ref[...]


def _sample_attend(page_table, cache_k_t, cache_v_t, l, sc, thr, q_rows, k_new_t, v_new_t, *, P):
    NB, n_pages = page_table.shape
    R = q_rows.shape[1]
    TQ = R // N_HEADS
    n_steps = n_pages // P

    def page_spec(r):
        return pl.BlockSpec((None, None, KV_DIM, PAGE_SIZE), lambda b, j, pt: (l, pt[b, j * P + r], 0, 0))

    kern = functools.partial(_sample_attend_kernel, P=P, n_steps=n_steps)
    grid_spec = pltpu.PrefetchScalarGridSpec(
        num_scalar_prefetch=1,
        grid=(NB, n_steps),
        in_specs=[page_spec(r) for r in range(P)] + [page_spec(r) for r in range(P)] + [
            pl.BlockSpec((None, P, TQ, LANES), lambda b, j, pt: (b, j, 0, 0)),
            pl.BlockSpec((None, 1, TQ, LANES), lambda b, j, pt: (b, n_pages, 0, 0)),
            pl.BlockSpec((None, TQ, LANES), lambda b, j, pt: (b, 0, 0)),
            pl.BlockSpec((None, R, KV_DIM), lambda b, j, pt: (b, 0, 0)),
            pl.BlockSpec((None, KV_DIM, PAGE_SIZE), lambda b, j, pt: (b, 0, 0)),
            pl.BlockSpec((None, KV_DIM, PAGE_SIZE), lambda b, j, pt: (b, 0, 0))],
        out_specs=pl.BlockSpec((None, R, KV_DIM), lambda b, j, pt: (b, 0, 0)),
        scratch_shapes=[pltpu.VMEM((R, LANES), F32), pltpu.VMEM((R, LANES), F32), pltpu.VMEM((R, KV_DIM), F32)],
    )
    return pl.pallas_call(
        kern,
        grid_spec=grid_spec,
        out_shape=jax.ShapeDtypeStruct((NB, R, KV_DIM), F32),
        compiler_params=_cparams(("arbitrary", "arbitrary")),
        name="sample_attend",
    )(page_table, *([cache_k_t] * P), *([cache_v_t] * P), sc, sc, thr, q_rows, k_new_t, v_new_t)


def _mix_out_kernel(x_ref, g_ref, wg_ref, ca_ref, ao_ref, py_ref, wco_ref, wao_ref, wpo_ref, wo_ref, o_ref):
    x = x_ref[...]
    ms = jnp.mean(x * x, axis=-1, keepdims=True)
    h = (x * lax.rsqrt(ms + RMS_EPS) * g_ref[...]).astype(BF16)
    gates = jax.nn.sigmoid(jnp.dot(h, wg_ref[...], preferred_element_type=F32))
    y_a = jnp.dot(ca_ref[...], wco_ref[...], preferred_element_type=F32)
    y_b = jnp.dot(ao_ref[...], wao_ref[...], preferred_element_type=F32)
    y_c = jnp.dot(py_ref[...], wpo_ref[...], preferred_element_type=F32)
    merged = (gates[:, 0:D_MODEL] * y_a + gates[:, D_MODEL:2 * D_MODEL] * y_b
              + gates[:, 2 * D_MODEL:3 * D_MODEL] * y_c)
    o_ref[...] = x + jnp.dot(merged.astype(BF16), wo_ref[...], preferred_element_type=F32)


def _mix_out(x2, ca, ao, py, sw, l, *, TM):
    N = x2.shape[0]

    def tspec(c):
        return pl.BlockSpec((TM, c), lambda i: (i, 0))

    return pl.pallas_call(
        _mix_out_kernel,
        grid=(N // TM,),
        in_specs=[tspec(D_MODEL), _layer_spec((1, D_MODEL), l), _layer_spec((D_MODEL, N_BRANCH * D_MODEL), l),
                  tspec(D_CONV), tspec(D_ATTN), tspec(D_POOL),
                  _layer_spec((D_CONV, D_MODEL), l), _layer_spec((D_ATTN, D_MODEL), l),
                  _layer_spec((D_POOL, D_MODEL), l), _layer_spec((D_MODEL, D_MODEL), l)],
        out_specs=tspec(D_MODEL),
        out_shape=jax.ShapeDtypeStruct((N, D_MODEL), F32),
        compiler_params=_cparams(("arbitrary",)),
        name="mix_out",
    )(x2, sw["g_mix"], sw["w_g"], ca, ao, py, sw["w_conv_out"], sw["w_attn_out"], sw["w_pool_out"], sw["w_out"])


def _ffn_kernel(x_ref, g_ref, w1_ref, w2_ref, gf_ref, o_ref, *, final):
    x = x_ref[...]
    ms = jnp.mean(x * x, axis=-1, keepdims=True)
    h = (x * lax.rsqrt(ms + RMS_EPS) * g_ref[...]).astype(BF16)
    r = jnp.maximum(jnp.dot(h, w1_ref[...], preferred_element_type=F32), 0.0)
    y = x + jnp.dot((r * r).astype(BF16), w2_ref[...], preferred_element_type=F32)
    if final:
        ms2 = jnp.mean(y * y, axis=-1, keepdims=True)
        y = y * lax.rsqrt(ms2 + RMS_EPS) * gf_ref[...]
    o_ref[...] = y


def _ffn(x2, sw, l, g_final, *, TM, final):
    N = x2.shape[0]
    once = pl.Buffered(1)
    return pl.pallas_call(
        functools.partial(_ffn_kernel, final=final),
        grid=(N // TM,),
        in_specs=[pl.BlockSpec((TM, D_MODEL), lambda i: (i, 0)), _layer_spec((1, D_MODEL), l),
                  _layer_spec((D_MODEL, D_FF), l, pipeline_mode=once),
                  _layer_spec((D_FF, D_MODEL), l, pipeline_mode=once),
                  pl.BlockSpec((1, D_MODEL), lambda i: (0, 0))],
        out_specs=pl.BlockSpec((TM, D_MODEL), lambda i: (i, 0)),
        out_shape=jax.ShapeDtypeStruct((N, D_MODEL), F32),
        compiler_params=_cparams(("arbitrary",)),
        name="ffn",
    )(x2, sw["g_ffn"], sw["w_ff1"], sw["w_ff2"], g_final)


def _stacked_weights(g_mix, w_in, conv_w, conv_b, conv_ln_g, conv_ln_b, w_conv_out, w_attn_out,
                     pool_w, pool_scale, w_pool_out, w_out, g_ffn, w_ff1, w_ff2):
    depth = w_in.shape[0]
    o_ki = C_QI + IDX_HEADS * IDX_DIM
    o_wi = o_ki + IDX_DIM
    o_xc = o_wi + IDX_HEADS
    o_g = o_xc + D_POOL

    def pad_to_tile(a):
        return jnp.pad(a, ((0, 0), (0, 0), (0, LANES - a.shape[2])))

    w_a = jnp.concatenate([w_in[:, :, :o_ki], w_in[:, :, o_xc:o_g], pad_to_tile(w_in[:, :, o_ki:o_wi]),
                           pad_to_tile(w_in[:, :, o_wi:o_xc])], axis=2).astype(BF16)
    pw = jnp.zeros((depth, D_POOL, D_POOL), F32)
    for g in range(len(POOL_WINDOWS)):
        pw = pw.at[:, g * POOL_GC:(g + 1) * POOL_GC, g * POOL_GC:(g + 1) * POOL_GC].set(pool_w[:, g])

    def row(a):
        return a[:, None, :]

    return dict(
        g_mix=row(g_mix), w_a=w_a, w_g=w_in[:, :, o_g:].astype(BF16),
        conv_w=jnp.pad(conv_w, ((0, 0), (0, CONV_PAD - CONV_W), (0, 0))), conv_b=row(conv_b),
        ln_g=row(conv_ln_g), ln_b=row(conv_ln_b),
        pool_w=pw.astype(BF16), pool_scale=row(pool_scale),
        w_conv_out=w_conv_out.astype(BF16), w_attn_out=w_attn_out.astype(BF16),
        w_pool_out=w_pool_out.astype(BF16), w_out=w_out.astype(BF16),
        g_ffn=row(g_ffn), w_ff1=w_ff1.astype(BF16), w_ff2=w_ff2.astype(BF16))


def _tail_state(hist, new, keep):
    return jnp.concatenate([hist, new], axis=1)[:, -keep:]


def kernel(x_prompt, x_sample, cache_k, cache_v, cache_kidx, state_conv, state_pool, page_table, g_mix, w_in,
           conv_w, conv_b, conv_ln_g, conv_ln_b, w_conv_out, w_attn_out, pool_w, pool_scale, w_pool_out, w_out,
           g_ffn, w_ff1, w_ff2, g_final):
    B, T, _ = x_prompt.shape
    NB, TS, _ = x_sample.shape
    depth = w_in.shape[0]
    n_pool = cache_k.shape[1]
    past_len = page_table.shape[1] * PAGE_SIZE
    TM_P = min(512, T)
    gf = g_final[None]
    xp = x_prompt
    xs = x_sample
    zero_conv = jnp.zeros((B, CONV_PAD, D_CONV), F32)
    zero_pool = jnp.zeros((B, POOL_PAD, D_POOL), F32)
    cache_k_t = cache_k.transpose(0, 1, 3, 4, 2).reshape(depth, n_pool, KV_DIM, PAGE_SIZE)
    cache_v_t = cache_v.transpose(0, 1, 3, 4, 2).reshape(depth, n_pool, KV_DIM, PAGE_SIZE)
    cache_kidx_t = cache_kidx.transpose(0, 1, 3, 2)
    sw = _stacked_weights(g_mix, w_in, conv_w, conv_b, conv_ln_g, conv_ln_b, w_conv_out, w_attn_out,
                          pool_w, pool_scale, w_pool_out, w_out, g_ffn, w_ff1, w_ff2)
    tables_p = _rope_tables(0, T)
    tables_s = _rope_tables(past_len, TS)
    hist_conv_s = jnp.pad(state_conv, ((0, 0), (0, 0), (CONV_PAD - CONV_HIST, 0), (0, 0)))
    hist_pool_s = jnp.pad(state_pool, ((0, 0), (0, 0), (POOL_PAD - POOL_HIST, 0), (0, 0)))
    st_p = [[] for _ in range(5)]
    st_s = [[] for _ in range(5)]
    for l in range(depth):
        final = l == depth - 1

        (k, v, ki, u, xc, kibf, q, _, _, ca, py, qit, wq, kt, va) = _proj(
            xp, 0, tables_p, zero_conv, zero_pool, sw, l, G=1, TM=TM_P, adt=BF16, attn=True)
        ao = _attend_prompt(q, qit, wq, kibf, kt, va, CKS=TM_P)
        x2 = _mix_out(xp.reshape(B * T, D_MODEL), ca.reshape(B * T, D_CONV), ao.reshape(B * T, D_ATTN),
                      py.reshape(B * T, D_POOL), sw, l, TM=TM_P)
        xp = _ffn(x2, sw, l, gf, TM=TM_P, final=final).reshape(B, T, D_MODEL)
        st_p[0].append(k.reshape(B, T, N_KV_HEADS, HEAD_DIM))
        st_p[1].append(v.reshape(B, T, N_KV_HEADS, HEAD_DIM))
        st_p[2].append(ki)
        st_p[3].append(u[:, T - CONV_HIST:])
        st_p[4].append(xc[:, T - POOL_HIST:])

        (k, v, ki, u, xc, _, q, qi, wi, ca, py) = _proj(
            xs, past_len, tables_s, hist_conv_s[l], hist_pool_s[l], sw, l, G=NB, TM=TS, adt=F32, attn=False)
        q, qi, ca, py = q.astype(BF16), qi.astype(BF16), ca.astype(BF16), py.astype(BF16)
        qi_rows = qi.reshape(NB, TS, IDX_HEADS, IDX_DIM).transpose(0, 2, 1, 3).reshape(NB, IDX_HEADS * TS, IDX_DIM)
        w_rows = jnp.broadcast_to(wi[:, :, :IDX_HEADS].transpose(0, 2, 1).reshape(NB, IDX_HEADS * TS, 1),
                                  (NB, IDX_HEADS * TS, LANES))
        qh = q.reshape(NB, TS, N_HEADS, HEAD_DIM).transpose(0, 2, 1, 3)
        head_kv = (jnp.arange(N_HEADS) // GROUP)[None, :, None, None]
        q_rows = jnp.concatenate([jnp.where(head_kv == n, qh, jnp.zeros_like(qh)) for n in range(N_KV_HEADS)],
                                 axis=-1).reshape(NB, N_HEADS * TS, KV_DIM)
        pad_rows = ((0, 0), (0, PAGE_SIZE - TS), (0, 0))
        pages_per_step = min(32, page_table.shape[1])

        def new_page_t(a):
            return jnp.pad(a, pad_rows).transpose(0, 2, 1)

        sc, thr = _sample_scores(page_table, cache_kidx_t, l, qi_rows, w_rows, new_page_t(ki), P=pages_per_step)
        o_rows = _sample_attend(page_table, cache_k_t, cache_v_t, l, sc, thr, q_rows, new_page_t(k), new_page_t(v),
                                P=pages_per_step)
        o5 = o_rows.reshape(NB, N_KV_HEADS, GROUP, TS, N_KV_HEADS, HEAD_DIM)
        ao = jnp.stack([o5[:, n, :, :, n, :] for n in range(N_KV_HEADS)], axis=1)
        ao = ao.reshape(NB, N_HEADS, TS, HEAD_DIM).transpose(0, 2, 1, 3).reshape(NB * TS, D_ATTN).astype(BF16)
        x2 = _mix_out(xs.reshape(NB * TS, D_MODEL), ca.reshape(NB * TS, D_CONV), ao,
                      py.reshape(NB * TS, D_POOL), sw, l, TM=NB * TS)
        xs = _ffn(x2, sw, l, gf, TM=NB * TS, final=final).reshape(NB, TS, D_MODEL)
        st_s[0].append(k.reshape(NB, TS, N_KV_HEADS, HEAD_DIM))
        st_s[1].append(v.reshape(NB, TS, N_KV_HEADS, HEAD_DIM))
        st_s[2].append(ki)
        st_s[3].append(_tail_state(state_conv[l], u, CONV_HIST))
        st_s[4].append(_tail_state(state_pool[l], xc, POOL_HIST))

    sp = [jnp.stack(a, axis=0) for a in st_p]
    ss = [jnp.stack(a, axis=0) for a in st_s]
    return (xp, xs, sp[0], sp[1], sp[2], sp[3], sp[4], ss[0], ss[1], ss[2], ss[3], ss[4])
```

```python
import functools

import jax
import jax.numpy as jnp
from jax import lax
from jax.experimental import pallas as pl
from jax.experimental.pallas import tpu as pltpu

F32 = jnp.float32
BF16 = jnp.bfloat16

D_MODEL = 1024
D_CONV = 256
CONV_W = 31
CONV_HIST = CONV_W - 1
HEAD_DIM = 64
D_ATTN = 512
N_HEADS = 8
N_KV_HEADS = 2
KV_DIM = N_KV_HEADS * HEAD_DIM
GROUP = N_HEADS // N_KV_HEADS
IDX_HEADS = 8
IDX_DIM = 64
TOPK_MAX = 256
D_POOL = 256
POOL_WINDOWS = (2, 4, 8, 16)
POOL_GC = D_POOL // len(POOL_WINDOWS)
POOL_HIST = max(POOL_WINDOWS) - 1
N_BRANCH = 3
D_FF = 4 * D_MODEL
ROPE_THETA = 10000.0
RMS_EPS = 1e-6
LN_EPS = 1e-5
PAGE_SIZE = 128

LANES = 128
CONV_PAD = 32
POOL_PAD = 16
Q_SCALE = HEAD_DIM ** -0.5 * 1.4426950408889634
BIG = 3.0e38
NEG_INF = float("-inf")

C_AIN, C_AGATE, C_Q, C_K, C_V, C_QI, C_XC, C_KI, C_WI, D_A = 0, 256, 512, 1024, 1152, 1280, 1792, 2048, 2176, 2304

VMEM_LIMIT = 56 * 1024 * 1024


def _cparams(sem):
    return pltpu.CompilerParams(dimension_semantics=sem, vmem_limit_bytes=VMEM_LIMIT)


def _layer_spec(shape, l, **kw):
    n = len(shape)
    return pl.BlockSpec((None,) + tuple(shape), lambda *_: (l,) + (0,) * n, **kw)


def _proj_kernel(x_ref, g_ref, w_ref, cos_ref, sin_ref, chist_ref, phist_ref, convw_ref, convb_ref,
                 lng_ref, lnb_ref, poolw_ref, pscale_ref,
                 k_ref, v_ref, ki_ref, u_ref, xc_ref, kibf_ref, q_ref, qi_ref, wi_ref, ca_ref, py_ref,
                 *rest, G, TM, pos0, n_tiles, attn):
    if attn:
        qit_ref, wq_ref, kt_ref, va_ref = rest[:4]
        rest = rest[4:]
    cs_ref, ps_ref, p2_ref, p4_ref, p8_ref, p16_ref = rest
    j = pl.program_id(1)
    R = G * TM
    x = x_ref[...].reshape(R, D_MODEL)
    ms = jnp.mean(x * x, axis=-1, keepdims=True)
    h = (x * lax.rsqrt(ms + RMS_EPS) * g_ref[...]).astype(BF16)
    z = jnp.dot(h, w_ref[...], preferred_element_type=F32)

    cos = cos_ref[...]
    sin = sin_ref[...]
    if G > 1:
        cos = jnp.concatenate([cos] * G, axis=0)
        sin = jnp.concatenate([sin] * G, axis=0)

    def rope(t):
        n = t.shape[1] // LANES
        c = jnp.concatenate([cos] * n, axis=1) if n > 1 else cos
        s = jnp.concatenate([sin] * n, axis=1) if n > 1 else sin
        lane = lax.broadcasted_iota(jnp.int32, t.shape, 1)
        first = (lane & (HEAD_DIM - 1)) < HEAD_DIM // 2
        sw = jnp.where(first, pltpu.roll(t, t.shape[1] - HEAD_DIM // 2, 1), pltpu.roll(t, HEAD_DIM // 2, 1))
        return t * c + sw * s

    q = rope(z[:, C_Q:C_Q + D_ATTN]) * Q_SCALE
    k = rope(z[:, C_K:C_K + KV_DIM])
    v = z[:, C_V:C_V + KV_DIM]
    qi = rope(z[:, C_QI:C_QI + IDX_HEADS * IDX_DIM])
    ki = rope(z[:, C_KI:C_KI + LANES])[:, :IDX_DIM]
    wi = z[:, C_WI:C_WI + LANES] * (IDX_HEADS ** -0.5 * IDX_DIM ** -0.5)

    k_ref[...] = k.reshape(G, TM, KV_DIM)
    v_ref[...] = v.reshape(G, TM, KV_DIM)
    ki_ref[...] = ki.reshape(G, TM, IDX_DIM)
    kibf_ref[...] = ki.reshape(G, TM, IDX_DIM).astype(kibf_ref.dtype)
    q_ref[...] = q.reshape(G, TM, D_ATTN).astype(q_ref.dtype)
    qi_ref[...] = qi.reshape(G, TM, IDX_HEADS * IDX_DIM).astype(qi_ref.dtype)
    wi_ref[...] = wi.reshape(G, TM, LANES)
    if attn:
        for b in range(TM // LANES):
            blk = slice(b * LANES, (b + 1) * LANES)
            qi_t = qi[blk, :].T
            qit_ref[b] = jnp.concatenate([qi_t[hh * IDX_DIM:(hh + 1) * IDX_DIM, :] for hh in range(IDX_HEADS)],
                                         axis=1).astype(BF16)
            wi_t = wi[blk, :].T
            wq_ref[b] = jnp.concatenate([wi_t[hh:hh + 1, :] for hh in range(IDX_HEADS)], axis=1)
        k_t = k.T
        lane = lax.broadcasted_iota(jnp.int32, (TM, KV_DIM), 1)
        tail = jnp.where(lane == HEAD_DIM, 1.0, 0.0)
        for n in range(N_KV_HEADS):
            kt_ref[0, n] = k_t[n * HEAD_DIM:(n + 1) * HEAD_DIM, :].astype(BF16)
            vn = v if n == 0 else pltpu.roll(v, KV_DIM - n * HEAD_DIM, 1)
            va_ref[0, n] = jnp.where(lane < HEAD_DIM, vn, tail).astype(BF16)

    a_in = z[:, C_AIN:C_AIN + D_CONV]
    a_gate = z[:, C_AGATE:C_AGATE + D_CONV]
    u = (a_in * jax.nn.sigmoid(a_gate)).reshape(G, TM, D_CONV)
    u_ref[...] = u

    @pl.when(j == 0)
    def _():
        cs_ref[:, 0:CONV_PAD, :] = chist_ref[...]
        ps_ref[:, 0:POOL_PAD, :] = phist_ref[...]

    cs_ref[:, CONV_PAD:CONV_PAD + TM, :] = u
    off = CONV_PAD - CONV_HIST
    rc = min(TM, 64)
    gc = min(G, 8)
    for g0 in range(0, G, gc):
        for r0 in range(0, TM, rc):
            acc = jnp.zeros((gc, rc, D_CONV), F32) + convb_ref[...][None]
            for t in range(CONV_W):
                acc = acc + convw_ref[t:t + 1, :][None] * cs_ref[g0:g0 + gc, off + t + r0:off + t + r0 + rc, :]
            mu = jnp.mean(acc, axis=-1, keepdims=True)
            var = jnp.mean(jnp.square(acc - mu), axis=-1, keepdims=True)
            y = (acc - mu) * lax.rsqrt(var + LN_EPS) * lng_ref[...][None] + lnb_ref[...][None]
            ca_ref[g0:g0 + gc, r0:r0 + rc, :] = (y * jax.nn.sigmoid(y)).astype(ca_ref.dtype)
    if n_tiles > 1:
        cs_ref[:, 0:CONV_PAD, :] = cs_ref[:, TM:TM + CONV_PAD, :]

    xc = z[:, C_XC:C_XC + D_POOL].reshape(G, TM, D_POOL)
    xc_ref[...] = xc
    ps_ref[:, POOL_PAD:POOL_PAD + TM, :] = xc
    RR = POOL_PAD + TM
    p2_ref[:, 1:RR, :] = ps_ref[:, 1:RR, :] + ps_ref[:, 0:RR - 1, :]
    p4_ref[:, 3:RR, :] = p2_ref[:, 3:RR, :] + p2_ref[:, 1:RR - 2, :]
    p8_ref[:, 7:RR, :] = p4_ref[:, 7:RR, :] + p4_ref[:, 3:RR - 4, :]
    p16_ref[:, 15:RR, :] = p8_ref[:, 15:RR, :] + p8_ref[:, 7:RR - 8, :]
    lane = lax.broadcasted_iota(jnp.int32, (G, TM, D_POOL), 2)
    row = lax.broadcasted_iota(jnp.int32, (G, TM, D_POOL), 1)
    g0m, g1m, g2m = lane < POOL_GC, lane < 2 * POOL_GC, lane < 3 * POOL_GC
    s_tok = slice(POOL_PAD, POOL_PAD + TM)
    wsum = jnp.where(g0m, p2_ref[:, s_tok, :],
                     jnp.where(g1m, p4_ref[:, s_tok, :], jnp.where(g2m, p8_ref[:, s_tok, :], p16_ref[:, s_tok, :])))
    win = jnp.where(g0m, POOL_WINDOWS[0], jnp.where(g1m, POOL_WINDOWS[1],
                                                     jnp.where(g2m, POOL_WINDOWS[2], POOL_WINDOWS[3])))
    cnt = jnp.minimum(pos0 + j * TM + row + 1, win).astype(F32)
    d = (wsum / cnt - xc).reshape(R, D_POOL).astype(BF16)
    py = jnp.dot(d, poolw_ref[...], preferred_element_type=F32) * pscale_ref[...]
    py_ref[...] = py.reshape(G, TM, D_POOL).astype(py_ref.dtype)
    if n_tiles > 1:
        ps_ref[:, 0:POOL_PAD, :] = ps_ref[:, TM:TM + POOL_PAD, :]


def _rope_tables(pos0, T):
    half = HEAD_DIM // 2
    pos = (pos0 + jnp.arange(T)).astype(F32)
    inv = ROPE_THETA ** (-jnp.arange(half, dtype=F32) / half)
    ang = pos[:, None] * inv[None, :]
    cos = jnp.tile(jnp.cos(ang), (1, 2 * LANES // HEAD_DIM))
    sin = jnp.tile(jnp.concatenate([-jnp.sin(ang), jnp.sin(ang)], axis=1), (1, LANES // HEAD_DIM))
    return cos, sin


def _proj(x3, pos0, tables, chist, phist, sw, l, *, G, TM, adt, attn):
    NS, T, _ = x3.shape
    n_tiles = T // TM
    cos, sin = tables

    def tok(c, dt=F32):
        return jax.ShapeDtypeStruct((NS, T, c), dt)

    def tspec(c):
        return pl.BlockSpec((G, TM, c), lambda s, j: (s, j, 0))

    out_cols = [(KV_DIM, F32), (KV_DIM, F32), (IDX_DIM, F32), (D_CONV, F32), (D_POOL, F32),
                (IDX_DIM, adt), (D_ATTN, adt), (IDX_HEADS * IDX_DIM, adt), (LANES, F32), (D_CONV, adt), (D_POOL, adt)]
    out_specs = [tspec(c) for c, _ in out_cols]
    out_shape = [tok(c, dt) for c, dt in out_cols]
    if attn:
        assert G == 1 and TM % LANES == 0
        nb = TM // LANES
        out_specs += [pl.BlockSpec((None, nb, IDX_DIM, IDX_HEADS * LANES), lambda s, j: (s, j, 0, 0)),
                      pl.BlockSpec((None, nb, 1, IDX_HEADS * LANES), lambda s, j: (s, j, 0, 0)),
                      pl.BlockSpec((None, 1, N_KV_HEADS, HEAD_DIM, TM), lambda s, j: (s, j, 0, 0, 0)),
                      pl.BlockSpec((None, 1, N_KV_HEADS, TM, LANES), lambda s, j: (s, j, 0, 0, 0))]
        out_shape += [jax.ShapeDtypeStruct((NS, T // LANES, IDX_DIM, IDX_HEADS * LANES), BF16),
                      jax.ShapeDtypeStruct((NS, T // LANES, 1, IDX_HEADS * LANES), F32),
                      jax.ShapeDtypeStruct((NS, n_tiles, N_KV_HEADS, HEAD_DIM, TM), BF16),
                      jax.ShapeDtypeStruct((NS, n_tiles, N_KV_HEADS, TM, LANES), BF16)]
    kern = functools.partial(_proj_kernel, G=G, TM=TM, pos0=pos0, n_tiles=n_tiles, attn=attn)
    return pl.pallas_call(
        kern,
        grid=(NS // G, n_tiles),
        in_specs=[tspec(D_MODEL), _layer_spec((1, D_MODEL), l), _layer_spec((D_MODEL, D_A), l),
                  pl.BlockSpec((TM, LANES), lambda s, j: (j, 0)), pl.BlockSpec((TM, LANES), lambda s, j: (j, 0)),
                  pl.BlockSpec((G, CONV_PAD, D_CONV), lambda s, j: (s, 0, 0)),
                  pl.BlockSpec((G, POOL_PAD, D_POOL), lambda s, j: (s, 0, 0)),
                  _layer_spec((CONV_PAD, D_CONV), l), _layer_spec((1, D_CONV), l), _layer_spec((1, D_CONV), l),
                  _layer_spec((1, D_CONV), l), _layer_spec((D_POOL, D_POOL), l), _layer_spec((1, D_POOL), l)],
        out_specs=out_specs,
        out_shape=out_shape,
        scratch_shapes=[pltpu.VMEM((G, CONV_PAD + TM, D_CONV), F32)] +
                       [pltpu.VMEM((G, POOL_PAD + TM, D_POOL), F32)] * 5,
        compiler_params=_cparams(("arbitrary", "arbitrary")),
        name="proj",
    )(x3, sw["g_mix"], sw["w_a"], cos, sin, chist, phist, sw["conv_w"], sw["conv_b"], sw["ln_g"], sw["ln_b"],
      sw["pool_w"], sw["pool_scale"])


def _kth_bisect(count, smin, smax, nvalid, kk, steps, warm=0, zero_counts=None):
    zero = jnp.zeros_like(smin)
    c_ge0, c_gt0 = zero_counts if zero_counts is not None else (count(zero, False), count(zero, True))
    pos = c_gt0 >= kk
    neg = c_ge0 < kk
    lo0 = jnp.where(neg, smin, zero)
    clo0 = jnp.where(neg, nvalid, c_ge0)
    hi0 = jnp.where(pos, BIG, zero)
    chi0 = jnp.where(pos, zero, jnp.where(neg, c_ge0, c_gt0))
    cand0 = jnp.where(pos, smax, smin * 0.5)
    small = nvalid <= kk
    done0 = (small | jnp.logical_not(pos | neg)).astype(F32)
    init = (jnp.where(small, smin, lo0), hi0, jnp.where(small, nvalid, clo0), chi0, cand0, done0)

    def count_ge(cand):
        return count(cand, False)

    def cond(st):
        return jnp.min(st[5]) < 0.5

    def body(st):
        for _ in range(steps):
            st = step(st)
        return st

    def step(st):
        lo, hi, clo, chi, cand, done = st
        c = count_ge(cand)
        act = done < 0.5
        ge = c >= kk
        up = act & ge
        dn = act & jnp.logical_not(ge)
        lo = jnp.where(up, cand, lo)
        clo = jnp.where(up, c, clo)
        hi = jnp.where(dn, cand, hi)
        chi = jnp.where(dn, c, chi)
        mid = lo * 0.5 + hi * 0.5
        inside = (mid > lo) & (mid < hi)
        fin = (clo == kk) | jnp.logical_not(inside)
        done = jnp.where(fin, 1.0, done)
        return lo, hi, clo, chi, mid, done

    if warm:
        init = lax.fori_loop(0, warm, lambda _, st: step(st), init)
    lo, _, clo, chi, _, _ = lax.while_loop(cond, body, init)
    return lo, clo, chi


def _attend_prompt_kernel(qit_ref, wq_ref, q_ref, ki_ref, kt_ref, va_ref, o_ref, sc_ref, acc_ref, m_ref, *,
                          CKS, CK, kk):
    i = pl.program_id(1)
    QB = LANES
    nchs = (i * QB + QB + CKS - 1) // CKS
    nch = nchs * (CKS // CK)
    qit = qit_ref[...]
    wq = wq_ref[...]
    qcol = i * QB + lax.broadcasted_iota(jnp.int32, (CKS, QB), 1)
    krow0 = lax.broadcasted_iota(jnp.int32, (CKS, QB), 0)

    def score_chunk(c, carry, causal):
        smax, smin, n_ge0, n_gt0 = carry
        s = jnp.dot(ki_ref[c], qit, preferred_element_type=F32)
        s = jnp.maximum(s, 0.0) * wq
        tot = s[:, 0:QB]
        for hh in range(1, IDX_HEADS):
            tot = tot + s[:, hh * QB:(hh + 1) * QB]
        if causal:
            valid = (krow0 + c * CKS) <= qcol
            masked = jnp.where(valid, tot, NEG_INF)
            lowest = jnp.where(valid, tot, BIG)
        else:
            masked = lowest = tot
        sc_ref[pl.ds(pl.multiple_of(c * CKS, CKS), CKS), :] = masked
        smax = jnp.maximum(smax, jnp.max(masked.reshape(CKS // 8, 8, QB), axis=0))
        smin = jnp.minimum(smin, jnp.min(lowest.reshape(CKS // 8, 8, QB), axis=0))
        n_ge0 = n_ge0 + jnp.sum(jnp.where(masked >= 0.0, 1.0, 0.0).reshape(CKS // 8, 8, QB), axis=0)
        n_gt0 = n_gt0 + jnp.sum(jnp.where(masked > 0.0, 1.0, 0.0).reshape(CKS // 8, 8, QB), axis=0)
        return smax, smin, n_ge0, n_gt0

    def score_run(first, count, carry):
        for u in range(count):
            carry = score_chunk(first + u, carry, causal=False)
        return carry

    n_full = nchs - 1
    full4 = (n_full // 4) * 4
    carry = lax.fori_loop(0, n_full // 4, lambda j, t: score_run(4 * j, 4, t),
                          (jnp.full((8, QB), NEG_INF, F32), jnp.full((8, QB), BIG, F32),
                           jnp.zeros((8, QB), F32), jnp.zeros((8, QB), F32)))
    carry = lax.cond(n_full % 4 >= 2, lambda t: score_run(full4, 2, t), lambda t: t, carry)
    carry = lax.cond(n_full % 2 == 1, lambda t: score_run(n_full - 1, 1, t), lambda t: t, carry)
    smax, smin, n_ge0, n_gt0 = score_chunk(nchs - 1, carry, causal=True)
    smax = jnp.max(smax, axis=0, keepdims=True)
    smin = jnp.min(smin, axis=0, keepdims=True)
    zero_counts = (jnp.sum(n_ge0, axis=0, keepdims=True), jnp.sum(n_gt0, axis=0, keepdims=True))

    n_acc = 4
    rows = CKS // n_acc

    def count(cand, strict):
        def body(c, accs):
            base = pl.multiple_of(c * CKS, CKS)
            out = []
            for a in range(n_acc):
                blk = sc_ref[pl.ds(base + a * rows, rows), :]
                hit = (blk > cand) if strict else (blk >= cand)
                ones = jnp.where(hit, 1.0, 0.0).reshape(rows // 8, 8, QB)
                out.append(accs[a] + jnp.sum(ones, axis=0))
            return tuple(out)
        def quad(j, accs):
            for u in range(4):
                accs = body(4 * j + u, accs)
            return accs
        accs = lax.fori_loop(0, nchs // 4, quad, tuple(jnp.zeros((8, QB), F32) for _ in range(n_acc)))
        done4 = (nchs // 4) * 4
        accs = lax.cond(nchs % 4 >= 2, lambda t: body(done4 + 1, body(done4, t)), lambda t: t, accs)
        accs = lax.cond(nchs % 2 == 1, lambda t: body(nchs - 1, t), lambda t: t, accs)
        return jnp.sum((accs[0] + accs[1]) + (accs[2] + accs[3]), axis=0, keepdims=True)

    nvalid = (i * QB + lax.broadcasted_iota(jnp.int32, (1, QB), 1) + 1).astype(F32)
    lo, clo, chi = _kth_bisect(count, smin, smax, nvalid, float(kk), steps=4, warm=12, zero_counts=zero_counts)

    fix = (clo > kk) & (nvalid > kk)

    @pl.when(jnp.max(fix.astype(F32)) > 0.5)
    def _():
        keep = kk - chi
        TF = min(256, CKS)
        tri = (lax.broadcasted_iota(jnp.int32, (TF, TF), 1)
               <= lax.broadcasted_iota(jnp.int32, (TF, TF), 0)).astype(BF16)

        def body(c, run):
            rows_c = pl.ds(pl.multiple_of(c * TF, TF), TF)
            blk = sc_ref[rows_c, :]
            tie = jnp.where(fix & (blk == lo), 1.0, 0.0)
            rank = run + jnp.dot(tri, tie.astype(BF16), preferred_element_type=F32)
            sc_ref[rows_c, :] = jnp.where((tie > 0.5) & (rank > keep), NEG_INF, blk)
            return run + jnp.sum(tie, axis=0, keepdims=True)
        lax.fori_loop(0, nchs * (CKS // TF), body, jnp.zeros((1, QB), F32))

    GQ = GROUP * QB
    q32 = q_ref[...].astype(F32)
    qn = [jnp.concatenate([q32[:, hh * HEAD_DIM:(hh + 1) * HEAD_DIM] for hh in range(n * GROUP, (n + 1) * GROUP)],
                          axis=0).astype(BF16) for n in range(N_KV_HEADS)]
    acc_ref[...] = jnp.zeros_like(acc_ref)
    m_ref[...] = jnp.full_like(m_ref, -BIG)
    reps = CK // LANES

    def att_unit(first, count):
        chunks = [first + u for u in range(count)]
        bias_t = [jnp.where(sc_ref[pl.ds(pl.multiple_of(c * CK, CK), CK), :] >= lo, 0.0, NEG_INF).T
                  for c in chunks]
        for n in range(N_KV_HEADS):
            ss = []
            for c, bt in zip(chunks, bias_t):
                s = jnp.dot(qn[n], kt_ref[c, n], preferred_element_type=F32)
                ss.append((s.reshape(GROUP, QB, CK) + bt[None]).reshape(GQ, CK))
            m_old = m_ref[n]
            mnew = m_old
            for s in ss:
                mnew = jnp.maximum(mnew, jnp.max(s, axis=1, keepdims=True))
            mrep = jnp.concatenate([mnew] * reps, axis=1)
            pv = None
            for c, s in zip(chunks, ss):
                part = jnp.dot(jnp.exp2(s - mrep).astype(BF16), va_ref[c, n], preferred_element_type=F32)
                pv = part if pv is None else pv + part
            acc_ref[n] = jnp.exp2(m_old - mnew) * acc_ref[n] + pv
            m_ref[n] = mnew

    def att_quad(j, _):
        att_unit(4 * j, 2)
        att_unit(4 * j + 2, 2)
        return 0

    lax.fori_loop(0, nch // 4, att_quad, 0)
    done4 = (nch // 4) * 4

    @pl.when(nch % 4 >= 2)
    def _():
        att_unit(done4, 2)

    @pl.when(nch % 2 == 1)
    def _():
        att_unit(nch - 1, 1)

    outs = []
    for n in range(N_KV_HEADS):
        a = acc_ref[n]
        o = a[:, :HEAD_DIM] / a[:, HEAD_DIM:HEAD_DIM + 1]
        outs += [o[g * QB:(g + 1) * QB, :] for g in range(GROUP)]
    o_ref[...] = jnp.concatenate(outs, axis=1).astype(BF16)


def _attend_prompt(q, qit, wq, kibf, kt, va, *, CKS):
    B, T, _ = q.shape
    QB = LANES
    nq = T // QB
    nc, CK = kt.shape[1], kt.shape[4]
    ncs = T // CKS
    kk = min(TOPK_MAX, T // 4)
    kib = kibf.reshape(B, ncs, CKS, IDX_DIM)

    kern = functools.partial(_attend_prompt_kernel, CKS=CKS, CK=CK, kk=kk)
    return pl.pallas_call(
        kern,
        grid=(B, nq),
        in_specs=[pl.BlockSpec((None, None, IDX_DIM, 8 * QB), lambda b, i: (b, i, 0, 0)),
                  pl.BlockSpec((None, None, 1, 8 * QB), lambda b, i: (b, i, 0, 0)),
                  pl.BlockSpec((None, QB, D_ATTN), lambda b, i: (b, i, 0)),
                  pl.BlockSpec((None, ncs, CKS, IDX_DIM), lambda b, i: (b, 0, 0, 0)),
                  pl.BlockSpec((None, nc, N_KV_HEADS, HEAD_DIM, CK), lambda b, i: (b, 0, 0, 0, 0)),
                  pl.BlockSpec((None, nc, N_KV_HEADS, CK, LANES), lambda b, i: (b, 0, 0, 0, 0))],
        out_specs=pl.BlockSpec((None, QB, D_ATTN), lambda b, i: (b, i, 0)),
        out_shape=jax.ShapeDtypeStruct((B, T, D_ATTN), BF16),
        scratch_shapes=[pltpu.VMEM((T, QB), F32), pltpu.VMEM((N_KV_HEADS, GROUP * QB, LANES), F32),
                        pltpu.VMEM((N_KV_HEADS, GROUP * QB, LANES), F32)],
        compiler_params=_cparams(("arbitrary", "arbitrary")),
        name="attend_prompt",
    )(qit, wq, q, kib, kt, va)


def _sample_score_kernel(pt_ref, *refs, P, n_steps, past_len, kk):
    pages = refs[:P]
    qi_ref, wb_ref, kin_ref, sc_ref, thr_ref = refs[P:P + 5]
    j = pl.program_id(1)
    TQ = sc_ref.shape[1]
    NBLK = sc_ref.shape[0]
    qi = qi_ref[...]
    wb = wb_ref[...]

    def block_scores(kpage_t):
        s = jnp.dot(qi, kpage_t.astype(BF16), preferred_element_type=F32)
        s = jnp.maximum(s, 0.0) * wb
        return jnp.sum(s.reshape(IDX_HEADS, TQ, LANES), axis=0)

    for r in range(P):
        sc_ref[j * P + r] = block_scores(pages[r][...])

    @pl.when(j == n_steps - 1)
    def _():
        tot = block_scores(kin_ref[...])
        qrow = lax.broadcasted_iota(jnp.int32, (TQ, LANES), 0)
        kcol = lax.broadcasted_iota(jnp.int32, (TQ, LANES), 1)
        sc_ref[NBLK - 1] = jnp.where(kcol <= qrow, tot, NEG_INF)

        def lane_bcast(a):
            return jnp.broadcast_to(a, (TQ, LANES))

        sc = sc_ref[...]
        smax = lane_bcast(jnp.max(jnp.max(sc, axis=0), axis=1, keepdims=True))
        smin = lane_bcast(jnp.min(jnp.min(jnp.where(sc > NEG_INF, sc, BIG), axis=0), axis=1, keepdims=True))
        nvalid = (past_len + qrow + 1).astype(F32)

        def count(cand, strict):
            blk = sc_ref[...]
            hit = (blk > cand[None]) if strict else (blk >= cand[None])
            c = jnp.sum(jnp.where(hit, 1.0, 0.0), axis=0)
            return lane_bcast(jnp.sum(c, axis=1, keepdims=True))

        lo, clo, chi = _kth_bisect(count, smin, smax, nvalid, float(kk), steps=4)
        thr_ref[...] = lo

        fix = (clo > kk) & (nvalid > kk)

        @pl.when(jnp.max(fix.astype(F32)) > 0.5)
        def _():
            keep = kk - chi
            tri = (lax.broadcasted_iota(jnp.int32, (LANES, LANES), 0)
                   <= lax.broadcasted_iota(jnp.int32, (LANES, LANES), 1)).astype(BF16)

            def body(c, run):
                blk = sc_ref[c]
                tie = fix & (blk == lo)
                rank = run + jnp.dot(tie.astype(BF16), tri, preferred_element_type=F32)
                sc_ref[c] = jnp.where(tie & (rank > keep), NEG_INF, blk)
                return run + lane_bcast(jnp.sum(tie.astype(F32), axis=1, keepdims=True))
            lax.fori_loop(0, NBLK, body, jnp.zeros((TQ, LANES), F32))


def _sample_scores(page_table, cache_kidx_t, l, qi_rows, w_rows, ki_new_t, *, P):
    NB, n_pages = page_table.shape
    TQ = qi_rows.shape[1] // IDX_HEADS
    n_steps = n_pages // P
    past_len = n_pages * PAGE_SIZE
    kk = min(TOPK_MAX, (past_len + TQ) // 4)
    NBLK = n_pages + 1

    def page_spec(r):
        return pl.BlockSpec((None, None, IDX_DIM, PAGE_SIZE), lambda b, j, pt: (l, pt[b, j * P + r], 0, 0))

    kern = functools.partial(_sample_score_kernel, P=P, n_steps=n_steps, past_len=past_len, kk=kk)
    grid_spec = pltpu.PrefetchScalarGridSpec(
        num_scalar_prefetch=1,
        grid=(NB, n_steps),
        in_specs=[page_spec(r) for r in range(P)] + [
            pl.BlockSpec((None, IDX_HEADS * TQ, IDX_DIM), lambda b, j, pt: (b, 0, 0)),
            pl.BlockSpec((None, IDX_HEADS * TQ, LANES), lambda b, j, pt: (b, 0, 0)),
            pl.BlockSpec((None, IDX_DIM, PAGE_SIZE), lambda b, j, pt: (b, 0, 0))],
        out_specs=[pl.BlockSpec((None, NBLK, TQ, LANES), lambda b, j, pt: (b, 0, 0, 0)),
                   pl.BlockSpec((None, TQ, LANES), lambda b, j, pt: (b, 0, 0))],
    )
    return pl.pallas_call(
        kern,
        grid_spec=grid_spec,
        out_shape=[jax.ShapeDtypeStruct((NB, NBLK, TQ, LANES), F32), jax.ShapeDtypeStruct((NB, TQ, LANES), F32)],
        compiler_params=_cparams(("arbitrary", "arbitrary")),
        name="sample_scores",
    )(page_table, *([cache_kidx_t] * P), qi_rows, w_rows, ki_new_t)


def _sample_attend_kernel(pt_ref, *refs, P, n_steps):
    kpages = refs[:P]
    vpages = refs[P:2 * P]
    sc_ref, scn_ref, thr_ref, q_ref, kn_ref, vn_ref, o_ref, m_ref, l_ref, acc_ref = refs[2 * P:2 * P + 10]
    j = pl.program_id(1)
    qz = q_ref[...]
    thr = thr_ref[...]

    @pl.when(j == 0)
    def _():
        m_ref[...] = jnp.full_like(m_ref, -BIG)
        l_ref[...] = jnp.zeros_like(l_ref)
        acc_ref[...] = jnp.zeros_like(acc_ref)

    def update(kblocks, vblocks, scblocks):
        s_parts = []
        for kb, scb in zip(kblocks, scblocks):
            s = jnp.dot(qz, kb.astype(BF16), preferred_element_type=F32)
            bias = jnp.where(scb >= thr, 0.0, NEG_INF)
            s_parts.append(s + jnp.concatenate([bias] * N_HEADS, axis=0))
        s = jnp.concatenate(s_parts, axis=1) if len(s_parts) > 1 else s_parts[0]
        m_old = m_ref[...]
        mnew = jnp.maximum(m_old, jnp.broadcast_to(jnp.max(s, axis=1, keepdims=True), m_old.shape))
        alpha = jnp.exp2(m_old - mnew)
        p = jnp.exp2(s - jnp.concatenate([mnew] * len(s_parts), axis=1) if len(s_parts) > 1 else s - mnew)
        l_ref[...] = alpha * l_ref[...] + jnp.broadcast_to(jnp.sum(p, axis=1, keepdims=True), m_old.shape)
        m_ref[...] = mnew
        pv = jnp.zeros(acc_ref.shape, F32)
        for r, vb in enumerate(vblocks):
            pv = pv + lax.dot_general(p[:, r * LANES:(r + 1) * LANES].astype(BF16), vb.astype(BF16),
                                      (((1,), (1,)), ((), ())), preferred_element_type=F32)
        acc_ref[...] = alpha * acc_ref[...] + pv

    update([kp[...] for kp in kpages], [vp[...] for vp in vpages], [sc_ref[r] for r in range(P)])

    @pl.when(j == n_steps - 1)
    def _():
        update([kn_ref[...]], [vn_ref[...]], [scn_ref[0]])
        o_ref[...] = acc_ref[...] / l_ref[...]


def _sample_attend(page_table, cache_k_t, cache_v_t, l, sc, thr, q_rows, k_new_t, v_new_t, *, P):
    NB, n_pages = page_table.shape
    R = q_rows.shape[1]
    TQ = R // N_HEADS
    n_steps = n_pages // P

    def page_spec(r):
        return pl.BlockSpec((None, None, KV_DIM, PAGE_SIZE), lambda b, j, pt: (l, pt[b, j * P + r], 0, 0))

    kern = functools.partial(_sample_attend_kernel, P=P, n_steps=n_steps)
    grid_spec = pltpu.PrefetchScalarGridSpec(
        num_scalar_prefetch=1,
        grid=(NB, n_steps),
        in_specs=[page_spec(r) for r in range(P)] + [page_spec(r) for r in range(P)] + [
            pl.BlockSpec((None, P, TQ, LANES), lambda b, j, pt: (b, j, 0, 0)),
            pl.BlockSpec((None, 1, TQ, LANES), lambda b, j, pt: (b, n_pages, 0, 0)),
            pl.BlockSpec((None, TQ, LANES), lambda b, j, pt: (b, 0, 0)),
            pl.BlockSpec((None, R, KV_DIM), lambda b, j, pt: (b, 0, 0)),
            pl.BlockSpec((None, KV_DIM, PAGE_SIZE), lambda b, j, pt: (b, 0, 0)),
            pl.BlockSpec((None, KV_DIM, PAGE_SIZE), lambda b, j, pt: (b, 0, 0))],
        out_specs=pl.BlockSpec((None, R, KV_DIM), lambda b, j, pt: (b, 0, 0)),
        scratch_shapes=[pltpu.VMEM((R, LANES), F32), pltpu.VMEM((R, LANES), F32), pltpu.VMEM((R, KV_DIM), F32)],
    )
    return pl.pallas_call(
        kern,
        grid_spec=grid_spec,
        out_shape=jax.ShapeDtypeStruct((NB, R, KV_DIM), F32),
        compiler_params=_cparams(("arbitrary", "arbitrary")),
        name="sample_attend",
    )(page_table, *([cache_k_t] * P), *([cache_v_t] * P), sc, sc, thr, q_rows, k_new_t, v_new_t)


def _mix_out_kernel(x_ref, g_ref, wg_ref, ca_ref, ao_ref, py_ref, wco_ref, wao_ref, wpo_ref, wo_ref, o_ref):
    x = x_ref[...]
    ms = jnp.mean(x * x, axis=-1, keepdims=True)
    h = (x * lax.rsqrt(ms + RMS_EPS) * g_ref[...]).astype(BF16)
    gates = jax.nn.sigmoid(jnp.dot(h, wg_ref[...], preferred_element_type=F32))
    y_a = jnp.dot(ca_ref[...], wco_ref[...], preferred_element_type=F32)
    y_b = jnp.dot(ao_ref[...], wao_ref[...], preferred_element_type=F32)
    y_c = jnp.dot(py_ref[...], wpo_ref[...], preferred_element_type=F32)
    merged = (gates[:, 0:D_MODEL] * y_a + gates[:, D_MODEL:2 * D_MODEL] * y_b
              + gates[:, 2 * D_MODEL:3 * D_MODEL] * y_c)
    o_ref[...] = x + jnp.dot(merged.astype(BF16), wo_ref[...], preferred_element_type=F32)


def _mix_out(x2, ca, ao, py, sw, l, *, TM):
    N = x2.shape[0]

    def tspec(c):
        return pl.BlockSpec((TM, c), lambda i: (i, 0))

    return pl.pallas_call(
        _mix_out_kernel,
        grid=(N // TM,),
        in_specs=[tspec(D_MODEL), _layer_spec((1, D_MODEL), l), _layer_spec((D_MODEL, N_BRANCH * D_MODEL), l),
                  tspec(D_CONV), tspec(D_ATTN), tspec(D_POOL),
                  _layer_spec((D_CONV, D_MODEL), l), _layer_spec((D_ATTN, D_MODEL), l),
                  _layer_spec((D_POOL, D_MODEL), l), _layer_spec((D_MODEL, D_MODEL), l)],
        out_specs=tspec(D_MODEL),
        out_shape=jax.ShapeDtypeStruct((N, D_MODEL), F32),
        compiler_params=_cparams(("arbitrary",)),
        name="mix_out",
    )(x2, sw["g_mix"], sw["w_g"], ca, ao, py, sw["w_conv_out"], sw["w_attn_out"], sw["w_pool_out"], sw["w_out"])


def _ffn_kernel(x_ref, g_ref, w1_ref, w2_ref, gf_ref, o_ref, *, final):
    x = x_ref[...]
    ms = jnp.mean(x * x, axis=-1, keepdims=True)
    h = (x * lax.rsqrt(ms + RMS_EPS) * g_ref[...]).astype(BF16)
    r = jnp.maximum(jnp.dot(h, w1_ref[...], preferred_element_type=F32), 0.0)
    y = x + jnp.dot((r * r).astype(BF16), w2_ref[...], preferred_element_type=F32)
    if final:
        ms2 = jnp.mean(y * y, axis=-1, keepdims=True)
        y = y * lax.rsqrt(ms2 + RMS_EPS) * gf_ref[...]
    o_ref[...] = y


def _ffn(x2, sw, l, g_final, *, TM, final):
    N = x2.shape[0]
    once = pl.Buffered(1)
    return pl.pallas_call(
        functools.partial(_ffn_kernel, final=final),
        grid=(N // TM,),
        in_specs=[pl.BlockSpec((TM, D_MODEL), lambda i: (i, 0)), _layer_spec((1, D_MODEL), l),
                  _layer_spec((D_MODEL, D_FF), l, pipeline_mode=once),
                  _layer_spec((D_FF, D_MODEL), l, pipeline_mode=once),
                  pl.BlockSpec((1, D_MODEL), lambda i: (0, 0))],
        out_specs=pl.BlockSpec((TM, D_MODEL), lambda i: (i, 0)),
        out_shape=jax.ShapeDtypeStruct((N, D_MODEL), F32),
        compiler_params=_cparams(("arbitrary",)),
        name="ffn",
    )(x2, sw["g_ffn"], sw["w_ff1"], sw["w_ff2"], g_final)


def _stacked_weights(g_mix, w_in, conv_w, conv_b, conv_ln_g, conv_ln_b, w_conv_out, w_attn_out,
                     pool_w, pool_scale, w_pool_out, w_out, g_ffn, w_ff1, w_ff2):
    depth = w_in.shape[0]
    o_ki = C_QI + IDX_HEADS * IDX_DIM
    o_wi = o_ki + IDX_DIM
    o_xc = o_wi + IDX_HEADS
    o_g = o_xc + D_POOL

    def pad_to_tile(a):
        return jnp.pad(a, ((0, 0), (0, 0), (0, LANES - a.shape[2])))

    w_a = jnp.concatenate([w_in[:, :, :o_ki], w_in[:, :, o_xc:o_g], pad_to_tile(w_in[:, :, o_ki:o_wi]),
                           pad_to_tile(w_in[:, :, o_wi:o_xc])], axis=2).astype(BF16)
    pw = jnp.zeros((depth, D_POOL, D_POOL), F32)
    for g in range(len(POOL_WINDOWS)):
        pw = pw.at[:, g * POOL_GC:(g + 1) * POOL_GC, g * POOL_GC:(g + 1) * POOL_GC].set(pool_w[:, g])

    def row(a):
        return a[:, None, :]

    return dict(
        g_mix=row(g_mix), w_a=w_a, w_g=w_in[:, :, o_g:].astype(BF16),
        conv_w=jnp.pad(conv_w, ((0, 0), (0, CONV_PAD - CONV_W), (0, 0))), conv_b=row(conv_b),
        ln_g=row(conv_ln_g), ln_b=row(conv_ln_b),
        pool_w=pw.astype(BF16), pool_scale=row(pool_scale),
        w_conv_out=w_conv_out.astype(BF16), w_attn_out=w_attn_out.astype(BF16),
        w_pool_out=w_pool_out.astype(BF16), w_out=w_out.astype(BF16),
        g_ffn=row(g_ffn), w_ff1=w_ff1.astype(BF16), w_ff2=w_ff2.astype(BF16))


def _tail_state(hist, new, keep):
    return jnp.concatenate([hist, new], axis=1)[:, -keep:]


def kernel(x_prompt, x_sample, cache_k, cache_v, cache_kidx, state_conv, state_pool, page_table, g_mix, w_in,
           conv_w, conv_b, conv_ln_g, conv_ln_b, w_conv_out, w_attn_out, pool_w, pool_scale, w_pool_out, w_out,
           g_ffn, w_ff1, w_ff2, g_final):
    B, T, _ = x_prompt.shape
    NB, TS, _ = x_sample.shape
    depth = w_in.shape[0]
    n_pool = cache_k.shape[1]
    past_len = page_table.shape[1] * PAGE_SIZE
    TM_P = min(512, T)
    gf = g_final[None]
    xp = x_prompt
    xs = x_sample
    zero_conv = jnp.zeros((B, CONV_PAD, D_CONV), F32)
    zero_pool = jnp.zeros((B, POOL_PAD, D_POOL), F32)
    cache_k_t = cache_k.transpose(0, 1, 3, 4, 2).reshape(depth, n_pool, KV_DIM, PAGE_SIZE)
    cache_v_t = cache_v.transpose(0, 1, 3, 4, 2).reshape(depth, n_pool, KV_DIM, PAGE_SIZE)
    cache_kidx_t = cache_kidx.transpose(0, 1, 3, 2)
    sw = _stacked_weights(g_mix, w_in, conv_w, conv_b, conv_ln_g, conv_ln_b, w_conv_out, w_attn_out,
                          pool_w, pool_scale, w_pool_out, w_out, g_ffn, w_ff1, w_ff2)
    tables_p = _rope_tables(0, T)
    tables_s = _rope_tables(past_len, TS)
    hist_conv_s = jnp.pad(state_conv, ((0, 0), (0, 0), (CONV_PAD - CONV_HIST, 0), (0, 0)))
    hist_pool_s = jnp.pad(state_pool, ((0, 0), (0, 0), (POOL_PAD - POOL_HIST, 0), (0, 0)))
    st_p = [[] for _ in range(5)]
    st_s = [[] for _ in range(5)]
    for l in range(depth):
        final = l == depth - 1

        (k, v, ki, u, xc, kibf, q, _, _, ca, py, qit, wq, kt, va) = _proj(
            xp, 0, tables_p, zero_conv, zero_pool, sw, l, G=1, TM=TM_P, adt=BF16, attn=True)
        ao = _attend_prompt(q, qit, wq, kibf, kt, va, CKS=TM_P)
        x2 = _mix_out(xp.reshape(B * T, D_MODEL), ca.reshape(B * T, D_CONV), ao.reshape(B * T, D_ATTN),
                      py.reshape(B * T, D_POOL), sw, l, TM=TM_P)
        xp = _ffn(x2, sw, l, gf, TM=TM_P, final=final).reshape(B, T, D_MODEL)
        st_p[0].append(k.reshape(B, T, N_KV_HEADS, HEAD_DIM))
        st_p[1].append(v.reshape(B, T, N_KV_HEADS, HEAD_DIM))
        st_p[2].append(ki)
        st_p[3].append(u[:, T - CONV_HIST:])
        st_p[4].append(xc[:, T - POOL_HIST:])

        (k, v, ki, u, xc, _, q, qi, wi, ca, py) = _proj(
            xs, past_len, tables_s, hist_conv_s[l], hist_pool_s[l], sw, l, G=NB, TM=TS, adt=F32, attn=False)
        q, qi, ca, py = q.astype(BF16), qi.astype(BF16), ca.astype(BF16), py.astype(BF16)
        qi_rows = qi.reshape(NB, TS, IDX_HEADS, IDX_DIM).transpose(0, 2, 1, 3).reshape(NB, IDX_HEADS * TS, IDX_DIM)
        w_rows = jnp.broadcast_to(wi[:, :, :IDX_HEADS].transpose(0, 2, 1).reshape(NB, IDX_HEADS * TS, 1),
                                  (NB, IDX_HEADS * TS, LANES))
        qh = q.reshape(NB, TS, N_HEADS, HEAD_DIM).transpose(0, 2, 1, 3)
        head_kv = (jnp.arange(N_HEADS) // GROUP)[None, :, None, None]
        q_rows = jnp.concatenate([jnp.where(head_kv == n, qh, jnp.zeros_like(qh)) for n in range(N_KV_HEADS)],
                                 axis=-1).reshape(NB, N_HEADS * TS, KV_DIM)
        pad_rows = ((0, 0), (0, PAGE_SIZE - TS), (0, 0))
        pages_per_step = min(32, page_table.shape[1])

        def new_page_t(a):
            return jnp.pad(a, pad_rows).transpose(0, 2, 1)

        sc, thr = _sample_scores(page_table, cache_kidx_t, l, qi_rows, w_rows, new_page_t(ki), P=pages_per_step)
        o_rows = _sample_attend(page_table, cache_k_t, cache_v_t, l, sc, thr, q_rows, new_page_t(k), new_page_t(v),
                                P=pages_per_step)
        o5 = o_rows.reshape(NB, N_KV_HEADS, GROUP, TS, N_KV_HEADS, HEAD_DIM)
        ao = jnp.stack([o5[:, n, :, :, n, :] for n in range(N_KV_HEADS)], axis=1)
        ao = ao.reshape(NB, N_HEADS, TS, HEAD_DIM).transpose(0, 2, 1, 3).reshape(NB * TS, D_ATTN).astype(BF16)
        x2 = _mix_out(xs.reshape(NB * TS, D_MODEL), ca.reshape(NB * TS, D_CONV), ao,
                      py.reshape(NB * TS, D_POOL), sw, l, TM=NB * TS)
        xs = _ffn(x2, sw, l, gf, TM=NB * TS, final=final).reshape(NB, TS, D_MODEL)
        st_s[0].append(k.reshape(NB, TS, N_KV_HEADS, HEAD_DIM))
        st_s[1].append(v.reshape(NB, TS, N_KV_HEADS, HEAD_DIM))
        st_s[2].append(ki)
        st_s[3].append(_tail_state(state_conv[l], u, CONV_HIST))
        st_s[4].append(_tail_state(state_pool[l], xc, POOL_HIST))

    sp = [jnp.stack(a, axis=0) for a in st_p]
    ss = [jnp.stack(a, axis=0) for a in st_s]
    return (xp, xs, sp[0], sp[1], sp[2], sp[3], sp[4], ss[0], ss[1], ss[2], ss[3], ss[4])
```

```python
import functools

import jax
import jax.numpy as jnp
from jax import lax
from jax.experimental import pallas as pl
from jax.experimental.pallas import tpu as pltpu

F32 = jnp.float32
BF16 = jnp.bfloat16

D_MODEL = 1024
D_CONV = 256
CONV_W = 31
CONV_HIST = CONV_W - 1
HEAD_DIM = 64
D_ATTN = 512
N_HEADS = 8
N_KV_HEADS = 2
KV_DIM = N_KV_HEADS * HEAD_DIM
GROUP = N_HEADS // N_KV_HEADS
IDX_HEADS = 8
IDX_DIM = 64
TOPK_MAX = 256
D_POOL = 256
POOL_WINDOWS = (2, 4, 8, 16)
POOL_GC = D_POOL // len(POOL_WINDOWS)
POOL_HIST = max(POOL_WINDOWS) - 1
N_BRANCH = 3
D_FF = 4 * D_MODEL
ROPE_THETA = 10000.0
RMS_EPS = 1e-6
LN_EPS = 1e-5
PAGE_SIZE = 128

LANES = 128
CONV_PAD = 32
POOL_PAD = 16
Q_SCALE = HEAD_DIM ** -0.5 * 1.4426950408889634
BIG = 3.0e38
NEG_INF = float("-inf")

C_AIN, C_AGATE, C_Q, C_K, C_V, C_QI, C_XC, C_KI, C_WI, D_A = 0, 256, 512, 1024, 1152, 1280, 1792, 2048, 2176, 2304

VMEM_LIMIT = 56 * 1024 * 1024


def _cparams(sem):
    return pltpu.CompilerParams(dimension_semantics=sem, vmem_limit_bytes=VMEM_LIMIT)


def _layer_spec(shape, l, **kw):
    n = len(shape)
    return pl.BlockSpec((None,) + tuple(shape), lambda *_: (l,) + (0,) * n, **kw)


def _proj_kernel(x_ref, g_ref, w_ref, cos_ref, sin_ref, chist_ref, phist_ref, convw_ref, convb_ref,
                 lng_ref, lnb_ref, poolw_ref, pscale_ref,
                 k_ref, v_ref, ki_ref, u_ref, xc_ref, kibf_ref, q_ref, qi_ref, wi_ref, ca_ref, py_ref,
                 *rest, G, TM, pos0, n_tiles, attn):
    if attn:
        qit_ref, wq_ref, kt_ref, va_ref = rest[:4]
        rest = rest[4:]
    cs_ref, ps_ref, p2_ref, p4_ref, p8_ref, p16_ref = rest
    j = pl.program_id(1)
    R = G * TM
    x = x_ref[...].reshape(R, D_MODEL)
    ms = jnp.mean(x * x, axis=-1, keepdims=True)
    h = (x * lax.rsqrt(ms + RMS_EPS) * g_ref[...]).astype(BF16)
    z = jnp.dot(h, w_ref[...], preferred_element_type=F32)

    cos = cos_ref[...]
    sin = sin_ref[...]
    if G > 1:
        cos = jnp.concatenate([cos] * G, axis=0)
        sin = jnp.concatenate([sin] * G, axis=0)

    def rope(t):
        n = t.shape[1] // LANES
        c = jnp.concatenate([cos] * n, axis=1) if n > 1 else cos
        s = jnp.concatenate([sin] * n, axis=1) if n > 1 else sin
        lane = lax.broadcasted_iota(jnp.int32, t.shape, 1)
        first = (lane & (HEAD_DIM - 1)) < HEAD_DIM // 2
        sw = jnp.where(first, pltpu.roll(t, t.shape[1] - HEAD_DIM // 2, 1), pltpu.roll(t, HEAD_DIM // 2, 1))
        return t * c + sw * s

    q = rope(z[:, C_Q:C_Q + D_ATTN]) * Q_SCALE
    k = rope(z[:, C_K:C_K + KV_DIM])
    v = z[:, C_V:C_V + KV_DIM]
    qi = rope(z[:, C_QI:C_QI + IDX_HEADS * IDX_DIM])
    ki = rope(z[:, C_KI:C_KI + LANES])[:, :IDX_DIM]
    wi = z[:, C_WI:C_WI + LANES] * (IDX_HEADS ** -0.5 * IDX_DIM ** -0.5)

    k_ref[...] = k.reshape(G, TM, KV_DIM)
    v_ref[...] = v.reshape(G, TM, KV_DIM)
    ki_ref[...] = ki.reshape(G, TM, IDX_DIM)
    kibf_ref[...] = ki.reshape(G, TM, IDX_DIM).astype(kibf_ref.dtype)
    q_ref[...] = q.reshape(G, TM, D_ATTN).astype(q_ref.dtype)
    qi_ref[...] = qi.reshape(G, TM, IDX_HEADS * IDX_DIM).astype(qi_ref.dtype)
    wi_ref[...] = wi.reshape(G, TM, LANES)
    if attn:
        for b in range(TM // LANES):
            blk = slice(b * LANES, (b + 1) * LANES)
            qi_t = qi[blk, :].T
            qit_ref[b] = jnp.concatenate([qi_t[hh * IDX_DIM:(hh + 1) * IDX_DIM, :] for hh in range(IDX_HEADS)],
                                         axis=1).astype(BF16)
            wi_t = wi[blk, :].T
            wq_ref[b] = jnp.concatenate([wi_t[hh:hh + 1, :] for hh in range(IDX_HEADS)], axis=1)
        k_t = k.T
        lane = lax.broadcasted_iota(jnp.int32, (TM, KV_DIM), 1)
        tail = jnp.where(lane == HEAD_DIM, 1.0, 0.0)
        for n in range(N_KV_HEADS):
            kt_ref[0, n] = k_t[n * HEAD_DIM:(n + 1) * HEAD_DIM, :].astype(BF16)
            vn = v if n == 0 else pltpu.roll(v, KV_DIM - n * HEAD_DIM, 1)
            va_ref[0, n] = jnp.where(lane < HEAD_DIM, vn, tail).astype(BF16)

    a_in = z[:, C_AIN:C_AIN + D_CONV]
    a_gate = z[:, C_AGATE:C_AGATE + D_CONV]
    u = (a_in * jax.nn.sigmoid(a_gate)).reshape(G, TM, D_CONV)
    u_ref[...] = u

    @pl.when(j == 0)
    def _():
        cs_ref[:, 0:CONV_PAD, :] = chist_ref[...]
        ps_ref[:, 0:POOL_PAD, :] = phist_ref[...]

    cs_ref[:, CONV_PAD:CONV_PAD + TM, :] = u
    off = CONV_PAD - CONV_HIST
    rc = min(TM, 64)
    gc = min(G, 8)
    for g0 in range(0, G, gc):
        for r0 in range(0, TM, rc):
            acc = jnp.zeros((gc, rc, D_CONV), F32) + convb_ref[...][None]
            for t in range(CONV_W):
                acc = acc + convw_ref[t:t + 1, :][None] * cs_ref[g0:g0 + gc, off + t + r0:off + t + r0 + rc, :]
            mu = jnp.mean(acc, axis=-1, keepdims=True)
            var = jnp.mean(jnp.square(acc - mu), axis=-1, keepdims=True)
            y = (acc - mu) * lax.rsqrt(var + LN_EPS) * lng_ref[...][None] + lnb_ref[...][None]
            ca_ref[g0:g0 + gc, r0:r0 + rc, :] = (y * jax.nn.sigmoid(y)).astype(ca_ref.dtype)
    if n_tiles > 1:
        cs_ref[:, 0:CONV_PAD, :] = cs_ref[:, TM:TM + CONV_PAD, :]

    xc = z[:, C_XC:C_XC + D_POOL].reshape(G, TM, D_POOL)
    xc_ref[...] = xc
    ps_ref[:, POOL_PAD:POOL_PAD + TM, :] = xc
    RR = POOL_PAD + TM
    p2_ref[:, 1:RR, :] = ps_ref[:, 1:RR, :] + ps_ref[:, 0:RR - 1, :]
    p4_ref[:, 3:RR, :] = p2_ref[:, 3:RR, :] + p2_ref[:, 1:RR - 2, :]
    p8_ref[:, 7:RR, :] = p4_ref[:, 7:RR, :] + p4_ref[:, 3:RR - 4, :]
    p16_ref[:, 15:RR, :] = p8_ref[:, 15:RR, :] + p8_ref[:, 7:RR - 8, :]
    lane = lax.broadcasted_iota(jnp.int32, (G, TM, D_POOL), 2)
    row = lax.broadcasted_iota(jnp.int32, (G, TM, D_POOL), 1)
    g0m, g1m, g2m = lane < POOL_GC, lane < 2 * POOL_GC, lane < 3 * POOL_GC
    s_tok = slice(POOL_PAD, POOL_PAD + TM)
    wsum = jnp.where(g0m, p2_ref[:, s_tok, :],
                     jnp.where(g1m, p4_ref[:, s_tok, :], jnp.where(g2m, p8_ref[:, s_tok, :], p16_ref[:, s_tok, :])))
    win = jnp.where(g0m, POOL_WINDOWS[0], jnp.where(g1m, POOL_WINDOWS[1],
                                                     jnp.where(g2m, POOL_WINDOWS[2], POOL_WINDOWS[3])))
    cnt = jnp.minimum(pos0 + j * TM + row + 1, win).astype(F32)
    d = (wsum / cnt - xc).reshape(R, D_POOL).astype(BF16)
    py = jnp.dot(d, poolw_ref[...], preferred_element_type=F32) * pscale_ref[...]
    py_ref[...] = py.reshape(G, TM, D_POOL).astype(py_ref.dtype)
    if n_tiles > 1:
        ps_ref[:, 0:POOL_PAD, :] = ps_ref[:, TM:TM + POOL_PAD, :]


def _rope_tables(pos0, T):
    half = HEAD_DIM // 2
    pos = (pos0 + jnp.arange(T)).astype(F32)
    inv = ROPE_THETA ** (-jnp.arange(half, dtype=F32) / half)
    ang = pos[:, None] * inv[None, :]
    cos = jnp.tile(jnp.cos(ang), (1, 2 * LANES // HEAD_DIM))
    sin = jnp.tile(jnp.concatenate([-jnp.sin(ang), jnp.sin(ang)], axis=1), (1, LANES // HEAD_DIM))
    return cos, sin


def _proj(x3, pos0, tables, chist, phist, sw, l, *, G, TM, adt, attn):
    NS, T, _ = x3.shape
    n_tiles = T // TM
    cos, sin = tables

    def tok(c, dt=F32):
        return jax.ShapeDtypeStruct((NS, T, c), dt)

    def tspec(c):
        return pl.BlockSpec((G, TM, c), lambda s, j: (s, j, 0))

    out_cols = [(KV_DIM, F32), (KV_DIM, F32), (IDX_DIM, F32), (D_CONV, F32), (D_POOL, F32),
                (IDX_DIM, adt), (D_ATTN, adt), (IDX_HEADS * IDX_DIM, adt), (LANES, F32), (D_CONV, adt), (D_POOL, adt)]
    out_specs = [tspec(c) for c, _ in out_cols]
    out_shape = [tok(c, dt) for c, dt in out_cols]
    if attn:
        assert G == 1 and TM % LANES == 0
        nb = TM // LANES
        out_specs += [pl.BlockSpec((None, nb, IDX_DIM, IDX_HEADS * LANES), lambda s, j: (s, j, 0, 0)),
                      pl.BlockSpec((None, nb, 1, IDX_HEADS * LANES), lambda s, j: (s, j, 0, 0)),
                      pl.BlockSpec((None, 1, N_KV_HEADS, HEAD_DIM, TM), lambda s, j: (s, j, 0, 0, 0)),
                      pl.BlockSpec((None, 1, N_KV_HEADS, TM, LANES), lambda s, j: (s, j, 0, 0, 0))]
        out_shape += [jax.ShapeDtypeStruct((NS, T // LANES, IDX_DIM, IDX_HEADS * LANES), BF16),
                      jax.ShapeDtypeStruct((NS, T // LANES, 1, IDX_HEADS * LANES), F32),
                      jax.ShapeDtypeStruct((NS, n_tiles, N_KV_HEADS, HEAD_DIM, TM), BF16),
                      jax.ShapeDtypeStruct((NS, n_tiles, N_KV_HEADS, TM, LANES), BF16)]
    kern = functools.partial(_proj_kernel, G=G, TM=TM, pos0=pos0, n_tiles=n_tiles, attn=attn)
    return pl.pallas_call(
        kern,
        grid=(NS // G, n_tiles),
        in_specs=[tspec(D_MODEL), _layer_spec((1, D_MODEL), l), _layer_spec((D_MODEL, D_A), l),
                  pl.BlockSpec((TM, LANES), lambda s, j: (j, 0)), pl.BlockSpec((TM, LANES), lambda s, j: (j, 0)),
                  pl.BlockSpec((G, CONV_PAD, D_CONV), lambda s, j: (s, 0, 0)),
                  pl.BlockSpec((G, POOL_PAD, D_POOL), lambda s, j: (s, 0, 0)),
                  _layer_spec((CONV_PAD, D_CONV), l), _layer_spec((1, D_CONV), l), _layer_spec((1, D_CONV), l),
                  _layer_spec((1, D_CONV), l), _layer_spec((D_POOL, D_POOL), l), _layer_spec((1, D_POOL), l)],
        out_specs=out_specs,
        out_shape=out_shape,
        scratch_shapes=[pltpu.VMEM((G, CONV_PAD + TM, D_CONV), F32)] +
                       [pltpu.VMEM((G, POOL_PAD + TM, D_POOL), F32)] * 5,
        compiler_params=_cparams(("arbitrary", "arbitrary")),
        name="proj",
    )(x3, sw["g_mix"], sw["w_a"], cos, sin, chist, phist, sw["conv_w"], sw["conv_b"], sw["ln_g"], sw["ln_b"],
      sw["pool_w"], sw["pool_scale"])


def _kth_bisect(count, smin, smax, nvalid, kk, steps, warm=0, zero_counts=None):
    zero = jnp.zeros_like(smin)
    c_ge0, c_gt0 = zero_counts if zero_counts is not None else (count(zero, False), count(zero, True))
    pos = c_gt0 >= kk
    neg = c_ge0 < kk
    lo0 = jnp.where(neg, smin, zero)
    clo0 = jnp.where(neg, nvalid, c_ge0)
    hi0 = jnp.where(pos, BIG, zero)
    chi0 = jnp.where(pos, zero, jnp.where(neg, c_ge0, c_gt0))
    cand0 = jnp.where(pos, smax, smin * 0.5)
    small = nvalid <= kk
    done0 = (small | jnp.logical_not(pos | neg)).astype(F32)
    init = (jnp.where(small, smin, lo0), hi0, jnp.where(small, nvalid, clo0), chi0, cand0, done0)

    def count_ge(cand):
        return count(cand, False)

    def cond(st):
        return jnp.min(st[5]) < 0.5

    def body(st):
        for _ in range(steps):
            st = step(st)
        return st

    def step(st):
        lo, hi, clo, chi, cand, done = st
        c = count_ge(cand)
        act = done < 0.5
        ge = c >= kk
        up = act & ge
        dn = act & jnp.logical_not(ge)
        lo = jnp.where(up, cand, lo)
        clo = jnp.where(up, c, clo)
        hi = jnp.where(dn, cand, hi)
        chi = jnp.where(dn, c, chi)
        mid = lo * 0.5 + hi * 0.5
        inside = (mid > lo) & (mid < hi)
        fin = (clo == kk) | jnp.logical_not(inside)
        done = jnp.where(fin, 1.0, done)
        return lo, hi, clo, chi, mid, done

    if warm:
        init = lax.fori_loop(0, warm, lambda _, st: step(st), init)
    lo, _, clo, chi, _, _ = lax.while_loop(cond, body, init)
    return lo, clo, chi


def _attend_prompt_kernel(qit_ref, wq_ref, q_ref, ki_ref, kt_ref, va_ref, o_ref, sc_ref, acc_ref, m_ref, *,
                          CKS, CK, kk):
    i = pl.program_id(1)
    QB = LANES
    nchs = (i * QB + QB + CKS - 1) // CKS
    nch = nchs * (CKS // CK)
    qit = qit_ref[...]
    wq = wq_ref[...]
    qcol = i * QB + lax.broadcasted_iota(jnp.int32, (CKS, QB), 1)
    krow0 = lax.broadcasted_iota(jnp.int32, (CKS, QB), 0)

    def score_chunk(c, carry, causal):
        smax, smin, n_ge0, n_gt0 = carry
        s = jnp.dot(ki_ref[c], qit, preferred_element_type=F32)
        s = jnp.maximum(s, 0.0) * wq
        tot = s[:, 0:QB]
        for hh in range(1, IDX_HEADS):
            tot = tot + s[:, hh * QB:(hh + 1) * QB]
        if causal:
            valid = (krow0 + c * CKS) <= qcol
            masked = jnp.where(valid, tot, NEG_INF)
            lowest = jnp.where(valid, tot, BIG)
        else:
            masked = lowest = tot
        sc_ref[pl.ds(pl.multiple_of(c * CKS, CKS), CKS), :] = masked
        smax = jnp.maximum(smax, jnp.max(masked.reshape(CKS // 8, 8, QB), axis=0))
        smin = jnp.minimum(smin, jnp.min(lowest.reshape(CKS // 8, 8, QB), axis=0))
        n_ge0 = n_ge0 + jnp.sum(jnp.where(masked >= 0.0, 1.0, 0.0).reshape(CKS // 8, 8, QB), axis=0)
        n_gt0 = n_gt0 + jnp.sum(jnp.where(masked > 0.0, 1.0, 0.0).reshape(CKS // 8, 8, QB), axis=0)
        return smax, smin, n_ge0, n_gt0

    def score_run(first, count, carry):
        for u in range(count):
            carry = score_chunk(first + u, carry, causal=False)
        return carry

    n_full = nchs - 1
    full4 = (n_full // 4) * 4
    carry = lax.fori_loop(0, n_full // 4, lambda j, t: score_run(4 * j, 4, t),
                          (jnp.full((8, QB), NEG_INF, F32), jnp.full((8, QB), BIG, F32),
                           jnp.zeros((8, QB), F32), jnp.zeros((8, QB), F32)))
    carry = lax.cond(n_full % 4 >= 2, lambda t: score_run(full4, 2, t), lambda t: t, carry)
    carry = lax.cond(n_full % 2 == 1, lambda t: score_run(n_full - 1, 1, t), lambda t: t, carry)
    smax, smin, n_ge0, n_gt0 = score_chunk(nchs - 1, carry, causal=True)
    smax = jnp.max(smax, axis=0, keepdims=True)
    smin = jnp.min(smin, axis=0, keepdims=True)
    zero_counts = (jnp.sum(n_ge0, axis=0, keepdims=True), jnp.sum(n_gt0, axis=0, keepdims=True))

    n_acc = 4
    rows = CKS // n_acc

    def count(cand, strict):
        def body(c, accs):
            base = pl.multiple_of(c * CKS, CKS)
            out = []
            for a in range(n_acc):
                blk = sc_ref[pl.ds(base + a * rows, rows), :]
                hit = (blk > cand) if strict else (blk >= cand)
                ones = jnp.where(hit, 1.0, 0.0).reshape(rows // 8, 8, QB)
                out.append(accs[a] + jnp.sum(ones, axis=0))
            return tuple(out)
        def quad(j, accs):
            for u in range(4):
                accs = body(4 * j + u, accs)
            return accs
        accs = lax.fori_loop(0, nchs // 4, quad, tuple(jnp.zeros((8, QB), F32) for _ in range(n_acc)))
        done4 = (nchs // 4) * 4
        accs = lax.cond(nchs % 4 >= 2, lambda t: body(done4 + 1, body(done4, t)), lambda t: t, accs)
        accs = lax.cond(nchs % 2 == 1, lambda t: body(nchs - 1, t), lambda t: t, accs)
        return jnp.sum((accs[0] + accs[1]) + (accs[2] + accs[3]), axis=0, keepdims=True)

    nvalid = (i * QB + lax.broadcasted_iota(jnp.int32, (1, QB), 1) + 1).astype(F32)
    lo, clo, chi = _kth_bisect(count, smin, smax, nvalid, float(kk), steps=4, warm=12, zero_counts=zero_counts)

    fix = (clo > kk) & (nvalid > kk)

    @pl.when(jnp.max(fix.astype(F32)) > 0.5)
    def _():
        keep = kk - chi
        TF = min(256, CKS)
        tri = (lax.broadcasted_iota(jnp.int32, (TF, TF), 1)
               <= lax.broadcasted_iota(jnp.int32, (TF, TF), 0)).astype(BF16)

        def body(c, run):
            rows_c = pl.ds(pl.multiple_of(c * TF, TF), TF)
            blk = sc_ref[rows_c, :]
            tie = jnp.where(fix & (blk == lo), 1.0, 0.0)
            rank = run + jnp.dot(tri, tie.astype(BF16), preferred_element_type=F32)
            sc_ref[rows_c, :] = jnp.where((tie > 0.5) & (rank > keep), NEG_INF, blk)
            return run + jnp.sum(tie, axis=0, keepdims=True)
        lax.fori_loop(0, nchs * (CKS // TF), body, jnp.zeros((1, QB), F32))

    GQ = GROUP * QB
    q32 = q_ref[...].astype(F32)
    qn = [jnp.concatenate([q32[:, hh * HEAD_DIM:(hh + 1) * HEAD_DIM] for hh in range(n * GROUP, (n + 1) * GROUP)],
                          axis=0).astype(BF16) for n in range(N_KV_HEADS)]
    acc_ref[...] = jnp.zeros_like(acc_ref)
    m_ref[...] = jnp.full_like(m_ref, -BIG)
    reps = CK // LANES

    def att_unit(first, count):
        chunks = [first + u for u in range(count)]
        bias_t = [jnp.where(sc_ref[pl.ds(pl.multiple_of(c * CK, CK), CK), :] >= lo, 0.0, NEG_INF).T
                  for c in chunks]
        for n in range(N_KV_HEADS):
            ss = []
            for c, bt in zip(chunks, bias_t):
                s = jnp.dot(qn[n], kt_ref[c, n], preferred_element_type=F32)
                ss.append((s.reshape(GROUP, QB, CK) + bt[None]).reshape(GQ, CK))
            m_old = m_ref[n]
            mnew = m_old
            for s in ss:
                mnew = jnp.maximum(mnew, jnp.max(s, axis=1, keepdims=True))
            mrep = jnp.concatenate([mnew] * reps, axis=1)
            pv = None
            for c, s in zip(chunks, ss):
                part = jnp.dot(jnp.exp2(s - mrep).astype(BF16), va_ref[c, n], preferred_element_type=F32)
                pv = part if pv is None else pv + part
            acc_ref[n] = jnp.exp2(m_old - mnew) * acc_ref[n] + pv
            m_ref[n] = mnew

    def att_quad(j, _):
        att_unit(4 * j, 4)
        return 0

    lax.fori_loop(0, nch // 4, att_quad, 0)
    done4 = (nch // 4) * 4

    @pl.when(nch % 4 >= 2)
    def _():
        att_unit(done4, 2)

    @pl.when(nch % 2 == 1)
    def _():
        att_unit(nch - 1, 1)

    outs = []
    for n in range(N_KV_HEADS):
        a = acc_ref[n]
        o = a[:, :HEAD_DIM] / a[:, HEAD_DIM:HEAD_DIM + 1]
        outs += [o[g * QB:(g + 1) * QB, :] for g in range(GROUP)]
    o_ref[...] = jnp.concatenate(outs, axis=1).astype(BF16)


def _attend_prompt(q, qit, wq, kibf, kt, va, *, CKS):
    B, T, _ = q.shape
    QB = LANES
    nq = T // QB
    nc, CK = kt.shape[1], kt.shape[4]
    ncs = T // CKS
    kk = min(TOPK_MAX, T // 4)
    kib = kibf.reshape(B, ncs, CKS, IDX_DIM)

    kern = functools.partial(_attend_prompt_kernel, CKS=CKS, CK=CK, kk=kk)
    return pl.pallas_call(
        kern,
        grid=(B, nq),
        in_specs=[pl.BlockSpec((None, None, IDX_DIM, 8 * QB), lambda b, i: (b, i, 0, 0)),
                  pl.BlockSpec((None, None, 1, 8 * QB), lambda b, i: (b, i, 0, 0)),
                  pl.BlockSpec((None, QB, D_ATTN), lambda b, i: (b, i, 0)),
                  pl.BlockSpec((None, ncs, CKS, IDX_DIM), lambda b, i: (b, 0, 0, 0)),
                  pl.BlockSpec((None, nc, N_KV_HEADS, HEAD_DIM, CK), lambda b, i: (b, 0, 0, 0, 0)),
                  pl.BlockSpec((None, nc, N_KV_HEADS, CK, LANES), lambda b, i: (b, 0, 0, 0, 0))],
        out_specs=pl.BlockSpec((None, QB, D_ATTN), lambda b, i: (b, i, 0)),
        out_shape=jax.ShapeDtypeStruct((B, T, D_ATTN), BF16),
        scratch_shapes=[pltpu.VMEM((T, QB), F32), pltpu.VMEM((N_KV_HEADS, GROUP * QB, LANES), F32),
                        pltpu.VMEM((N_KV_HEADS, GROUP * QB, LANES), F32)],
        compiler_params=_cparams(("arbitrary", "arbitrary")),
        name="attend_prompt",
    )(qit, wq, q, kib, kt, va)


def _sample_score_kernel(pt_ref, *refs, P, n_steps, past_len, kk):
    pages = refs[:P]
    qi_ref, wb_ref, kin_ref, sc_ref, thr_ref = refs[P:P + 5]
    j = pl.program_id(1)
    TQ = sc_ref.shape[1]
    NBLK = sc_ref.shape[0]
    qi = qi_ref[...]
    wb = wb_ref[...]

    def block_scores(kpage_t):
        s = jnp.dot(qi, kpage_t.astype(BF16), preferred_element_type=F32)
        s = jnp.maximum(s, 0.0) * wb
        return jnp.sum(s.reshape(IDX_HEADS, TQ, LANES), axis=0)

    for r in range(P):
        sc_ref[j * P + r] = block_scores(pages[r][...])

    @pl.when(j == n_steps - 1)
    def _():
        tot = block_scores(kin_ref[...])
        qrow = lax.broadcasted_iota(jnp.int32, (TQ, LANES), 0)
        kcol = lax.broadcasted_iota(jnp.int32, (TQ, LANES), 1)
        sc_ref[NBLK - 1] = jnp.where(kcol <= qrow, tot, NEG_INF)

        def lane_bcast(a):
            return jnp.broadcast_to(a, (TQ, LANES))

        sc = sc_ref[...]
        smax = lane_bcast(jnp.max(jnp.max(sc, axis=0), axis=1, keepdims=True))
        smin = lane_bcast(jnp.min(jnp.min(jnp.where(sc > NEG_INF, sc, BIG), axis=0), axis=1, keepdims=True))
        nvalid = (past_len + qrow + 1).astype(F32)

        def count(cand, strict):
            blk = sc_ref[...]
            hit = (blk > cand[None]) if strict else (blk >= cand[None])
            c = jnp.sum(jnp.where(hit, 1.0, 0.0), axis=0)
            return lane_bcast(jnp.sum(c, axis=1, keepdims=True))

        lo, clo, chi = _kth_bisect(count, smin, smax, nvalid, float(kk), steps=4)
        thr_ref[...] = lo

        fix = (clo > kk) & (nvalid > kk)

        @pl.when(jnp.max(fix.astype(F32)) > 0.5)
        def _():
            keep = kk - chi
            tri = (lax.broadcasted_iota(jnp.int32, (LANES, LANES), 0)
                   <= lax.broadcasted_iota(jnp.int32, (LANES, LANES), 1)).astype(BF16)

            def body(c, run):
                blk = sc_ref[c]
                tie = fix & (blk == lo)
                rank = run + jnp.dot(tie.astype(BF16), tri, preferred_element_type=F32)
                sc_ref[c] = jnp.where(tie & (rank > keep), NEG_INF, blk)
                return run + lane_bcast(jnp.sum(tie.astype(F32), axis=1, keepdims=True))
            lax.fori_loop(0, NBLK, body, jnp.zeros((TQ, LANES), F32))


def _sample_scores(page_table, cache_kidx_t, l, qi_rows, w_rows, ki_new_t, *, P):
    NB, n_pages = page_table.shape
    TQ = qi_rows.shape[1] // IDX_HEADS
    n_steps = n_pages // P
    past_len = n_pages * PAGE_SIZE
    kk = min(TOPK_MAX, (past_len + TQ) // 4)
    NBLK = n_pages + 1

    def page_spec(r):
        return pl.BlockSpec((None, None, IDX_DIM, PAGE_SIZE), lambda b, j, pt: (l, pt[b, j * P + r], 0, 0))

    kern = functools.partial(_sample_score_kernel, P=P, n_steps=n_steps, past_len=past_len, kk=kk)
    grid_spec = pltpu.PrefetchScalarGridSpec(
        num_scalar_prefetch=1,
        grid=(NB, n_steps),
        in_specs=[page_spec(r) for r in range(P)] + [
            pl.BlockSpec((None, IDX_HEADS * TQ, IDX_DIM), lambda b, j, pt: (b, 0, 0)),
            pl.BlockSpec((None, IDX_HEADS * TQ, LANES), lambda b, j, pt: (b, 0, 0)),
            pl.BlockSpec((None, IDX_DIM, PAGE_SIZE), lambda b, j, pt: (b, 0, 0))],
        out_specs=[pl.BlockSpec((None, NBLK, TQ, LANES), lambda b, j, pt: (b, 0, 0, 0)),
                   pl.BlockSpec((None, TQ, LANES), lambda b, j, pt: (b, 0, 0))],
    )
    return pl.pallas_call(
        kern,
        grid_spec=grid_spec,
        out_shape=[jax.ShapeDtypeStruct((NB, NBLK, TQ, LANES), F32), jax.ShapeDtypeStruct((NB, TQ, LANES), F32)],
        compiler_params=_cparams(("arbitrary", "arbitrary")),
        name="sample_scores",
    )(page_table, *([cache_kidx_t] * P), qi_rows, w_rows, ki_new_t)


def _sample_attend_kernel(pt_ref, *refs, P, n_steps):
    kpages = refs[:P]
    vpages = refs[P:2 * P]
    sc_ref, scn_ref, thr_ref, q_ref, kn_ref, vn_ref, o_ref, m_ref, l_ref, acc_ref = refs[2 * P:2 * P + 10]
    j = pl.program_id(1)
    qz = q_ref[...]
    thr = thr_ref[...]

    @pl.when(j == 0)
    def _():
        m_ref[...] = jnp.full_like(m_ref, -BIG)
        l_ref[...] = jnp.zeros_like(l_ref)
        acc_ref[...] = jnp.zeros_like(acc_ref)

    def update(kblocks, vblocks, scblocks):
        s_parts = []
        for kb, scb in zip(kblocks, scblocks):
            s = jnp.dot(qz, kb.astype(BF16), preferred_element_type=F32)
            bias = jnp.where(scb >= thr, 0.0, NEG_INF)
            s_parts.append(s + jnp.concatenate([bias] * N_HEADS, axis=0))
        s = jnp.concatenate(s_parts, axis=1) if len(s_parts) > 1 else s_parts[0]
        m_old = m_ref[...]
        mnew = jnp.maximum(m_old, jnp.broadcast_to(jnp.max(s, axis=1, keepdims=True), m_old.shape))
        alpha = jnp.exp2(m_old - mnew)
        p = jnp.exp2(s - jnp.concatenate([mnew] * len(s_parts), axis=1) if len(s_parts) > 1 else s - mnew)
        l_ref[...] = alpha * l_ref[...] + jnp.broadcast_to(jnp.sum(p, axis=1, keepdims=True), m_old.shape)
        m_ref[...] = mnew
        pv = jnp.zeros(acc_ref.shape, F32)
        for r, vb in enumerate(vblocks):
            pv = pv + lax.dot_general(p[:, r * LANES:(r + 1) * LANES].astype(BF16), vb.astype(BF16),
                                      (((1,), (1,)), ((), ())), preferred_element_type=F32)
        acc_ref[...] = alpha * acc_ref[...] + pv

    update([kp[...] for kp in kpages], [vp[...] for vp in vpages], [sc_ref[r] for r in range(P)])

    @pl.when(j == n_steps - 1)
    def _():
        update([kn_ref[...]], [vn_ref[...]], [scn_ref[0]])
        o_ref[...] = acc_ref[...] / l_ref[...]


def _sample_attend(page_table, cache_k_t, cache_v_t, l, sc, thr, q_rows, k_new_t, v_new_t, *, P):
    NB, n_pages = page_table.shape
    R = q_rows.shape[1]
    TQ = R // N_HEADS
    n_steps = n_pages // P

    def page_spec(r):
        return pl.BlockSpec((None, None, KV_DIM, PAGE_SIZE), lambda b, j, pt: (l, pt[b, j * P + r], 0, 0))

    kern = functools.partial(_sample_attend_kernel, P=P, n_steps=n_steps)
    grid_spec = pltpu.PrefetchScalarGridSpec(
        num_scalar_prefetch=1,
        grid=(NB, n_steps),
        in_specs=[page_spec(r) for r in range(P)] + [page_spec(r) for r in range(P)] + [
            pl.BlockSpec((None, P, TQ, LANES), lambda b, j, pt: (b, j, 0, 0)),
            pl.BlockSpec((None, 1, TQ, LANES), lambda b, j, pt: (b, n_pages, 0, 0)),
            pl.BlockSpec((None, TQ, LANES), lambda b, j, pt: (b, 0, 0)),
            pl.BlockSpec((None, R, KV_DIM), lambda b, j, pt: (b, 0, 0)),
            pl.BlockSpec((None, KV_DIM, PAGE_SIZE), lambda b, j, pt: (b, 0, 0)),
            pl.BlockSpec((None, KV_DIM, PAGE_SIZE), lambda b, j, pt: (b, 0, 0))],
        out_specs=pl.BlockSpec((None, R, KV_DIM), lambda b, j, pt: (b, 0, 0)),
        scratch_shapes=[pltpu.VMEM((R, LANES), F32), pltpu.VMEM((R, LANES), F32), pltpu.VMEM((R, KV_DIM), F32)],
    )
    return pl.pallas_call(
        kern,
        grid_spec=grid_spec,
        out_shape=jax.ShapeDtypeStruct((NB, R, KV_DIM), F32),
        compiler_params=_cparams(("arbitrary", "arbitrary")),
        name="sample_attend",
    )(page_table, *([cache_k_t] * P), *([cache_v_t] * P), sc, sc, thr, q_rows, k_new_t, v_new_t)


def _mix_out_kernel(x_ref, g_ref, wg_ref, ca_ref, ao_ref, py_ref, wco_ref, wao_ref, wpo_ref, wo_ref, o_ref):
    x = x_ref[...]
    ms = jnp.mean(x * x, axis=-1, keepdims=True)
    h = (x * lax.rsqrt(ms + RMS_EPS) * g_ref[...]).astype(BF16)
    gates = jax.nn.sigmoid(jnp.dot(h, wg_ref[...], preferred_element_type=F32))
    y_a = jnp.dot(ca_ref[...], wco_ref[...], preferred_element_type=F32)
    y_b = jnp.dot(ao_ref[...], wao_ref[...], preferred_element_type=F32)
    y_c = jnp.dot(py_ref[...], wpo_ref[...], preferred_element_type=F32)
    merged = (gates[:, 0:D_MODEL] * y_a + gates[:, D_MODEL:2 * D_MODEL] * y_b
              + gates[:, 2 * D_MODEL:3 * D_MODEL] * y_c)
    o_ref[...] = x + jnp.dot(merged.astype(BF16), wo_ref[...], preferred_element_type=F32)


def _mix_out(x2, ca, ao, py, sw, l, *, TM):
    N = x2.shape[0]

    def tspec(c):
        return pl.BlockSpec((TM, c), lambda i: (i, 0))

    return pl.pallas_call(
        _mix_out_kernel,
        grid=(N // TM,),
        in_specs=[tspec(D_MODEL), _layer_spec((1, D_MODEL), l), _layer_spec((D_MODEL, N_BRANCH * D_MODEL), l),
                  tspec(D_CONV), tspec(D_ATTN), tspec(D_POOL),
                  _layer_spec((D_CONV, D_MODEL), l), _layer_spec((D_ATTN, D_MODEL), l),
                  _layer_spec((D_POOL, D_MODEL), l), _layer_spec((D_MODEL, D_MODEL), l)],
        out_specs=tspec(D_MODEL),
        out_shape=jax.ShapeDtypeStruct((N, D_MODEL), F32),
        compiler_params=_cparams(("arbitrary",)),
        name="mix_out",
    )(x2, sw["g_mix"], sw["w_g"], ca, ao, py, sw["w_conv_out"], sw["w_attn_out"], sw["w_pool_out"], sw["w_out"])


def _ffn_kernel(x_ref, g_ref, w1_ref, w2_ref, gf_ref, o_ref, *, final):
    x = x_ref[...]
    ms = jnp.mean(x * x, axis=-1, keepdims=True)
    h = (x * lax.rsqrt(ms + RMS_EPS) * g_ref[...]).astype(BF16)
    r = jnp.maximum(jnp.dot(h, w1_ref[...], preferred_element_type=F32), 0.0)
    y = x + jnp.dot((r * r).astype(BF16), w2_ref[...], preferred_element_type=F32)
    if final:
        ms2 = jnp.mean(y * y, axis=-1, keepdims=True)
        y = y * lax.rsqrt(ms2 + RMS_EPS) * gf_ref[...]
    o_ref[...] = y


def _ffn(x2, sw, l, g_final, *, TM, final):
    N = x2.shape[0]
    once = pl.Buffered(1)
    return pl.pallas_call(
        functools.partial(_ffn_kernel, final=final),
        grid=(N // TM,),
        in_specs=[pl.BlockSpec((TM, D_MODEL), lambda i: (i, 0)), _layer_spec((1, D_MODEL), l),
                  _layer_spec((D_MODEL, D_FF), l, pipeline_mode=once),
                  _layer_spec((D_FF, D_MODEL), l, pipeline_mode=once),
                  pl.BlockSpec((1, D_MODEL), lambda i: (0, 0))],
        out_specs=pl.BlockSpec((TM, D_MODEL), lambda i: (i, 0)),
        out_shape=jax.ShapeDtypeStruct((N, D_MODEL), F32),
        compiler_params=_cparams(("arbitrary",)),
        name="ffn",
    )(x2, sw["g_ffn"], sw["w_ff1"], sw["w_ff2"], g_final)


def _stacked_weights(g_mix, w_in, conv_w, conv_b, conv_ln_g, conv_ln_b, w_conv_out, w_attn_out,
                     pool_w, pool_scale, w_pool_out, w_out, g_ffn, w_ff1, w_ff2):
    depth = w_in.shape[0]
    o_ki = C_QI + IDX_HEADS * IDX_DIM
    o_wi = o_ki + IDX_DIM
    o_xc = o_wi + IDX_HEADS
    o_g = o_xc + D_POOL

    def pad_to_tile(a):
        return jnp.pad(a, ((0, 0), (0, 0), (0, LANES - a.shape[2])))

    w_a = jnp.concatenate([w_in[:, :, :o_ki], w_in[:, :, o_xc:o_g], pad_to_tile(w_in[:, :, o_ki:o_wi]),
                           pad_to_tile(w_in[:, :, o_wi:o_xc])], axis=2).astype(BF16)
    pw = jnp.zeros((depth, D_POOL, D_POOL), F32)
    for g in range(len(POOL_WINDOWS)):
        pw = pw.at[:, g * POOL_GC:(g + 1) * POOL_GC, g * POOL_GC:(g + 1) * POOL_GC].set(pool_w[:, g])

    def row(a):
        return a[:, None, :]

    return dict(
        g_mix=row(g_mix), w_a=w_a, w_g=w_in[:, :, o_g:].astype(BF16),
        conv_w=jnp.pad(conv_w, ((0, 0), (0, CONV_PAD - CONV_W), (0, 0))), conv_b=row(conv_b),
        ln_g=row(conv_ln_g), ln_b=row(conv_ln_b),
        pool_w=pw.astype(BF16), pool_scale=row(pool_scale),
        w_conv_out=w_conv_out.astype(BF16), w_attn_out=w_attn_out.astype(BF16),
        w_pool_out=w_pool_out.astype(BF16), w_out=w_out.astype(BF16),
        g_ffn=row(g_ffn), w_ff1=w_ff1.astype(BF16), w_ff2=w_ff2.astype(BF16))


def _tail_state(hist, new, keep):
    return jnp.concatenate([hist, new], axis=1)[:, -keep:]


def kernel(x_prompt, x_sample, cache_k, cache_v, cache_kidx, state_conv, state_pool, page_table, g_mix, w_in,
           conv_w, conv_b, conv_ln_g, conv_ln_b, w_conv_out, w_attn_out, pool_w, pool_scale, w_pool_out, w_out,
           g_ffn, w_ff1, w_ff2, g_final):
    B, T, _ = x_prompt.shape
    NB, TS, _ = x_sample.shape
    depth = w_in.shape[0]
    n_pool = cache_k.shape[1]
    past_len = page_table.shape[1] * PAGE_SIZE
    TM_P = min(512, T)
    gf = g_final[None]
    xp = x_prompt
    xs = x_sample
    zero_conv = jnp.zeros((B, CONV_PAD, D_CONV), F32)
    zero_pool = jnp.zeros((B, POOL_PAD, D_POOL), F32)
    cache_k_t = cache_k.transpose(0, 1, 3, 4, 2).reshape(depth, n_pool, KV_DIM, PAGE_SIZE)
    cache_v_t = cache_v.transpose(0, 1, 3, 4, 2).reshape(depth, n_pool, KV_DIM, PAGE_SIZE)
    cache_kidx_t = cache_kidx.transpose(0, 1, 3, 2)
    sw = _stacked_weights(g_mix, w_in, conv_w, conv_b, conv_ln_g, conv_ln_b, w_conv_out, w_attn_out,
                          pool_w, pool_scale, w_pool_out, w_out, g_ffn, w_ff1, w_ff2)
    tables_p = _rope_tables(0, T)
    tables_s = _rope_tables(past_len, TS)
    hist_conv_s = jnp.pad(state_conv, ((0, 0), (0, 0), (CONV_PAD - CONV_HIST, 0), (0, 0)))
    hist_pool_s = jnp.pad(state_pool, ((0, 0), (0, 0), (POOL_PAD - POOL_HIST, 0), (0, 0)))
    st_p = [[] for _ in range(5)]
    st_s = [[] for _ in range(5)]
    for l in range(depth):
        final = l == depth - 1

        (k, v, ki, u, xc, kibf, q, _, _, ca, py, qit, wq, kt, va) = _proj(
            xp, 0, tables_p, zero_conv, zero_pool, sw, l, G=1, TM=TM_P, adt=BF16, attn=True)
        ao = _attend_prompt(q, qit, wq, kibf, kt, va, CKS=TM_P)
        x2 = _mix_out(xp.reshape(B * T, D_MODEL), ca.reshape(B * T, D_CONV), ao.reshape(B * T, D_ATTN),
                      py.reshape(B * T, D_POOL), sw, l, TM=TM_P)
        xp = _ffn(x2, sw, l, gf, TM=TM_P, final=final).reshape(B, T, D_MODEL)
        st_p[0].append(k.reshape(B, T, N_KV_HEADS, HEAD_DIM))
        st_p[1].append(v.reshape(B, T, N_KV_HEADS, HEAD_DIM))
        st_p[2].append(ki)
        st_p[3].append(u[:, T - CONV_HIST:])
        st_p[4].append(xc[:, T - POOL_HIST:])

        (k, v, ki, u, xc, _, q, qi, wi, ca, py) = _proj(
            xs, past_len, tables_s, hist_conv_s[l], hist_pool_s[l], sw, l, G=NB, TM=TS, adt=F32, attn=False)
        q, qi, ca, py = q.astype(BF16), qi.astype(BF16), ca.astype(BF16), py.astype(BF16)
        qi_rows = qi.reshape(NB, TS, IDX_HEADS, IDX_DIM).transpose(0, 2, 1, 3).reshape(NB, IDX_HEADS * TS, IDX_DIM)
        w_rows = jnp.broadcast_to(wi[:, :, :IDX_HEADS].transpose(0, 2, 1).reshape(NB, IDX_HEADS * TS, 1),
                                  (NB, IDX_HEADS * TS, LANES))
        qh = q.reshape(NB, TS, N_HEADS, HEAD_DIM).transpose(0, 2, 1, 3)
        head_kv = (jnp.arange(N_HEADS) // GROUP)[None, :, None, None]
        q_rows = jnp.concatenate([jnp.where(head_kv == n, qh, jnp.zeros_like(qh)) for n in range(N_KV_HEADS)],
                                 axis=-1).reshape(NB, N_HEADS * TS, KV_DIM)
        pad_rows = ((0, 0), (0, PAGE_SIZE - TS), (0, 0))
        pages_per_step = min(32, page_table.shape[1])

        def new_page_t(a):
            return jnp.pad(a, pad_rows).transpose(0, 2, 1)

        sc, thr = _sample_scores(page_table, cache_kidx_t, l, qi_rows, w_rows, new_page_t(ki), P=pages_per_step)
        o_rows = _sample_attend(page_table, cache_k_t, cache_v_t, l, sc, thr, q_rows, new_page_t(k), new_page_t(v),
                                P=pages_per_step)
        o5 = o_rows.reshape(NB, N_KV_HEADS, GROUP, TS, N_KV_HEADS, HEAD_DIM)
        ao = jnp.stack([o5[:, n, :, :, n, :] for n in range(N_KV_HEADS)], axis=1)
        ao = ao.reshape(NB, N_HEADS, TS, HEAD_DIM).transpose(0, 2, 1, 3).reshape(NB * TS, D_ATTN).astype(BF16)
        x2 = _mix_out(xs.reshape(NB * TS, D_MODEL), ca.reshape(NB * TS, D_CONV), ao,
                      py.reshape(NB * TS, D_POOL), sw, l, TM=NB * TS)
        xs = _ffn(x2, sw, l, gf, TM=NB * TS, final=final).reshape(NB, TS, D_MODEL)
        st_s[0].append(k.reshape(NB, TS, N_KV_HEADS, HEAD_DIM))
        st_s[1].append(v.reshape(NB, TS, N_KV_HEADS, HEAD_DIM))
        st_s[2].append(ki)
        st_s[3].append(_tail_state(state_conv[l], u, CONV_HIST))
        st_s[4].append(_tail_state(state_pool[l], xc, POOL_HIST))

    sp = [jnp.stack(a, axis=0) for a in st_p]
    ss = [jnp.stack(a, axis=0) for a in st_s]
    return (xp, xs, sp[0], sp[1], sp[2], sp[3], sp[4], ss[0], ss[1], ss[2], ss[3], ss[4])
```

```python
import functools

import jax
import jax.numpy as jnp
from jax import lax
from jax.experimental import pallas as pl
from jax.experimental.pallas import tpu as pltpu

F32 = jnp.float32
BF16 = jnp.bfloat16

D_MODEL = 1024
D_CONV = 256
CONV_W = 31
CONV_HIST = CONV_W - 1
HEAD_DIM = 64
D_ATTN = 512
N_HEADS = 8
N_KV_HEADS = 2
KV_DIM = N_KV_HEADS * HEAD_DIM
GROUP = N_HEADS // N_KV_HEADS
IDX_HEADS = 8
IDX_DIM = 64
TOPK_MAX = 256
D_POOL = 256
POOL_WINDOWS = (2, 4, 8, 16)
POOL_GC = D_POOL // len(POOL_WINDOWS)
POOL_HIST = max(POOL_WINDOWS) - 1
N_BRANCH = 3
D_FF = 4 * D_MODEL
ROPE_THETA = 10000.0
RMS_EPS = 1e-6
LN_EPS = 1e-5
PAGE_SIZE = 128

LANES = 128
SUBLANES = 8
PROMPT_TILE = 512
PAGES_PER_STEP = 32
TIE_CHUNK = 256
BISECT_WARM = 12
BISECT_STEPS = 4
CHUNKS_PER_TRIP = 4
CONV_PAD = 32
POOL_PAD = 16
Q_SCALE = HEAD_DIM ** -0.5 * 1.4426950408889634
BIG = 3.0e38
NEG_INF = float("-inf")

C_AIN, C_AGATE, C_Q, C_K, C_V, C_QI, C_XC, C_KI, C_WI, D_A = 0, 256, 512, 1024, 1152, 1280, 1792, 2048, 2176, 2304

VMEM_LIMIT = 56 * 1024 * 1024


def _cparams(sem):
    return pltpu.CompilerParams(dimension_semantics=sem, vmem_limit_bytes=VMEM_LIMIT)


def _layer_spec(shape, l, **kw):
    n = len(shape)
    return pl.BlockSpec((None,) + tuple(shape), lambda *_: (l,) + (0,) * n, **kw)


def _proj_kernel(x_ref, g_ref, w_ref, cos_ref, sin_ref, chist_ref, phist_ref, convw_ref, convb_ref,
                 lng_ref, lnb_ref, poolw_ref, pscale_ref,
                 k_ref, v_ref, ki_ref, u_ref, xc_ref, kibf_ref, q_ref, qi_ref, wi_ref, ca_ref, py_ref,
                 *rest, G, TM, pos0, n_tiles, attn):
    if attn:
        qit_ref, wq_ref, kt_ref, va_ref = rest[:4]
        rest = rest[4:]
    cs_ref, ps_ref, p2_ref, p4_ref, p8_ref, p16_ref = rest
    j = pl.program_id(1)
    R = G * TM
    x = x_ref[...].reshape(R, D_MODEL)
    ms = jnp.mean(x * x, axis=-1, keepdims=True)
    h = (x * lax.rsqrt(ms + RMS_EPS) * g_ref[...]).astype(BF16)
    z = jnp.dot(h, w_ref[...], preferred_element_type=F32)

    cos = cos_ref[...]
    sin = sin_ref[...]
    if G > 1:
        cos = jnp.concatenate([cos] * G, axis=0)
        sin = jnp.concatenate([sin] * G, axis=0)

    def rope(t):
        n = t.shape[1] // LANES
        c = jnp.concatenate([cos] * n, axis=1) if n > 1 else cos
        s = jnp.concatenate([sin] * n, axis=1) if n > 1 else sin
        lane = lax.broadcasted_iota(jnp.int32, t.shape, 1)
        first = (lane & (HEAD_DIM - 1)) < HEAD_DIM // 2
        sw = jnp.where(first, pltpu.roll(t, t.shape[1] - HEAD_DIM // 2, 1), pltpu.roll(t, HEAD_DIM // 2, 1))
        return t * c + sw * s

    q = rope(z[:, C_Q:C_Q + D_ATTN]) * Q_SCALE
    k = rope(z[:, C_K:C_K + KV_DIM])
    v = z[:, C_V:C_V + KV_DIM]
    qi = rope(z[:, C_QI:C_QI + IDX_HEADS * IDX_DIM])
    ki = rope(z[:, C_KI:C_KI + LANES])[:, :IDX_DIM]
    wi = z[:, C_WI:C_WI + LANES] * (IDX_HEADS ** -0.5 * IDX_DIM ** -0.5)

    k_ref[...] = k.reshape(G, TM, KV_DIM)
    v_ref[...] = v.reshape(G, TM, KV_DIM)
    ki_ref[...] = ki.reshape(G, TM, IDX_DIM)
    kibf_ref[...] = ki.reshape(G, TM, IDX_DIM).astype(kibf_ref.dtype)
    q_ref[...] = q.reshape(G, TM, D_ATTN).astype(q_ref.dtype)
    qi_ref[...] = qi.reshape(G, TM, IDX_HEADS * IDX_DIM).astype(qi_ref.dtype)
    wi_ref[...] = wi.reshape(G, TM, LANES)
    if attn:
        for b in range(TM // LANES):
            blk = slice(b * LANES, (b + 1) * LANES)
            qi_t = qi[blk, :].T
            qit_ref[b] = jnp.concatenate([qi_t[hh * IDX_DIM:(hh + 1) * IDX_DIM, :] for hh in range(IDX_HEADS)],
                                         axis=1).astype(BF16)
            wi_t = wi[blk, :].T
            wq_ref[b] = jnp.concatenate([wi_t[hh:hh + 1, :] for hh in range(IDX_HEADS)], axis=1)
        k_t = k.T
        lane = lax.broadcasted_iota(jnp.int32, (TM, KV_DIM), 1)
        tail = jnp.where(lane == HEAD_DIM, 1.0, 0.0)
        for n in range(N_KV_HEADS):
            kt_ref[0, n] = k_t[n * HEAD_DIM:(n + 1) * HEAD_DIM, :].astype(BF16)
            vn = v if n == 0 else pltpu.roll(v, KV_DIM - n * HEAD_DIM, 1)
            va_ref[0, n] = jnp.where(lane < HEAD_DIM, vn, tail).astype(BF16)

    a_in = z[:, C_AIN:C_AIN + D_CONV]
    a_gate = z[:, C_AGATE:C_AGATE + D_CONV]
    u = (a_in * jax.nn.sigmoid(a_gate)).reshape(G, TM, D_CONV)
    u_ref[...] = u

    @pl.when(j == 0)
    def _():
        cs_ref[:, 0:CONV_PAD, :] = chist_ref[...]
        ps_ref[:, 0:POOL_PAD, :] = phist_ref[...]

    cs_ref[:, CONV_PAD:CONV_PAD + TM, :] = u
    off = CONV_PAD - CONV_HIST
    rc = min(TM, 64)
    gc = min(G, 8)
    for g0 in range(0, G, gc):
        for r0 in range(0, TM, rc):
            acc = jnp.zeros((gc, rc, D_CONV), F32) + convb_ref[...][None]
            for t in range(CONV_W):
                acc = acc + convw_ref[t:t + 1, :][None] * cs_ref[g0:g0 + gc, off + t + r0:off + t + r0 + rc, :]
            mu = jnp.mean(acc, axis=-1, keepdims=True)
            var = jnp.mean(jnp.square(acc - mu), axis=-1, keepdims=True)
            y = (acc - mu) * lax.rsqrt(var + LN_EPS) * lng_ref[...][None] + lnb_ref[...][None]
            ca_ref[g0:g0 + gc, r0:r0 + rc, :] = (y * jax.nn.sigmoid(y)).astype(ca_ref.dtype)
    if n_tiles > 1:
        cs_ref[:, 0:CONV_PAD, :] = cs_ref[:, TM:TM + CONV_PAD, :]

    xc = z[:, C_XC:C_XC + D_POOL].reshape(G, TM, D_POOL)
    xc_ref[...] = xc
    ps_ref[:, POOL_PAD:POOL_PAD + TM, :] = xc
    RR = POOL_PAD + TM
    p2_ref[:, 1:RR, :] = ps_ref[:, 1:RR, :] + ps_ref[:, 0:RR - 1, :]
    p4_ref[:, 3:RR, :] = p2_ref[:, 3:RR, :] + p2_ref[:, 1:RR - 2, :]
    p8_ref[:, 7:RR, :] = p4_ref[:, 7:RR, :] + p4_ref[:, 3:RR - 4, :]
    p16_ref[:, 15:RR, :] = p8_ref[:, 15:RR, :] + p8_ref[:, 7:RR - 8, :]
    lane = lax.broadcasted_iota(jnp.int32, (G, TM, D_POOL), 2)
    row = lax.broadcasted_iota(jnp.int32, (G, TM, D_POOL), 1)
    g0m, g1m, g2m = lane < POOL_GC, lane < 2 * POOL_GC, lane < 3 * POOL_GC
    s_tok = slice(POOL_PAD, POOL_PAD + TM)
    wsum = jnp.where(g0m, p2_ref[:, s_tok, :],
                     jnp.where(g1m, p4_ref[:, s_tok, :], jnp.where(g2m, p8_ref[:, s_tok, :], p16_ref[:, s_tok, :])))
    win = jnp.where(g0m, POOL_WINDOWS[0], jnp.where(g1m, POOL_WINDOWS[1],
                                                     jnp.where(g2m, POOL_WINDOWS[2], POOL_WINDOWS[3])))
    cnt = jnp.minimum(pos0 + j * TM + row + 1, win).astype(F32)
    d = (wsum / cnt - xc).reshape(R, D_POOL).astype(BF16)
    py = jnp.dot(d, poolw_ref[...], preferred_element_type=F32) * pscale_ref[...]
    py_ref[...] = py.reshape(G, TM, D_POOL).astype(py_ref.dtype)
    if n_tiles > 1:
        ps_ref[:, 0:POOL_PAD, :] = ps_ref[:, TM:TM + POOL_PAD, :]


def _rope_tables(pos0, T):
    half = HEAD_DIM // 2
    pos = (pos0 + jnp.arange(T)).astype(F32)
    inv = ROPE_THETA ** (-jnp.arange(half, dtype=F32) / half)
    ang = pos[:, None] * inv[None, :]
    cos = jnp.tile(jnp.cos(ang), (1, 2 * LANES // HEAD_DIM))
    sin = jnp.tile(jnp.concatenate([-jnp.sin(ang), jnp.sin(ang)], axis=1), (1, LANES // HEAD_DIM))
    return cos, sin


def _proj(x3, pos0, tables, chist, phist, sw, l, *, G, TM, adt, attn):
    NS, T, _ = x3.shape
    n_tiles = T // TM
    cos, sin = tables

    def tok(c, dt=F32):
        return jax.ShapeDtypeStruct((NS, T, c), dt)

    def tspec(c):
        return pl.BlockSpec((G, TM, c), lambda s, j: (s, j, 0))

    out_cols = [(KV_DIM, F32), (KV_DIM, F32), (IDX_DIM, F32), (D_CONV, F32), (D_POOL, F32),
                (IDX_DIM, adt), (D_ATTN, adt), (IDX_HEADS * IDX_DIM, adt), (LANES, F32), (D_CONV, adt), (D_POOL, adt)]
    out_specs = [tspec(c) for c, _ in out_cols]
    out_shape = [tok(c, dt) for c, dt in out_cols]
    if attn:
        assert G == 1 and TM % LANES == 0
        nb = TM // LANES
        out_specs += [pl.BlockSpec((None, nb, IDX_DIM, IDX_HEADS * LANES), lambda s, j: (s, j, 0, 0)),
                      pl.BlockSpec((None, nb, 1, IDX_HEADS * LANES), lambda s, j: (s, j, 0, 0)),
                      pl.BlockSpec((None, 1, N_KV_HEADS, HEAD_DIM, TM), lambda s, j: (s, j, 0, 0, 0)),
                      pl.BlockSpec((None, 1, N_KV_HEADS, TM, LANES), lambda s, j: (s, j, 0, 0, 0))]
        out_shape += [jax.ShapeDtypeStruct((NS, T // LANES, IDX_DIM, IDX_HEADS * LANES), BF16),
                      jax.ShapeDtypeStruct((NS, T // LANES, 1, IDX_HEADS * LANES), F32),
                      jax.ShapeDtypeStruct((NS, n_tiles, N_KV_HEADS, HEAD_DIM, TM), BF16),
                      jax.ShapeDtypeStruct((NS, n_tiles, N_KV_HEADS, TM, LANES), BF16)]
    kern = functools.partial(_proj_kernel, G=G, TM=TM, pos0=pos0, n_tiles=n_tiles, attn=attn)
    return pl.pallas_call(
        kern,
        grid=(NS // G, n_tiles),
        in_specs=[tspec(D_MODEL), _layer_spec((1, D_MODEL), l), _layer_spec((D_MODEL, D_A), l),
                  pl.BlockSpec((TM, LANES), lambda s, j: (j, 0)), pl.BlockSpec((TM, LANES), lambda s, j: (j, 0)),
                  pl.BlockSpec((G, CONV_PAD, D_CONV), lambda s, j: (s, 0, 0)),
                  pl.BlockSpec((G, POOL_PAD, D_POOL), lambda s, j: (s, 0, 0)),
                  _layer_spec((CONV_PAD, D_CONV), l), _layer_spec((1, D_CONV), l), _layer_spec((1, D_CONV), l),
                  _layer_spec((1, D_CONV), l), _layer_spec((D_POOL, D_POOL), l), _layer_spec((1, D_POOL), l)],
        out_specs=out_specs,
        out_shape=out_shape,
        scratch_shapes=[pltpu.VMEM((G, CONV_PAD + TM, D_CONV), F32)] +
                       [pltpu.VMEM((G, POOL_PAD + TM, D_POOL), F32)] * 5,
        compiler_params=_cparams(("arbitrary", "arbitrary")),
        name="proj",
    )(x3, sw["g_mix"], sw["w_a"], cos, sin, chist, phist, sw["conv_w"], sw["conv_b"], sw["ln_g"], sw["ln_b"],
      sw["pool_w"], sw["pool_scale"])


def _for_chunks(n, body, carry):
    carry = lax.fori_loop(0, n // CHUNKS_PER_TRIP, lambda j, t: body(CHUNKS_PER_TRIP * j, CHUNKS_PER_TRIP, t), carry)
    done = (n // CHUNKS_PER_TRIP) * CHUNKS_PER_TRIP
    size = CHUNKS_PER_TRIP // 2
    while size >= 1:
        take = (n - done) >= size
        carry = lax.cond(take, functools.partial(body, done, size), lambda t: t, carry)
        done = done + jnp.where(take, size, 0)
        size //= 2
    return carry


def _kth_bisect(count, smin, smax, nvalid, kk, steps, warm=0, zero_counts=None):
    zero = jnp.zeros_like(smin)
    c_ge0, c_gt0 = zero_counts if zero_counts is not None else (count(zero, False), count(zero, True))
    pos = c_gt0 >= kk
    neg = c_ge0 < kk
    lo0 = jnp.where(neg, smin, zero)
    clo0 = jnp.where(neg, nvalid, c_ge0)
    hi0 = jnp.where(pos, BIG, zero)
    chi0 = jnp.where(pos, zero, jnp.where(neg, c_ge0, c_gt0))
    cand0 = jnp.where(pos, smax, smin * 0.5)
    small = nvalid <= kk
    done0 = (small | jnp.logical_not(pos | neg)).astype(F32)
    init = (jnp.where(small, smin, lo0), hi0, jnp.where(small, nvalid, clo0), chi0, cand0, done0)

    def count_ge(cand):
        return count(cand, False)

    def cond(st):
        return jnp.min(st[5]) < 0.5

    def body(st):
        for _ in range(steps):
            st = step(st)
        return st

    def step(st):
        lo, hi, clo, chi, cand, done = st
        c = count_ge(cand)
        act = done < 0.5
        ge = c >= kk
        up = act & ge
        dn = act & jnp.logical_not(ge)
        lo = jnp.where(up, cand, lo)
        clo = jnp.where(up, c, clo)
        hi = jnp.where(dn, cand, hi)
        chi = jnp.where(dn, c, chi)
        mid = lo * 0.5 + hi * 0.5
        inside = (mid > lo) & (mid < hi)
        fin = (clo == kk) | jnp.logical_not(inside)
        done = jnp.where(fin, 1.0, done)
        return lo, hi, clo, chi, mid, done

    if warm:
        init = lax.fori_loop(0, warm, lambda _, st: step(st), init)
    lo, _, clo, chi, _, _ = lax.while_loop(cond, body, init)
    return lo, clo, chi


def _attend_prompt_kernel(qit_ref, wq_ref, q_ref, ki_ref, kt_ref, va_ref, o_ref, sc_ref, acc_ref, m_ref, *,
                          CKS, CK, kk):
    i = pl.program_id(1)
    QB = LANES
    nchs = (i * QB + QB + CKS - 1) // CKS
    nch = nchs * (CKS // CK)
    qit = qit_ref[...]
    wq = wq_ref[...]
    qcol = i * QB + lax.broadcasted_iota(jnp.int32, (CKS, QB), 1)
    krow0 = lax.broadcasted_iota(jnp.int32, (CKS, QB), 0)

    def score_chunk(c, carry, causal):
        smax, smin, n_ge0, n_gt0 = carry
        s = jnp.dot(ki_ref[c], qit, preferred_element_type=F32)
        s = jnp.maximum(s, 0.0) * wq
        tot = s[:, 0:QB]
        for hh in range(1, IDX_HEADS):
            tot = tot + s[:, hh * QB:(hh + 1) * QB]
        if causal:
            valid = (krow0 + c * CKS) <= qcol
            masked = jnp.where(valid, tot, NEG_INF)
            lowest = jnp.where(valid, tot, BIG)
        else:
            masked = lowest = tot
        sc_ref[pl.ds(pl.multiple_of(c * CKS, CKS), CKS), :] = masked
        smax = jnp.maximum(smax, jnp.max(masked.reshape(CKS // SUBLANES, SUBLANES, QB), axis=0))
        smin = jnp.minimum(smin, jnp.min(lowest.reshape(CKS // SUBLANES, SUBLANES, QB), axis=0))
        n_ge0 = n_ge0 + jnp.sum(jnp.where(masked >= 0.0, 1.0, 0.0).reshape(CKS // SUBLANES, SUBLANES, QB), axis=0)
        n_gt0 = n_gt0 + jnp.sum(jnp.where(masked > 0.0, 1.0, 0.0).reshape(CKS // SUBLANES, SUBLANES, QB), axis=0)
        return smax, smin, n_ge0, n_gt0

    def score_run(first, count, carry):
        for u in range(count):
            carry = score_chunk(first + u, carry, causal=False)
        return carry

    carry = _for_chunks(nchs - 1, score_run,
                        (jnp.full((SUBLANES, QB), NEG_INF, F32), jnp.full((SUBLANES, QB), BIG, F32),
                         jnp.zeros((SUBLANES, QB), F32), jnp.zeros((SUBLANES, QB), F32)))
    smax, smin, n_ge0, n_gt0 = score_chunk(nchs - 1, carry, causal=True)
    smax = jnp.max(smax, axis=0, keepdims=True)
    smin = jnp.min(smin, axis=0, keepdims=True)
    zero_counts = (jnp.sum(n_ge0, axis=0, keepdims=True), jnp.sum(n_gt0, axis=0, keepdims=True))

    n_acc = 4
    rows = CKS // n_acc

    def count(cand, strict):
        def body(c, accs):
            base = pl.multiple_of(c * CKS, CKS)
            out = []
            for a in range(n_acc):
                blk = sc_ref[pl.ds(base + a * rows, rows), :]
                hit = (blk > cand) if strict else (blk >= cand)
                ones = jnp.where(hit, 1.0, 0.0).reshape(rows // SUBLANES, SUBLANES, QB)
                out.append(accs[a] + jnp.sum(ones, axis=0))
            return tuple(out)
        def run(first, count, accs):
            for u in range(count):
                accs = body(first + u, accs)
            return accs
        accs = _for_chunks(nchs, run, tuple(jnp.zeros((SUBLANES, QB), F32) for _ in range(n_acc)))
        return jnp.sum((accs[0] + accs[1]) + (accs[2] + accs[3]), axis=0, keepdims=True)

    nvalid = (i * QB + lax.broadcasted_iota(jnp.int32, (1, QB), 1) + 1).astype(F32)
    lo, clo, chi = _kth_bisect(count, smin, smax, nvalid, float(kk), steps=BISECT_STEPS, warm=BISECT_WARM,
                               zero_counts=zero_counts)

    fix = (clo > kk) & (nvalid > kk)

    @pl.when(jnp.max(fix.astype(F32)) > 0.5)
    def _():
        keep = kk - chi
        TF = min(TIE_CHUNK, CKS)
        tri = (lax.broadcasted_iota(jnp.int32, (TF, TF), 1)
               <= lax.broadcasted_iota(jnp.int32, (TF, TF), 0)).astype(BF16)

        def body(c, run):
            rows_c = pl.ds(pl.multiple_of(c * TF, TF), TF)
            blk = sc_ref[rows_c, :]
            tie = jnp.where(fix & (blk == lo), 1.0, 0.0)
            rank = run + jnp.dot(tri, tie.astype(BF16), preferred_element_type=F32)
            sc_ref[rows_c, :] = jnp.where((tie > 0.5) & (rank > keep), NEG_INF, blk)
            return run + jnp.sum(tie, axis=0, keepdims=True)
        lax.fori_loop(0, nchs * (CKS // TF), body, jnp.zeros((1, QB), F32))

    GQ = GROUP * QB
    q32 = q_ref[...].astype(F32)
    qn = [jnp.concatenate([q32[:, hh * HEAD_DIM:(hh + 1) * HEAD_DIM] for hh in range(n * GROUP, (n + 1) * GROUP)],
                          axis=0).astype(BF16) for n in range(N_KV_HEADS)]
    acc_ref[...] = jnp.zeros_like(acc_ref)
    m_ref[...] = jnp.full_like(m_ref, -BIG)
    reps = CK // LANES

    def att_unit(first, count, carry):
        chunks = [first + u for u in range(count)]
        bias_t = [jnp.where(sc_ref[pl.ds(pl.multiple_of(c * CK, CK), CK), :] >= lo, 0.0, NEG_INF).T
                  for c in chunks]
        for n in range(N_KV_HEADS):
            ss = []
            for c, bt in zip(chunks, bias_t):
                s = jnp.dot(qn[n], kt_ref[c, n], preferred_element_type=F32)
                ss.append((s.reshape(GROUP, QB, CK) + bt[None]).reshape(GQ, CK))
            m_old = m_ref[n]
            mnew = m_old
            for s in ss:
                mnew = jnp.maximum(mnew, jnp.max(s, axis=1, keepdims=True))
            mrep = jnp.concatenate([mnew] * reps, axis=1)
            pv = None
            for c, s in zip(chunks, ss):
                part = jnp.dot(jnp.exp2(s - mrep).astype(BF16), va_ref[c, n], preferred_element_type=F32)
                pv = part if pv is None else pv + part
            acc_ref[n] = jnp.exp2(m_old - mnew) * acc_ref[n] + pv
            m_ref[n] = mnew
        return carry

    _for_chunks(nch, att_unit, 0)

    outs = []
    for n in range(N_KV_HEADS):
        a = acc_ref[n]
        o = a[:, :HEAD_DIM] / a[:, HEAD_DIM:HEAD_DIM + 1]
        outs += [o[g * QB:(g + 1) * QB, :] for g in range(GROUP)]
    o_ref[...] = jnp.concatenate(outs, axis=1).astype(BF16)


def _attend_prompt(q, qit, wq, kibf, kt, va, *, CKS):
    B, T, _ = q.shape
    QB = LANES
    nq = T // QB
    nc, CK = kt.shape[1], kt.shape[4]
    ncs = T // CKS
    kk = min(TOPK_MAX, T // 4)
    kib = kibf.reshape(B, ncs, CKS, IDX_DIM)

    kern = functools.partial(_attend_prompt_kernel, CKS=CKS, CK=CK, kk=kk)
    return pl.pallas_call(
        kern,
        grid=(B, nq),
        in_specs=[pl.BlockSpec((None, None, IDX_DIM, 8 * QB), lambda b, i: (b, i, 0, 0)),
                  pl.BlockSpec((None, None, 1, 8 * QB), lambda b, i: (b, i, 0, 0)),
                  pl.BlockSpec((None, QB, D_ATTN), lambda b, i: (b, i, 0)),
                  pl.BlockSpec((None, ncs, CKS, IDX_DIM), lambda b, i: (b, 0, 0, 0)),
                  pl.BlockSpec((None, nc, N_KV_HEADS, HEAD_DIM, CK), lambda b, i: (b, 0, 0, 0, 0)),
                  pl.BlockSpec((None, nc, N_KV_HEADS, CK, LANES), lambda b, i: (b, 0, 0, 0, 0))],
        out_specs=pl.BlockSpec((None, QB, D_ATTN), lambda b, i: (b, i, 0)),
        out_shape=jax.ShapeDtypeStruct((B, T, D_ATTN), BF16),
        scratch_shapes=[pltpu.VMEM((T, QB), F32), pltpu.VMEM((N_KV_HEADS, GROUP * QB, LANES), F32),
                        pltpu.VMEM((N_KV_HEADS, GROUP * QB, LANES), F32)],
        compiler_params=_cparams(("arbitrary", "arbitrary")),
        name="attend_prompt",
    )(qit, wq, q, kib, kt, va)


def _sample_score_kernel(pt_ref, *refs, P, n_steps, past_len, kk):
    pages = refs[:P]
    qi_ref, wb_ref, kin_ref, sc_ref, thr_ref = refs[P:P + 5]
    j = pl.program_id(1)
    TQ = sc_ref.shape[1]
    NBLK = sc_ref.shape[0]
    qi = qi_ref[...]
    wb = wb_ref[...]

    def block_scores(kpage_t):
        s = jnp.dot(qi, kpage_t.astype(BF16), preferred_element_type=F32)
        s = jnp.maximum(s, 0.0) * wb
        return jnp.sum(s.reshape(IDX_HEADS, TQ, LANES), axis=0)

    for r in range(P):
        sc_ref[j * P + r] = block_scores(pages[r][...])

    @pl.when(j == n_steps - 1)
    def _():
        tot = block_scores(kin_ref[...])
        qrow = lax.broadcasted_iota(jnp.int32, (TQ, LANES), 0)
        kcol = lax.broadcasted_iota(jnp.int32, (TQ, LANES), 1)
        sc_ref[NBLK - 1] = jnp.where(kcol <= qrow, tot, NEG_INF)

        def lane_bcast(a):
            return jnp.broadcast_to(a, (TQ, LANES))

        sc = sc_ref[...]
        smax = lane_bcast(jnp.max(jnp.max(sc, axis=0), axis=1, keepdims=True))
        smin = lane_bcast(jnp.min(jnp.min(jnp.where(sc > NEG_INF, sc, BIG), axis=0), axis=1, keepdims=True))
        nvalid = (past_len + qrow + 1).astype(F32)

        def count(cand, strict):
            blk = sc_ref[...]
            hit = (blk > cand[None]) if strict else (blk >= cand[None])
            c = jnp.sum(jnp.where(hit, 1.0, 0.0), axis=0)
            return lane_bcast(jnp.sum(c, axis=1, keepdims=True))

        lo, clo, chi = _kth_bisect(count, smin, smax, nvalid, float(kk), steps=BISECT_STEPS, warm=BISECT_WARM)
        thr_ref[...] = lo

        fix = (clo > kk) & (nvalid > kk)

        @pl.when(jnp.max(fix.astype(F32)) > 0.5)
        def _():
            keep = kk - chi
            tri = (lax.broadcasted_iota(jnp.int32, (LANES, LANES), 0)
                   <= lax.broadcasted_iota(jnp.int32, (LANES, LANES), 1)).astype(BF16)

            def body(c, run):
                blk = sc_ref[c]
                tie = fix & (blk == lo)
                rank = run + jnp.dot(tie.astype(BF16), tri, preferred_element_type=F32)
                sc_ref[c] = jnp.where(tie & (rank > keep), NEG_INF, blk)
                return run + lane_bcast(jnp.sum(tie.astype(F32), axis=1, keepdims=True))
            lax.fori_loop(0, NBLK, body, jnp.zeros((TQ, LANES), F32))


def _sample_scores(page_table, cache_kidx_t, l, qi_rows, w_rows, ki_new_t, *, P):
    NB, n_pages = page_table.shape
    TQ = qi_rows.shape[1] // IDX_HEADS
    n_steps = n_pages // P
    past_len = n_pages * PAGE_SIZE
    kk = min(TOPK_MAX, (past_len + TQ) // 4)
    NBLK = n_pages + 1

    def page_spec(r):
        return pl.BlockSpec((None, None, IDX_DIM, PAGE_SIZE), lambda b, j, pt: (l, pt[b, j * P + r], 0, 0))

    kern = functools.partial(_sample_score_kernel, P=P, n_steps=n_steps, past_len=past_len, kk=kk)
    grid_spec = pltpu.PrefetchScalarGridSpec(
        num_scalar_prefetch=1,
        grid=(NB, n_steps),
        in_specs=[page_spec(r) for r in range(P)] + [
            pl.BlockSpec((None, IDX_HEADS * TQ, IDX_DIM), lambda b, j, pt: (b, 0, 0)),
            pl.BlockSpec((None, IDX_HEADS * TQ, LANES), lambda b, j, pt: (b, 0, 0)),
            pl.BlockSpec((None, IDX_DIM, PAGE_SIZE), lambda b, j, pt: (b, 0, 0))],
        out_specs=[pl.BlockSpec((None, NBLK, TQ, LANES), lambda b, j, pt: (b, 0, 0, 0)),
                   pl.BlockSpec((None, TQ, LANES), lambda b, j, pt: (b, 0, 0))],
    )
    return pl.pallas_call(
        kern,
        grid_spec=grid_spec,
        out_shape=[jax.ShapeDtypeStruct((NB, NBLK, TQ, LANES), F32), jax.ShapeDtypeStruct((NB, TQ, LANES), F32)],
        compiler_params=_cparams(("arbitrary", "arbitrary")),
        name="sample_scores",
    )(page_table, *([cache_kidx_t] * P), qi_rows, w_rows, ki_new_t)


def _sample_attend_kernel(pt_ref, *refs, P, n_steps):
    kpages = refs[:P]
    vpages = refs[P:2 * P]
    sc_ref, scn_ref, thr_ref, q_ref, kn_ref, vn_ref, o_ref, m_ref, l_ref, acc_ref = refs[2 * P:2 * P + 10]
    j = pl.program_id(1)
    qz = q_ref[...]
    thr = thr_ref[...]

    @pl.when(j == 0)
    def _():
        m_ref[...] = jnp.full_like(m_ref, -BIG)
        l_ref[...] = jnp.zeros_like(l_ref)
        acc_ref[...] = jnp.zeros_like(acc_ref)

    def update(kblocks, vblocks, scblocks):
        s_parts = []
        for kb, scb in zip(kblocks, scblocks):
            s = jnp.dot(qz, kb.astype(BF16), preferred_element_type=F32)
            bias = jnp.where(scb >= thr, 0.0, NEG_INF)
            s_parts.append(s + jnp.concatenate([bias] * N_HEADS, axis=0))
        s = jnp.concatenate(s_parts, axis=1) if len(s_parts) > 1 else s_parts[0]
        m_old = m_ref[...]
        mnew = jnp.maximum(m_old, jnp.broadcast_to(jnp.max(s, axis=1, keepdims=True), m_old.shape))
        alpha = jnp.exp2(m_old - mnew)
        p = jnp.exp2(s - jnp.concatenate([mnew] * len(s_parts), axis=1) if len(s_parts) > 1 else s - mnew)
        l_ref[...] = alpha * l_ref[...] + jnp.broadcast_to(jnp.sum(p, axis=1, keepdims=True), m_old.shape)
        m_ref[...] = mnew
        pv = jnp.zeros(acc_ref.shape, F32)
        for r, vb in enumerate(vblocks):
            pv = pv + lax.dot_general(p[:, r * LANES:(r + 1) * LANES].astype(BF16), vb.astype(BF16),
                                      (((1,), (1,)), ((), ())), preferred_element_type=F32)
        acc_ref[...] = alpha * acc_ref[...] + pv

    update([kp[...] for kp in kpages], [vp[...] for vp in vpages], [sc_ref[r] for r in range(P)])

    @pl.when(j == n_steps - 1)
    def _():
        update([kn_ref[...]], [vn_ref[...]], [scn_ref[0]])
        o_ref[...] = acc_ref[...] / l_ref[...]


def _sample_attend(page_table, cache_k_t, cache_v_t, l, sc, thr, q_rows, k_new_t, v_new_t, *, P):
    NB, n_pages = page_table.shape
    R = q_rows.shape[1]
    TQ = R // N_HEADS
    n_steps = n_pages // P

    def page_spec(r):
        return pl.BlockSpec((None, None, KV_DIM, PAGE_SIZE), lambda b, j, pt: (l, pt[b, j * P + r], 0, 0))

    kern = functools.partial(_sample_attend_kernel, P=P, n_steps=n_steps)
    grid_spec = pltpu.PrefetchScalarGridSpec(
        num_scalar_prefetch=1,
        grid=(NB, n_steps),
        in_specs=[page_spec(r) for r in range(P)] + [page_spec(r) for r in range(P)] + [
            pl.BlockSpec((None, P, TQ, LANES), lambda b, j, pt: (b, j, 0, 0)),
            pl.BlockSpec((None, 1, TQ, LANES), lambda b, j, pt: (b, n_pages, 0, 0)),
            pl.BlockSpec((None, TQ, LANES), lambda b, j, pt: (b, 0, 0)),
            pl.BlockSpec((None, R, KV_DIM), lambda b, j, pt: (b, 0, 0)),
            pl.BlockSpec((None, KV_DIM, PAGE_SIZE), lambda b, j, pt: (b, 0, 0)),
            pl.BlockSpec((None, KV_DIM, PAGE_SIZE), lambda b, j, pt: (b, 0, 0))],
        out_specs=pl.BlockSpec((None, R, KV_DIM), lambda b, j, pt: (b, 0, 0)),
        scratch_shapes=[pltpu.VMEM((R, LANES), F32), pltpu.VMEM((R, LANES), F32), pltpu.VMEM((R, KV_DIM), F32)],
    )
    return pl.pallas_call(
        kern,
        grid_spec=grid_spec,
        out_shape=jax.ShapeDtypeStruct((NB, R, KV_DIM), F32),
        compiler_params=_cparams(("arbitrary", "arbitrary")),
        name="sample_attend",
    )(page_table, *([cache_k_t] * P), *([cache_v_t] * P), sc, sc, thr, q_rows, k_new_t, v_new_t)


def _mix_out_kernel(x_ref, g_ref, wg_ref, ca_ref, ao_ref, py_ref, wco_ref, wao_ref, wpo_ref, wo_ref, o_ref):
    x = x_ref[...]
    ms = jnp.mean(x * x, axis=-1, keepdims=True)
    h = (x * lax.rsqrt(ms + RMS_EPS) * g_ref[...]).astype(BF16)
    gates = jax.nn.sigmoid(jnp.dot(h, wg_ref[...], preferred_element_type=F32))
    y_a = jnp.dot(ca_ref[...], wco_ref[...], preferred_element_type=F32)
    y_b = jnp.dot(ao_ref[...], wao_ref[...], preferred_element_type=F32)
    y_c = jnp.dot(py_ref[...], wpo_ref[...], preferred_element_type=F32)
    merged = (gates[:, 0:D_MODEL] * y_a + gates[:, D_MODEL:2 * D_MODEL] * y_b
              + gates[:, 2 * D_MODEL:3 * D_MODEL] * y_c)
    o_ref[...] = x + jnp.dot(merged.astype(BF16), wo_ref[...], preferred_element_type=F32)


def _mix_out(x2, ca, ao, py, sw, l, *, TM):
    N = x2.shape[0]

    def tspec(c):
        return pl.BlockSpec((TM, c), lambda i: (i, 0))

    return pl.pallas_call(
        _mix_out_kernel,
        grid=(N // TM,),
        in_specs=[tspec(D_MODEL), _layer_spec((1, D_MODEL), l), _layer_spec((D_MODEL, N_BRANCH * D_MODEL), l),
                  tspec(D_CONV), tspec(D_ATTN), tspec(D_POOL),
                  _layer_spec((D_CONV, D_MODEL), l), _layer_spec((D_ATTN, D_MODEL), l),
                  _layer_spec((D_POOL, D_MODEL), l), _layer_spec((D_MODEL, D_MODEL), l)],
        out_specs=tspec(D_MODEL),
        out_shape=jax.ShapeDtypeStruct((N, D_MODEL), F32),
        compiler_params=_cparams(("arbitrary",)),
        name="mix_out",
    )(x2, sw["g_mix"], sw["w_g"], ca, ao, py, sw["w_conv_out"], sw["w_attn_out"], sw["w_pool_out"], sw["w_out"])


def _ffn_kernel(x_ref, g_ref, w1_ref, w2_ref, gf_ref, o_ref, *, final):
    x = x_ref[...]
    ms = jnp.mean(x * x, axis=-1, keepdims=True)
    h = (x * lax.rsqrt(ms + RMS_EPS) * g_ref[...]).astype(BF16)
    r = jnp.maximum(jnp.dot(h, w1_ref[...], preferred_element_type=F32), 0.0)
    y = x + jnp.dot((r * r).astype(BF16), w2_ref[...], preferred_element_type=F32)
    if final:
        ms2 = jnp.mean(y * y, axis=-1, keepdims=True)
        y = y * lax.rsqrt(ms2 + RMS_EPS) * gf_ref[...]
    o_ref[...] = y


def _ffn(x2, sw, l, g_final, *, TM, final):
    N = x2.shape[0]
    once = pl.Buffered(1)
    return pl.pallas_call(
        functools.partial(_ffn_kernel, final=final),
        grid=(N // TM,),
        in_specs=[pl.BlockSpec((TM, D_MODEL), lambda i: (i, 0)), _layer_spec((1, D_MODEL), l),
                  _layer_spec((D_MODEL, D_FF), l, pipeline_mode=once),
                  _layer_spec((D_FF, D_MODEL), l, pipeline_mode=once),
                  pl.BlockSpec((1, D_MODEL), lambda i: (0, 0))],
        out_specs=pl.BlockSpec((TM, D_MODEL), lambda i: (i, 0)),
        out_shape=jax.ShapeDtypeStruct((N, D_MODEL), F32),
        compiler_params=_cparams(("arbitrary",)),
        name="ffn",
    )(x2, sw["g_ffn"], sw["w_ff1"], sw["w_ff2"], g_final)


def _stacked_weights(g_mix, w_in, conv_w, conv_b, conv_ln_g, conv_ln_b, w_conv_out, w_attn_out,
                     pool_w, pool_scale, w_pool_out, w_out, g_ffn, w_ff1, w_ff2):
    depth = w_in.shape[0]
    o_ki = C_QI + IDX_HEADS * IDX_DIM
    o_wi = o_ki + IDX_DIM
    o_xc = o_wi + IDX_HEADS
    o_g = o_xc + D_POOL

    def pad_to_tile(a):
        return jnp.pad(a, ((0, 0), (0, 0), (0, LANES - a.shape[2])))

    w_a = jnp.concatenate([w_in[:, :, :o_ki], w_in[:, :, o_xc:o_g], pad_to_tile(w_in[:, :, o_ki:o_wi]),
                           pad_to_tile(w_in[:, :, o_wi:o_xc])], axis=2).astype(BF16)
    pw = jnp.zeros((depth, D_POOL, D_POOL), F32)
    for g in range(len(POOL_WINDOWS)):
        pw = pw.at[:, g * POOL_GC:(g + 1) * POOL_GC, g * POOL_GC:(g + 1) * POOL_GC].set(pool_w[:, g])

    def row(a):
        return a[:, None, :]

    return dict(
        g_mix=row(g_mix), w_a=w_a, w_g=w_in[:, :, o_g:].astype(BF16),
        conv_w=jnp.pad(conv_w, ((0, 0), (0, CONV_PAD - CONV_W), (0, 0))), conv_b=row(conv_b),
        ln_g=row(conv_ln_g), ln_b=row(conv_ln_b),
        pool_w=pw.astype(BF16), pool_scale=row(pool_scale),
        w_conv_out=w_conv_out.astype(BF16), w_attn_out=w_attn_out.astype(BF16),
        w_pool_out=w_pool_out.astype(BF16), w_out=w_out.astype(BF16),
        g_ffn=row(g_ffn), w_ff1=w_ff1.astype(BF16), w_ff2=w_ff2.astype(BF16))


def _tail_state(hist, new, keep):
    return jnp.concatenate([hist, new], axis=1)[:, -keep:]


def kernel(x_prompt, x_sample, cache_k, cache_v, cache_kidx, state_conv, state_pool, page_table, g_mix, w_in,
           conv_w, conv_b, conv_ln_g, conv_ln_b, w_conv_out, w_attn_out, pool_w, pool_scale, w_pool_out, w_out,
           g_ffn, w_ff1, w_ff2, g_final):
    B, T, _ = x_prompt.shape
    NB, TS, _ = x_sample.shape
    depth = w_in.shape[0]
    n_pool = cache_k.shape[1]
    past_len = page_table.shape[1] * PAGE_SIZE
    TM_P = min(PROMPT_TILE, T)
    gf = g_final[None]
    xp = x_prompt
    xs = x_sample
    zero_conv = jnp.zeros((B, CONV_PAD, D_CONV), F32)
    zero_pool = jnp.zeros((B, POOL_PAD, D_POOL), F32)
    cache_k_t = cache_k.transpose(0, 1, 3, 4, 2).reshape(depth, n_pool, KV_DIM, PAGE_SIZE)
    cache_v_t = cache_v.transpose(0, 1, 3, 4, 2).reshape(depth, n_pool, KV_DIM, PAGE_SIZE)
    cache_kidx_t = cache_kidx.transpose(0, 1, 3, 2)
    sw = _stacked_weights(g_mix, w_in, conv_w, conv_b, conv_ln_g, conv_ln_b, w_conv_out, w_attn_out,
                          pool_w, pool_scale, w_pool_out, w_out, g_ffn, w_ff1, w_ff2)
    tables_p = _rope_tables(0, T)
    tables_s = _rope_tables(past_len, TS)
    hist_conv_s = jnp.pad(state_conv, ((0, 0), (0, 0), (CONV_PAD - CONV_HIST, 0), (0, 0)))
    hist_pool_s = jnp.pad(state_pool, ((0, 0), (0, 0), (POOL_PAD - POOL_HIST, 0), (0, 0)))
    st_p = [[] for _ in range(5)]
    st_s = [[] for _ in range(5)]
    for l in range(depth):
        final = l == depth - 1

        (k, v, ki, u, xc, kibf, q, _, _, ca, py, qit, wq, kt, va) = _proj(
            xp, 0, tables_p, zero_conv, zero_pool, sw, l, G=1, TM=TM_P, adt=BF16, attn=True)
        ao = _attend_prompt(q, qit, wq, kibf, kt, va, CKS=TM_P)
        x2 = _mix_out(xp.reshape(B * T, D_MODEL), ca.reshape(B * T, D_CONV), ao.reshape(B * T, D_ATTN),
                      py.reshape(B * T, D_POOL), sw, l, TM=TM_P)
        xp = _ffn(x2, sw, l, gf, TM=TM_P, final=final).reshape(B, T, D_MODEL)
        st_p[0].append(k.reshape(B, T, N_KV_HEADS, HEAD_DIM))
        st_p[1].append(v.reshape(B, T, N_KV_HEADS, HEAD_DIM))
        st_p[2].append(ki)
        st_p[3].append(u[:, T - CONV_HIST:])
        st_p[4].append(xc[:, T - POOL_HIST:])

        (k, v, ki, u, xc, _, q, qi, wi, ca, py) = _proj(
            xs, past_len, tables_s, hist_conv_s[l], hist_pool_s[l], sw, l, G=NB, TM=TS, adt=F32, attn=False)
        q, qi, ca, py = q.astype(BF16), qi.astype(BF16), ca.astype(BF16), py.astype(BF16)
        qi_rows = qi.reshape(NB, TS, IDX_HEADS, IDX_DIM).transpose(0, 2, 1, 3).reshape(NB, IDX_HEADS * TS, IDX_DIM)
        w_rows = jnp.broadcast_to(wi[:, :, :IDX_HEADS].transpose(0, 2, 1).reshape(NB, IDX_HEADS * TS, 1),
                                  (NB, IDX_HEADS * TS, LANES))
        qh = q.reshape(NB, TS, N_HEADS, HEAD_DIM).transpose(0, 2, 1, 3)
        head_kv = (jnp.arange(N_HEADS) // GROUP)[None, :, None, None]
        q_rows = jnp.concatenate([jnp.where(head_kv == n, qh, jnp.zeros_like(qh)) for n in range(N_KV_HEADS)],
                                 axis=-1).reshape(NB, N_HEADS * TS, KV_DIM)
        pad_rows = ((0, 0), (0, PAGE_SIZE - TS), (0, 0))
        pages_per_step = min(PAGES_PER_STEP, page_table.shape[1])

        def new_page_t(a):
            return jnp.pad(a, pad_rows).transpose(0, 2, 1)

        sc, thr = _sample_scores(page_table, cache_kidx_t, l, qi_rows, w_rows, new_page_t(ki), P=pages_per_step)
        o_rows = _sample_attend(page_table, cache_k_t, cache_v_t, l, sc, thr, q_rows, new_page_t(k), new_page_t(v),
                                P=pages_per_step)
        o5 = o_rows.reshape(NB, N_KV_HEADS, GROUP, TS, N_KV_HEADS, HEAD_DIM)
        ao = jnp.stack([o5[:, n, :, :, n, :] for n in range(N_KV_HEADS)], axis=1)
        ao = ao.reshape(NB, N_HEADS, TS, HEAD_DIM).transpose(0, 2, 1, 3).reshape(NB * TS, D_ATTN).astype(BF16)
        x2 = _mix_out(xs.reshape(NB * TS, D_MODEL), ca.reshape(NB * TS, D_CONV), ao,
                      py.reshape(NB * TS, D_POOL), sw, l, TM=NB * TS)
        xs = _ffn(x2, sw, l, gf, TM=NB * TS, final=final).reshape(NB, TS, D_MODEL)
        st_s[0].append(k.reshape(NB, TS, N_KV_HEADS, HEAD_DIM))
        st_s[1].append(v.reshape(NB, TS, N_KV_HEADS, HEAD_DIM))
        st_s[2].append(ki)
        st_s[3].append(_tail_state(state_conv[l], u, CONV_HIST))
        st_s[4].append(_tail_state(state_pool[l], xc, POOL_HIST))

    sp = [jnp.stack(a, axis=0) for a in st_p]
    ss = [jnp.stack(a, axis=0) for a in st_s]
    return (xp, xs, sp[0], sp[1], sp[2], sp[3], sp[4], ss[0], ss[1], ss[2], ss[3], ss[4])
```

```python
import functools

import jax
import jax.numpy as jnp
from jax import lax
from jax.experimental import pallas as pl
from jax.experimental.pallas import tpu as pltpu

F32 = jnp.float32
BF16 = jnp.bfloat16

D_MODEL = 1024
D_CONV = 256
CONV_W = 31
CONV_HIST = CONV_W - 1
HEAD_DIM = 64
D_ATTN = 512
N_HEADS = 8
N_KV_HEADS = 2
KV_DIM = N_KV_HEADS * HEAD_DIM
GROUP = N_HEADS // N_KV_HEADS
IDX_HEADS = 8
IDX_DIM = 64
TOPK_MAX = 256
D_POOL = 256
POOL_WINDOWS = (2, 4, 8, 16)
POOL_GC = D_POOL // len(POOL_WINDOWS)
POOL_HIST = max(POOL_WINDOWS) - 1
N_BRANCH = 3
D_FF = 4 * D_MODEL
ROPE_THETA = 10000.0
RMS_EPS = 1e-6
LN_EPS = 1e-5
PAGE_SIZE = 128

LANES = 128
SUBLANES = 8
PROMPT_TILE = 512
PAGES_PER_STEP = 32
TIE_CHUNK = 256
BISECT_WARM = 14
BISECT_STEPS = 2
CHUNKS_PER_TRIP = 4
SCORE_CHUNKS_PER_TRIP = 8
CONV_PAD = 32
POOL_PAD = 16
Q_SCALE = HEAD_DIM ** -0.5 * 1.4426950408889634
BIG = 3.0e38
NEG_INF = float("-inf")

C_AIN, C_AGATE, C_Q, C_K, C_V, C_QI, C_XC, C_KI, C_WI, D_A = 0, 256, 512, 1024, 1152, 1280, 1792, 2048, 2176, 2304

VMEM_LIMIT = 56 * 1024 * 1024


def _cparams(sem):
    return pltpu.CompilerParams(dimension_semantics=sem, vmem_limit_bytes=VMEM_LIMIT)


def _layer_spec(shape, l, **kw):
    n = len(shape)
    return pl.BlockSpec((None,) + tuple(shape), lambda *_: (l,) + (0,) * n, **kw)


def _proj_kernel(x_ref, g_ref, w_ref, cos_ref, sin_ref, chist_ref, phist_ref, convw_ref, convb_ref,
                 lng_ref, lnb_ref, poolw_ref, pscale_ref,
                 k_ref, v_ref, ki_ref, u_ref, xc_ref, kibf_ref, q_ref, qi_ref, wi_ref, ca_ref, py_ref,
                 *rest, G, TM, pos0, n_tiles, attn):
    if attn:
        qit_ref, wq_ref, kt_ref, va_ref = rest[:4]
        rest = rest[4:]
    cs_ref, ps_ref, p2_ref, p4_ref, p8_ref, p16_ref = rest
    j = pl.program_id(1)
    R = G * TM
    x = x_ref[...].reshape(R, D_MODEL)
    ms = jnp.mean(x * x, axis=-1, keepdims=True)
    h = (x * lax.rsqrt(ms + RMS_EPS) * g_ref[...]).astype(BF16)
    z = jnp.dot(h, w_ref[...], preferred_element_type=F32)

    cos = cos_ref[...]
    sin = sin_ref[...]
    if G > 1:
        cos = jnp.concatenate([cos] * G, axis=0)
        sin = jnp.concatenate([sin] * G, axis=0)

    def rope(t):
        n = t.shape[1] // LANES
        c = jnp.concatenate([cos] * n, axis=1) if n > 1 else cos
        s = jnp.concatenate([sin] * n, axis=1) if n > 1 else sin
        lane = lax.broadcasted_iota(jnp.int32, t.shape, 1)
        first = (lane & (HEAD_DIM - 1)) < HEAD_DIM // 2
        sw = jnp.where(first, pltpu.roll(t, t.shape[1] - HEAD_DIM // 2, 1), pltpu.roll(t, HEAD_DIM // 2, 1))
        return t * c + sw * s

    q = rope(z[:, C_Q:C_Q + D_ATTN]) * Q_SCALE
    k = rope(z[:, C_K:C_K + KV_DIM])
    v = z[:, C_V:C_V + KV_DIM]
    qi = rope(z[:, C_QI:C_QI + IDX_HEADS * IDX_DIM])
    ki = rope(z[:, C_KI:C_KI + LANES])[:, :IDX_DIM]
    wi = z[:, C_WI:C_WI + LANES] * (IDX_HEADS ** -0.5 * IDX_DIM ** -0.5)

    k_ref[...] = k.reshape(G, TM, KV_DIM)
    v_ref[...] = v.reshape(G, TM, KV_DIM)
    ki_ref[...] = ki.reshape(G, TM, IDX_DIM)
    kibf_ref[...] = ki.reshape(G, TM, IDX_DIM).astype(kibf_ref.dtype)
    q_ref[...] = q.reshape(G, TM, D_ATTN).astype(q_ref.dtype)
    qi_ref[...] = qi.reshape(G, TM, IDX_HEADS * IDX_DIM).astype(qi_ref.dtype)
    wi_ref[...] = wi.reshape(G, TM, LANES)
    if attn:
        for b in range(TM // LANES):
            blk = slice(b * LANES, (b + 1) * LANES)
            qi_t = qi[blk, :].T
            qit_ref[b] = jnp.concatenate([qi_t[hh * IDX_DIM:(hh + 1) * IDX_DIM, :] for hh in range(IDX_HEADS)],
                                         axis=1).astype(BF16)
            wi_t = wi[blk, :].T
            wq_ref[b] = jnp.concatenate([wi_t[hh:hh + 1, :] for hh in range(IDX_HEADS)], axis=1)
        k_t = k.T
        lane = lax.broadcasted_iota(jnp.int32, (TM, KV_DIM), 1)
        tail = jnp.where(lane == HEAD_DIM, 1.0, 0.0)
        for n in range(N_KV_HEADS):
            kt_ref[0, n] = k_t[n * HEAD_DIM:(n + 1) * HEAD_DIM, :].astype(BF16)
            vn = v if n == 0 else pltpu.roll(v, KV_DIM - n * HEAD_DIM, 1)
            va_ref[0, n] = jnp.where(lane < HEAD_DIM, vn, tail).astype(BF16)

    a_in = z[:, C_AIN:C_AIN + D_CONV]
    a_gate = z[:, C_AGATE:C_AGATE + D_CONV]
    u = (a_in * jax.nn.sigmoid(a_gate)).reshape(G, TM, D_CONV)
    u_ref[...] = u

    @pl.when(j == 0)
    def _():
        cs_ref[:, 0:CONV_PAD, :] = chist_ref[...]
        ps_ref[:, 0:POOL_PAD, :] = phist_ref[...]

    cs_ref[:, CONV_PAD:CONV_PAD + TM, :] = u
    off = CONV_PAD - CONV_HIST
    rc = min(TM, 64)
    gc = min(G, 8)
    for g0 in range(0, G, gc):
        for r0 in range(0, TM, rc):
            acc = jnp.zeros((gc, rc, D_CONV), F32) + convb_ref[...][None]
            for t in range(CONV_W):
                acc = acc + convw_ref[t:t + 1, :][None] * cs_ref[g0:g0 + gc, off + t + r0:off + t + r0 + rc, :]
            mu = jnp.mean(acc, axis=-1, keepdims=True)
            var = jnp.mean(jnp.square(acc - mu), axis=-1, keepdims=True)
            y = (acc - mu) * lax.rsqrt(var + LN_EPS) * lng_ref[...][None] + lnb_ref[...][None]
            ca_ref[g0:g0 + gc, r0:r0 + rc, :] = (y * jax.nn.sigmoid(y)).astype(ca_ref.dtype)
    if n_tiles > 1:
        cs_ref[:, 0:CONV_PAD, :] = cs_ref[:, TM:TM + CONV_PAD, :]

    xc = z[:, C_XC:C_XC + D_POOL].reshape(G, TM, D_POOL)
    xc_ref[...] = xc
    ps_ref[:, POOL_PAD:POOL_PAD + TM, :] = xc
    RR = POOL_PAD + TM
    p2_ref[:, 1:RR, :] = ps_ref[:, 1:RR, :] + ps_ref[:, 0:RR - 1, :]
    p4_ref[:, 3:RR, :] = p2_ref[:, 3:RR, :] + p2_ref[:, 1:RR - 2, :]
    p8_ref[:, 7:RR, :] = p4_ref[:, 7:RR, :] + p4_ref[:, 3:RR - 4, :]
    p16_ref[:, 15:RR, :] = p8_ref[:, 15:RR, :] + p8_ref[:, 7:RR - 8, :]
    lane = lax.broadcasted_iota(jnp.int32, (G, TM, D_POOL), 2)
    row = lax.broadcasted_iota(jnp.int32, (G, TM, D_POOL), 1)
    g0m, g1m, g2m = lane < POOL_GC, lane < 2 * POOL_GC, lane < 3 * POOL_GC
    s_tok = slice(POOL_PAD, POOL_PAD + TM)
    wsum = jnp.where(g0m, p2_ref[:, s_tok, :],
                     jnp.where(g1m, p4_ref[:, s_tok, :], jnp.where(g2m, p8_ref[:, s_tok, :], p16_ref[:, s_tok, :])))
    win = jnp.where(g0m, POOL_WINDOWS[0], jnp.where(g1m, POOL_WINDOWS[1],
                                                     jnp.where(g2m, POOL_WINDOWS[2], POOL_WINDOWS[3])))
    cnt = jnp.minimum(pos0 + j * TM + row + 1, win).astype(F32)
    d = (wsum / cnt - xc).reshape(R, D_POOL).astype(BF16)
    py = jnp.dot(d, poolw_ref[...], preferred_element_type=F32) * pscale_ref[...]
    py_ref[...] = py.reshape(G, TM, D_POOL).astype(py_ref.dtype)
    if n_tiles > 1:
        ps_ref[:, 0:POOL_PAD, :] = ps_ref[:, TM:TM + POOL_PAD, :]


def _rope_tables(pos0, T):
    half = HEAD_DIM // 2
    pos = (pos0 + jnp.arange(T)).astype(F32)
    inv = ROPE_THETA ** (-jnp.arange(half, dtype=F32) / half)
    ang = pos[:, None] * inv[None, :]
    cos = jnp.tile(jnp.cos(ang), (1, 2 * LANES // HEAD_DIM))
    sin = jnp.tile(jnp.concatenate([-jnp.sin(ang), jnp.sin(ang)], axis=1), (1, LANES // HEAD_DIM))
    return cos, sin


def _proj(x3, pos0, tables, chist, phist, sw, l, *, G, TM, adt, attn):
    NS, T, _ = x3.shape
    n_tiles = T // TM
    cos, sin = tables

    def tok(c, dt=F32):
        return jax.ShapeDtypeStruct((NS, T, c), dt)

    def tspec(c):
        return pl.BlockSpec((G, TM, c), lambda s, j: (s, j, 0))

    out_cols = [(KV_DIM, F32), (KV_DIM, F32), (IDX_DIM, F32), (D_CONV, F32), (D_POOL, F32),
                (IDX_DIM, adt), (D_ATTN, adt), (IDX_HEADS * IDX_DIM, adt), (LANES, F32), (D_CONV, adt), (D_POOL, adt)]
    out_specs = [tspec(c) for c, _ in out_cols]
    out_shape = [tok(c, dt) for c, dt in out_cols]
    if attn:
        assert G == 1 and TM % LANES == 0
        nb = TM // LANES
        out_specs += [pl.BlockSpec((None, nb, IDX_DIM, IDX_HEADS * LANES), lambda s, j: (s, j, 0, 0)),
                      pl.BlockSpec((None, nb, 1, IDX_HEADS * LANES), lambda s, j: (s, j, 0, 0)),
                      pl.BlockSpec((None, 1, N_KV_HEADS, HEAD_DIM, TM), lambda s, j: (s, j, 0, 0, 0)),
                      pl.BlockSpec((None, 1, N_KV_HEADS, TM, LANES), lambda s, j: (s, j, 0, 0, 0))]
        out_shape += [jax.ShapeDtypeStruct((NS, T // LANES, IDX_DIM, IDX_HEADS * LANES), BF16),
                      jax.ShapeDtypeStruct((NS, T // LANES, 1, IDX_HEADS * LANES), F32),
                      jax.ShapeDtypeStruct((NS, n_tiles, N_KV_HEADS, HEAD_DIM, TM), BF16),
                      jax.ShapeDtypeStruct((NS, n_tiles, N_KV_HEADS, TM, LANES), BF16)]
    kern = functools.partial(_proj_kernel, G=G, TM=TM, pos0=pos0, n_tiles=n_tiles, attn=attn)
    return pl.pallas_call(
        kern,
        grid=(NS // G, n_tiles),
        in_specs=[tspec(D_MODEL), _layer_spec((1, D_MODEL), l), _layer_spec((D_MODEL, D_A), l),
                  pl.BlockSpec((TM, LANES), lambda s, j: (j, 0)), pl.BlockSpec((TM, LANES), lambda s, j: (j, 0)),
                  pl.BlockSpec((G, CONV_PAD, D_CONV), lambda s, j: (s, 0, 0)),
                  pl.BlockSpec((G, POOL_PAD, D_POOL), lambda s, j: (s, 0, 0)),
                  _layer_spec((CONV_PAD, D_CONV), l), _layer_spec((1, D_CONV), l), _layer_spec((1, D_CONV), l),
                  _layer_spec((1, D_CONV), l), _layer_spec((D_POOL, D_POOL), l), _layer_spec((1, D_POOL), l)],
        out_specs=out_specs,
        out_shape=out_shape,
        scratch_shapes=[pltpu.VMEM((G, CONV_PAD + TM, D_CONV), F32)] +
                       [pltpu.VMEM((G, POOL_PAD + TM, D_POOL), F32)] * 5,
        compiler_params=_cparams(("arbitrary", "arbitrary")),
        name="proj",
    )(x3, sw["g_mix"], sw["w_a"], cos, sin, chist, phist, sw["conv_w"], sw["conv_b"], sw["ln_g"], sw["ln_b"],
      sw["pool_w"], sw["pool_scale"])


def _for_chunks(n, body, carry, per_trip=CHUNKS_PER_TRIP):
    carry = lax.fori_loop(0, n // per_trip, lambda j, t: body(per_trip * j, per_trip, t), carry)
    done = (n // per_trip) * per_trip
    size = per_trip // 2
    while size >= 1:
        take = (n - done) >= size
        carry = lax.cond(take, functools.partial(body, done, size), lambda t: t, carry)
        done = done + jnp.where(take, size, 0)
        size //= 2
    return carry


def _kth_bisect(count, smin, smax, nvalid, kk, steps, warm=0, zero_counts=None):
    zero = jnp.zeros_like(smin)
    c_ge0, c_gt0 = zero_counts if zero_counts is not None else (count(zero, False), count(zero, True))
    pos = c_gt0 >= kk
    neg = c_ge0 < kk
    lo0 = jnp.where(neg, smin, zero)
    clo0 = jnp.where(neg, nvalid, c_ge0)
    hi0 = jnp.where(pos, BIG, zero)
    chi0 = jnp.where(pos, zero, jnp.where(neg, c_ge0, c_gt0))
    cand0 = jnp.where(pos, smax, smin * 0.5)
    small = nvalid <= kk
    done0 = (small | jnp.logical_not(pos | neg)).astype(F32)
    init = (jnp.where(small, smin, lo0), hi0, jnp.where(small, nvalid, clo0), chi0, cand0, done0)

    def count_ge(cand):
        return count(cand, False)

    def cond(st):
        return jnp.min(st[5]) < 0.5

    def body(st):
        for _ in range(steps):
            st = step(st)
        return st

    def step(st):
        lo, hi, clo, chi, cand, done = st
        c = count_ge(cand)
        act = done < 0.5
        ge = c >= kk
        up = act & ge
        dn = act & jnp.logical_not(ge)
        lo = jnp.where(up, cand, lo)
        clo = jnp.where(up, c, clo)
        hi = jnp.where(dn, cand, hi)
        chi = jnp.where(dn, c, chi)
        mid = lo * 0.5 + hi * 0.5
        inside = (mid > lo) & (mid < hi)
        fin = (clo == kk) | jnp.logical_not(inside)
        done = jnp.where(fin, 1.0, done)
        return lo, hi, clo, chi, mid, done

    if warm:
        init = lax.fori_loop(0, warm, lambda _, st: step(st), init)
    lo, _, clo, chi, _, _ = lax.while_loop(cond, body, init)
    return lo, clo, chi


def _attend_prompt_kernel(qit_ref, wq_ref, q_ref, ki_ref, kt_ref, va_ref, o_ref, sc_ref, acc_ref, m_ref, *,
                          CKS, CK, kk):
    i = pl.program_id(1)
    QB = LANES
    nchs = (i * QB + QB + CKS - 1) // CKS
    nch = nchs * (CKS // CK)
    qit = qit_ref[...]
    wq = wq_ref[...]
    qcol = i * QB + lax.broadcasted_iota(jnp.int32, (CKS, QB), 1)
    krow0 = lax.broadcasted_iota(jnp.int32, (CKS, QB), 0)

    def score_chunk(c, carry, causal):
        smax, smin, n_ge0, n_gt0 = carry
        s = jnp.dot(ki_ref[c], qit, preferred_element_type=F32)
        s = jnp.maximum(s, 0.0) * wq
        tot = s[:, 0:QB]
        for hh in range(1, IDX_HEADS):
            tot = tot + s[:, hh * QB:(hh + 1) * QB]
        if causal:
            valid = (krow0 + c * CKS) <= qcol
            masked = jnp.where(valid, tot, NEG_INF)
            lowest = jnp.where(valid, tot, BIG)
        else:
            masked = lowest = tot
        sc_ref[pl.ds(pl.multiple_of(c * CKS, CKS), CKS), :] = masked
        smax = jnp.maximum(smax, jnp.max(masked.reshape(CKS // SUBLANES, SUBLANES, QB), axis=0))
        smin = jnp.minimum(smin, jnp.min(lowest.reshape(CKS // SUBLANES, SUBLANES, QB), axis=0))
        n_ge0 = n_ge0 + jnp.sum(jnp.where(masked >= 0.0, 1.0, 0.0).reshape(CKS // SUBLANES, SUBLANES, QB), axis=0)
        n_gt0 = n_gt0 + jnp.sum(jnp.where(masked > 0.0, 1.0, 0.0).reshape(CKS // SUBLANES, SUBLANES, QB), axis=0)
        return smax, smin, n_ge0, n_gt0

    def score_run(first, count, carry):
        for u in range(count):
            carry = score_chunk(first + u, carry, causal=False)
        return carry

    carry = _for_chunks(nchs - 1, score_run,
                        (jnp.full((SUBLANES, QB), NEG_INF, F32), jnp.full((SUBLANES, QB), BIG, F32),
                         jnp.zeros((SUBLANES, QB), F32), jnp.zeros((SUBLANES, QB), F32)),
                        per_trip=SCORE_CHUNKS_PER_TRIP)
    smax, smin, n_ge0, n_gt0 = score_chunk(nchs - 1, carry, causal=True)
    smax = jnp.max(smax, axis=0, keepdims=True)
    smin = jnp.min(smin, axis=0, keepdims=True)
    zero_counts = (jnp.sum(n_ge0, axis=0, keepdims=True), jnp.sum(n_gt0, axis=0, keepdims=True))

    n_acc = 4
    rows = CKS // n_acc

    def count(cand, strict):
        def body(c, accs):
            base = pl.multiple_of(c * CKS, CKS)
            out = []
            for a in range(n_acc):
                blk = sc_ref[pl.ds(base + a * rows, rows), :]
                hit = (blk > cand) if strict else (blk >= cand)
                ones = jnp.where(hit, 1.0, 0.0).reshape(rows // SUBLANES, SUBLANES, QB)
                out.append(accs[a] + jnp.sum(ones, axis=0))
            return tuple(out)
        def run(first, count, accs):
            for u in range(count):
                accs = body(first + u, accs)
            return accs
        accs = _for_chunks(nchs, run, tuple(jnp.zeros((SUBLANES, QB), F32) for _ in range(n_acc)))
        return jnp.sum((accs[0] + accs[1]) + (accs[2] + accs[3]), axis=0, keepdims=True)

    nvalid = (i * QB + lax.broadcasted_iota(jnp.int32, (1, QB), 1) + 1).astype(F32)
    lo, clo, chi = _kth_bisect(count, smin, smax, nvalid, float(kk), steps=BISECT_STEPS, warm=BISECT_WARM,
                               zero_counts=zero_counts)

    fix = (clo > kk) & (nvalid > kk)

    @pl.when(jnp.max(fix.astype(F32)) > 0.5)
    def _():
        keep = kk - chi
        TF = min(TIE_CHUNK, CKS)
        tri = (lax.broadcasted_iota(jnp.int32, (TF, TF), 1)
               <= lax.broadcasted_iota(jnp.int32, (TF, TF), 0)).astype(BF16)

        def body(c, run):
            rows_c = pl.ds(pl.multiple_of(c * TF, TF), TF)
            blk = sc_ref[rows_c, :]
            tie = jnp.where(fix & (blk == lo), 1.0, 0.0)
            rank = run + jnp.dot(tri, tie.astype(BF16), preferred_element_type=F32)
            sc_ref[rows_c, :] = jnp.where((tie > 0.5) & (rank > keep), NEG_INF, blk)
            return run + jnp.sum(tie, axis=0, keepdims=True)
        lax.fori_loop(0, nchs * (CKS // TF), body, jnp.zeros((1, QB), F32))

    GQ = GROUP * QB
    q32 = q_ref[...].astype(F32)
    qn = [jnp.concatenate([q32[:, hh * HEAD_DIM:(hh + 1) * HEAD_DIM] for hh in range(n * GROUP, (n + 1) * GROUP)],
                          axis=0).astype(BF16) for n in range(N_KV_HEADS)]
    acc_ref[...] = jnp.zeros_like(acc_ref)
    m_ref[...] = jnp.full_like(m_ref, -BIG)
    reps = CK // LANES

    def att_unit(first, count, carry):
        chunks = [first + u for u in range(count)]
        bias_t = [jnp.where(sc_ref[pl.ds(pl.multiple_of(c * CK, CK), CK), :] >= lo, 0.0, NEG_INF).T
                  for c in chunks]
        for n in range(N_KV_HEADS):
            ss = []
            for c, bt in zip(chunks, bias_t):
                s = jnp.dot(qn[n], kt_ref[c, n], preferred_element_type=F32)
                ss.append((s.reshape(GROUP, QB, CK) + bt[None]).reshape(GQ, CK))
            m_old = m_ref[n]
            mnew = m_old
            for s in ss:
                mnew = jnp.maximum(mnew, jnp.max(s, axis=1, keepdims=True))
            mrep = jnp.concatenate([mnew] * reps, axis=1)
            pv = None
            for c, s in zip(chunks, ss):
                part = jnp.dot(jnp.exp2(s - mrep).astype(BF16), va_ref[c, n], preferred_element_type=F32)
                pv = part if pv is None else pv + part
            acc_ref[n] = jnp.exp2(m_old - mnew) * acc_ref[n] + pv
            m_ref[n] = mnew
        return carry

    _for_chunks(nch, att_unit, 0)

    outs = []
    for n in range(N_KV_HEADS):
        a = acc_ref[n]
        o = a[:, :HEAD_DIM] / a[:, HEAD_DIM:HEAD_DIM + 1]
        outs += [o[g * QB:(g + 1) * QB, :] for g in range(GROUP)]
    o_ref[...] = jnp.concatenate(outs, axis=1).astype(BF16)


def _attend_prompt(q, qit, wq, kibf, kt, va, *, CKS):
    B, T, _ = q.shape
    QB = LANES
    nq = T // QB
    nc, CK = kt.shape[1], kt.shape[4]
    ncs = T // CKS
    kk = min(TOPK_MAX, T // 4)
    kib = kibf.reshape(B, ncs, CKS, IDX_DIM)

    kern = functools.partial(_attend_prompt_kernel, CKS=CKS, CK=CK, kk=kk)
    return pl.pallas_call(
        kern,
        grid=(B, nq),
        in_specs=[pl.BlockSpec((None, None, IDX_DIM, 8 * QB), lambda b, i: (b, i, 0, 0)),
                  pl.BlockSpec((None, None, 1, 8 * QB), lambda b, i: (b, i, 0, 0)),
                  pl.BlockSpec((None, QB, D_ATTN), lambda b, i: (b, i, 0)),
                  pl.BlockSpec((None, ncs, CKS, IDX_DIM), lambda b, i: (b, 0, 0, 0)),
                  pl.BlockSpec((None, nc, N_KV_HEADS, HEAD_DIM, CK), lambda b, i: (b, 0, 0, 0, 0)),
                  pl.BlockSpec((None, nc, N_KV_HEADS, CK, LANES), lambda b, i: (b, 0, 0, 0, 0))],
        out_specs=pl.BlockSpec((None, QB, D_ATTN), lambda b, i: (b, i, 0)),
        out_shape=jax.ShapeDtypeStruct((B, T, D_ATTN), BF16),
        scratch_shapes=[pltpu.VMEM((T, QB), F32), pltpu.VMEM((N_KV_HEADS, GROUP * QB, LANES), F32),
                        pltpu.VMEM((N_KV_HEADS, GROUP * QB, LANES), F32)],
        compiler_params=_cparams(("arbitrary", "arbitrary")),
        name="attend_prompt",
    )(qit, wq, q, kib, kt, va)


def _sample_score_kernel(pt_ref, *refs, P, n_steps, past_len, kk):
    pages = refs[:P]
    qi_ref, wb_ref, kin_ref, sc_ref, thr_ref = refs[P:P + 5]
    j = pl.program_id(1)
    TQ = sc_ref.shape[1]
    NBLK = sc_ref.shape[0]
    qi = qi_ref[...]
    wb = wb_ref[...]

    def block_scores(kpage_t):
        s = jnp.dot(qi, kpage_t.astype(BF16), preferred_element_type=F32)
        s = jnp.maximum(s, 0.0) * wb
        return jnp.sum(s.reshape(IDX_HEADS, TQ, LANES), axis=0)

    for r in range(P):
        sc_ref[j * P + r] = block_scores(pages[r][...])

    @pl.when(j == n_steps - 1)
    def _():
        tot = block_scores(kin_ref[...])
        qrow = lax.broadcasted_iota(jnp.int32, (TQ, LANES), 0)
        kcol = lax.broadcasted_iota(jnp.int32, (TQ, LANES), 1)
        sc_ref[NBLK - 1] = jnp.where(kcol <= qrow, tot, NEG_INF)

        def lane_bcast(a):
            return jnp.broadcast_to(a, (TQ, LANES))

        sc = sc_ref[...]
        smax = lane_bcast(jnp.max(jnp.max(sc, axis=0), axis=1, keepdims=True))
        smin = lane_bcast(jnp.min(jnp.min(jnp.where(sc > NEG_INF, sc, BIG), axis=0), axis=1, keepdims=True))
        nvalid = (past_len + qrow + 1).astype(F32)

        def count(cand, strict):
            blk = sc_ref[...]
            hit = (blk > cand[None]) if strict else (blk >= cand[None])
            c = jnp.sum(jnp.where(hit, 1.0, 0.0), axis=0)
            return lane_bcast(jnp.sum(c, axis=1, keepdims=True))

        lo, clo, chi = _kth_bisect(count, smin, smax, nvalid, float(kk), steps=BISECT_STEPS, warm=BISECT_WARM)
        thr_ref[...] = lo

        fix = (clo > kk) & (nvalid > kk)

        @pl.when(jnp.max(fix.astype(F32)) > 0.5)
        def _():
            keep = kk - chi
            tri = (lax.broadcasted_iota(jnp.int32, (LANES, LANES), 0)
                   <= lax.broadcasted_iota(jnp.int32, (LANES, LANES), 1)).astype(BF16)

            def body(c, run):
                blk = sc_ref[c]
                tie = fix & (blk == lo)
                rank = run + jnp.dot(tie.astype(BF16), tri, preferred_element_type=F32)
                sc_ref[c] = jnp.where(tie & (rank > keep), NEG_INF, blk)
                return run + lane_bcast(jnp.sum(tie.astype(F32), axis=1, keepdims=True))
            lax.fori_loop(0, NBLK, body, jnp.zeros((TQ, LANES), F32))


def _sample_scores(page_table, cache_kidx_t, l, qi_rows, w_rows, ki_new_t, *, P):
    NB, n_pages = page_table.shape
    TQ = qi_rows.shape[1] // IDX_HEADS
    n_steps = n_pages // P
    past_len = n_pages * PAGE_SIZE
    kk = min(TOPK_MAX, (past_len + TQ) // 4)
    NBLK = n_pages + 1

    def page_spec(r):
        return pl.BlockSpec((None, None, IDX_DIM, PAGE_SIZE), lambda b, j, pt: (l, pt[b, j * P + r], 0, 0))

    kern = functools.partial(_sample_score_kernel, P=P, n_steps=n_steps, past_len=past_len, kk=kk)
    grid_spec = pltpu.PrefetchScalarGridSpec(
        num_scalar_prefetch=1,
        grid=(NB, n_steps),
        in_specs=[page_spec(r) for r in range(P)] + [
            pl.BlockSpec((None, IDX_HEADS * TQ, IDX_DIM), lambda b, j, pt: (b, 0, 0)),
            pl.BlockSpec((None, IDX_HEADS * TQ, LANES), lambda b, j, pt: (b, 0, 0)),
            pl.BlockSpec((None, IDX_DIM, PAGE_SIZE), lambda b, j, pt: (b, 0, 0))],
        out_specs=[pl.BlockSpec((None, NBLK, TQ, LANES), lambda b, j, pt: (b, 0, 0, 0)),
                   pl.BlockSpec((None, TQ, LANES), lambda b, j, pt: (b, 0, 0))],
    )
    return pl.pallas_call(
        kern,
        grid_spec=grid_spec,
        out_shape=[jax.ShapeDtypeStruct((NB, NBLK, TQ, LANES), F32), jax.ShapeDtypeStruct((NB, TQ, LANES), F32)],
        compiler_params=_cparams(("arbitrary", "arbitrary")),
        name="sample_scores",
    )(page_table, *([cache_kidx_t] * P), qi_rows, w_rows, ki_new_t)


def _sample_attend_kernel(pt_ref, *refs, P, n_steps):
    kpages = refs[:P]
    vpages = refs[P:2 * P]
    sc_ref, scn_ref, thr_ref, q_ref, kn_ref, vn_ref, o_ref, m_ref, l_ref, acc_ref = refs[2 * P:2 * P + 10]
    j = pl.program_id(1)
    qz = q_ref[...]
    thr = thr_ref[...]

    @pl.when(j == 0)
    def _():
        m_ref[...] = jnp.full_like(m_ref, -BIG)
        l_ref[...] = jnp.zeros_like(l_ref)
        acc_ref[...] = jnp.zeros_like(acc_ref)

    def update(kblocks, vblocks, scblocks):
        s_parts = []
        for kb, scb in zip(kblocks, scblocks):
            s = jnp.dot(qz, kb.astype(BF16), preferred_element_type=F32)
            bias = jnp.where(scb >= thr, 0.0, NEG_INF)
            s_parts.append(s + jnp.concatenate([bias] * N_HEADS, axis=0))
        s = jnp.concatenate(s_parts, axis=1) if len(s_parts) > 1 else s_parts[0]
        m_old = m_ref[...]
        mnew = jnp.maximum(m_old, jnp.broadcast_to(jnp.max(s, axis=1, keepdims=True), m_old.shape))
        alpha = jnp.exp2(m_old - mnew)
        p = jnp.exp2(s - jnp.concatenate([mnew] * len(s_parts), axis=1) if len(s_parts) > 1 else s - mnew)
        l_ref[...] = alpha * l_ref[...] + jnp.broadcast_to(jnp.sum(p, axis=1, keepdims=True), m_old.shape)
        m_ref[...] = mnew
        pv = jnp.zeros(acc_ref.shape, F32)
        for r, vb in enumerate(vblocks):
            pv = pv + lax.dot_general(p[:, r * LANES:(r + 1) * LANES].astype(BF16), vb.astype(BF16),
                                      (((1,), (1,)), ((), ())), preferred_element_type=F32)
        acc_ref[...] = alpha * acc_ref[...] + pv

    update([kp[...] for kp in kpages], [vp[...] for vp in vpages], [sc_ref[r] for r in range(P)])

    @pl.when(j == n_steps - 1)
    def _():
        update([kn_ref[...]], [vn_ref[...]], [scn_ref[0]])
        o_ref[...] = acc_ref[...] / l_ref[...]


def _sample_attend(page_table, cache_k_t, cache_v_t, l, sc, thr, q_rows, k_new_t, v_new_t, *, P):
    NB, n_pages = page_table.shape
    R = q_rows.shape[1]
    TQ = R // N_HEADS
    n_steps = n_pages // P

    def page_spec(r):
        return pl.BlockSpec((None, None, KV_DIM, PAGE_SIZE), lambda b, j, pt: (l, pt[b, j * P + r], 0, 0))

    kern = functools.partial(_sample_attend_kernel, P=P, n_steps=n_steps)
    grid_spec = pltpu.PrefetchScalarGridSpec(
        num_scalar_prefetch=1,
        grid=(NB, n_steps),
        in_specs=[page_spec(r) for r in range(P)] + [page_spec(r) for r in range(P)] + [
            pl.BlockSpec((None, P, TQ, LANES), lambda b, j, pt: (b, j, 0, 0)),
            pl.BlockSpec((None, 1, TQ, LANES), lambda b, j, pt: (b, n_pages, 0, 0)),
            pl.BlockSpec((None, TQ, LANES), lambda b, j, pt: (b, 0, 0)),
            pl.BlockSpec((None, R, KV_DIM), lambda b, j, pt: (b, 0, 0)),
            pl.BlockSpec((None, KV_DIM, PAGE_SIZE), lambda b, j, pt: (b, 0, 0)),
            pl.BlockSpec((None, KV_DIM, PAGE_SIZE), lambda b, j, pt: (b, 0, 0))],
        out_specs=pl.BlockSpec((None, R, KV_DIM), lambda b, j, pt: (b, 0, 0)),
        scratch_shapes=[pltpu.VMEM((R, LANES), F32), pltpu.VMEM((R, LANES), F32), pltpu.VMEM((R, KV_DIM), F32)],
    )
    return pl.pallas_call(
        kern,
        grid_spec=grid_spec,
        out_shape=jax.ShapeDtypeStruct((NB, R, KV_DIM), F32),
        compiler_params=_cparams(("arbitrary", "arbitrary")),
        name="sample_attend",
    )(page_table, *([cache_k_t] * P), *([cache_v_t] * P), sc, sc, thr, q_rows, k_new_t, v_new_t)


def _mix_out_kernel(x_ref, g_ref, wg_ref, ca_ref, ao_ref, py_ref, wco_ref, wao_ref, wpo_ref, wo_ref, o_ref):
    x = x_ref[...]
    ms = jnp.mean(x * x, axis=-1, keepdims=True)
    h = (x * lax.rsqrt(ms + RMS_EPS) * g_ref[...]).astype(BF16)
    gates = jax.nn.sigmoid(jnp.dot(h, wg_ref[...], preferred_element_type=F32))
    y_a = jnp.dot(ca_ref[...], wco_ref[...], preferred_element_type=F32)
    y_b = jnp.dot(ao_ref[...], wao_ref[...], preferred_element_type=F32)
    y_c = jnp.dot(py_ref[...], wpo_ref[...], preferred_element_type=F32)
    merged = (gates[:, 0:D_MODEL] * y_a + gates[:, D_MODEL:2 * D_MODEL] * y_b
              + gates[:, 2 * D_MODEL:3 * D_MODEL] * y_c)
    o_ref[...] = x + jnp.dot(merged.astype(BF16), wo_ref[...], preferred_element_type=F32)


def _mix_out(x2, ca, ao, py, sw, l, *, TM):
    N = x2.shape[0]

    def tspec(c):
        return pl.BlockSpec((TM, c), lambda i: (i, 0))

    return pl.pallas_call(
        _mix_out_kernel,
        grid=(N // TM,),
        in_specs=[tspec(D_MODEL), _layer_spec((1, D_MODEL), l), _layer_spec((D_MODEL, N_BRANCH * D_MODEL), l),
                  tspec(D_CONV), tspec(D_ATTN), tspec(D_POOL),
                  _layer_spec((D_CONV, D_MODEL), l), _layer_spec((D_ATTN, D_MODEL), l),
                  _layer_spec((D_POOL, D_MODEL), l), _layer_spec((D_MODEL, D_MODEL), l)],
        out_specs=tspec(D_MODEL),
        out_shape=jax.ShapeDtypeStruct((N, D_MODEL), F32),
        compiler_params=_cparams(("arbitrary",)),
        name="mix_out",
    )(x2, sw["g_mix"], sw["w_g"], ca, ao, py, sw["w_conv_out"], sw["w_attn_out"], sw["w_pool_out"], sw["w_out"])


def _ffn_kernel(x_ref, g_ref, w1_ref, w2_ref, gf_ref, o_ref, *, final):
    x = x_ref[...]
    ms = jnp.mean(x * x, axis=-1, keepdims=True)
    h = (x * lax.rsqrt(ms + RMS_EPS) * g_ref[...]).astype(BF16)
    r = jnp.maximum(jnp.dot(h, w1_ref[...], preferred_element_type=F32), 0.0)
    y = x + jnp.dot((r * r).astype(BF16), w2_ref[...], preferred_element_type=F32)
    if final:
        ms2 = jnp.mean(y * y, axis=-1, keepdims=True)
        y = y * lax.rsqrt(ms2 + RMS_EPS) * gf_ref[...]
    o_ref[...] = y


def _ffn(x2, sw, l, g_final, *, TM, final):
    N = x2.shape[0]
    once = pl.Buffered(1)
    return pl.pallas_call(
        functools.partial(_ffn_kernel, final=final),
        grid=(N // TM,),
        in_specs=[pl.BlockSpec((TM, D_MODEL), lambda i: (i, 0)), _layer_spec((1, D_MODEL), l),
                  _layer_spec((D_MODEL, D_FF), l, pipeline_mode=once),
                  _layer_spec((D_FF, D_MODEL), l, pipeline_mode=once),
                  pl.BlockSpec((1, D_MODEL), lambda i: (0, 0))],
        out_specs=pl.BlockSpec((TM, D_MODEL), lambda i: (i, 0)),
        out_shape=jax.ShapeDtypeStruct((N, D_MODEL), F32),
        compiler_params=_cparams(("arbitrary",)),
        name="ffn",
    )(x2, sw["g_ffn"], sw["w_ff1"], sw["w_ff2"], g_final)


def _stacked_weights(g_mix, w_in, conv_w, conv_b, conv_ln_g, conv_ln_b, w_conv_out, w_attn_out,
                     pool_w, pool_scale, w_pool_out, w_out, g_ffn, w_ff1, w_ff2):
    depth = w_in.shape[0]
    o_ki = C_QI + IDX_HEADS * IDX_DIM
    o_wi = o_ki + IDX_DIM
    o_xc = o_wi + IDX_HEADS
    o_g = o_xc + D_POOL

    def pad_to_tile(a):
        return jnp.pad(a, ((0, 0), (0, 0), (0, LANES - a.shape[2])))

    w_a = jnp.concatenate([w_in[:, :, :o_ki], w_in[:, :, o_xc:o_g], pad_to_tile(w_in[:, :, o_ki:o_wi]),
                           pad_to_tile(w_in[:, :, o_wi:o_xc])], axis=2).astype(BF16)
    pw = jnp.zeros((depth, D_POOL, D_POOL), F32)
    for g in range(len(POOL_WINDOWS)):
        pw = pw.at[:, g * POOL_GC:(g + 1) * POOL_GC, g * POOL_GC:(g + 1) * POOL_GC].set(pool_w[:, g])

    def row(a):
        return a[:, None, :]

    return dict(
        g_mix=row(g_mix), w_a=w_a, w_g=w_in[:, :, o_g:].astype(BF16),
        conv_w=jnp.pad(conv_w, ((0, 0), (0, CONV_PAD - CONV_W), (0, 0))), conv_b=row(conv_b),
        ln_g=row(conv_ln_g), ln_b=row(conv_ln_b),
        pool_w=pw.astype(BF16), pool_scale=row(pool_scale),
        w_conv_out=w_conv_out.astype(BF16), w_attn_out=w_attn_out.astype(BF16),
        w_pool_out=w_pool_out.astype(BF16), w_out=w_out.astype(BF16),
        g_ffn=row(g_ffn), w_ff1=w_ff1.astype(BF16), w_ff2=w_ff2.astype(BF16))


def _tail_state(hist, new, keep):
    return jnp.concatenate([hist, new], axis=1)[:, -keep:]


def kernel(x_prompt, x_sample, cache_k, cache_v, cache_kidx, state_conv, state_pool, page_table, g_mix, w_in,
           conv_w, conv_b, conv_ln_g, conv_ln_b, w_conv_out, w_attn_out, pool_w, pool_scale, w_pool_out, w_out,
           g_ffn, w_ff1, w_ff2, g_final):
    B, T, _ = x_prompt.shape
    NB, TS, _ = x_sample.shape
    depth = w_in.shape[0]
    n_pool = cache_k.shape[1]
    past_len = page_table.shape[1] * PAGE_SIZE
    TM_P = min(PROMPT_TILE, T)
    gf = g_final[None]
    xp = x_prompt
    xs = x_sample
    zero_conv = jnp.zeros((B, CONV_PAD, D_CONV), F32)
    zero_pool = jnp.zeros((B, POOL_PAD, D_POOL), F32)
    cache_k_t = cache_k.transpose(0, 1, 3, 4, 2).reshape(depth, n_pool, KV_DIM, PAGE_SIZE)
    cache_v_t = cache_v.transpose(0, 1, 3, 4, 2).reshape(depth, n_pool, KV_DIM, PAGE_SIZE)
    cache_kidx_t = cache_kidx.transpose(0, 1, 3, 2)
    sw = _stacked_weights(g_mix, w_in, conv_w, conv_b, conv_ln_g, conv_ln_b, w_conv_out, w_attn_out,
                          pool_w, pool_scale, w_pool_out, w_out, g_ffn, w_ff1, w_ff2)
    tables_p = _rope_tables(0, T)
    tables_s = _rope_tables(past_len, TS)
    hist_conv_s = jnp.pad(state_conv, ((0, 0), (0, 0), (CONV_PAD - CONV_HIST, 0), (0, 0)))
    hist_pool_s = jnp.pad(state_pool, ((0, 0), (0, 0), (POOL_PAD - POOL_HIST, 0), (0, 0)))
    st_p = [[] for _ in range(5)]
    st_s = [[] for _ in range(5)]
    for l in range(depth):
        final = l == depth - 1

        (k, v, ki, u, xc, kibf, q, _, _, ca, py, qit, wq, kt, va) = _proj(
            xp, 0, tables_p, zero_conv, zero_pool, sw, l, G=1, TM=TM_P, adt=BF16, attn=True)
        ao = _attend_prompt(q, qit, wq, kibf, kt, va, CKS=TM_P)
        x2 = _mix_out(xp.reshape(B * T, D_MODEL), ca.reshape(B * T, D_CONV), ao.reshape(B * T, D_ATTN),
                      py.reshape(B * T, D_POOL), sw, l, TM=TM_P)
        xp = _ffn(x2, sw, l, gf, TM=TM_P, final=final).reshape(B, T, D_MODEL)
        st_p[0].append(k.reshape(B, T, N_KV_HEADS, HEAD_DIM))
        st_p[1].append(v.reshape(B, T, N_KV_HEADS, HEAD_DIM))
        st_p[2].append(ki)
        st_p[3].append(u[:, T - CONV_HIST:])
        st_p[4].append(xc[:, T - POOL_HIST:])

        (k, v, ki, u, xc, _, q, qi, wi, ca, py) = _proj(
            xs, past_len, tables_s, hist_conv_s[l], hist_pool_s[l], sw, l, G=NB, TM=TS, adt=F32, attn=False)
        q, qi, ca, py = q.astype(BF16), qi.astype(BF16), ca.astype(BF16), py.astype(BF16)
        qi_rows = qi.reshape(NB, TS, IDX_HEADS, IDX_DIM).transpose(0, 2, 1, 3).reshape(NB, IDX_HEADS * TS, IDX_DIM)
        w_rows = jnp.broadcast_to(wi[:, :, :IDX_HEADS].transpose(0, 2, 1).reshape(NB, IDX_HEADS * TS, 1),
                                  (NB, IDX_HEADS * TS, LANES))
        qh = q.reshape(NB, TS, N_HEADS, HEAD_DIM).transpose(0, 2, 1, 3)
        head_kv = (jnp.arange(N_HEADS) // GROUP)[None, :, None, None]
        q_rows = jnp.concatenate([jnp.where(head_kv == n, qh, jnp.zeros_like(qh)) for n in range(N_KV_HEADS)],
                                 axis=-1).reshape(NB, N_HEADS * TS, KV_DIM)
        pad_rows = ((0, 0), (0, PAGE_SIZE - TS), (0, 0))
        pages_per_step = min(PAGES_PER_STEP, page_table.shape[1])

        def new_page_t(a):
            return jnp.pad(a, pad_rows).transpose(0, 2, 1)

        sc, thr = _sample_scores(page_table, cache_kidx_t, l, qi_rows, w_rows, new_page_t(ki), P=pages_per_step)
        o_rows = _sample_attend(page_table, cache_k_t, cache_v_t, l, sc, thr, q_rows, new_page_t(k), new_page_t(v),
                                P=pages_per_step)
        o5 = o_rows.reshape(NB, N_KV_HEADS, GROUP, TS, N_KV_HEADS, HEAD_DIM)
        ao = jnp.stack([o5[:, n, :, :, n, :] for n in range(N_KV_HEADS)], axis=1)
        ao = ao.reshape(NB, N_HEADS, TS, HEAD_DIM).transpose(0, 2, 1, 3).reshape(NB * TS, D_ATTN).astype(BF16)
        x2 = _mix_out(xs.reshape(NB * TS, D_MODEL), ca.reshape(NB * TS, D_CONV), ao,
                      py.reshape(NB * TS, D_POOL), sw, l, TM=NB * TS)
        xs = _ffn(x2, sw, l, gf, TM=NB * TS, final=final).reshape(NB, TS, D_MODEL)
        st_s[0].append(k.reshape(NB, TS, N_KV_HEADS, HEAD_DIM))
        st_s[1].append(v.reshape(NB, TS, N_KV_HEADS, HEAD_DIM))
        st_s[2].append(ki)
        st_s[3].append(_tail_state(state_conv[l], u, CONV_HIST))
        st_s[4].append(_tail_state(state_pool[l], xc, POOL_HIST))

    sp = [jnp.stack(a, axis=0) for a in st_p]
    ss = [jnp.stack(a, axis=0) for a in st_s]
    return (xp, xs, sp[0], sp[1], sp[2], sp[3], sp[4], ss[0], ss[1], ss[2], ss[3], ss[4])
```

```python
import functools

import jax
import jax.numpy as jnp
from jax import lax
from jax.experimental import pallas as pl
from jax.experimental.pallas import tpu as pltpu

F32 = jnp.float32
BF16 = jnp.bfloat16

D_MODEL = 1024
D_CONV = 256
CONV_W = 31
CONV_HIST = CONV_W - 1
HEAD_DIM = 64
D_ATTN = 512
N_HEADS = 8
N_KV_HEADS = 2
KV_DIM = N_KV_HEADS * HEAD_DIM
GROUP = N_HEADS // N_KV_HEADS
IDX_HEADS = 8
IDX_DIM = 64
TOPK_MAX = 256
D_POOL = 256
POOL_WINDOWS = (2, 4, 8, 16)
POOL_GC = D_POOL // len(POOL_WINDOWS)
POOL_HIST = max(POOL_WINDOWS) - 1
N_BRANCH = 3
D_FF = 4 * D_MODEL
ROPE_THETA = 10000.0
RMS_EPS = 1e-6
LN_EPS = 1e-5
PAGE_SIZE = 128

LANES = 128
SUBLANES = 8
PROMPT_TILE = 512
PAGES_PER_STEP = 32
TIE_CHUNK = 256
BISECT_WARM = 16
BISECT_STEPS = 2
CHUNKS_PER_TRIP = 4
SCORE_CHUNKS_PER_TRIP = 8
CONV_PAD = 32
POOL_PAD = 16
Q_SCALE = HEAD_DIM ** -0.5 * 1.4426950408889634
BIG = 3.0e38
NEG_INF = float("-inf")

C_AIN, C_AGATE, C_Q, C_K, C_V, C_QI, C_XC, C_KI, C_WI, D_A = 0, 256, 512, 1024, 1152, 1280, 1792, 2048, 2176, 2304

VMEM_LIMIT = 56 * 1024 * 1024


def _cparams(sem):
    return pltpu.CompilerParams(dimension_semantics=sem, vmem_limit_bytes=VMEM_LIMIT)


def _layer_spec(shape, l, **kw):
    n = len(shape)
    return pl.BlockSpec((None,) + tuple(shape), lambda *_: (l,) + (0,) * n, **kw)


def _proj_kernel(x_ref, g_ref, w_ref, cos_ref, sin_ref, chist_ref, phist_ref, convw_ref, convb_ref,
                 lng_ref, lnb_ref, poolw_ref, pscale_ref,
                 k_ref, v_ref, ki_ref, u_ref, xc_ref, kibf_ref, q_ref, qi_ref, wi_ref, ca_ref, py_ref,
                 *rest, G, TM, pos0, n_tiles, attn):
    if attn:
        qit_ref, wq_ref, kt_ref, va_ref = rest[:4]
        rest = rest[4:]
    cs_ref, ps_ref, p2_ref, p4_ref, p8_ref, p16_ref = rest
    j = pl.program_id(1)
    R = G * TM
    x = x_ref[...].reshape(R, D_MODEL)
    ms = jnp.mean(x * x, axis=-1, keepdims=True)
    h = (x * lax.rsqrt(ms + RMS_EPS) * g_ref[...]).astype(BF16)
    z = jnp.dot(h, w_ref[...], preferred_element_type=F32)

    cos = cos_ref[...]
    sin = sin_ref[...]
    if G > 1:
        cos = jnp.concatenate([cos] * G, axis=0)
        sin = jnp.concatenate([sin] * G, axis=0)

    def rope(t):
        n = t.shape[1] // LANES
        c = jnp.concatenate([cos] * n, axis=1) if n > 1 else cos
        s = jnp.concatenate([sin] * n, axis=1) if n > 1 else sin
        lane = lax.broadcasted_iota(jnp.int32, t.shape, 1)
        first = (lane & (HEAD_DIM - 1)) < HEAD_DIM // 2
        sw = jnp.where(first, pltpu.roll(t, t.shape[1] - HEAD_DIM // 2, 1), pltpu.roll(t, HEAD_DIM // 2, 1))
        return t * c + sw * s

    q = rope(z[:, C_Q:C_Q + D_ATTN]) * Q_SCALE
    k = rope(z[:, C_K:C_K + KV_DIM])
    v = z[:, C_V:C_V + KV_DIM]
    qi = rope(z[:, C_QI:C_QI + IDX_HEADS * IDX_DIM])
    ki = rope(z[:, C_KI:C_KI + LANES])[:, :IDX_DIM]
    wi = z[:, C_WI:C_WI + LANES] * (IDX_HEADS ** -0.5 * IDX_DIM ** -0.5)

    k_ref[...] = k.reshape(G, TM, KV_DIM)
    v_ref[...] = v.reshape(G, TM, KV_DIM)
    ki_ref[...] = ki.reshape(G, TM, IDX_DIM)
    kibf_ref[...] = ki.reshape(G, TM, IDX_DIM).astype(kibf_ref.dtype)
    q_ref[...] = q.reshape(G, TM, D_ATTN).astype(q_ref.dtype)
    qi_ref[...] = qi.reshape(G, TM, IDX_HEADS * IDX_DIM).astype(qi_ref.dtype)
    wi_ref[...] = wi.reshape(G, TM, LANES)
    if attn:
        for b in range(TM // LANES):
            blk = slice(b * LANES, (b + 1) * LANES)
            qi_t = qi[blk, :].T
            qit_ref[b] = jnp.concatenate([qi_t[hh * IDX_DIM:(hh + 1) * IDX_DIM, :] for hh in range(IDX_HEADS)],
                                         axis=1).astype(BF16)
            wi_t = wi[blk, :].T
            wq_ref[b] = jnp.concatenate([wi_t[hh:hh + 1, :] for hh in range(IDX_HEADS)], axis=1)
        k_t = k.T
        lane = lax.broadcasted_iota(jnp.int32, (TM, KV_DIM), 1)
        tail = jnp.where(lane == HEAD_DIM, 1.0, 0.0)
        for n in range(N_KV_HEADS):
            kt_ref[0, n] = k_t[n * HEAD_DIM:(n + 1) * HEAD_DIM, :].astype(BF16)
            vn = v if n == 0 else pltpu.roll(v, KV_DIM - n * HEAD_DIM, 1)
            va_ref[0, n] = jnp.where(lane < HEAD_DIM, vn, tail).astype(BF16)

    a_in = z[:, C_AIN:C_AIN + D_CONV]
    a_gate = z[:, C_AGATE:C_AGATE + D_CONV]
    u = (a_in * jax.nn.sigmoid(a_gate)).reshape(G, TM, D_CONV)
    u_ref[...] = u

    @pl.when(j == 0)
    def _():
        cs_ref[:, 0:CONV_PAD, :] = chist_ref[...]
        ps_ref[:, 0:POOL_PAD, :] = phist_ref[...]

    cs_ref[:, CONV_PAD:CONV_PAD + TM, :] = u
    off = CONV_PAD - CONV_HIST
    rc = min(TM, 64)
    gc = min(G, 8)
    for g0 in range(0, G, gc):
        for r0 in range(0, TM, rc):
            acc = jnp.zeros((gc, rc, D_CONV), F32) + convb_ref[...][None]
            for t in range(CONV_W):
                acc = acc + convw_ref[t:t + 1, :][None] * cs_ref[g0:g0 + gc, off + t + r0:off + t + r0 + rc, :]
            mu = jnp.mean(acc, axis=-1, keepdims=True)
            var = jnp.mean(jnp.square(acc - mu), axis=-1, keepdims=True)
            y = (acc - mu) * lax.rsqrt(var + LN_EPS) * lng_ref[...][None] + lnb_ref[...][None]
            ca_ref[g0:g0 + gc, r0:r0 + rc, :] = (y * jax.nn.sigmoid(y)).astype(ca_ref.dtype)
    if n_tiles > 1:
        cs_ref[:, 0:CONV_PAD, :] = cs_ref[:, TM:TM + CONV_PAD, :]

    xc = z[:, C_XC:C_XC + D_POOL].reshape(G, TM, D_POOL)
    xc_ref[...] = xc
    ps_ref[:, POOL_PAD:POOL_PAD + TM, :] = xc
    RR = POOL_PAD + TM
    p2_ref[:, 1:RR, :] = ps_ref[:, 1:RR, :] + ps_ref[:, 0:RR - 1, :]
    p4_ref[:, 3:RR, :] = p2_ref[:, 3:RR, :] + p2_ref[:, 1:RR - 2, :]
    p8_ref[:, 7:RR, :] = p4_ref[:, 7:RR, :] + p4_ref[:, 3:RR - 4, :]
    p16_ref[:, 15:RR, :] = p8_ref[:, 15:RR, :] + p8_ref[:, 7:RR - 8, :]
    lane = lax.broadcasted_iota(jnp.int32, (G, TM, D_POOL), 2)
    row = lax.broadcasted_iota(jnp.int32, (G, TM, D_POOL), 1)
    g0m, g1m, g2m = lane < POOL_GC, lane < 2 * POOL_GC, lane < 3 * POOL_GC
    s_tok = slice(POOL_PAD, POOL_PAD + TM)
    wsum = jnp.where(g0m, p2_ref[:, s_tok, :],
                     jnp.where(g1m, p4_ref[:, s_tok, :], jnp.where(g2m, p8_ref[:, s_tok, :], p16_ref[:, s_tok, :])))
    win = jnp.where(g0m, POOL_WINDOWS[0], jnp.where(g1m, POOL_WINDOWS[1],
                                                     jnp.where(g2m, POOL_WINDOWS[2], POOL_WINDOWS[3])))
    cnt = jnp.minimum(pos0 + j * TM + row + 1, win).astype(F32)
    d = (wsum / cnt - xc).reshape(R, D_POOL).astype(BF16)
    py = jnp.dot(d, poolw_ref[...], preferred_element_type=F32) * pscale_ref[...]
    py_ref[...] = py.reshape(G, TM, D_POOL).astype(py_ref.dtype)
    if n_tiles > 1:
        ps_ref[:, 0:POOL_PAD, :] = ps_ref[:, TM:TM + POOL_PAD, :]


def _rope_tables(pos0, T):
    half = HEAD_DIM // 2
    pos = (pos0 + jnp.arange(T)).astype(F32)
    inv = ROPE_THETA ** (-jnp.arange(half, dtype=F32) / half)
    ang = pos[:, None] * inv[None, :]
    cos = jnp.tile(jnp.cos(ang), (1, 2 * LANES // HEAD_DIM))
    sin = jnp.tile(jnp.concatenate([-jnp.sin(ang), jnp.sin(ang)], axis=1), (1, LANES // HEAD_DIM))
    return cos, sin


def _proj(x3, pos0, tables, chist, phist, sw, l, *, G, TM, adt, attn):
    NS, T, _ = x3.shape
    n_tiles = T // TM
    cos, sin = tables

    def tok(c, dt=F32):
        return jax.ShapeDtypeStruct((NS, T, c), dt)

    def tspec(c):
        return pl.BlockSpec((G, TM, c), lambda s, j: (s, j, 0))

    out_cols = [(KV_DIM, F32), (KV_DIM, F32), (IDX_DIM, F32), (D_CONV, F32), (D_POOL, F32),
                (IDX_DIM, adt), (D_ATTN, adt), (IDX_HEADS * IDX_DIM, adt), (LANES, F32), (D_CONV, adt), (D_POOL, adt)]
    out_specs = [tspec(c) for c, _ in out_cols]
    out_shape = [tok(c, dt) for c, dt in out_cols]
    if attn:
        assert G == 1 and TM % LANES == 0
        nb = TM // LANES
        out_specs += [pl.BlockSpec((None, nb, IDX_DIM, IDX_HEADS * LANES), lambda s, j: (s, j, 0, 0)),
                      pl.BlockSpec((None, nb, 1, IDX_HEADS * LANES), lambda s, j: (s, j, 0, 0)),
                      pl.BlockSpec((None, 1, N_KV_HEADS, HEAD_DIM, TM), lambda s, j: (s, j, 0, 0, 0)),
                      pl.BlockSpec((None, 1, N_KV_HEADS, TM, LANES), lambda s, j: (s, j, 0, 0, 0))]
        out_shape += [jax.ShapeDtypeStruct((NS, T // LANES, IDX_DIM, IDX_HEADS * LANES), BF16),
                      jax.ShapeDtypeStruct((NS, T // LANES, 1, IDX_HEADS * LANES), F32),
                      jax.ShapeDtypeStruct((NS, n_tiles, N_KV_HEADS, HEAD_DIM, TM), BF16),
                      jax.ShapeDtypeStruct((NS, n_tiles, N_KV_HEADS, TM, LANES), BF16)]
    kern = functools.partial(_proj_kernel, G=G, TM=TM, pos0=pos0, n_tiles=n_tiles, attn=attn)
    return pl.pallas_call(
        kern,
        grid=(NS // G, n_tiles),
        in_specs=[tspec(D_MODEL), _layer_spec((1, D_MODEL), l), _layer_spec((D_MODEL, D_A), l),
                  pl.BlockSpec((TM, LANES), lambda s, j: (j, 0)), pl.BlockSpec((TM, LANES), lambda s, j: (j, 0)),
                  pl.BlockSpec((G, CONV_PAD, D_CONV), lambda s, j: (s, 0, 0)),
                  pl.BlockSpec((G, POOL_PAD, D_POOL), lambda s, j: (s, 0, 0)),
                  _layer_spec((CONV_PAD, D_CONV), l), _layer_spec((1, D_CONV), l), _layer_spec((1, D_CONV), l),
                  _layer_spec((1, D_CONV), l), _layer_spec((D_POOL, D_POOL), l), _layer_spec((1, D_POOL), l)],
        out_specs=out_specs,
        out_shape=out_shape,
        scratch_shapes=[pltpu.VMEM((G, CONV_PAD + TM, D_CONV), F32)] +
                       [pltpu.VMEM((G, POOL_PAD + TM, D_POOL), F32)] * 5,
        compiler_params=_cparams(("arbitrary", "arbitrary")),
        name="proj",
    )(x3, sw["g_mix"], sw["w_a"], cos, sin, chist, phist, sw["conv_w"], sw["conv_b"], sw["ln_g"], sw["ln_b"],
      sw["pool_w"], sw["pool_scale"])


def _for_chunks(n, body, carry, per_trip=CHUNKS_PER_TRIP):
    carry = lax.fori_loop(0, n // per_trip, lambda j, t: body(per_trip * j, per_trip, t), carry)
    done = (n // per_trip) * per_trip
    size = per_trip // 2
    while size >= 1:
        take = (n - done) >= size
        carry = lax.cond(take, functools.partial(body, done, size), lambda t: t, carry)
        done = done + jnp.where(take, size, 0)
        size //= 2
    return carry


def _kth_bisect(count, smin, smax, nvalid, kk, steps, warm=0, zero_counts=None):
    zero = jnp.zeros_like(smin)
    c_ge0, c_gt0 = zero_counts if zero_counts is not None else (count(zero, False), count(zero, True))
    pos = c_gt0 >= kk
    neg = c_ge0 < kk
    lo0 = jnp.where(neg, smin, zero)
    clo0 = jnp.where(neg, nvalid, c_ge0)
    hi0 = jnp.where(pos, BIG, zero)
    chi0 = jnp.where(pos, zero, jnp.where(neg, c_ge0, c_gt0))
    cand0 = jnp.where(pos, smax, smin * 0.5)
    small = nvalid <= kk
    done0 = (small | jnp.logical_not(pos | neg)).astype(F32)
    init = (jnp.where(small, smin, lo0), hi0, jnp.where(small, nvalid, clo0), chi0, cand0, done0)

    def count_ge(cand):
        return count(cand, False)

    def cond(st):
        return jnp.min(st[5]) < 0.5

    def body(st):
        for _ in range(steps):
            st = step(st)
        return st

    def step(st):
        lo, hi, clo, chi, cand, done = st
        c = count_ge(cand)
        act = done < 0.5
        ge = c >= kk
        up = act & ge
        dn = act & jnp.logical_not(ge)
        lo = jnp.where(up, cand, lo)
        clo = jnp.where(up, c, clo)
        hi = jnp.where(dn, cand, hi)
        chi = jnp.where(dn, c, chi)
        mid = lo * 0.5 + hi * 0.5
        inside = (mid > lo) & (mid < hi)
        fin = (clo == kk) | jnp.logical_not(inside)
        done = jnp.where(fin, 1.0, done)
        return lo, hi, clo, chi, mid, done

    if warm:
        init = lax.fori_loop(0, warm, lambda _, st: step(st), init)
    lo, _, clo, chi, _, _ = lax.while_loop(cond, body, init)
    return lo, clo, chi


def _attend_prompt_kernel(qit_ref, wq_ref, q_ref, ki_ref, kt_ref, va_ref, o_ref, sc_ref, acc_ref, m_ref, *,
                          CKS, CK, kk):
    i = pl.program_id(1)
    QB = LANES
    nchs = (i * QB + QB + CKS - 1) // CKS
    nch = nchs * (CKS // CK)
    qit = qit_ref[...]
    wq = wq_ref[...]
    qcol = i * QB + lax.broadcasted_iota(jnp.int32, (CKS, QB), 1)
    krow0 = lax.broadcasted_iota(jnp.int32, (CKS, QB), 0)

    def score_chunk(c, carry, causal):
        smax, smin, n_ge0, n_gt0 = carry
        s = jnp.dot(ki_ref[c], qit, preferred_element_type=F32)
        s = jnp.maximum(s, 0.0) * wq
        tot = s[:, 0:QB]
        for hh in range(1, IDX_HEADS):
            tot = tot + s[:, hh * QB:(hh + 1) * QB]
        if causal:
            valid = (krow0 + c * CKS) <= qcol
            masked = jnp.where(valid, tot, NEG_INF)
            lowest = jnp.where(valid, tot, BIG)
        else:
            masked = lowest = tot
        sc_ref[pl.ds(pl.multiple_of(c * CKS, CKS), CKS), :] = masked
        smax = jnp.maximum(smax, jnp.max(masked.reshape(CKS // SUBLANES, SUBLANES, QB), axis=0))
        smin = jnp.minimum(smin, jnp.min(lowest.reshape(CKS // SUBLANES, SUBLANES, QB), axis=0))
        n_ge0 = n_ge0 + jnp.sum(jnp.where(masked >= 0.0, 1.0, 0.0).reshape(CKS // SUBLANES, SUBLANES, QB), axis=0)
        n_gt0 = n_gt0 + jnp.sum(jnp.where(masked > 0.0, 1.0, 0.0).reshape(CKS // SUBLANES, SUBLANES, QB), axis=0)
        return smax, smin, n_ge0, n_gt0

    def score_run(first, count, carry):
        for u in range(count):
            carry = score_chunk(first + u, carry, causal=False)
        return carry

    carry = _for_chunks(nchs - 1, score_run,
                        (jnp.full((SUBLANES, QB), NEG_INF, F32), jnp.full((SUBLANES, QB), BIG, F32),
                         jnp.zeros((SUBLANES, QB), F32), jnp.zeros((SUBLANES, QB), F32)),
                        per_trip=SCORE_CHUNKS_PER_TRIP)
    smax, smin, n_ge0, n_gt0 = score_chunk(nchs - 1, carry, causal=True)
    smax = jnp.max(smax, axis=0, keepdims=True)
    smin = jnp.min(smin, axis=0, keepdims=True)
    zero_counts = (jnp.sum(n_ge0, axis=0, keepdims=True), jnp.sum(n_gt0, axis=0, keepdims=True))

    n_acc = 4
    rows = CKS // n_acc

    def count(cand, strict):
        def body(c, accs):
            base = pl.multiple_of(c * CKS, CKS)
            out = []
            for a in range(n_acc):
                blk = sc_ref[pl.ds(base + a * rows, rows), :]
                hit = (blk > cand) if strict else (blk >= cand)
                ones = jnp.where(hit, 1.0, 0.0).reshape(rows // SUBLANES, SUBLANES, QB)
                out.append(accs[a] + jnp.sum(ones, axis=0))
            return tuple(out)
        def run(first, count, accs):
            for u in range(count):
                accs = body(first + u, accs)
            return accs
        accs = _for_chunks(nchs, run, tuple(jnp.zeros((SUBLANES, QB), F32) for _ in range(n_acc)))
        return jnp.sum((accs[0] + accs[1]) + (accs[2] + accs[3]), axis=0, keepdims=True)

    nvalid = (i * QB + lax.broadcasted_iota(jnp.int32, (1, QB), 1) + 1).astype(F32)
    lo, clo, chi = _kth_bisect(count, smin, smax, nvalid, float(kk), steps=BISECT_STEPS, warm=BISECT_WARM,
                               zero_counts=zero_counts)

    fix = (clo > kk) & (nvalid > kk)

    @pl.when(jnp.max(fix.astype(F32)) > 0.5)
    def _():
        keep = kk - chi
        TF = min(TIE_CHUNK, CKS)
        tri = (lax.broadcasted_iota(jnp.int32, (TF, TF), 1)
               <= lax.broadcasted_iota(jnp.int32, (TF, TF), 0)).astype(BF16)

        def body(c, run):
            rows_c = pl.ds(pl.multiple_of(c * TF, TF), TF)
            blk = sc_ref[rows_c, :]
            tie = jnp.where(fix & (blk == lo), 1.0, 0.0)
            rank = run + jnp.dot(tri, tie.astype(BF16), preferred_element_type=F32)
            sc_ref[rows_c, :] = jnp.where((tie > 0.5) & (rank > keep), NEG_INF, blk)
            return run + jnp.sum(tie, axis=0, keepdims=True)
        lax.fori_loop(0, nchs * (CKS // TF), body, jnp.zeros((1, QB), F32))

    GQ = GROUP * QB
    q32 = q_ref[...].astype(F32)
    qn = [jnp.concatenate([q32[:, hh * HEAD_DIM:(hh + 1) * HEAD_DIM] for hh in range(n * GROUP, (n + 1) * GROUP)],
                          axis=0).astype(BF16) for n in range(N_KV_HEADS)]
    acc_ref[...] = jnp.zeros_like(acc_ref)
    m_ref[...] = jnp.full_like(m_ref, -BIG)
    reps = CK // LANES

    def att_unit(first, count, carry):
        chunks = [first + u for u in range(count)]
        bias_t = [jnp.where(sc_ref[pl.ds(pl.multiple_of(c * CK, CK), CK), :] >= lo, 0.0, NEG_INF).T
                  for c in chunks]
        for n in range(N_KV_HEADS):
            ss = []
            for c, bt in zip(chunks, bias_t):
                s = jnp.dot(qn[n], kt_ref[c, n], preferred_element_type=F32)
                ss.append((s.reshape(GROUP, QB, CK) + bt[None]).reshape(GQ, CK))
            m_old = m_ref[n]
            mnew = m_old
            for s in ss:
                mnew = jnp.maximum(mnew, jnp.max(s, axis=1, keepdims=True))
            mrep = jnp.concatenate([mnew] * reps, axis=1)
            pv = None
            for c, s in zip(chunks, ss):
                part = jnp.dot(jnp.exp2(s - mrep).astype(BF16), va_ref[c, n], preferred_element_type=F32)
                pv = part if pv is None else pv + part
            acc_ref[n] = jnp.exp2(m_old - mnew) * acc_ref[n] + pv
            m_ref[n] = mnew
        return carry

    _for_chunks(nch, att_unit, 0)

    outs = []
    for n in range(N_KV_HEADS):
        a = acc_ref[n]
        o = a[:, :HEAD_DIM] / a[:, HEAD_DIM:HEAD_DIM + 1]
        outs += [o[g * QB:(g + 1) * QB, :] for g in range(GROUP)]
    o_ref[...] = jnp.concatenate(outs, axis=1).astype(BF16)


def _attend_prompt(q, qit, wq, kibf, kt, va, *, CKS):
    B, T, _ = q.shape
    QB = LANES
    nq = T // QB
    nc, CK = kt.shape[1], kt.shape[4]
    ncs = T // CKS
    kk = min(TOPK_MAX, T // 4)
    kib = kibf.reshape(B, ncs, CKS, IDX_DIM)

    kern = functools.partial(_attend_prompt_kernel, CKS=CKS, CK=CK, kk=kk)
    return pl.pallas_call(
        kern,
        grid=(B, nq),
        in_specs=[pl.BlockSpec((None, None, IDX_DIM, 8 * QB), lambda b, i: (b, i, 0, 0)),
                  pl.BlockSpec((None, None, 1, 8 * QB), lambda b, i: (b, i, 0, 0)),
                  pl.BlockSpec((None, QB, D_ATTN), lambda b, i: (b, i, 0)),
                  pl.BlockSpec((None, ncs, CKS, IDX_DIM), lambda b, i: (b, 0, 0, 0)),
                  pl.BlockSpec((None, nc, N_KV_HEADS, HEAD_DIM, CK), lambda b, i: (b, 0, 0, 0, 0)),
                  pl.BlockSpec((None, nc, N_KV_HEADS, CK, LANES), lambda b, i: (b, 0, 0, 0, 0))],
        out_specs=pl.BlockSpec((None, QB, D_ATTN), lambda b, i: (b, i, 0)),
        out_shape=jax.ShapeDtypeStruct((B, T, D_ATTN), BF16),
        scratch_shapes=[pltpu.VMEM((T, QB), F32), pltpu.VMEM((N_KV_HEADS, GROUP * QB, LANES), F32),
                        pltpu.VMEM((N_KV_HEADS, GROUP * QB, LANES), F32)],
        compiler_params=_cparams(("arbitrary", "arbitrary")),
        name="attend_prompt",
    )(qit, wq, q, kib, kt, va)


def _sample_score_kernel(pt_ref, *refs, P, n_steps, past_len, kk):
    pages = refs[:P]
    qi_ref, wb_ref, kin_ref, sc_ref, thr_ref = refs[P:P + 5]
    j = pl.program_id(1)
    TQ = sc_ref.shape[1]
    NBLK = sc_ref.shape[0]
    qi = qi_ref[...]
    wb = wb_ref[...]

    def block_scores(kpage_t):
        s = jnp.dot(qi, kpage_t.astype(BF16), preferred_element_type=F32)
        s = jnp.maximum(s, 0.0) * wb
        return jnp.sum(s.reshape(IDX_HEADS, TQ, LANES), axis=0)

    for r in range(P):
        sc_ref[j * P + r] = block_scores(pages[r][...])

    @pl.when(j == n_steps - 1)
    def _():
        tot = block_scores(kin_ref[...])
        qrow = lax.broadcasted_iota(jnp.int32, (TQ, LANES), 0)
        kcol = lax.broadcasted_iota(jnp.int32, (TQ, LANES), 1)
        sc_ref[NBLK - 1] = jnp.where(kcol <= qrow, tot, NEG_INF)

        def lane_bcast(a):
            return jnp.broadcast_to(a, (TQ, LANES))

        sc = sc_ref[...]
        smax = lane_bcast(jnp.max(jnp.max(sc, axis=0), axis=1, keepdims=True))
        smin = lane_bcast(jnp.min(jnp.min(jnp.where(sc > NEG_INF, sc, BIG), axis=0), axis=1, keepdims=True))
        nvalid = (past_len + qrow + 1).astype(F32)

        def count(cand, strict):
            blk = sc_ref[...]
            hit = (blk > cand[None]) if strict else (blk >= cand[None])
            c = jnp.sum(jnp.where(hit, 1.0, 0.0), axis=0)
            return lane_bcast(jnp.sum(c, axis=1, keepdims=True))

        lo, clo, chi = _kth_bisect(count, smin, smax, nvalid, float(kk), steps=BISECT_STEPS, warm=BISECT_WARM)
        thr_ref[...] = lo

        fix = (clo > kk) & (nvalid > kk)

        @pl.when(jnp.max(fix.astype(F32)) > 0.5)
        def _():
            keep = kk - chi
            tri = (lax.broadcasted_iota(jnp.int32, (LANES, LANES), 0)
                   <= lax.broadcasted_iota(jnp.int32, (LANES, LANES), 1)).astype(BF16)

            def body(c, run):
                blk = sc_ref[c]
                tie = fix & (blk == lo)
                rank = run + jnp.dot(tie.astype(BF16), tri, preferred_element_type=F32)
                sc_ref[c] = jnp.where(tie & (rank > keep), NEG_INF, blk)
                return run + lane_bcast(jnp.sum(tie.astype(F32), axis=1, keepdims=True))
            lax.fori_loop(0, NBLK, body, jnp.zeros((TQ, LANES), F32))


def _sample_scores(page_table, cache_kidx_t, l, qi_rows, w_rows, ki_new_t, *, P):
    NB, n_pages = page_table.shape
    TQ = qi_rows.shape[1] // IDX_HEADS
    n_steps = n_pages // P
    past_len = n_pages * PAGE_SIZE
    kk = min(TOPK_MAX, (past_len + TQ) // 4)
    NBLK = n_pages + 1

    def page_spec(r):
        return pl.BlockSpec((None, None, IDX_DIM, PAGE_SIZE), lambda b, j, pt: (l, pt[b, j * P + r], 0, 0))

    kern = functools.partial(_sample_score_kernel, P=P, n_steps=n_steps, past_len=past_len, kk=kk)
    grid_spec = pltpu.PrefetchScalarGridSpec(
        num_scalar_prefetch=1,
        grid=(NB, n_steps),
        in_specs=[page_spec(r) for r in range(P)] + [
            pl.BlockSpec((None, IDX_HEADS * TQ, IDX_DIM), lambda b, j, pt: (b, 0, 0)),
            pl.BlockSpec((None, IDX_HEADS * TQ, LANES), lambda b, j, pt: (b, 0, 0)),
            pl.BlockSpec((None, IDX_DIM, PAGE_SIZE), lambda b, j, pt: (b, 0, 0))],
        out_specs=[pl.BlockSpec((None, NBLK, TQ, LANES), lambda b, j, pt: (b, 0, 0, 0)),
                   pl.BlockSpec((None, TQ, LANES), lambda b, j, pt: (b, 0, 0))],
    )
    return pl.pallas_call(
        kern,
        grid_spec=grid_spec,
        out_shape=[jax.ShapeDtypeStruct((NB, NBLK, TQ, LANES), F32), jax.ShapeDtypeStruct((NB, TQ, LANES), F32)],
        compiler_params=_cparams(("arbitrary", "arbitrary")),
        name="sample_scores",
    )(page_table, *([cache_kidx_t] * P), qi_rows, w_rows, ki_new_t)


def _sample_attend_kernel(pt_ref, *refs, P, n_steps):
    kpages = refs[:P]
    vpages = refs[P:2 * P]
    sc_ref, scn_ref, thr_ref, q_ref, kn_ref, vn_ref, o_ref, m_ref, l_ref, acc_ref = refs[2 * P:2 * P + 10]
    j = pl.program_id(1)
    qz = q_ref[...]
    thr = thr_ref[...]

    @pl.when(j == 0)
    def _():
        m_ref[...] = jnp.full_like(m_ref, -BIG)
        l_ref[...] = jnp.zeros_like(l_ref)
        acc_ref[...] = jnp.zeros_like(acc_ref)

    def update(kblocks, vblocks, scblocks):
        s_parts = []
        for kb, scb in zip(kblocks, scblocks):
            s = jnp.dot(qz, kb.astype(BF16), preferred_element_type=F32)
            bias = jnp.where(scb >= thr, 0.0, NEG_INF)
            s_parts.append(s + jnp.concatenate([bias] * N_HEADS, axis=0))
        s = jnp.concatenate(s_parts, axis=1) if len(s_parts) > 1 else s_parts[0]
        m_old = m_ref[...]
        mnew = jnp.maximum(m_old, jnp.broadcast_to(jnp.max(s, axis=1, keepdims=True), m_old.shape))
        alpha = jnp.exp2(m_old - mnew)
        p = jnp.exp2(s - jnp.concatenate([mnew] * len(s_parts), axis=1) if len(s_parts) > 1 else s - mnew)
        l_ref[...] = alpha * l_ref[...] + jnp.broadcast_to(jnp.sum(p, axis=1, keepdims=True), m_old.shape)
        m_ref[...] = mnew
        pv = jnp.zeros(acc_ref.shape, F32)
        for r, vb in enumerate(vblocks):
            pv = pv + lax.dot_general(p[:, r * LANES:(r + 1) * LANES].astype(BF16), vb.astype(BF16),
                                      (((1,), (1,)), ((), ())), preferred_element_type=F32)
        acc_ref[...] = alpha * acc_ref[...] + pv

    update([kp[...] for kp in kpages], [vp[...] for vp in vpages], [sc_ref[r] for r in range(P)])

    @pl.when(j == n_steps - 1)
    def _():
        update([kn_ref[...]], [vn_ref[...]], [scn_ref[0]])
        o_ref[...] = acc_ref[...] / l_ref[...]


def _sample_attend(page_table, cache_k_t, cache_v_t, l, sc, thr, q_rows, k_new_t, v_new_t, *, P):
    NB, n_pages = page_table.shape
    R = q_rows.shape[1]
    TQ = R // N_HEADS
    n_steps = n_pages // P

    def page_spec(r):
        return pl.BlockSpec((None, None, KV_DIM, PAGE_SIZE), lambda b, j, pt: (l, pt[b, j * P + r], 0, 0))

    kern = functools.partial(_sample_attend_kernel, P=P, n_steps=n_steps)
    grid_spec = pltpu.PrefetchScalarGridSpec(
        num_scalar_prefetch=1,
        grid=(NB, n_steps),
        in_specs=[page_spec(r) for r in range(P)] + [page_spec(r) for r in range(P)] + [
            pl.BlockSpec((None, P, TQ, LANES), lambda b, j, pt: (b, j, 0, 0)),
            pl.BlockSpec((None, 1, TQ, LANES), lambda b, j, pt: (b, n_pages, 0, 0)),
            pl.BlockSpec((None, TQ, LANES), lambda b, j, pt: (b, 0, 0)),
            pl.BlockSpec((None, R, KV_DIM), lambda b, j, pt: (b, 0, 0)),
            pl.BlockSpec((None, KV_DIM, PAGE_SIZE), lambda b, j, pt: (b, 0, 0)),
            pl.BlockSpec((None, KV_DIM, PAGE_SIZE), lambda b, j, pt: (b, 0, 0))],
        out_specs=pl.BlockSpec((None, R, KV_DIM), lambda b, j, pt: (b, 0, 0)),
        scratch_shapes=[pltpu.VMEM((R, LANES), F32), pltpu.VMEM((R, LANES), F32), pltpu.VMEM((R, KV_DIM), F32)],
    )
    return pl.pallas_call(
        kern,
        grid_spec=grid_spec,
        out_shape=jax.ShapeDtypeStruct((NB, R, KV_DIM), F32),
        compiler_params=_cparams(("arbitrary", "arbitrary")),
        name="sample_attend",
    )(page_table, *([cache_k_t] * P), *([cache_v_t] * P), sc, sc, thr, q_rows, k_new_t, v_new_t)


def _mix_out_kernel(x_ref, g_ref, wg_ref, ca_ref, ao_ref, py_ref, wco_ref, wao_ref, wpo_ref, wo_ref, o_ref):
    x = x_ref[...]
    ms = jnp.mean(x * x, axis=-1, keepdims=True)
    h = (x * lax.rsqrt(ms + RMS_EPS) * g_ref[...]).astype(BF16)
    gates = jax.nn.sigmoid(jnp.dot(h, wg_ref[...], preferred_element_type=F32))
    y_a = jnp.dot(ca_ref[...], wco_ref[...], preferred_element_type=F32)
    y_b = jnp.dot(ao_ref[...], wao_ref[...], preferred_element_type=F32)
    y_c = jnp.dot(py_ref[...], wpo_ref[...], preferred_element_type=F32)
    merged = (gates[:, 0:D_MODEL] * y_a + gates[:, D_MODEL:2 * D_MODEL] * y_b
              + gates[:, 2 * D_MODEL:3 * D_MODEL] * y_c)
    o_ref[...] = x + jnp.dot(merged.astype(BF16), wo_ref[...], preferred_element_type=F32)


def _mix_out(x2, ca, ao, py, sw, l, *, TM):
    N = x2.shape[0]

    def tspec(c):
        return pl.BlockSpec((TM, c), lambda i: (i, 0))

    return pl.pallas_call(
        _mix_out_kernel,
        grid=(N // TM,),
        in_specs=[tspec(D_MODEL), _layer_spec((1, D_MODEL), l), _layer_spec((D_MODEL, N_BRANCH * D_MODEL), l),
                  tspec(D_CONV), tspec(D_ATTN), tspec(D_POOL),
                  _layer_spec((D_CONV, D_MODEL), l), _layer_spec((D_ATTN, D_MODEL), l),
                  _layer_spec((D_POOL, D_MODEL), l), _layer_spec((D_MODEL, D_MODEL), l)],
        out_specs=tspec(D_MODEL),
        out_shape=jax.ShapeDtypeStruct((N, D_MODEL), F32),
        compiler_params=_cparams(("arbitrary",)),
        name="mix_out",
    )(x2, sw["g_mix"], sw["w_g"], ca, ao, py, sw["w_conv_out"], sw["w_attn_out"], sw["w_pool_out"], sw["w_out"])


def _ffn_kernel(x_ref, g_ref, w1_ref, w2_ref, gf_ref, o_ref, *, final):
    x = x_ref[...]
    ms = jnp.mean(x * x, axis=-1, keepdims=True)
    h = (x * lax.rsqrt(ms + RMS_EPS) * g_ref[...]).astype(BF16)
    r = jnp.maximum(jnp.dot(h, w1_ref[...], preferred_element_type=F32), 0.0)
    y = x + jnp.dot((r * r).astype(BF16), w2_ref[...], preferred_element_type=F32)
    if final:
        ms2 = jnp.mean(y * y, axis=-1, keepdims=True)
        y = y * lax.rsqrt(ms2 + RMS_EPS) * gf_ref[...]
    o_ref[...] = y


def _ffn(x2, sw, l, g_final, *, TM, final):
    N = x2.shape[0]
    once = pl.Buffered(1)
    return pl.pallas_call(
        functools.partial(_ffn_kernel, final=final),
        grid=(N // TM,),
        in_specs=[pl.BlockSpec((TM, D_MODEL), lambda i: (i, 0)), _layer_spec((1, D_MODEL), l),
                  _layer_spec((D_MODEL, D_FF), l, pipeline_mode=once),
                  _layer_spec((D_FF, D_MODEL), l, pipeline_mode=once),
                  pl.BlockSpec((1, D_MODEL), lambda i: (0, 0))],
        out_specs=pl.BlockSpec((TM, D_MODEL), lambda i: (i, 0)),
        out_shape=jax.ShapeDtypeStruct((N, D_MODEL), F32),
        compiler_params=_cparams(("arbitrary",)),
        name="ffn",
    )(x2, sw["g_ffn"], sw["w_ff1"], sw["w_ff2"], g_final)


def _stacked_weights(g_mix, w_in, conv_w, conv_b, conv_ln_g, conv_ln_b, w_conv_out, w_attn_out,
                     pool_w, pool_scale, w_pool_out, w_out, g_ffn, w_ff1, w_ff2):
    depth = w_in.shape[0]
    o_ki = C_QI + IDX_HEADS * IDX_DIM
    o_wi = o_ki + IDX_DIM
    o_xc = o_wi + IDX_HEADS
    o_g = o_xc + D_POOL

    def pad_to_tile(a):
        return jnp.pad(a, ((0, 0), (0, 0), (0, LANES - a.shape[2])))

    w_a = jnp.concatenate([w_in[:, :, :o_ki], w_in[:, :, o_xc:o_g], pad_to_tile(w_in[:, :, o_ki:o_wi]),
                           pad_to_tile(w_in[:, :, o_wi:o_xc])], axis=2).astype(BF16)
    pw = jnp.zeros((depth, D_POOL, D_POOL), F32)
    for g in range(len(POOL_WINDOWS)):
        pw = pw.at[:, g * POOL_GC:(g + 1) * POOL_GC, g * POOL_GC:(g + 1) * POOL_GC].set(pool_w[:, g])

    def row(a):
        return a[:, None, :]

    return dict(
        g_mix=row(g_mix), w_a=w_a, w_g=w_in[:, :, o_g:].astype(BF16),
        conv_w=jnp.pad(conv_w, ((0, 0), (0, CONV_PAD - CONV_W), (0, 0))), conv_b=row(conv_b),
        ln_g=row(conv_ln_g), ln_b=row(conv_ln_b),
        pool_w=pw.astype(BF16), pool_scale=row(pool_scale),
        w_conv_out=w_conv_out.astype(BF16), w_attn_out=w_attn_out.astype(BF16),
        w_pool_out=w_pool_out.astype(BF16), w_out=w_out.astype(BF16),
        g_ffn=row(g_ffn), w_ff1=w_ff1.astype(BF16), w_ff2=w_ff2.astype(BF16))


def _tail_state(hist, new, keep):
    return jnp.concatenate([hist, new], axis=1)[:, -keep:]


def kernel(x_prompt, x_sample, cache_k, cache_v, cache_kidx, state_conv, state_pool, page_table, g_mix, w_in,
           conv_w, conv_b, conv_ln_g, conv_ln_b, w_conv_out, w_attn_out, pool_w, pool_scale, w_pool_out, w_out,
           g_ffn, w_ff1, w_ff2, g_final):
    B, T, _ = x_prompt.shape
    NB, TS, _ = x_sample.shape
    depth = w_in.shape[0]
    n_pool = cache_k.shape[1]
    past_len = page_table.shape[1] * PAGE_SIZE
    TM_P = min(PROMPT_TILE, T)
    gf = g_final[None]
    xp = x_prompt
    xs = x_sample
    zero_conv = jnp.zeros((B, CONV_PAD, D_CONV), F32)
    zero_pool = jnp.zeros((B, POOL_PAD, D_POOL), F32)
    cache_k_t = cache_k.transpose(0, 1, 3, 4, 2).reshape(depth, n_pool, KV_DIM, PAGE_SIZE)
    cache_v_t = cache_v.transpose(0, 1, 3, 4, 2).reshape(depth, n_pool, KV_DIM, PAGE_SIZE)
    cache_kidx_t = cache_kidx.transpose(0, 1, 3, 2)
    sw = _stacked_weights(g_mix, w_in, conv_w, conv_b, conv_ln_g, conv_ln_b, w_conv_out, w_attn_out,
                          pool_w, pool_scale, w_pool_out, w_out, g_ffn, w_ff1, w_ff2)
    tables_p = _rope_tables(0, T)
    tables_s = _rope_tables(past_len, TS)
    hist_conv_s = jnp.pad(state_conv, ((0, 0), (0, 0), (CONV_PAD - CONV_HIST, 0), (0, 0)))
    hist_pool_s = jnp.pad(state_pool, ((0, 0), (0, 0), (POOL_PAD - POOL_HIST, 0), (0, 0)))
    st_p = [[] for _ in range(5)]
    st_s = [[] for _ in range(5)]
    for l in range(depth):
        final = l == depth - 1

        (k, v, ki, u, xc, kibf, q, _, _, ca, py, qit, wq, kt, va) = _proj(
            xp, 0, tables_p, zero_conv, zero_pool, sw, l, G=1, TM=TM_P, adt=BF16, attn=True)
        ao = _attend_prompt(q, qit, wq, kibf, kt, va, CKS=TM_P)
        x2 = _mix_out(xp.reshape(B * T, D_MODEL), ca.reshape(B * T, D_CONV), ao.reshape(B * T, D_ATTN),
                      py.reshape(B * T, D_POOL), sw, l, TM=TM_P)
        xp = _ffn(x2, sw, l, gf, TM=TM_P, final=final).reshape(B, T, D_MODEL)
        st_p[0].append(k.reshape(B, T, N_KV_HEADS, HEAD_DIM))
        st_p[1].append(v.reshape(B, T, N_KV_HEADS, HEAD_DIM))
        st_p[2].append(ki)
        st_p[3].append(u[:, T - CONV_HIST:])
        st_p[4].append(xc[:, T - POOL_HIST:])

        (k, v, ki, u, xc, _, q, qi, wi, ca, py) = _proj(
            xs, past_len, tables_s, hist_conv_s[l], hist_pool_s[l], sw, l, G=NB, TM=TS, adt=F32, attn=False)
        q, qi, ca, py = q.astype(BF16), qi.astype(BF16), ca.astype(BF16), py.astype(BF16)
        qi_rows = qi.reshape(NB, TS, IDX_HEADS, IDX_DIM).transpose(0, 2, 1, 3).reshape(NB, IDX_HEADS * TS, IDX_DIM)
        w_rows = jnp.broadcast_to(wi[:, :, :IDX_HEADS].transpose(0, 2, 1).reshape(NB, IDX_HEADS * TS, 1),
                                  (NB, IDX_HEADS * TS, LANES))
        qh = q.reshape(NB, TS, N_HEADS, HEAD_DIM).transpose(0, 2, 1, 3)
        head_kv = (jnp.arange(N_HEADS) // GROUP)[None, :, None, None]
        q_rows = jnp.concatenate([jnp.where(head_kv == n, qh, jnp.zeros_like(qh)) for n in range(N_KV_HEADS)],
                                 axis=-1).reshape(NB, N_HEADS * TS, KV_DIM)
        pad_rows = ((0, 0), (0, PAGE_SIZE - TS), (0, 0))
        pages_per_step = min(PAGES_PER_STEP, page_table.shape[1])

        def new_page_t(a):
            return jnp.pad(a, pad_rows).transpose(0, 2, 1)

        sc, thr = _sample_scores(page_table, cache_kidx_t, l, qi_rows, w_rows, new_page_t(ki), P=pages_per_step)
        o_rows = _sample_attend(page_table, cache_k_t, cache_v_t, l, sc, thr, q_rows, new_page_t(k), new_page_t(v),
                                P=pages_per_step)
        o5 = o_rows.reshape(NB, N_KV_HEADS, GROUP, TS, N_KV_HEADS, HEAD_DIM)
        ao = jnp.stack([o5[:, n, :, :, n, :] for n in range(N_KV_HEADS)], axis=1)
        ao = ao.reshape(NB, N_HEADS, TS, HEAD_DIM).transpose(0, 2, 1, 3).reshape(NB * TS, D_ATTN).astype(BF16)
        x2 = _mix_out(xs.reshape(NB * TS, D_MODEL), ca.reshape(NB * TS, D_CONV), ao,
                      py.reshape(NB * TS, D_POOL), sw, l, TM=NB * TS)
        xs = _ffn(x2, sw, l, gf, TM=NB * TS, final=final).reshape(NB, TS, D_MODEL)
        st_s[0].append(k.reshape(NB, TS, N_KV_HEADS, HEAD_DIM))
        st_s[1].append(v.reshape(NB, TS, N_KV_HEADS, HEAD_DIM))
        st_s[2].append(ki)
        st_s[3].append(_tail_state(state_conv[l], u, CONV_HIST))
        st_s[4].append(_tail_state(state_pool[l], xc, POOL_HIST))

    sp = [jnp.stack(a, axis=0) for a in st_p]
    ss = [jnp.stack(a, axis=0) for a in st_s]
    return (xp, xs, sp[0], sp[1], sp[2], sp[3], sp[4], ss[0], ss[1], ss[2], ss[3], ss[4])
```

```python
import functools

import jax
import jax.numpy as jnp
from jax import lax
from jax.experimental import pallas as pl
from jax.experimental.pallas import tpu as pltpu

F32 = jnp.float32
BF16 = jnp.bfloat16

D_MODEL = 1024
D_CONV = 256
CONV_W = 31
CONV_HIST = CONV_W - 1
HEAD_DIM = 64
D_ATTN = 512
N_HEADS = 8
N_KV_HEADS = 2
KV_DIM = N_KV_HEADS * HEAD_DIM
GROUP = N_HEADS // N_KV_HEADS
IDX_HEADS = 8
IDX_DIM = 64
TOPK_MAX = 256
D_POOL = 256
POOL_WINDOWS = (2, 4, 8, 16)
POOL_GC = D_POOL // len(POOL_WINDOWS)
POOL_HIST = max(POOL_WINDOWS) - 1
N_BRANCH = 3
D_FF = 4 * D_MODEL
ROPE_THETA = 10000.0
RMS_EPS = 1e-6
LN_EPS = 1e-5
PAGE_SIZE = 128

LANES = 128
SUBLANES = 8
PROMPT_TILE = 512
PAGES_PER_STEP = 64
TIE_CHUNK = 256
BISECT_WARM = 16
BISECT_STEPS = 2
CHUNKS_PER_TRIP = 4
SCORE_CHUNKS_PER_TRIP = 8
CONV_PAD = 32
POOL_PAD = 16
Q_SCALE = HEAD_DIM ** -0.5 * 1.4426950408889634
BIG = 3.0e38
NEG_INF = float("-inf")

C_AIN, C_AGATE, C_Q, C_K, C_V, C_QI, C_XC, C_KI, C_WI, D_A = 0, 256, 512, 1024, 1152, 1280, 1792, 2048, 2176, 2304

VMEM_LIMIT = 56 * 1024 * 1024


def _cparams(sem):
    return pltpu.CompilerParams(dimension_semantics=sem, vmem_limit_bytes=VMEM_LIMIT)


def _layer_spec(shape, l, **kw):
    n = len(shape)
    return pl.BlockSpec((None,) + tuple(shape), lambda *_: (l,) + (0,) * n, **kw)


def _proj_kernel(x_ref, g_ref, w_ref, cos_ref, sin_ref, chist_ref, phist_ref, convw_ref, convb_ref,
                 lng_ref, lnb_ref, poolw_ref, pscale_ref,
                 k_ref, v_ref, ki_ref, u_ref, xc_ref, kibf_ref, q_ref, qi_ref, wi_ref, ca_ref, py_ref,
                 *rest, G, TM, pos0, n_tiles, attn):
    if attn:
        qit_ref, wq_ref, kt_ref, va_ref = rest[:4]
        rest = rest[4:]
    cs_ref, ps_ref, p2_ref, p4_ref, p8_ref, p16_ref = rest
    j = pl.program_id(1)
    R = G * TM
    x = x_ref[...].reshape(R, D_MODEL)
    ms = jnp.mean(x * x, axis=-1, keepdims=True)
    h = (x * lax.rsqrt(ms + RMS_EPS) * g_ref[...]).astype(BF16)
    z = jnp.dot(h, w_ref[...], preferred_element_type=F32)

    cos = cos_ref[...]
    sin = sin_ref[...]
    if G > 1:
        cos = jnp.concatenate([cos] * G, axis=0)
        sin = jnp.concatenate([sin] * G, axis=0)

    def rope(t):
        n = t.shape[1] // LANES
        c = jnp.concatenate([cos] * n, axis=1) if n > 1 else cos
        s = jnp.concatenate([sin] * n, axis=1) if n > 1 else sin
        lane = lax.broadcasted_iota(jnp.int32, t.shape, 1)
        first = (lane & (HEAD_DIM - 1)) < HEAD_DIM // 2
        sw = jnp.where(first, pltpu.roll(t, t.shape[1] - HEAD_DIM // 2, 1), pltpu.roll(t, HEAD_DIM // 2, 1))
        return t * c + sw * s

    q = rope(z[:, C_Q:C_Q + D_ATTN]) * Q_SCALE
    k = rope(z[:, C_K:C_K + KV_DIM])
    v = z[:, C_V:C_V + KV_DIM]
    qi = rope(z[:, C_QI:C_QI + IDX_HEADS * IDX_DIM])
    ki = rope(z[:, C_KI:C_KI + LANES])[:, :IDX_DIM]
    wi = z[:, C_WI:C_WI + LANES] * (IDX_HEADS ** -0.5 * IDX_DIM ** -0.5)

    k_ref[...] = k.reshape(G, TM, KV_DIM)
    v_ref[...] = v.reshape(G, TM, KV_DIM)
    ki_ref[...] = ki.reshape(G, TM, IDX_DIM)
    kibf_ref[...] = ki.reshape(G, TM, IDX_DIM).astype(kibf_ref.dtype)
    q_ref[...] = q.reshape(G, TM, D_ATTN).astype(q_ref.dtype)
    qi_ref[...] = qi.reshape(G, TM, IDX_HEADS * IDX_DIM).astype(qi_ref.dtype)
    wi_ref[...] = wi.reshape(G, TM, LANES)
    if attn:
        for b in range(TM // LANES):
            blk = slice(b * LANES, (b + 1) * LANES)
            qi_t = qi[blk, :].T
            qit_ref[b] = jnp.concatenate([qi_t[hh * IDX_DIM:(hh + 1) * IDX_DIM, :] for hh in range(IDX_HEADS)],
                                         axis=1).astype(BF16)
            wi_t = wi[blk, :].T
            wq_ref[b] = jnp.concatenate([wi_t[hh:hh + 1, :] for hh in range(IDX_HEADS)], axis=1)
        k_t = k.T
        lane = lax.broadcasted_iota(jnp.int32, (TM, KV_DIM), 1)
        tail = jnp.where(lane == HEAD_DIM, 1.0, 0.0)
        for n in range(N_KV_HEADS):
            kt_ref[0, n] = k_t[n * HEAD_DIM:(n + 1) * HEAD_DIM, :].astype(BF16)
            vn = v if n == 0 else pltpu.roll(v, KV_DIM - n * HEAD_DIM, 1)
            va_ref[0, n] = jnp.where(lane < HEAD_DIM, vn, tail).astype(BF16)

    a_in = z[:, C_AIN:C_AIN + D_CONV]
    a_gate = z[:, C_AGATE:C_AGATE + D_CONV]
    u = (a_in * jax.nn.sigmoid(a_gate)).reshape(G, TM, D_CONV)
    u_ref[...] = u

    @pl.when(j == 0)
    def _():
        cs_ref[:, 0:CONV_PAD, :] = chist_ref[...]
        ps_ref[:, 0:POOL_PAD, :] = phist_ref[...]

    cs_ref[:, CONV_PAD:CONV_PAD + TM, :] = u
    off = CONV_PAD - CONV_HIST
    rc = min(TM, 64)
    gc = min(G, 8)
    for g0 in range(0, G, gc):
        for r0 in range(0, TM, rc):
            acc = jnp.zeros((gc, rc, D_CONV), F32) + convb_ref[...][None]
            for t in range(CONV_W):
                acc = acc + convw_ref[t:t + 1, :][None] * cs_ref[g0:g0 + gc, off + t + r0:off + t + r0 + rc, :]
            mu = jnp.mean(acc, axis=-1, keepdims=True)
            var = jnp.mean(jnp.square(acc - mu), axis=-1, keepdims=True)
            y = (acc - mu) * lax.rsqrt(var + LN_EPS) * lng_ref[...][None] + lnb_ref[...][None]
            ca_ref[g0:g0 + gc, r0:r0 + rc, :] = (y * jax.nn.sigmoid(y)).astype(ca_ref.dtype)
    if n_tiles > 1:
        cs_ref[:, 0:CONV_PAD, :] = cs_ref[:, TM:TM + CONV_PAD, :]

    xc = z[:, C_XC:C_XC + D_POOL].reshape(G, TM, D_POOL)
    xc_ref[...] = xc
    ps_ref[:, POOL_PAD:POOL_PAD + TM, :] = xc
    RR = POOL_PAD + TM
    p2_ref[:, 1:RR, :] = ps_ref[:, 1:RR, :] + ps_ref[:, 0:RR - 1, :]
    p4_ref[:, 3:RR, :] = p2_ref[:, 3:RR, :] + p2_ref[:, 1:RR - 2, :]
    p8_ref[:, 7:RR, :] = p4_ref[:, 7:RR, :] + p4_ref[:, 3:RR - 4, :]
    p16_ref[:, 15:RR, :] = p8_ref[:, 15:RR, :] + p8_ref[:, 7:RR - 8, :]
    lane = lax.broadcasted_iota(jnp.int32, (G, TM, D_POOL), 2)
    row = lax.broadcasted_iota(jnp.int32, (G, TM, D_POOL), 1)
    g0m, g1m, g2m = lane < POOL_GC, lane < 2 * POOL_GC, lane < 3 * POOL_GC
    s_tok = slice(POOL_PAD, POOL_PAD + TM)
    wsum = jnp.where(g0m, p2_ref[:, s_tok, :],
                     jnp.where(g1m, p4_ref[:, s_tok, :], jnp.where(g2m, p8_ref[:, s_tok, :], p16_ref[:, s_tok, :])))
    win = jnp.where(g0m, POOL_WINDOWS[0], jnp.where(g1m, POOL_WINDOWS[1],
                                                     jnp.where(g2m, POOL_WINDOWS[2], POOL_WINDOWS[3])))
    cnt = jnp.minimum(pos0 + j * TM + row + 1, win).astype(F32)
    d = (wsum / cnt - xc).reshape(R, D_POOL).astype(BF16)
    py = jnp.dot(d, poolw_ref[...], preferred_element_type=F32) * pscale_ref[...]
    py_ref[...] = py.reshape(G, TM, D_POOL).astype(py_ref.dtype)
    if n_tiles > 1:
        ps_ref[:, 0:POOL_PAD, :] = ps_ref[:, TM:TM + POOL_PAD, :]


def _rope_tables(pos0, T):
    half = HEAD_DIM // 2
    pos = (pos0 + jnp.arange(T)).astype(F32)
    inv = ROPE_THETA ** (-jnp.arange(half, dtype=F32) / half)
    ang = pos[:, None] * inv[None, :]
    cos = jnp.tile(jnp.cos(ang), (1, 2 * LANES // HEAD_DIM))
    sin = jnp.tile(jnp.concatenate([-jnp.sin(ang), jnp.sin(ang)], axis=1), (1, LANES // HEAD_DIM))
    return cos, sin


def _proj(x3, pos0, tables, chist, phist, sw, l, *, G, TM, adt, attn):
    NS, T, _ = x3.shape
    n_tiles = T // TM
    cos, sin = tables

    def tok(c, dt=F32):
        return jax.ShapeDtypeStruct((NS, T, c), dt)

    def tspec(c):
        return pl.BlockSpec((G, TM, c), lambda s, j: (s, j, 0))

    out_cols = [(KV_DIM, F32), (KV_DIM, F32), (IDX_DIM, F32), (D_CONV, F32), (D_POOL, F32),
                (IDX_DIM, adt), (D_ATTN, adt), (IDX_HEADS * IDX_DIM, adt), (LANES, F32), (D_CONV, adt), (D_POOL, adt)]
    out_specs = [tspec(c) for c, _ in out_cols]
    out_shape = [tok(c, dt) for c, dt in out_cols]
    if attn:
        assert G == 1 and TM % LANES == 0
        nb = TM // LANES
        out_specs += [pl.BlockSpec((None, nb, IDX_DIM, IDX_HEADS * LANES), lambda s, j: (s, j, 0, 0)),
                      pl.BlockSpec((None, nb, 1, IDX_HEADS * LANES), lambda s, j: (s, j, 0, 0)),
                      pl.BlockSpec((None, 1, N_KV_HEADS, HEAD_DIM, TM), lambda s, j: (s, j, 0, 0, 0)),
                      pl.BlockSpec((None, 1, N_KV_HEADS, TM, LANES), lambda s, j: (s, j, 0, 0, 0))]
        out_shape += [jax.ShapeDtypeStruct((NS, T // LANES, IDX_DIM, IDX_HEADS * LANES), BF16),
                      jax.ShapeDtypeStruct((NS, T // LANES, 1, IDX_HEADS * LANES), F32),
                      jax.ShapeDtypeStruct((NS, n_tiles, N_KV_HEADS, HEAD_DIM, TM), BF16),
                      jax.ShapeDtypeStruct((NS, n_tiles, N_KV_HEADS, TM, LANES), BF16)]
    kern = functools.partial(_proj_kernel, G=G, TM=TM, pos0=pos0, n_tiles=n_tiles, attn=attn)
    return pl.pallas_call(
        kern,
        grid=(NS // G, n_tiles),
        in_specs=[tspec(D_MODEL), _layer_spec((1, D_MODEL), l), _layer_spec((D_MODEL, D_A), l),
                  pl.BlockSpec((TM, LANES), lambda s, j: (j, 0)), pl.BlockSpec((TM, LANES), lambda s, j: (j, 0)),
                  pl.BlockSpec((G, CONV_PAD, D_CONV), lambda s, j: (s, 0, 0)),
                  pl.BlockSpec((G, POOL_PAD, D_POOL), lambda s, j: (s, 0, 0)),
                  _layer_spec((CONV_PAD, D_CONV), l), _layer_spec((1, D_CONV), l), _layer_spec((1, D_CONV), l),
                  _layer_spec((1, D_CONV), l), _layer_spec((D_POOL, D_POOL), l), _layer_spec((1, D_POOL), l)],
        out_specs=out_specs,
        out_shape=out_shape,
        scratch_shapes=[pltpu.VMEM((G, CONV_PAD + TM, D_CONV), F32)] +
                       [pltpu.VMEM((G, POOL_PAD + TM, D_POOL), F32)] * 5,
        compiler_params=_cparams(("arbitrary", "arbitrary")),
        name="proj",
    )(x3, sw["g_mix"], sw["w_a"], cos, sin, chist, phist, sw["conv_w"], sw["conv_b"], sw["ln_g"], sw["ln_b"],
      sw["pool_w"], sw["pool_scale"])


def _for_chunks(n, body, carry, per_trip=CHUNKS_PER_TRIP):
    carry = lax.fori_loop(0, n // per_trip, lambda j, t: body(per_trip * j, per_trip, t), carry)
    done = (n // per_trip) * per_trip
    size = per_trip // 2
    while size >= 1:
        take = (n - done) >= size
        carry = lax.cond(take, functools.partial(body, done, size), lambda t: t, carry)
        done = done + jnp.where(take, size, 0)
        size //= 2
    return carry


def _kth_bisect(count, smin, smax, nvalid, kk, steps, warm=0, zero_counts=None):
    zero = jnp.zeros_like(smin)
    c_ge0, c_gt0 = zero_counts if zero_counts is not None else (count(zero, False), count(zero, True))
    pos = c_gt0 >= kk
    neg = c_ge0 < kk
    lo0 = jnp.where(neg, smin, zero)
    clo0 = jnp.where(neg, nvalid, c_ge0)
    hi0 = jnp.where(pos, BIG, zero)
    chi0 = jnp.where(pos, zero, jnp.where(neg, c_ge0, c_gt0))
    cand0 = jnp.where(pos, smax, smin * 0.5)
    small = nvalid <= kk
    done0 = (small | jnp.logical_not(pos | neg)).astype(F32)
    init = (jnp.where(small, smin, lo0), hi0, jnp.where(small, nvalid, clo0), chi0, cand0, done0)

    def count_ge(cand):
        return count(cand, False)

    def cond(st):
        return jnp.min(st[5]) < 0.5

    def body(st):
        for _ in range(steps):
            st = step(st)
        return st

    def step(st):
        lo, hi, clo, chi, cand, done = st
        c = count_ge(cand)
        act = done < 0.5
        ge = c >= kk
        up = act & ge
        dn = act & jnp.logical_not(ge)
        lo = jnp.where(up, cand, lo)
        clo = jnp.where(up, c, clo)
        hi = jnp.where(dn, cand, hi)
        chi = jnp.where(dn, c, chi)
        mid = lo * 0.5 + hi * 0.5
        inside = (mid > lo) & (mid < hi)
        fin = (clo == kk) | jnp.logical_not(inside)
        done = jnp.where(fin, 1.0, done)
        return lo, hi, clo, chi, mid, done

    if warm:
        init = lax.fori_loop(0, warm, lambda _, st: step(st), init)
    lo, _, clo, chi, _, _ = lax.while_loop(cond, body, init)
    return lo, clo, chi


def _attend_prompt_kernel(qit_ref, wq_ref, q_ref, ki_ref, kt_ref, va_ref, o_ref, sc_ref, acc_ref, m_ref, *,
                          CKS, CK, kk):
    i = pl.program_id(1)
    QB = LANES
    nchs = (i * QB + QB + CKS - 1) // CKS
    nch = nchs * (CKS // CK)
    qit = qit_ref[...]
    wq = wq_ref[...]
    qcol = i * QB + lax.broadcasted_iota(jnp.int32, (CKS, QB), 1)
    krow0 = lax.broadcasted_iota(jnp.int32, (CKS, QB), 0)

    def score_chunk(c, carry, causal):
        smax, smin, n_ge0, n_gt0 = carry
        s = jnp.dot(ki_ref[c], qit, preferred_element_type=F32)
        s = jnp.maximum(s, 0.0) * wq
        tot = s[:, 0:QB]
        for hh in range(1, IDX_HEADS):
            tot = tot + s[:, hh * QB:(hh + 1) * QB]
        if causal:
            valid = (krow0 + c * CKS) <= qcol
            masked = jnp.where(valid, tot, NEG_INF)
            lowest = jnp.where(valid, tot, BIG)
        else:
            masked = lowest = tot
        sc_ref[pl.ds(pl.multiple_of(c * CKS, CKS), CKS), :] = masked
        smax = jnp.maximum(smax, jnp.max(masked.reshape(CKS // SUBLANES, SUBLANES, QB), axis=0))
        smin = jnp.minimum(smin, jnp.min(lowest.reshape(CKS // SUBLANES, SUBLANES, QB), axis=0))
        n_ge0 = n_ge0 + jnp.sum(jnp.where(masked >= 0.0, 1.0, 0.0).reshape(CKS // SUBLANES, SUBLANES, QB), axis=0)
        n_gt0 = n_gt0 + jnp.sum(jnp.where(masked > 0.0, 1.0, 0.0).reshape(CKS // SUBLANES, SUBLANES, QB), axis=0)
        return smax, smin, n_ge0, n_gt0

    def score_run(first, count, carry):
        for u in range(count):
            carry = score_chunk(first + u, carry, causal=False)
        return carry

    carry = _for_chunks(nchs - 1, score_run,
                        (jnp.full((SUBLANES, QB), NEG_INF, F32), jnp.full((SUBLANES, QB), BIG, F32),
                         jnp.zeros((SUBLANES, QB), F32), jnp.zeros((SUBLANES, QB), F32)),
                        per_trip=SCORE_CHUNKS_PER_TRIP)
    smax, smin, n_ge0, n_gt0 = score_chunk(nchs - 1, carry, causal=True)
    smax = jnp.max(smax, axis=0, keepdims=True)
    smin = jnp.min(smin, axis=0, keepdims=True)
    zero_counts = (jnp.sum(n_ge0, axis=0, keepdims=True), jnp.sum(n_gt0, axis=0, keepdims=True))

    n_acc = 4
    rows = CKS // n_acc

    def count(cand, strict):
        def body(c, accs):
            base = pl.multiple_of(c * CKS, CKS)
            out = []
            for a in range(n_acc):
                blk = sc_ref[pl.ds(base + a * rows, rows), :]
                hit = (blk > cand) if strict else (blk >= cand)
                ones = jnp.where(hit, 1.0, 0.0).reshape(rows // SUBLANES, SUBLANES, QB)
                out.append(accs[a] + jnp.sum(ones, axis=0))
            return tuple(out)
        def run(first, count, accs):
            for u in range(count):
                accs = body(first + u, accs)
            return accs
        accs = _for_chunks(nchs, run, tuple(jnp.zeros((SUBLANES, QB), F32) for _ in range(n_acc)))
        return jnp.sum((accs[0] + accs[1]) + (accs[2] + accs[3]), axis=0, keepdims=True)

    nvalid = (i * QB + lax.broadcasted_iota(jnp.int32, (1, QB), 1) + 1).astype(F32)
    lo, clo, chi = _kth_bisect(count, smin, smax, nvalid, float(kk), steps=BISECT_STEPS, warm=BISECT_WARM,
                               zero_counts=zero_counts)

    fix = (clo > kk) & (nvalid > kk)

    @pl.when(jnp.max(fix.astype(F32)) > 0.5)
    def _():
        keep = kk - chi
        TF = min(TIE_CHUNK, CKS)
        tri = (lax.broadcasted_iota(jnp.int32, (TF, TF), 1)
               <= lax.broadcasted_iota(jnp.int32, (TF, TF), 0)).astype(BF16)

        def body(c, run):
            rows_c = pl.ds(pl.multiple_of(c * TF, TF), TF)
            blk = sc_ref[rows_c, :]
            tie = jnp.where(fix & (blk == lo), 1.0, 0.0)
            rank = run + jnp.dot(tri, tie.astype(BF16), preferred_element_type=F32)
            sc_ref[rows_c, :] = jnp.where((tie > 0.5) & (rank > keep), NEG_INF, blk)
            return run + jnp.sum(tie, axis=0, keepdims=True)
        lax.fori_loop(0, nchs * (CKS // TF), body, jnp.zeros((1, QB), F32))

    GQ = GROUP * QB
    q32 = q_ref[...].astype(F32)
    qn = [jnp.concatenate([q32[:, hh * HEAD_DIM:(hh + 1) * HEAD_DIM] for hh in range(n * GROUP, (n + 1) * GROUP)],
                          axis=0).astype(BF16) for n in range(N_KV_HEADS)]
    acc_ref[...] = jnp.zeros_like(acc_ref)
    m_ref[...] = jnp.full_like(m_ref, -BIG)
    reps = CK // LANES

    def att_unit(first, count, carry):
        chunks = [first + u for u in range(count)]
        bias_t = [jnp.where(sc_ref[pl.ds(pl.multiple_of(c * CK, CK), CK), :] >= lo, 0.0, NEG_INF).T
                  for c in chunks]
        for n in range(N_KV_HEADS):
            ss = []
            for c, bt in zip(chunks, bias_t):
                s = jnp.dot(qn[n], kt_ref[c, n], preferred_element_type=F32)
                ss.append((s.reshape(GROUP, QB, CK) + bt[None]).reshape(GQ, CK))
            m_old = m_ref[n]
            mnew = m_old
            for s in ss:
                mnew = jnp.maximum(mnew, jnp.max(s, axis=1, keepdims=True))
            mrep = jnp.concatenate([mnew] * reps, axis=1)
            pv = None
            for c, s in zip(chunks, ss):
                part = jnp.dot(jnp.exp2(s - mrep).astype(BF16), va_ref[c, n], preferred_element_type=F32)
                pv = part if pv is None else pv + part
            acc_ref[n] = jnp.exp2(m_old - mnew) * acc_ref[n] + pv
            m_ref[n] = mnew
        return carry

    _for_chunks(nch, att_unit, 0)

    outs = []
    for n in range(N_KV_HEADS):
        a = acc_ref[n]
        o = a[:, :HEAD_DIM] / a[:, HEAD_DIM:HEAD_DIM + 1]
        outs += [o[g * QB:(g + 1) * QB, :] for g in range(GROUP)]
    o_ref[...] = jnp.concatenate(outs, axis=1).astype(BF16)


def _attend_prompt(q, qit, wq, kibf, kt, va, *, CKS):
    B, T, _ = q.shape
    QB = LANES
    nq = T // QB
    nc, CK = kt.shape[1], kt.shape[4]
    ncs = T // CKS
    kk = min(TOPK_MAX, T // 4)
    kib = kibf.reshape(B, ncs, CKS, IDX_DIM)

    kern = functools.partial(_attend_prompt_kernel, CKS=CKS, CK=CK, kk=kk)
    return pl.pallas_call(
        kern,
        grid=(B, nq),
        in_specs=[pl.BlockSpec((None, None, IDX_DIM, 8 * QB), lambda b, i: (b, i, 0, 0)),
                  pl.BlockSpec((None, None, 1, 8 * QB), lambda b, i: (b, i, 0, 0)),
                  pl.BlockSpec((None, QB, D_ATTN), lambda b, i: (b, i, 0)),
                  pl.BlockSpec((None, ncs, CKS, IDX_DIM), lambda b, i: (b, 0, 0, 0)),
                  pl.BlockSpec((None, nc, N_KV_HEADS, HEAD_DIM, CK), lambda b, i: (b, 0, 0, 0, 0)),
                  pl.BlockSpec((None, nc, N_KV_HEADS, CK, LANES), lambda b, i: (b, 0, 0, 0, 0))],
        out_specs=pl.BlockSpec((None, QB, D_ATTN), lambda b, i: (b, i, 0)),
        out_shape=jax.ShapeDtypeStruct((B, T, D_ATTN), BF16),
        scratch_shapes=[pltpu.VMEM((T, QB), F32), pltpu.VMEM((N_KV_HEADS, GROUP * QB, LANES), F32),
                        pltpu.VMEM((N_KV_HEADS, GROUP * QB, LANES), F32)],
        compiler_params=_cparams(("arbitrary", "arbitrary")),
        name="attend_prompt",
    )(qit, wq, q, kib, kt, va)


def _sample_score_kernel(pt_ref, *refs, P, n_steps, past_len, kk):
    pages = refs[:P]
    qi_ref, wb_ref, kin_ref, sc_ref, thr_ref = refs[P:P + 5]
    j = pl.program_id(1)
    TQ = sc_ref.shape[1]
    NBLK = sc_ref.shape[0]
    qi = qi_ref[...]
    wb = wb_ref[...]

    def block_scores(kpage_t):
        s = jnp.dot(qi, kpage_t.astype(BF16), preferred_element_type=F32)
        s = jnp.maximum(s, 0.0) * wb
        return jnp.sum(s.reshape(IDX_HEADS, TQ, LANES), axis=0)

    for r in range(P):
        sc_ref[j * P + r] = block_scores(pages[r][...])

    @pl.when(j == n_steps - 1)
    def _():
        tot = block_scores(kin_ref[...])
        qrow = lax.broadcasted_iota(jnp.int32, (TQ, LANES), 0)
        kcol = lax.broadcasted_iota(jnp.int32, (TQ, LANES), 1)
        sc_ref[NBLK - 1] = jnp.where(kcol <= qrow, tot, NEG_INF)

        def lane_bcast(a):
            return jnp.broadcast_to(a, (TQ, LANES))

        sc = sc_ref[...]
        smax = lane_bcast(jnp.max(jnp.max(sc, axis=0), axis=1, keepdims=True))
        smin = lane_bcast(jnp.min(jnp.min(jnp.where(sc > NEG_INF, sc, BIG), axis=0), axis=1, keepdims=True))
        nvalid = (past_len + qrow + 1).astype(F32)

        def count(cand, strict):
            blk = sc_ref[...]
            hit = (blk > cand[None]) if strict else (blk >= cand[None])
            c = jnp.sum(jnp.where(hit, 1.0, 0.0), axis=0)
            return lane_bcast(jnp.sum(c, axis=1, keepdims=True))

        lo, clo, chi = _kth_bisect(count, smin, smax, nvalid, float(kk), steps=BISECT_STEPS, warm=BISECT_WARM)
        thr_ref[...] = lo

        fix = (clo > kk) & (nvalid > kk)

        @pl.when(jnp.max(fix.astype(F32)) > 0.5)
        def _():
            keep = kk - chi
            tri = (lax.broadcasted_iota(jnp.int32, (LANES, LANES), 0)
                   <= lax.broadcasted_iota(jnp.int32, (LANES, LANES), 1)).astype(BF16)

            def body(c, run):
                blk = sc_ref[c]
                tie = fix & (blk == lo)
                rank = run + jnp.dot(tie.astype(BF16), tri, preferred_element_type=F32)
                sc_ref[c] = jnp.where(tie & (rank > keep), NEG_INF, blk)
                return run + lane_bcast(jnp.sum(tie.astype(F32), axis=1, keepdims=True))
            lax.fori_loop(0, NBLK, body, jnp.zeros((TQ, LANES), F32))


def _sample_scores(page_table, cache_kidx_t, l, qi_rows, w_rows, ki_new_t, *, P):
    NB, n_pages = page_table.shape
    TQ = qi_rows.shape[1] // IDX_HEADS
    n_steps = n_pages // P
    past_len = n_pages * PAGE_SIZE
    kk = min(TOPK_MAX, (past_len + TQ) // 4)
    NBLK = n_pages + 1

    def page_spec(r):
        return pl.BlockSpec((None, None, IDX_DIM, PAGE_SIZE), lambda b, j, pt: (l, pt[b, j * P + r], 0, 0))

    kern = functools.partial(_sample_score_kernel, P=P, n_steps=n_steps, past_len=past_len, kk=kk)
    grid_spec = pltpu.PrefetchScalarGridSpec(
        num_scalar_prefetch=1,
        grid=(NB, n_steps),
        in_specs=[page_spec(r) for r in range(P)] + [
            pl.BlockSpec((None, IDX_HEADS * TQ, IDX_DIM), lambda b, j, pt: (b, 0, 0)),
            pl.BlockSpec((None, IDX_HEADS * TQ, LANES), lambda b, j, pt: (b, 0, 0)),
            pl.BlockSpec((None, IDX_DIM, PAGE_SIZE), lambda b, j, pt: (b, 0, 0))],
        out_specs=[pl.BlockSpec((None, NBLK, TQ, LANES), lambda b, j, pt: (b, 0, 0, 0)),
                   pl.BlockSpec((None, TQ, LANES), lambda b, j, pt: (b, 0, 0))],
    )
    return pl.pallas_call(
        kern,
        grid_spec=grid_spec,
        out_shape=[jax.ShapeDtypeStruct((NB, NBLK, TQ, LANES), F32), jax.ShapeDtypeStruct((NB, TQ, LANES), F32)],
        compiler_params=_cparams(("arbitrary", "arbitrary")),
        name="sample_scores",
    )(page_table, *([cache_kidx_t] * P), qi_rows, w_rows, ki_new_t)


def _sample_attend_kernel(pt_ref, *refs, P, n_steps):
    kpages = refs[:P]
    vpages = refs[P:2 * P]
    sc_ref, scn_ref, thr_ref, q_ref, kn_ref, vn_ref, o_ref, m_ref, l_ref, acc_ref = refs[2 * P:2 * P + 10]
    j = pl.program_id(1)
    qz = q_ref[...]
    thr = thr_ref[...]

    @pl.when(j == 0)
    def _():
        m_ref[...] = jnp.full_like(m_ref, -BIG)
        l_ref[...] = jnp.zeros_like(l_ref)
        acc_ref[...] = jnp.zeros_like(acc_ref)

    def update(kblocks, vblocks, scblocks):
        s_parts = []
        for kb, scb in zip(kblocks, scblocks):
            s = jnp.dot(qz, kb.astype(BF16), preferred_element_type=F32)
            bias = jnp.where(scb >= thr, 0.0, NEG_INF)
            s_parts.append(s + jnp.concatenate([bias] * N_HEADS, axis=0))
        s = jnp.concatenate(s_parts, axis=1) if len(s_parts) > 1 else s_parts[0]
        m_old = m_ref[...]
        mnew = jnp.maximum(m_old, jnp.broadcast_to(jnp.max(s, axis=1, keepdims=True), m_old.shape))
        alpha = jnp.exp2(m_old - mnew)
        p = jnp.exp2(s - jnp.concatenate([mnew] * len(s_parts), axis=1) if len(s_parts) > 1 else s - mnew)
        l_ref[...] = alpha * l_ref[...] + jnp.broadcast_to(jnp.sum(p, axis=1, keepdims=True), m_old.shape)
        m_ref[...] = mnew
        pv = jnp.zeros(acc_ref.shape, F32)
        for r, vb in enumerate(vblocks):
            pv = pv + lax.dot_general(p[:, r * LANES:(r + 1) * LANES].astype(BF16), vb.astype(BF16),
                                      (((1,), (1,)), ((), ())), preferred_element_type=F32)
        acc_ref[...] = alpha * acc_ref[...] + pv

    update([kp[...] for kp in kpages], [vp[...] for vp in vpages], [sc_ref[r] for r in range(P)])

    @pl.when(j == n_steps - 1)
    def _():
        update([kn_ref[...]], [vn_ref[...]], [scn_ref[0]])
        o_ref[...] = acc_ref[...] / l_ref[...]


def _sample_attend(page_table, cache_k_t, cache_v_t, l, sc, thr, q_rows, k_new_t, v_new_t, *, P):
    NB, n_pages = page_table.shape
    R = q_rows.shape[1]
    TQ = R // N_HEADS
    n_steps = n_pages // P

    def page_spec(r):
        return pl.BlockSpec((None, None, KV_DIM, PAGE_SIZE), lambda b, j, pt: (l, pt[b, j * P + r], 0, 0))

    kern = functools.partial(_sample_attend_kernel, P=P, n_steps=n_steps)
    grid_spec = pltpu.PrefetchScalarGridSpec(
        num_scalar_prefetch=1,
        grid=(NB, n_steps),
        in_specs=[page_spec(r) for r in range(P)] + [page_spec(r) for r in range(P)] + [
            pl.BlockSpec((None, P, TQ, LANES), lambda b, j, pt: (b, j, 0, 0)),
            pl.BlockSpec((None, 1, TQ, LANES), lambda b, j, pt: (b, n_pages, 0, 0)),
            pl.BlockSpec((None, TQ, LANES), lambda b, j, pt: (b, 0, 0)),
            pl.BlockSpec((None, R, KV_DIM), lambda b, j, pt: (b, 0, 0)),
            pl.BlockSpec((None, KV_DIM, PAGE_SIZE), lambda b, j, pt: (b, 0, 0)),
            pl.BlockSpec((None, KV_DIM, PAGE_SIZE), lambda b, j, pt: (b, 0, 0))],
        out_specs=pl.BlockSpec((None, R, KV_DIM), lambda b, j, pt: (b, 0, 0)),
        scratch_shapes=[pltpu.VMEM((R, LANES), F32), pltpu.VMEM((R, LANES), F32), pltpu.VMEM((R, KV_DIM), F32)],
    )
    return pl.pallas_call(
        kern,
        grid_spec=grid_spec,
        out_shape=jax.ShapeDtypeStruct((NB, R, KV_DIM), F32),
        compiler_params=_cparams(("arbitrary", "arbitrary")),
        name="sample_attend",
    )(page_table, *([cache_k_t] * P), *([cache_v_t] * P), sc, sc, thr, q_rows, k_new_t, v_new_t)


def _mix_out_kernel(x_ref, g_ref, wg_ref, ca_ref, ao_ref, py_ref, wco_ref, wao_ref, wpo_ref, wo_ref, o_ref):
    x = x_ref[...]
    ms = jnp.mean(x * x, axis=-1, keepdims=True)
    h = (x * lax.rsqrt(ms + RMS_EPS) * g_ref[...]).astype(BF16)
    gates = jax.nn.sigmoid(jnp.dot(h, wg_ref[...], preferred_element_type=F32))
    y_a = jnp.dot(ca_ref[...], wco_ref[...], preferred_element_type=F32)
    y_b = jnp.dot(ao_ref[...], wao_ref[...], preferred_element_type=F32)
    y_c = jnp.dot(py_ref[...], wpo_ref[...], preferred_element_type=F32)
    merged = (gates[:, 0:D_MODEL] * y_a + gates[:, D_MODEL:2 * D_MODEL] * y_b
              + gates[:, 2 * D_MODEL:3 * D_MODEL] * y_c)
    o_ref[...] = x + jnp.dot(merged.astype(BF16), wo_ref[...], preferred_element_type=F32)


def _mix_out(x2, ca, ao, py, sw, l, *, TM):
    N = x2.shape[0]

    def tspec(c):
        return pl.BlockSpec((TM, c), lambda i: (i, 0))

    return pl.pallas_call(
        _mix_out_kernel,
        grid=(N // TM,),
        in_specs=[tspec(D_MODEL), _layer_spec((1, D_MODEL), l), _layer_spec((D_MODEL, N_BRANCH * D_MODEL), l),
                  tspec(D_CONV), tspec(D_ATTN), tspec(D_POOL),
                  _layer_spec((D_CONV, D_MODEL), l), _layer_spec((D_ATTN, D_MODEL), l),
                  _layer_spec((D_POOL, D_MODEL), l), _layer_spec((D_MODEL, D_MODEL), l)],
        out_specs=tspec(D_MODEL),
        out_shape=jax.ShapeDtypeStruct((N, D_MODEL), F32),
        compiler_params=_cparams(("arbitrary",)),
        name="mix_out",
    )(x2, sw["g_mix"], sw["w_g"], ca, ao, py, sw["w_conv_out"], sw["w_attn_out"], sw["w_pool_out"], sw["w_out"])


def _ffn_kernel(x_ref, g_ref, w1_ref, w2_ref, gf_ref, o_ref, *, final):
    x = x_ref[...]
    ms = jnp.mean(x * x, axis=-1, keepdims=True)
    h = (x * lax.rsqrt(ms + RMS_EPS) * g_ref[...]).astype(BF16)
    r = jnp.maximum(jnp.dot(h, w1_ref[...], preferred_element_type=F32), 0.0)
    y = x + jnp.dot((r * r).astype(BF16), w2_ref[...], preferred_element_type=F32)
    if final:
        ms2 = jnp.mean(y * y, axis=-1, keepdims=True)
        y = y * lax.rsqrt(ms2 + RMS_EPS) * gf_ref[...]
    o_ref[...] = y


def _ffn(x2, sw, l, g_final, *, TM, final):
    N = x2.shape[0]
    once = pl.Buffered(1)
    return pl.pallas_call(
        functools.partial(_ffn_kernel, final=final),
        grid=(N // TM,),
        in_specs=[pl.BlockSpec((TM, D_MODEL), lambda i: (i, 0)), _layer_spec((1, D_MODEL), l),
                  _layer_spec((D_MODEL, D_FF), l, pipeline_mode=once),
                  _layer_spec((D_FF, D_MODEL), l, pipeline_mode=once),
                  pl.BlockSpec((1, D_MODEL), lambda i: (0, 0))],
        out_specs=pl.BlockSpec((TM, D_MODEL), lambda i: (i, 0)),
        out_shape=jax.ShapeDtypeStruct((N, D_MODEL), F32),
        compiler_params=_cparams(("arbitrary",)),
        name="ffn",
    )(x2, sw["g_ffn"], sw["w_ff1"], sw["w_ff2"], g_final)


def _stacked_weights(g_mix, w_in, conv_w, conv_b, conv_ln_g, conv_ln_b, w_conv_out, w_attn_out,
                     pool_w, pool_scale, w_pool_out, w_out, g_ffn, w_ff1, w_ff2):
    depth = w_in.shape[0]
    o_ki = C_QI + IDX_HEADS * IDX_DIM
    o_wi = o_ki + IDX_DIM
    o_xc = o_wi + IDX_HEADS
    o_g = o_xc + D_POOL

    def pad_to_tile(a):
        return jnp.pad(a, ((0, 0), (0, 0), (0, LANES - a.shape[2])))

    w_a = jnp.concatenate([w_in[:, :, :o_ki], w_in[:, :, o_xc:o_g], pad_to_tile(w_in[:, :, o_ki:o_wi]),
                           pad_to_tile(w_in[:, :, o_wi:o_xc])], axis=2).astype(BF16)
    pw = jnp.zeros((depth, D_POOL, D_POOL), F32)
    for g in range(len(POOL_WINDOWS)):
        pw = pw.at[:, g * POOL_GC:(g + 1) * POOL_GC, g * POOL_GC:(g + 1) * POOL_GC].set(pool_w[:, g])

    def row(a):
        return a[:, None, :]

    return dict(
        g_mix=row(g_mix), w_a=w_a, w_g=w_in[:, :, o_g:].astype(BF16),
        conv_w=jnp.pad(conv_w, ((0, 0), (0, CONV_PAD - CONV_W), (0, 0))), conv_b=row(conv_b),
        ln_g=row(conv_ln_g), ln_b=row(conv_ln_b),
        pool_w=pw.astype(BF16), pool_scale=row(pool_scale),
        w_conv_out=w_conv_out.astype(BF16), w_attn_out=w_attn_out.astype(BF16),
        w_pool_out=w_pool_out.astype(BF16), w_out=w_out.astype(BF16),
        g_ffn=row(g_ffn), w_ff1=w_ff1.astype(BF16), w_ff2=w_ff2.astype(BF16))


def _tail_state(hist, new, keep):
    return jnp.concatenate([hist, new], axis=1)[:, -keep:]


def kernel(x_prompt, x_sample, cache_k, cache_v, cache_kidx, state_conv, state_pool, page_table, g_mix, w_in,
           conv_w, conv_b, conv_ln_g, conv_ln_b, w_conv_out, w_attn_out, pool_w, pool_scale, w_pool_out, w_out,
           g_ffn, w_ff1, w_ff2, g_final):
    B, T, _ = x_prompt.shape
    NB, TS, _ = x_sample.shape
    depth = w_in.shape[0]
    n_pool = cache_k.shape[1]
    past_len = page_table.shape[1] * PAGE_SIZE
    TM_P = min(PROMPT_TILE, T)
    gf = g_final[None]
    xp = x_prompt
    xs = x_sample
    zero_conv = jnp.zeros((B, CONV_PAD, D_CONV), F32)
    zero_pool = jnp.zeros((B, POOL_PAD, D_POOL), F32)
    cache_k_t = cache_k.transpose(0, 1, 3, 4, 2).reshape(depth, n_pool, KV_DIM, PAGE_SIZE)
    cache_v_t = cache_v.transpose(0, 1, 3, 4, 2).reshape(depth, n_pool, KV_DIM, PAGE_SIZE)
    cache_kidx_t = cache_kidx.transpose(0, 1, 3, 2)
    sw = _stacked_weights(g_mix, w_in, conv_w, conv_b, conv_ln_g, conv_ln_b, w_conv_out, w_attn_out,
                          pool_w, pool_scale, w_pool_out, w_out, g_ffn, w_ff1, w_ff2)
    tables_p = _rope_tables(0, T)
    tables_s = _rope_tables(past_len, TS)
    hist_conv_s = jnp.pad(state_conv, ((0, 0), (0, 0), (CONV_PAD - CONV_HIST, 0), (0, 0)))
    hist_pool_s = jnp.pad(state_pool, ((0, 0), (0, 0), (POOL_PAD - POOL_HIST, 0), (0, 0)))
    st_p = [[] for _ in range(5)]
    st_s = [[] for _ in range(5)]
    for l in range(depth):
        final = l == depth - 1

        (k, v, ki, u, xc, kibf, q, _, _, ca, py, qit, wq, kt, va) = _proj(
            xp, 0, tables_p, zero_conv, zero_pool, sw, l, G=1, TM=TM_P, adt=BF16, attn=True)
        ao = _attend_prompt(q, qit, wq, kibf, kt, va, CKS=TM_P)
        x2 = _mix_out(xp.reshape(B * T, D_MODEL), ca.reshape(B * T, D_CONV), ao.reshape(B * T, D_ATTN),
                      py.reshape(B * T, D_POOL), sw, l, TM=TM_P)
        xp = _ffn(x2, sw, l, gf, TM=TM_P, final=final).reshape(B, T, D_MODEL)
        st_p[0].append(k.reshape(B, T, N_KV_HEADS, HEAD_DIM))
        st_p[1].append(v.reshape(B, T, N_KV_HEADS, HEAD_DIM))
        st_p[2].append(ki)
        st_p[3].append(u[:, T - CONV_HIST:])
        st_p[4].append(xc[:, T - POOL_HIST:])

        (k, v, ki, u, xc, _, q, qi, wi, ca, py) = _proj(
            xs, past_len, tables_s, hist_conv_s[l], hist_pool_s[l], sw, l, G=NB, TM=TS, adt=F32, attn=False)
        q, qi, ca, py = q.astype(BF16), qi.astype(BF16), ca.astype(BF16), py.astype(BF16)
        qi_rows = qi.reshape(NB, TS, IDX_HEADS, IDX_DIM).transpose(0, 2, 1, 3).reshape(NB, IDX_HEADS * TS, IDX_DIM)
        w_rows = jnp.broadcast_to(wi[:, :, :IDX_HEADS].transpose(0, 2, 1).reshape(NB, IDX_HEADS * TS, 1),
                                  (NB, IDX_HEADS * TS, LANES))
        qh = q.reshape(NB, TS, N_HEADS, HEAD_DIM).transpose(0, 2, 1, 3)
        head_kv = (jnp.arange(N_HEADS) // GROUP)[None, :, None, None]
        q_rows = jnp.concatenate([jnp.where(head_kv == n, qh, jnp.zeros_like(qh)) for n in range(N_KV_HEADS)],
                                 axis=-1).reshape(NB, N_HEADS * TS, KV_DIM)
        pad_rows = ((0, 0), (0, PAGE_SIZE - TS), (0, 0))
        pages_per_step = min(PAGES_PER_STEP, page_table.shape[1])

        def new_page_t(a):
            return jnp.pad(a, pad_rows).transpose(0, 2, 1)

        sc, thr = _sample_scores(page_table, cache_kidx_t, l, qi_rows, w_rows, new_page_t(ki), P=pages_per_step)
        o_rows = _sample_attend(page_table, cache_k_t, cache_v_t, l, sc, thr, q_rows, new_page_t(k), new_page_t(v),
                                P=pages_per_step)
        o5 = o_rows.reshape(NB, N_KV_HEADS, GROUP, TS, N_KV_HEADS, HEAD_DIM)
        ao = jnp.stack([o5[:, n, :, :, n, :] for n in range(N_KV_HEADS)], axis=1)
        ao = ao.reshape(NB, N_HEADS, TS, HEAD_DIM).transpose(0, 2, 1, 3).reshape(NB * TS, D_ATTN).astype(BF16)
        x2 = _mix_out(xs.reshape(NB * TS, D_MODEL), ca.reshape(NB * TS, D_CONV), ao,
                      py.reshape(NB * TS, D_POOL), sw, l, TM=NB * TS)
        xs = _ffn(x2, sw, l, gf, TM=NB * TS, final=final).reshape(NB, TS, D_MODEL)
        st_s[0].append(k.reshape(NB, TS, N_KV_HEADS, HEAD_DIM))
        st_s[1].append(v.reshape(NB, TS, N_KV_HEADS, HEAD_DIM))
        st_s[2].append(ki)
        st_s[3].append(_tail_state(state_conv[l], u, CONV_HIST))
        st_s[4].append(_tail_state(state_pool[l], xc, POOL_HIST))

    sp = [jnp.stack(a, axis=0) for a in st_p]
    ss = [jnp.stack(a, axis=0) for a in st_s]
    return (xp, xs, sp[0], sp[1], sp[2], sp[3], sp[4], ss[0], ss[1], ss[2], ss[3], ss[4])
```
